```python
import jax
import jax.numpy as jnp
from jax import lax
import numpy as np

D_MODEL = 1024
BATCH = 2
SEQ = 8192
DEPTH = 1

GRID_W = 64
CTX_LEN = 256
EPS = 1e-6

A_WIDTH = 512
A_GROUPS = 4
A_GROUP_DIM = A_WIDTH // A_GROUPS
A_CHUNK = 128
A_ROW_GROUPS = 2

B_HEADS = 4
B_DK = 64
B_DV = 128
B_KEY_WIDTH = B_HEADS * B_DK
B_VAL_WIDTH = B_HEADS * B_DV
B_GATE_RANK = 16
B_GATE_TAU = 16.0
B_CHUNK = 64

Q0 = 0
K0 = Q0 + B_KEY_WIDTH
V0 = K0 + B_KEY_WIDTH
LR0 = V0 + B_VAL_WIDTH
ZB0 = LR0 + 2 * B_GATE_RANK
UA0 = ZB0 + B_VAL_WIDTH
VA0 = UA0 + A_WIDTH
ZA0 = VA0 + A_WIDTH
G0 = ZA0 + A_WIDTH
IN_WIDTH = G0 + 2 * D_MODEL

kernel_name = "hybrid_gmlp_gla_prefix_block"


def rmsnorm(x, g):
    xf = x.astype(jnp.float32)
    y = xf * lax.rsqrt(jnp.mean(xf * xf, axis=-1, keepdims=True) + EPS)
    return (y * g.astype(jnp.float32)).astype(x.dtype)


def layernorm(x, g, b):
    xf = x.astype(jnp.float32)
    xc = xf - jnp.mean(xf, axis=-1, keepdims=True)
    y = xc * lax.rsqrt(jnp.mean(xc * xc, axis=-1, keepdims=True) + EPS)
    return (y * g.astype(jnp.float32) + b.astype(jnp.float32)).astype(x.dtype)


def spatial_mix(vg, ws, bs):
    b, l, g, dg = vg.shape
    vr = vg.reshape(b, l // A_CHUNK, A_CHUNK, g, dg)
    s = jnp.einsum('gij,bnjgc->bnigc', ws, vr) + bs.T[None, None, :, :, None]
    return s.reshape(b, l, g, dg)


def to_colmajor(t, rows):
    b, l, g, dg = t.shape
    return t.reshape(b, rows, GRID_W, g, dg).transpose(0, 2, 1, 3, 4).reshape(b, l, g, dg)


def from_colmajor(t, rows):
    b, l, g, dg = t.shape
    return t.reshape(b, GRID_W, rows, g, dg).transpose(0, 2, 1, 3, 4).reshape(b, l, g, dg)


def chunk_mlp_branch(p, rows, ln_g, ln_b, ws, bs, w_proj):
    b, l, _ = p.shape
    u = p[..., UA0:VA0]
    z = p[..., ZA0:G0]
    vn = layernorm(p[..., VA0:ZA0], ln_g, ln_b).reshape(b, l, A_GROUPS, A_GROUP_DIM)
    if rows is None:
        sv = spatial_mix(vn, ws, bs)
    else:
        r = A_ROW_GROUPS
        sv_row = spatial_mix(vn[:, :, :r], ws[:r], bs[:r])
        sv_col = from_colmajor(spatial_mix(to_colmajor(vn[:, :, r:], rows), ws[r:], bs[r:]), rows)
        sv = jnp.concatenate([sv_row, sv_col], axis=2)
    return (u * sv.reshape(b, l, A_WIDTH) * jax.nn.silu(z)) @ w_proj


def to_chunks(t):
    b, l, h, d = t.shape
    return t.reshape(b, l // B_CHUNK, B_CHUNK, h, d).transpose(0, 3, 1, 2, 4)


def from_chunks(t):
    b, h, n, c, d = t.shape
    return t.transpose(0, 2, 3, 1, 4).reshape(b, n * c, h, d)


def gla_states(kc, vc, ac, s0):
    cum = jnp.cumsum(ac, axis=3)
    cum_last = cum[:, :, :, -1]
    k_dec = kc * jnp.exp(cum_last[:, :, :, None] - cum)
    kv = jnp.einsum('bhncd,bhnce->bhnde', k_dec, vc)

    def step(s, inp):
        decay, upd = inp
        return decay[..., None] * s + upd, s

    s_final, s_before = lax.scan(step, s0, (jnp.moveaxis(jnp.exp(cum_last), 2, 0), jnp.moveaxis(kv, 2, 0)))
    return s_final, jnp.moveaxis(s_before, 0, 2), cum


def gla_chunk_outputs(qc, kc, vc, cum, s_before):
    qd = qc * jnp.exp(cum)
    kd = kc * jnp.exp(-cum)
    scores = jnp.einsum('bhnid,bhnjd->bhnij', qd, kd)
    mask = jnp.tril(jnp.ones((B_CHUNK, B_CHUNK), dtype=bool))
    scores = jnp.where(mask, scores, 0.0)
    return jnp.einsum('bhnij,bhnje->bhnie', scores, vc) + jnp.einsum('bhnid,bhnde->bhnie', qd, s_before)


def gla_direction(q, k, v, log_a, s0, reverse):
    if reverse:
        k, v, log_a = jnp.flip(k, 1), jnp.flip(v, 1), jnp.flip(log_a, 1)
        if q is not None:
            q = jnp.flip(q, 1)
    kc, vc, ac = to_chunks(k), to_chunks(v), to_chunks(log_a)
    s_final, s_before, cum = gla_states(kc, vc, ac, s0)
    if q is None:
        return None, s_final
    o = from_chunks(gla_chunk_outputs(to_chunks(q), kc, vc, cum, s_before))
    if reverse:
        o = jnp.flip(o, 1)
    return o, s_final


def gla_q(p):
    b, l, _ = p.shape
    return p[..., Q0:K0].reshape(b, l, B_HEADS, B_DK).astype(jnp.float32) * (B_DK ** -0.5)


def gla_kva(p, base, w2, gb):
    b, l, _ = p.shape
    k = p[..., K0 - base:V0 - base].reshape(b, l, B_HEADS, B_DK).astype(jnp.float32)
    v = p[..., V0 - base:LR0 - base].reshape(b, l, B_HEADS, B_DV).astype(jnp.float32)
    lr = p[..., LR0 - base:ZB0 - base].reshape(b, l, 2, B_GATE_RANK)
    logits = jnp.einsum('blrk,rkd->blrd', lr, w2) + gb
    log_a = (jax.nn.log_sigmoid(logits.astype(jnp.float32)) / B_GATE_TAU).reshape(b, l, 2, B_HEADS, B_DK)
    return k, v, log_a[:, :, 0], log_a[:, :, 1]


def gla_branch_out(o, z, g, w_proj):
    b, l, _, _ = o.shape
    on = o * lax.rsqrt(jnp.mean(o * o, axis=-1, keepdims=True) + EPS) * g.reshape(B_HEADS, B_DV).astype(jnp.float32)
    on = on.reshape(b, l, B_VAL_WIDTH).astype(z.dtype)
    return (on * jax.nn.silu(z)) @ w_proj


def merge_branches(p, ya, yb, w_out):
    g = jax.nn.sigmoid(p[..., G0:])
    return (g[..., :D_MODEL] * ya + g[..., D_MODEL:] * yb) @ w_out


def setup_inputs(seed: int = 0) -> dict:
    key = jax.random.key(seed)
    ks = jax.random.split(key, 20)

    def nrm(k, shape, s):
        return jax.random.normal(k, shape, jnp.float32) * s

    return {
        "x": nrm(ks[0], (BATCH, SEQ, D_MODEL), 1.0),
        "c": nrm(ks[1], (BATCH, D_MODEL), 1.0),
        "ctx": nrm(ks[2], (BATCH, CTX_LEN, D_MODEL), 1.0),
        "c_ctx": nrm(ks[3], (D_MODEL,), 1.0),
        "w_mod": nrm(ks[4], (DEPTH, D_MODEL, 3 * D_MODEL), D_MODEL ** -0.5),
        "b_mod": nrm(ks[5], (DEPTH, 3 * D_MODEL), 0.01),
        "norm_g": 1.0 + nrm(ks[6], (DEPTH, D_MODEL), 0.01),
        "w_in": nrm(ks[7], (DEPTH, D_MODEL, IN_WIDTH), D_MODEL ** -0.5),
        "a_ln_g": 1.0 + nrm(ks[8], (DEPTH, A_WIDTH), 0.01),
        "a_ln_b": nrm(ks[9], (DEPTH, A_WIDTH), 0.01),
        "a_ws": nrm(ks[10], (DEPTH, A_GROUPS, A_CHUNK, A_CHUNK), A_CHUNK ** -0.5),
        "a_bs": 1.0 + nrm(ks[11], (DEPTH, A_GROUPS, A_CHUNK), 0.01),
        "b_gate_w2": nrm(ks[12], (DEPTH, 2, B_GATE_RANK, B_KEY_WIDTH), B_GATE_RANK ** -0.5),
        "b_gate_b": nrm(ks[13], (DEPTH, 2, B_KEY_WIDTH), 0.1),
        "b_norm_g": 1.0 + nrm(ks[14], (DEPTH, B_VAL_WIDTH), 0.01),
        "w_proj_a": nrm(ks[15], (DEPTH, A_WIDTH, D_MODEL), A_WIDTH ** -0.5),
        "w_proj_b": nrm(ks[16], (DEPTH, B_VAL_WIDTH, D_MODEL), B_VAL_WIDTH ** -0.5),
        "w_out": nrm(ks[17], (DEPTH, D_MODEL, D_MODEL), D_MODEL ** -0.5),
        "final_norm_g": 1.0 + nrm(ks[18], (D_MODEL,), 0.01),
    }


def reference(x, c, ctx, c_ctx, w_mod, b_mod, norm_g, w_in, a_ln_g, a_ln_b, a_ws, a_bs,
              b_gate_w2, b_gate_b, b_norm_g, w_proj_a, w_proj_b, w_out, final_norm_g):
    rows = x.shape[1] // GRID_W
    xc = ctx
    for layer in range(DEPTH):
        last = layer == DEPTH - 1
        wm, bm, wi = w_mod[layer], b_mod[layer], w_in[layer]

        mod = jax.nn.silu(c) @ wm + bm
        shift, scale, gate = jnp.split(mod, 3, axis=-1)
        h = rmsnorm(x, norm_g[layer]) * (1 + scale[:, None]) + shift[:, None]
        p = h @ wi

        n_mod = (2 if last else 3) * D_MODEL
        mod_c = jax.nn.silu(c_ctx) @ wm[:, :n_mod] + bm[:n_mod]
        hc = rmsnorm(xc, norm_g[layer]) * (1 + mod_c[D_MODEL:2 * D_MODEL]) + mod_c[:D_MODEL]
        base_c = K0 if last else 0
        pc = hc @ (wi[:, K0:ZB0] if last else wi)

        kc, vc, ac_f, ac_b = gla_kva(pc, base_c, b_gate_w2[layer], b_gate_b[layer])
        qc = None if last else gla_q(pc)
        s0 = jnp.zeros((xc.shape[0], B_HEADS, B_DK, B_DV), jnp.float32)
        oc_f, sc_f = gla_direction(qc, kc, vc, ac_f, s0, False)
        oc_b, sc_b = gla_direction(qc, kc, vc, ac_b, s0, True)

        k, v, a_f, a_b = gla_kva(p, 0, b_gate_w2[layer], b_gate_b[layer])
        q = gla_q(p)
        o_f, _ = gla_direction(q, k, v, a_f, sc_f, False)
        o_b, _ = gla_direction(q, k, v, a_b, sc_b, True)
        yb = gla_branch_out(o_f + o_b, p[..., ZB0:UA0], b_norm_g[layer], w_proj_b[layer])

        ya = chunk_mlp_branch(p, rows, a_ln_g[layer], a_ln_b[layer], a_ws[layer], a_bs[layer], w_proj_a[layer])

        x = x + gate[:, None] * merge_branches(p, ya, yb, w_out[layer])

        if not last:
            ybc = gla_branch_out(oc_f + oc_b, pc[..., ZB0:UA0], b_norm_g[layer], w_proj_b[layer])
            yac = chunk_mlp_branch(pc, None, a_ln_g[layer], a_ln_b[layer], a_ws[layer], a_bs[layer], w_proj_a[layer])
            xc = xc + mod_c[2 * D_MODEL:] * merge_branches(pc, yac, ybc, w_out[layer])
    return rmsnorm(x, final_norm_g)
```

```python
import functools

import jax
import jax.numpy as jnp
from jax import lax
from jax.experimental import pallas as pl
from jax.experimental.pallas import tpu as pltpu

D_MODEL = 1024
GRID_W = 64
EPS = 1e-6

A_WIDTH = 512
A_GROUPS = 4
A_GROUP_DIM = 128
A_CHUNK = 128
A_ROW_GROUPS = 2

B_HEADS = 4
B_DK = 64
B_DV = 128
B_KEY_WIDTH = 256
B_VAL_WIDTH = 512
B_GATE_RANK = 16
B_GATE_TAU = 16.0
B_CHUNK = 64

Q0 = 0
K0 = Q0 + B_KEY_WIDTH
V0 = K0 + B_KEY_WIDTH
LR0 = V0 + B_VAL_WIDTH
ZB0 = LR0 + 2 * B_GATE_RANK
UA0 = ZB0 + B_VAL_WIDTH
VA0 = UA0 + A_WIDTH
ZA0 = VA0 + A_WIDTH
G0 = ZA0 + A_WIDTH
IN_WIDTH = G0 + 2 * D_MODEL

VMEM_LIMIT_BYTES = 56 * 1024 * 1024

BF16 = jnp.bfloat16
F32 = jnp.float32


def _dot(a, b):
    return jnp.dot(a.astype(BF16), b.astype(BF16), preferred_element_type=F32)


def _dot_nt(a, b):
    return lax.dot_general(a.astype(BF16), b.astype(BF16), (((1,), (1,)), ((), ())),
                           preferred_element_type=F32)


def _dot_tn(a, b):
    return lax.dot_general(a.astype(BF16), b.astype(BF16), (((0,), (0,)), ((), ())),
                           preferred_element_type=F32)


def _silu(x):
    return x * jax.nn.sigmoid(x)


def _rms_rows(x):
    return x * lax.rsqrt(jnp.mean(x * x, axis=-1, keepdims=True) + EPS)


def _mod_kernel(c_ref, w_ref, b_ref, o_ref):
    o_ref[...] = _dot(_silu(c_ref[...]), w_ref[...]) + b_ref[...]


def _mod_call(cc, wm, bm):
    n = wm.shape[1]
    tn = 1024
    return pl.pallas_call(
        _mod_kernel,
        grid=(n // tn,),
        in_specs=[pl.BlockSpec((8, D_MODEL), lambda j: (0, 0)),
                  pl.BlockSpec((D_MODEL, tn), lambda j: (0, j)),
                  pl.BlockSpec((1, tn), lambda j: (0, j))],
        out_specs=pl.BlockSpec((8, tn), lambda j: (0, j)),
        out_shape=jax.ShapeDtypeStruct((8, n), F32),
        name="mod",
    )(cc, wm, bm)


def _gate_logs(lr, w2_ref, gb_ref):
    out = []
    for r in range(2):
        logits = _dot(lr[:, r * B_GATE_RANK:(r + 1) * B_GATE_RANK], w2_ref[r]) + gb_ref[r:r + 1, :]
        out.append(jax.nn.log_sigmoid(logits) / B_GATE_TAU)
    return out


def _chunk_consts(reverse):
    i = lax.broadcasted_iota(jnp.int32, (B_CHUNK, B_CHUNK), 0)
    j = lax.broadcasted_iota(jnp.int32, (B_CHUNK, B_CHUNK), 1)
    mask = (j >= i) if reverse else (j <= i)
    return mask.astype(F32), mask


def _gla_chunk(q, k, v, a, states, reverse, consts):
    tri, mask = consts
    cum = jnp.dot(tri, a, preferred_element_type=F32, precision=lax.Precision.HIGHEST)
    last = 0 if reverse else B_CHUNK - 1
    cum_last = cum[last:last + 1, :]
    ones = jnp.ones((B_CHUNK, B_DV), F32)
    tot = lax.dot_general(a, ones, (((0,), (0,)), ((), ())), preferred_element_type=F32,
                          precision=lax.Precision.HIGHEST)
    kdec = k * jnp.exp(cum_last - cum)
    if q is not None:
        qd = q * jnp.exp(cum)
        kd = k * jnp.exp(-cum)
    outs = []
    new_states = []
    for h in range(B_HEADS):
        ks = slice(h * B_DK, (h + 1) * B_DK)
        vs = slice(h * B_DV, (h + 1) * B_DV)
        s_h = states[h]
        if q is not None:
            sc = _dot_nt(qd[:, ks], kd[:, ks])
            sc = jnp.where(mask, sc, 0.0)
            outs.append(_dot(sc, v[:, vs]) + _dot(qd[:, ks], s_h))
        new_states.append(jnp.exp(tot[ks, :]) * s_h + _dot_tn(kdec[:, ks], v[:, vs]))
    o = jnp.concatenate(outs, axis=1) if q is not None else None
    return o, new_states


def _ctx_kernel(ctx_ref, mod_ref, ng_ref, w_ref, w2_ref, gb_ref, sf_ref, sb_ref):
    xc = ctx_ref[0]
    shift = mod_ref[2:3, 0:D_MODEL]
    scale = mod_ref[2:3, D_MODEL:2 * D_MODEL]
    hc = _rms_rows(xc) * ng_ref[...] * (1.0 + scale) + shift
    pc = _dot(hc, w_ref[...])
    k = pc[:, 0:B_KEY_WIDTH]
    v = pc[:, B_KEY_WIDTH:B_KEY_WIDTH + B_VAL_WIDTH]
    lr = pc[:, B_KEY_WIDTH + B_VAL_WIDTH:]
    a_f, a_b = _gate_logs(lr, w2_ref, gb_ref)
    n = xc.shape[0] // B_CHUNK
    zero = [jnp.zeros((B_DK, B_DV), F32) for _ in range(B_HEADS)]
    cf = _chunk_consts(False)
    cb = _chunk_consts(True)
    sf = zero
    for c in range(n):
        rs = slice(c * B_CHUNK, (c + 1) * B_CHUNK)
        _, sf = _gla_chunk(None, k[rs], v[rs], a_f[rs], sf, False, cf)
    sb = zero
    for c in reversed(range(n)):
        rs = slice(c * B_CHUNK, (c + 1) * B_CHUNK)
        _, sb = _gla_chunk(None, k[rs], v[rs], a_b[rs], sb, True, cb)
    for h in range(B_HEADS):
        sf_ref[0, h] = sf[h]
        sb_ref[0, h] = sb[h]


def _ctx_call(ctx, mod, ng, w_kvl, w2, gb):
    b, lc, _ = ctx.shape
    nw = w_kvl.shape[1]
    st = jax.ShapeDtypeStruct((b, B_HEADS, B_DK, B_DV), F32)
    st_spec = pl.BlockSpec((1, B_HEADS, B_DK, B_DV), lambda i: (i, 0, 0, 0))
    return pl.pallas_call(
        _ctx_kernel,
        grid=(b,),
        in_specs=[pl.BlockSpec((1, lc, D_MODEL), lambda i: (i, 0, 0)),
                  pl.BlockSpec((8, 3 * D_MODEL), lambda i: (0, 0)),
                  pl.BlockSpec((1, D_MODEL), lambda i: (0, 0)),
                  pl.BlockSpec((D_MODEL, nw), lambda i: (0, 0)),
                  pl.BlockSpec((2, B_GATE_RANK, B_KEY_WIDTH), lambda i: (0, 0, 0)),
                  pl.BlockSpec((2, B_KEY_WIDTH), lambda i: (0, 0))],
        out_specs=[st_spec, st_spec],
        out_shape=[st, st],
        name="ctx",
    )(ctx, mod, ng, w_kvl, w2, gb)


def _latent_h(x, mod_ref, ng_ref):
    shift = mod_ref[0, 0:1, :]
    scale = mod_ref[0, 1:2, :]
    return _rms_rows(x) * ng_ref[...] * (1.0 + scale) + shift


def _pre_kernel(x_ref, mod_ref, ng_ref, w_ref, w2_ref, gb_ref, lg_ref, lb_ref,
                q_ref, k_ref, v_ref, af_ref, ab_ref, vr_ref, vc_ref):
    h = _latent_h(x_ref[0], mod_ref, ng_ref)
    p = _dot(h, w_ref[...])
    q_ref[0] = p[:, Q0:K0] * (B_DK ** -0.5)
    k_ref[0] = p[:, K0:V0]
    v_ref[0] = p[:, V0:LR0]
    a_f, a_b = _gate_logs(p[:, LR0 + A_WIDTH:], w2_ref, gb_ref)
    af_ref[0] = a_f
    ab_ref[0] = a_b
    va = p[:, LR0:LR0 + A_WIDTH]
    vc = va - jnp.mean(va, axis=-1, keepdims=True)
    vn = vc * lax.rsqrt(jnp.mean(vc * vc, axis=-1, keepdims=True) + EPS) * lg_ref[...] + lb_ref[...]
    nr = A_ROW_GROUPS * A_GROUP_DIM
    vr_ref[0] = vn[:, 0:nr]
    for g in range(A_GROUPS - A_ROW_GROUPS):
        vc_ref[0, g] = vn[:, nr + g * A_GROUP_DIM:nr + (g + 1) * A_GROUP_DIM]


def _pre_call(x, mod3, ng, w_pre, w2, gb, lg, lb, t):
    b, l, _ = x.shape
    nw = w_pre.shape[1]
    ncg = A_GROUPS - A_ROW_GROUPS

    def tok(width):
        return (pl.BlockSpec((1, t, width), lambda i, j: (i, j, 0)),
                jax.ShapeDtypeStruct((b, l, width), F32))

    outs = [tok(B_KEY_WIDTH), tok(B_KEY_WIDTH), tok(B_VAL_WIDTH), tok(B_KEY_WIDTH), tok(B_KEY_WIDTH),
            tok(A_ROW_GROUPS * A_GROUP_DIM),
            (pl.BlockSpec((1, ncg, t, A_GROUP_DIM), lambda i, j: (i, 0, j, 0)),
             jax.ShapeDtypeStruct((b, ncg, l, A_GROUP_DIM), F32))]
    const2 = lambda i, j: (0, 0)
    return pl.pallas_call(
        _pre_kernel,
        grid=(b, l // t),
        in_specs=[pl.BlockSpec((1, t, D_MODEL), lambda i, j: (i, j, 0)),
                  pl.BlockSpec((1, 3, D_MODEL), lambda i, j: (i, 0, 0)),
                  pl.BlockSpec((1, D_MODEL), const2),
                  pl.BlockSpec((D_MODEL, nw), const2),
                  pl.BlockSpec((2, B_GATE_RANK, B_KEY_WIDTH), lambda i, j: (0, 0, 0)),
                  pl.BlockSpec((2, B_KEY_WIDTH), const2),
                  pl.BlockSpec((1, A_WIDTH), const2),
                  pl.BlockSpec((1, A_WIDTH), const2)],
        out_specs=[o[0] for o in outs],
        out_shape=[o[1] for o in outs],
        compiler_params=pltpu.CompilerParams(
            dimension_semantics=("parallel", "parallel"), vmem_limit_bytes=VMEM_LIMIT_BYTES),
        name="pre",
    )(x, mod3, ng, w_pre, w2, gb, lg, lb)


def _gla_kernel(qf_ref, kf_ref, vf_ref, af_ref, qb_ref, kb_ref, vb_ref, ab_ref, s0f_ref, s0b_ref,
                of_ref, ob_ref, sf_scr, sb_scr):
    @pl.when(pl.program_id(1) == 0)
    def _():
        sf_scr[...] = s0f_ref[0]
        sb_scr[...] = s0b_ref[0]

    n = qf_ref.shape[1] // B_CHUNK
    cf = _chunk_consts(False)
    cb = _chunk_consts(True)
    sf = [sf_scr[h] for h in range(B_HEADS)]
    sb = [sb_scr[h] for h in range(B_HEADS)]
    for c in range(n):
        rs = pl.ds(c * B_CHUNK, B_CHUNK)
        o, sf = _gla_chunk(qf_ref[0, rs, :], kf_ref[0, rs, :], vf_ref[0, rs, :], af_ref[0, rs, :], sf, False, cf)
        of_ref[0, rs, :] = o
        rb = pl.ds((n - 1 - c) * B_CHUNK, B_CHUNK)
        o, sb = _gla_chunk(qb_ref[0, rb, :], kb_ref[0, rb, :], vb_ref[0, rb, :], ab_ref[0, rb, :], sb, True, cb)
        ob_ref[0, rb, :] = o
    for h in range(B_HEADS):
        sf_scr[h] = sf[h]
        sb_scr[h] = sb[h]


def _gla_call(q, k, v, a_f, a_b, s0f, s0b, t):
    b, l, _ = q.shape
    nb = l // t
    fwd = lambda i, j: (i, j, 0)
    bwd = lambda i, j: (i, nb - 1 - j, 0)

    def tok(width, im):
        return pl.BlockSpec((1, t, width), im)

    st_spec = pl.BlockSpec((1, B_HEADS, B_DK, B_DV), lambda i, j: (i, 0, 0, 0))
    o_shape = jax.ShapeDtypeStruct((b, l, B_VAL_WIDTH), F32)
    return pl.pallas_call(
        _gla_kernel,
        grid=(b, nb),
        in_specs=[tok(B_KEY_WIDTH, fwd), tok(B_KEY_WIDTH, fwd), tok(B_VAL_WIDTH, fwd), tok(B_KEY_WIDTH, fwd),
                  tok(B_KEY_WIDTH, bwd), tok(B_KEY_WIDTH, bwd), tok(B_VAL_WIDTH, bwd), tok(B_KEY_WIDTH, bwd),
                  st_spec, st_spec],
        out_specs=[tok(B_VAL_WIDTH, fwd), tok(B_VAL_WIDTH, bwd)],
        out_shape=[o_shape, o_shape],
        scratch_shapes=[pltpu.VMEM((B_HEADS, B_DK, B_DV), F32), pltpu.VMEM((B_HEADS, B_DK, B_DV), F32)],
        compiler_params=pltpu.CompilerParams(
            dimension_semantics=("parallel", "arbitrary"), vmem_limit_bytes=VMEM_LIMIT_BYTES),
        name="gla",
    )(q, k, v, a_f, q, k, v, a_b, s0f, s0b)


def _colmix_kernel(ws_ref, bs_ref, vn_ref, o_ref):
    o_ref[0, 0] = _dot(ws_ref[0], vn_ref[0, 0]) + bs_ref[0][:, 0:1]


def _colmix_call(ws_col, bs_col, vn_col, tn):
    b, g, rows, n = vn_col.shape
    return pl.pallas_call(
        _colmix_kernel,
        grid=(b, g, n // tn),
        in_specs=[pl.BlockSpec((1, rows, rows), lambda i, j, m: (j, 0, 0)),
                  pl.BlockSpec((1, rows, A_GROUP_DIM), lambda i, j, m: (j, 0, 0)),
                  pl.BlockSpec((1, 1, rows, tn), lambda i, j, m: (i, j, 0, m))],
        out_specs=pl.BlockSpec((1, 1, rows, tn), lambda i, j, m: (i, j, 0, m)),
        out_shape=jax.ShapeDtypeStruct((b, g, rows, n), F32),
        compiler_params=pltpu.CompilerParams(
            dimension_semantics=("parallel", "parallel", "parallel"), vmem_limit_bytes=VMEM_LIMIT_BYTES),
        name="colmix",
    )(ws_col, bs_col, vn_col)


def _main_kernel(x_ref, mod_ref, ng_ref, w_ref, vr_ref, sc_ref, of_ref, ob_ref, ws_ref, bs_ref,
                 bg_ref, wpa_ref, wpb_ref, wo_ref, fg_ref, o_ref, acta_scr):
    x = x_ref[0]
    t = x.shape[0]
    h = _latent_h(x, mod_ref, ng_ref)
    p = _dot(h, w_ref[...])
    zb = p[:, 0:B_VAL_WIDTH]
    u = p[:, B_VAL_WIDTH:B_VAL_WIDTH + A_WIDTH]
    za = p[:, B_VAL_WIDTH + A_WIDTH:B_VAL_WIDTH + 2 * A_WIDTH]
    gates = jax.nn.sigmoid(p[:, B_VAL_WIDTH + 2 * A_WIDTH:])

    uz = u * _silu(za)
    for g in range(A_GROUPS):
        cs = slice(g * A_GROUP_DIM, (g + 1) * A_GROUP_DIM)
        if g < A_ROW_GROUPS:
            for c in range(t // A_CHUNK):
                rs = slice(c * A_CHUNK, (c + 1) * A_CHUNK)
                sv = _dot(ws_ref[g], vr_ref[0, rs, cs]) + bs_ref[g][:, 0:1]
                acta_scr[rs, cs] = (uz[rs, cs] * sv).astype(BF16)
        else:
            acta_scr[:, cs] = (uz[:, cs] * sc_ref[0, g - A_ROW_GROUPS]).astype(BF16)
    ya = jnp.dot(acta_scr[...], wpa_ref[...], preferred_element_type=F32)

    o = of_ref[0] + ob_ref[0]
    on = jnp.concatenate(
        [_rms_rows(o[:, hd * B_DV:(hd + 1) * B_DV]) for hd in range(B_HEADS)], axis=1) * bg_ref[...]
    yb = _dot(on * _silu(zb), wpb_ref[...])

    m = gates[:, 0:D_MODEL] * ya + gates[:, D_MODEL:] * yb
    y = _dot(m, wo_ref[...])
    xo = x + mod_ref[0, 2:3, :] * y
    o_ref[0] = _rms_rows(xo) * fg_ref[...]


def _main_call(x, mod3, ng, w_main, vn_row, sv_col, o_f, o_b, ws, bs, bg, wpa, wpb, wo, fg, t):
    b, l, _ = x.shape
    nw = w_main.shape[1]
    ncg = A_GROUPS - A_ROW_GROUPS
    const2 = lambda i, j: (0, 0)
    const3 = lambda i, j: (0, 0, 0)
    tok = lambda width: pl.BlockSpec((1, t, width), lambda i, j: (i, j, 0))
    return pl.pallas_call(
        _main_kernel,
        grid=(b, l // t),
        in_specs=[tok(D_MODEL),
                  pl.BlockSpec((1, 3, D_MODEL), lambda i, j: (i, 0, 0)),
                  pl.BlockSpec((1, D_MODEL), const2),
                  pl.BlockSpec((D_MODEL, nw), const2),
                  tok(A_ROW_GROUPS * A_GROUP_DIM),
                  pl.BlockSpec((1, ncg, t, A_GROUP_DIM), lambda i, j: (i, 0, j, 0)),
                  tok(B_VAL_WIDTH), tok(B_VAL_WIDTH),
                  pl.BlockSpec((A_GROUPS, A_CHUNK, A_CHUNK), const3),
                  pl.BlockSpec((A_GROUPS, A_CHUNK, A_GROUP_DIM), const3),
                  pl.BlockSpec((1, B_VAL_WIDTH), const2),
                  pl.BlockSpec((A_WIDTH, D_MODEL), const2),
                  pl.BlockSpec((B_VAL_WIDTH, D_MODEL), const2),
                  pl.BlockSpec((D_MODEL, D_MODEL), const2),
                  pl.BlockSpec((1, D_MODEL), const2)],
        out_specs=tok(D_MODEL),
        out_shape=jax.ShapeDtypeStruct((b, l, D_MODEL), F32),
        scratch_shapes=[pltpu.VMEM((t, A_WIDTH), BF16)],
        compiler_params=pltpu.CompilerParams(
            dimension_semantics=("parallel", "parallel"), vmem_limit_bytes=VMEM_LIMIT_BYTES),
        name="main",
    )(x, mod3, ng, w_main, vn_row, sv_col, o_f, o_b, ws, bs, bg, wpa, wpb, wo, fg)


def kernel(x, c, ctx, c_ctx, w_mod, b_mod, norm_g, w_in, a_ln_g, a_ln_b, a_ws, a_bs, b_gate_w2, b_gate_b,
           b_norm_g, w_proj_a, w_proj_b, w_out, final_norm_g):
    assert w_mod.shape[0] == 1, "single-layer block"
    b, l, _ = x.shape
    rows = l // GRID_W
    assert rows == A_CHUNK
    wm, bm, wi = w_mod[0], b_mod[0], w_in[0]
    ng = norm_g[0][None, :]

    cc = jnp.zeros((8, D_MODEL), F32).at[0:b].set(c).at[b].set(c_ctx)
    mod = _mod_call(cc, wm, bm[None, :])
    mod3 = mod[0:b].reshape(b, 3, D_MODEL)

    wi16 = wi.astype(BF16)
    w2, gb = b_gate_w2[0], b_gate_b[0]
    s0f, s0b = _ctx_call(ctx, mod, ng, wi16[:, K0:ZB0], w2, gb)

    t_pre = 512
    w_pre = jnp.concatenate([wi16[:, Q0:LR0], wi16[:, VA0:ZA0], wi16[:, LR0:ZB0]], axis=1)
    q, k, v, a_f, a_b, vn_row, vn_col = _pre_call(
        x, mod3, ng, w_pre, w2, gb, a_ln_g[0][None, :], a_ln_b[0][None, :], t_pre)

    o_f, o_b = _gla_call(q, k, v, a_f, a_b, s0f, s0b, 512)

    ncg = A_GROUPS - A_ROW_GROUPS
    bs_b = jnp.broadcast_to(a_bs[0][:, :, None], (A_GROUPS, A_CHUNK, A_GROUP_DIM))
    sv_col = _colmix_call(a_ws[0][A_ROW_GROUPS:], bs_b[A_ROW_GROUPS:],
                          vn_col.reshape(b, ncg, rows, GRID_W * A_GROUP_DIM), 2048)
    sv_col = sv_col.reshape(b, ncg, l, A_GROUP_DIM)

    w_main = jnp.concatenate([wi16[:, ZB0:VA0], wi16[:, ZA0:]], axis=1)
    return _main_call(x, mod3, ng, w_main, vn_row, sv_col, o_f, o_b, a_ws[0], bs_b, b_norm_g[0][None, :],
                      w_proj_a[0].astype(BF16), w_proj_b[0].astype(BF16), w_out[0].astype(BF16),
                      final_norm_g[None, :], 512)
```

```python
import functools

import jax
import jax.numpy as jnp
from jax import lax
from jax.experimental import pallas as pl
from jax.experimental.pallas import tpu as pltpu

D_MODEL = 1024
GRID_W = 64
EPS = 1e-6

A_WIDTH = 512
A_GROUPS = 4
A_GROUP_DIM = 128
A_CHUNK = 128
A_ROW_GROUPS = 2

B_HEADS = 4
B_DK = 64
B_DV = 128
B_KEY_WIDTH = 256
B_VAL_WIDTH = 512
B_GATE_RANK = 16
B_GATE_TAU = 16.0
B_CHUNK = 64

Q0 = 0
K0 = Q0 + B_KEY_WIDTH
V0 = K0 + B_KEY_WIDTH
LR0 = V0 + B_VAL_WIDTH
ZB0 = LR0 + 2 * B_GATE_RANK
UA0 = ZB0 + B_VAL_WIDTH
VA0 = UA0 + A_WIDTH
ZA0 = VA0 + A_WIDTH
G0 = ZA0 + A_WIDTH
IN_WIDTH = G0 + 2 * D_MODEL

LANES = 128
GLA_BLOCK = 256
GLA_NC = GLA_BLOCK // B_CHUNK
SCAN_GROUP = 8

VMEM_LIMIT_BYTES = 56 * 1024 * 1024

BF16 = jnp.bfloat16
F32 = jnp.float32


def _dot(a, b):
    return jnp.dot(a.astype(BF16), b.astype(BF16), preferred_element_type=F32)


def _dot_nt(a, b):
    return lax.dot_general(a.astype(BF16), b.astype(BF16), (((1,), (1,)), ((), ())),
                           preferred_element_type=F32)


def _dot_tn(a, b):
    return lax.dot_general(a.astype(BF16), b.astype(BF16), (((0,), (0,)), ((), ())),
                           preferred_element_type=F32)


def _silu(x):
    return x * jax.nn.sigmoid(x)


def _rms_rows(x):
    return x * lax.rsqrt(jnp.mean(x * x, axis=-1, keepdims=True) + EPS)


def _head_lane_mask(parity, dtype):
    lane = lax.broadcasted_iota(jnp.int32, (1, B_KEY_WIDTH), 1)
    return (((lane // B_DK) % 2) == parity).astype(dtype)


def _mod_kernel(c_ref, w_ref, b_ref, o_ref):
    o_ref[...] = _dot(_silu(c_ref[...]), w_ref[...]) + b_ref[...]


def _mod_call(cc, wm, bm):
    n = wm.shape[1]
    tn = 1024
    return pl.pallas_call(
        _mod_kernel,
        grid=(n // tn,),
        in_specs=[pl.BlockSpec((8, D_MODEL), lambda j: (0, 0)),
                  pl.BlockSpec((D_MODEL, tn), lambda j: (0, j)),
                  pl.BlockSpec((1, tn), lambda j: (0, j))],
        out_specs=pl.BlockSpec((8, tn), lambda j: (0, j)),
        out_shape=jax.ShapeDtypeStruct((8, n), F32),
        name="mod",
    )(cc, wm, bm)


def _gate_logs(lr, w2_ref, gb_ref):
    out = []
    for r in range(2):
        logits = _dot(lr[:, r * B_GATE_RANK:(r + 1) * B_GATE_RANK], w2_ref[r]) + gb_ref[r:r + 1, :]
        out.append(jax.nn.log_sigmoid(logits) / B_GATE_TAU)
    return out


def _chunk_tri(reverse):
    i = lax.broadcasted_iota(jnp.int32, (GLA_BLOCK, GLA_BLOCK), 0)
    j = lax.broadcasted_iota(jnp.int32, (GLA_BLOCK, GLA_BLOCK), 1)
    same = (i // B_CHUNK) == (j // B_CHUNK)
    tri = (j >= i) if reverse else (j <= i)
    return (same & tri).astype(BF16)


def _sum_rows(rows):
    acc = rows[0]
    for r in rows[1:]:
        acc = acc + r
    return acc


def _block_operands(q, k, a, tri, reverse):
    hi = a.astype(BF16)
    lo = (a - hi.astype(F32)).astype(BF16)
    cum = (jnp.dot(tri, hi, preferred_element_type=F32) + jnp.dot(tri, lo, preferred_element_type=F32))
    last = 0 if reverse else B_CHUNK - 1
    tots = [cum[c * B_CHUNK + last:c * B_CHUNK + last + 1, :] for c in range(GLA_NC)]
    totb = jnp.concatenate([jnp.broadcast_to(t, (B_CHUNK, B_KEY_WIDTH)) for t in tots], axis=0)
    kdec = k * jnp.exp(totb - cum)
    later = []
    for c in range(GLA_NC):
        idx = list(range(0, c)) if reverse else list(range(c + 1, GLA_NC))
        if idx:
            later.append(jnp.broadcast_to(jnp.exp(_sum_rows([tots[m] for m in idx])), (B_CHUNK, B_KEY_WIDTH)))
        else:
            later.append(jnp.ones((B_CHUNK, B_KEY_WIDTH), F32))
    kblk = kdec * jnp.concatenate(later, axis=0)
    if q is None:
        return None, None, kdec, tots, kblk
    qd = q * jnp.exp(cum)
    kd = k * jnp.exp(-cum)
    return qd, kd, kdec, tots, kblk


def _block_kv_t(v16, kblk):
    k16 = kblk.astype(BF16)
    lane = lax.broadcasted_iota(jnp.int32, (1, B_KEY_WIDTH), 1)
    acc = jnp.zeros((B_DV, B_KEY_WIDTH), F32)
    for h in range(B_HEADS):
        full = _dot_tn(v16[:, h * B_DV:(h + 1) * B_DV], k16)
        acc = acc + jnp.where((lane // B_DK) == h, full, 0.0)
    return acc


def _ctx_kernel(ctx_ref, mod_ref, ng_ref, w_ref, w2_ref, gb_ref, sf_ref, sb_ref, *, ctx_row):
    xc = ctx_ref[0]
    shift = mod_ref[ctx_row:ctx_row + 1, 0:D_MODEL]
    scale = mod_ref[ctx_row:ctx_row + 1, D_MODEL:2 * D_MODEL]
    hc = _rms_rows(xc) * ng_ref[...] * (1.0 + scale) + shift
    pc = _dot(hc, w_ref[...])
    k = pc[:, 0:B_KEY_WIDTH]
    v16 = pc[:, B_KEY_WIDTH:B_KEY_WIDTH + B_VAL_WIDTH].astype(BF16)
    lr = pc[:, B_KEY_WIDTH + B_VAL_WIDTH:]
    a_f, a_b = _gate_logs(lr, w2_ref, gb_ref)
    for a, reverse, out_ref in ((a_f, False, sf_ref), (a_b, True, sb_ref)):
        _, _, _, _, kblk = _block_operands(None, k, a, _chunk_tri(reverse), reverse)
        out_ref[0] = _block_kv_t(v16, kblk)


def _ctx_call(ctx, mod, ng, w_kvl, w2, gb, ctx_row):
    b, lc, _ = ctx.shape
    assert lc == GLA_BLOCK, "context length must be one GLA block"
    nw = w_kvl.shape[1]
    st = jax.ShapeDtypeStruct((b, B_DV, B_KEY_WIDTH), F32)
    st_spec = pl.BlockSpec((1, B_DV, B_KEY_WIDTH), lambda i: (i, 0, 0))
    return pl.pallas_call(
        functools.partial(_ctx_kernel, ctx_row=ctx_row),
        grid=(b,),
        in_specs=[pl.BlockSpec((1, lc, D_MODEL), lambda i: (i, 0, 0)),
                  pl.BlockSpec((8, 3 * D_MODEL), lambda i: (0, 0)),
                  pl.BlockSpec((1, D_MODEL), lambda i: (0, 0)),
                  pl.BlockSpec((D_MODEL, nw), lambda i: (0, 0)),
                  pl.BlockSpec((2, B_GATE_RANK, B_KEY_WIDTH), lambda i: (0, 0, 0)),
                  pl.BlockSpec((2, B_KEY_WIDTH), lambda i: (0, 0))],
        out_specs=[st_spec, st_spec],
        out_shape=[st, st],
        name="ctx",
    )(ctx, mod, ng, w_kvl, w2, gb)


def _latent_h(x, mod_ref, ng_ref):
    shift = mod_ref[0, 0:1, :]
    scale = mod_ref[0, 1:2, :]
    return _rms_rows(x) * ng_ref[...] * (1.0 + scale) + shift


def _pre_kernel(x_ref, mod_ref, ng_ref, w_ref, w2_ref, gb_ref, lg_ref, lb_ref,
                qdf_ref, kdf_ref, kcf_ref, qdb_ref, kdb_ref, kcb_ref, v_ref,
                totf_ref, totb_ref, kvf_ref, kvb_ref, vr_ref, vc_ref):
    h = _latent_h(x_ref[0], mod_ref, ng_ref)
    p = _dot(h, w_ref[...])
    q = p[:, Q0:K0] * (B_DK ** -0.5)
    k = p[:, K0:V0]
    v16 = p[:, V0:LR0].astype(BF16)
    v_ref[0] = v16
    a_f, a_b = _gate_logs(p[:, LR0 + A_WIDTH:], w2_ref, gb_ref)
    nblk = x_ref.shape[1] // GLA_BLOCK
    for a, reverse, qd_ref, kd_ref, kc_ref, tot_ref, kv_ref in (
            (a_f, False, qdf_ref, kdf_ref, kcf_ref, totf_ref, kvf_ref),
            (a_b, True, qdb_ref, kdb_ref, kcb_ref, totb_ref, kvb_ref)):
        tri = _chunk_tri(reverse)
        for blk in range(nblk):
            rs = slice(blk * GLA_BLOCK, (blk + 1) * GLA_BLOCK)
            qd, kd, kdec, tots, kblk = _block_operands(q[rs], k[rs], a[rs], tri, reverse)
            qd_ref[0, rs, :] = qd.astype(BF16)
            kd_ref[0, rs, :] = kd.astype(BF16)
            kc_ref[0, rs, :] = kdec.astype(BF16)
            for c in range(GLA_NC):
                tot_ref[0, blk, c:c + 1, :] = tots[c]
            kv_ref[0, blk] = _block_kv_t(v16[rs], kblk)

    va = p[:, LR0:LR0 + A_WIDTH]
    vc = va - jnp.mean(va, axis=-1, keepdims=True)
    vn = vc * lax.rsqrt(jnp.mean(vc * vc, axis=-1, keepdims=True) + EPS) * lg_ref[...] + lb_ref[...]
    nr = A_ROW_GROUPS * A_GROUP_DIM
    vr_ref[0] = vn[:, 0:nr].astype(BF16)
    for g in range(A_GROUPS - A_ROW_GROUPS):
        vc_ref[0, g] = vn[:, nr + g * A_GROUP_DIM:nr + (g + 1) * A_GROUP_DIM].astype(BF16)


def _pre_call(x, mod3, ng, w_pre, w2, gb, lg, lb, t):
    b, l, _ = x.shape
    nw = w_pre.shape[1]
    ncg = A_GROUPS - A_ROW_GROUPS
    nblk = t // GLA_BLOCK

    def tok(width, dtype):
        return (pl.BlockSpec((1, t, width), lambda i, j: (i, j, 0)),
                jax.ShapeDtypeStruct((b, l, width), dtype))

    tot = (pl.BlockSpec((1, nblk, GLA_NC, B_KEY_WIDTH), lambda i, j: (i, j, 0, 0)),
           jax.ShapeDtypeStruct((b, l // GLA_BLOCK, GLA_NC, B_KEY_WIDTH), F32))
    kv = (pl.BlockSpec((1, nblk, B_DV, B_KEY_WIDTH), lambda i, j: (i, j, 0, 0)),
          jax.ShapeDtypeStruct((b, l // GLA_BLOCK, B_DV, B_KEY_WIDTH), F32))
    outs = [tok(B_KEY_WIDTH, BF16)] * 6 + [tok(B_VAL_WIDTH, BF16), tot, tot, kv, kv,
                                           tok(A_ROW_GROUPS * A_GROUP_DIM, BF16),
                                           (pl.BlockSpec((1, ncg, t, A_GROUP_DIM), lambda i, j: (i, 0, j, 0)),
                                            jax.ShapeDtypeStruct((b, ncg, l, A_GROUP_DIM), BF16))]
    const2 = lambda i, j: (0, 0)
    return pl.pallas_call(
        _pre_kernel,
        grid=(b, l // t),
        in_specs=[pl.BlockSpec((1, t, D_MODEL), lambda i, j: (i, j, 0)),
                  pl.BlockSpec((1, 3, D_MODEL), lambda i, j: (i, 0, 0)),
                  pl.BlockSpec((1, D_MODEL), const2),
                  pl.BlockSpec((D_MODEL, nw), const2),
                  pl.BlockSpec((2, B_GATE_RANK, B_KEY_WIDTH), lambda i, j: (0, 0, 0)),
                  pl.BlockSpec((2, B_KEY_WIDTH), const2),
                  pl.BlockSpec((1, A_WIDTH), const2),
                  pl.BlockSpec((1, A_WIDTH), const2)],
        out_specs=[o[0] for o in outs],
        out_shape=[o[1] for o in outs],
        compiler_params=pltpu.CompilerParams(
            dimension_semantics=("parallel", "parallel"), vmem_limit_bytes=VMEM_LIMIT_BYTES),
        name="pre",
    )(x, mod3, ng, w_pre, w2, gb, lg, lb)


def _gscan_kernel(kvf_ref, kvb_ref, totf_ref, totb_ref, s0f_ref, s0b_ref, sf_ref, sb_ref, stf_scr, stb_scr):
    @pl.when(pl.program_id(1) == 0)
    def _():
        stf_scr[...] = s0f_ref[0]
        stb_scr[...] = s0b_ref[0]

    n = kvf_ref.shape[1]
    sf = stf_scr[...]
    sb = stb_scr[...]
    for i in range(n):
        sf_ref[0, i] = sf.astype(BF16)
        dec = jnp.exp(jnp.sum(totf_ref[0, i], axis=0, keepdims=True))
        sf = dec * sf + kvf_ref[0, i]
        ib = n - 1 - i
        sb_ref[0, ib] = sb.astype(BF16)
        dec = jnp.exp(jnp.sum(totb_ref[0, ib], axis=0, keepdims=True))
        sb = dec * sb + kvb_ref[0, ib]
    stf_scr[...] = sf
    stb_scr[...] = sb


def _gscan_call(kv_f, kv_b, tot_f, tot_b, s0f, s0b):
    b, nb = kv_f.shape[0], kv_f.shape[1]
    g = SCAN_GROUP
    nj = nb // g
    fwd = lambda i, j: (i, j, 0, 0)
    bwd = lambda i, j: (i, nj - 1 - j, 0, 0)
    kv_blk = (1, g, B_DV, B_KEY_WIDTH)
    tot_blk = (1, g, GLA_NC, B_KEY_WIDTH)
    st_spec = pl.BlockSpec((1, B_DV, B_KEY_WIDTH), lambda i, j: (i, 0, 0))
    s_shape = jax.ShapeDtypeStruct((b, nb, B_DV, B_KEY_WIDTH), BF16)
    return pl.pallas_call(
        _gscan_kernel,
        grid=(b, nj),
        in_specs=[pl.BlockSpec(kv_blk, fwd), pl.BlockSpec(kv_blk, bwd),
                  pl.BlockSpec(tot_blk, fwd), pl.BlockSpec(tot_blk, bwd), st_spec, st_spec],
        out_specs=[pl.BlockSpec(kv_blk, fwd), pl.BlockSpec(kv_blk, bwd)],
        out_shape=[s_shape, s_shape],
        scratch_shapes=[pltpu.VMEM((B_DV, B_KEY_WIDTH), F32), pltpu.VMEM((B_DV, B_KEY_WIDTH), F32)],
        compiler_params=pltpu.CompilerParams(
            dimension_semantics=("parallel", "arbitrary"), vmem_limit_bytes=VMEM_LIMIT_BYTES),
        name="gscan",
    )(kv_f, kv_b, tot_f, tot_b, s0f, s0b)


def _scale_rows(x16, scales):
    parts = []
    for c, s in enumerate(scales):
        xc = x16[c * B_CHUNK:(c + 1) * B_CHUNK, :]
        parts.append(xc if s is None else (xc.astype(F32) * s).astype(BF16))
    return parts[0] if len(parts) == 1 else jnp.concatenate(parts, axis=0)


def _exp_sum(tots, idx):
    return jnp.exp(_sum_rows([tots[m] for m in idx])) if idx else None


def _gla_kernel(qdf_ref, kdf_ref, kcf_ref, qdb_ref, kdb_ref, kcb_ref, v_ref, totf_ref, totb_ref,
                sf_ref, sb_ref, o_ref, p_scr):
    nc = GLA_NC
    ch = B_CHUNK
    qdf, kdf, kcf = qdf_ref[0], kdf_ref[0], kcf_ref[0]
    qdb, kdb, kcb = qdb_ref[0], kdb_ref[0], kcb_ref[0]
    tf = [totf_ref[0, 0, c:c + 1, :] for c in range(nc)]
    tb = [totb_ref[0, 0, c:c + 1, :] for c in range(nc)]
    hm = [_head_lane_mask(par, BF16) for par in range(2)]

    def lanes(h):
        return slice((h // 2) * LANES, (h // 2 + 1) * LANES)

    row = lax.broadcasted_iota(jnp.int32, (ch, 2 * ch), 0)
    col = lax.broadcasted_iota(jnp.int32, (ch, 2 * ch), 1)
    m_fe = (col < ch) & (col <= row)
    m_be = ((col < ch) & (col >= row)) | (col >= ch)
    m_fo = (col < ch) | (col - ch <= row)
    m_bo = (col >= ch) & (col - ch >= row)
    chunk_even = (lax.broadcasted_iota(jnp.int32, (GLA_BLOCK, 1), 0) // ch) % 2 == 0
    kmix_f = jnp.where(chunk_even, kcf, kdf)
    kmix_b = jnp.where(chunk_even, kdb, kcb)
    for par in range(2):
        kdf_m, kdb_m = kdf * hm[par], kdb * hm[par]
        kmf_m, kmb_m = kmix_f * hm[par], kmix_b * hm[par]
        for h in range(par, B_HEADS, 2):
            ls = lanes(h)
            for i in range(nc // 2):
                pr = slice(2 * i * ch, (2 * i + 2) * ch)
                re = slice(2 * i * ch, (2 * i + 1) * ch)
                ro = slice((2 * i + 1) * ch, (2 * i + 2) * ch)
                even = (jnp.where(m_fe, _dot_nt(qdf[re, ls], kdf_m[pr, ls]), 0.0)
                        + jnp.where(m_be, _dot_nt(qdb[re, ls], kmb_m[pr, ls]), 0.0))
                odd = (jnp.where(m_fo, _dot_nt(qdf[ro, ls], kmf_m[pr, ls]), 0.0)
                       + jnp.where(m_bo, _dot_nt(qdb[ro, ls], kdb_m[pr, ls]), 0.0))
                p_scr[h, re, pr] = even.astype(BF16)
                p_scr[h, ro, pr] = odd.astype(BF16)

    def cross(lo, hi):
        if hi - lo <= 2:
            return
        mid = (lo + hi) // 2
        cross(lo, mid)
        cross(mid, hi)
        rl = slice(lo * ch, mid * ch)
        rh = slice(mid * ch, hi * ch)
        qf = _scale_rows(qdf[rh], [_exp_sum(tf, range(mid, c)) for c in range(mid, hi)])
        kf = _scale_rows(kcf[rl], [_exp_sum(tf, range(c + 1, mid)) for c in range(lo, mid)])
        qb = _scale_rows(qdb[rl], [_exp_sum(tb, range(c + 1, mid)) for c in range(lo, mid)])
        kb = _scale_rows(kcb[rh], [_exp_sum(tb, range(mid, c)) for c in range(mid, hi)])
        for h in range(B_HEADS):
            ls = lanes(h)
            m = hm[h % 2]
            p_scr[h, rh, rl] = _dot_nt(qf[:, ls], (kf * m)[:, ls]).astype(BF16)
            p_scr[h, rl, rh] = _dot_nt(qb[:, ls], (kb * m)[:, ls]).astype(BF16)

    cross(0, nc)

    qsf = _scale_rows(qdf, [_exp_sum(tf, range(0, c)) for c in range(nc)])
    qsb = _scale_rows(qdb, [_exp_sum(tb, range(c + 1, nc)) for c in range(nc)])
    sf = sf_ref[0, 0]
    sb = sb_ref[0, 0]
    v16 = v_ref[0]
    for h in range(B_HEADS):
        ls = lanes(h)
        m = hm[h % 2]
        o_h = jnp.dot(p_scr[h], v16[:, h * B_DV:(h + 1) * B_DV], preferred_element_type=F32)
        o_h = o_h + _dot_nt(qsf[:, ls], (sf * m)[:, ls]) + _dot_nt(qsb[:, ls], (sb * m)[:, ls])
        o_ref[0, :, h * B_DV:(h + 1) * B_DV] = o_h


def _gla_call(qd_f, kd_f, kc_f, qd_b, kd_b, kc_b, v16, tot_f, tot_b, s_f, s_b):
    b, l, _ = v16.shape
    t = GLA_BLOCK
    tok = lambda width: pl.BlockSpec((1, t, width), lambda i, j: (i, j, 0))
    tot_spec = pl.BlockSpec((1, 1, GLA_NC, B_KEY_WIDTH), lambda i, j: (i, j, 0, 0))
    st_spec = pl.BlockSpec((1, 1, B_DV, B_KEY_WIDTH), lambda i, j: (i, j, 0, 0))
    return pl.pallas_call(
        _gla_kernel,
        grid=(b, l // t),
        in_specs=[tok(B_KEY_WIDTH)] * 6 + [tok(B_VAL_WIDTH), tot_spec, tot_spec, st_spec, st_spec],
        out_specs=tok(B_VAL_WIDTH),
        out_shape=jax.ShapeDtypeStruct((b, l, B_VAL_WIDTH), F32),
        scratch_shapes=[pltpu.VMEM((B_HEADS, t, t), BF16)],
        compiler_params=pltpu.CompilerParams(
            dimension_semantics=("parallel", "parallel"), vmem_limit_bytes=VMEM_LIMIT_BYTES),
        name="gla",
    )(qd_f, kd_f, kc_f, qd_b, kd_b, kc_b, v16, tot_f, tot_b, s_f, s_b)


def _colmix_kernel(ws_ref, bs_ref, vn_ref, o_ref):
    o_ref[0, 0] = _dot(ws_ref[0], vn_ref[0, 0]) + bs_ref[0][:, 0:1]


def _colmix_call(ws_col, bs_col, vn_col, tn):
    b, g, rows, n = vn_col.shape
    return pl.pallas_call(
        _colmix_kernel,
        grid=(b, g, n // tn),
        in_specs=[pl.BlockSpec((1, rows, rows), lambda i, j, m: (j, 0, 0)),
                  pl.BlockSpec((1, rows, A_GROUP_DIM), lambda i, j, m: (j, 0, 0)),
                  pl.BlockSpec((1, 1, rows, tn), lambda i, j, m: (i, j, 0, m))],
        out_specs=pl.BlockSpec((1, 1, rows, tn), lambda i, j, m: (i, j, 0, m)),
        out_shape=jax.ShapeDtypeStruct((b, g, rows, n), F32),
        compiler_params=pltpu.CompilerParams(
            dimension_semantics=("parallel", "parallel", "parallel"), vmem_limit_bytes=VMEM_LIMIT_BYTES),
        name="colmix",
    )(ws_col, bs_col, vn_col)


def _main_kernel(x_ref, mod_ref, ng_ref, w_ref, vr_ref, sc_ref, og_ref, ws_ref, bs_ref,
                 bg_ref, wpa_ref, wpb_ref, wo_ref, fg_ref, o_ref, acta_scr):
    x = x_ref[0]
    t = x.shape[0]
    h = _latent_h(x, mod_ref, ng_ref)
    p = _dot(h, w_ref[...])
    zb = p[:, 0:B_VAL_WIDTH]
    u = p[:, B_VAL_WIDTH:B_VAL_WIDTH + A_WIDTH]
    za = p[:, B_VAL_WIDTH + A_WIDTH:B_VAL_WIDTH + 2 * A_WIDTH]
    gates = jax.nn.sigmoid(p[:, B_VAL_WIDTH + 2 * A_WIDTH:])

    uz = u * _silu(za)
    for g in range(A_GROUPS):
        cs = slice(g * A_GROUP_DIM, (g + 1) * A_GROUP_DIM)
        if g < A_ROW_GROUPS:
            for c in range(t // A_CHUNK):
                rs = slice(c * A_CHUNK, (c + 1) * A_CHUNK)
                sv = _dot(ws_ref[g], vr_ref[0, rs, cs]) + bs_ref[g][:, 0:1]
                acta_scr[rs, cs] = (uz[rs, cs] * sv).astype(BF16)
        else:
            acta_scr[:, cs] = (uz[:, cs] * sc_ref[0, g - A_ROW_GROUPS]).astype(BF16)
    ya = jnp.dot(acta_scr[...], wpa_ref[...], preferred_element_type=F32)

    o = og_ref[0]
    on = jnp.concatenate(
        [_rms_rows(o[:, hd * B_DV:(hd + 1) * B_DV]) for hd in range(B_HEADS)], axis=1) * bg_ref[...]
    yb = _dot(on * _silu(zb), wpb_ref[...])

    m = gates[:, 0:D_MODEL] * ya + gates[:, D_MODEL:] * yb
    y = _dot(m, wo_ref[...])
    xo = x + mod_ref[0, 2:3, :] * y
    o_ref[0] = _rms_rows(xo) * fg_ref[...]


def _main_call(x, mod3, ng, w_main, vn_row, sv_col, o_gla, ws, bs, bg, wpa, wpb, wo, fg, t):
    b, l, _ = x.shape
    nw = w_main.shape[1]
    ncg = A_GROUPS - A_ROW_GROUPS
    const2 = lambda i, j: (0, 0)
    const3 = lambda i, j: (0, 0, 0)
    tok = lambda width: pl.BlockSpec((1, t, width), lambda i, j: (i, j, 0))
    return pl.pallas_call(
        _main_kernel,
        grid=(b, l // t),
        in_specs=[tok(D_MODEL),
                  pl.BlockSpec((1, 3, D_MODEL), lambda i, j: (i, 0, 0)),
                  pl.BlockSpec((1, D_MODEL), const2),
                  pl.BlockSpec((D_MODEL, nw), const2),
                  tok(A_ROW_GROUPS * A_GROUP_DIM),
                  pl.BlockSpec((1, ncg, t, A_GROUP_DIM), lambda i, j: (i, 0, j, 0)),
                  tok(B_VAL_WIDTH),
                  pl.BlockSpec((A_GROUPS, A_CHUNK, A_CHUNK), const3),
                  pl.BlockSpec((A_GROUPS, A_CHUNK, A_GROUP_DIM), const3),
                  pl.BlockSpec((1, B_VAL_WIDTH), const2),
                  pl.BlockSpec((A_WIDTH, D_MODEL), const2),
                  pl.BlockSpec((B_VAL_WIDTH, D_MODEL), const2),
                  pl.BlockSpec((D_MODEL, D_MODEL), const2),
                  pl.BlockSpec((1, D_MODEL), const2)],
        out_specs=tok(D_MODEL),
        out_shape=jax.ShapeDtypeStruct((b, l, D_MODEL), F32),
        scratch_shapes=[pltpu.VMEM((t, A_WIDTH), BF16)],
        compiler_params=pltpu.CompilerParams(
            dimension_semantics=("parallel", "parallel"), vmem_limit_bytes=VMEM_LIMIT_BYTES),
        name="main",
    )(x, mod3, ng, w_main, vn_row, sv_col, o_gla, ws, bs, bg, wpa, wpb, wo, fg)


def kernel(x, c, ctx, c_ctx, w_mod, b_mod, norm_g, w_in, a_ln_g, a_ln_b, a_ws, a_bs, b_gate_w2, b_gate_b,
           b_norm_g, w_proj_a, w_proj_b, w_out, final_norm_g):
    assert w_mod.shape[0] == 1, "single-layer block"
    b, l, _ = x.shape
    rows = l // GRID_W
    assert rows == A_CHUNK
    wm, bm, wi = w_mod[0], b_mod[0], w_in[0]
    ng = norm_g[0][None, :]

    cc = jnp.zeros((8, D_MODEL), F32).at[0:b].set(c).at[b].set(c_ctx)
    mod = _mod_call(cc, wm, bm[None, :])
    mod3 = mod[0:b].reshape(b, 3, D_MODEL)

    wi16 = wi.astype(BF16)
    w2, gb = b_gate_w2[0], b_gate_b[0]
    s0f, s0b = _ctx_call(ctx, mod, ng, wi16[:, K0:ZB0], w2, gb, b)

    w_pre = jnp.concatenate([wi16[:, Q0:LR0], wi16[:, VA0:ZA0], wi16[:, LR0:ZB0]], axis=1)
    (qd_f, kd_f, kc_f, qd_b, kd_b, kc_b, v16, tot_f, tot_b, kv_f, kv_b, vn_row, vn_col) = _pre_call(
        x, mod3, ng, w_pre, w2, gb, a_ln_g[0][None, :], a_ln_b[0][None, :], 512)

    s_f, s_b = _gscan_call(kv_f, kv_b, tot_f, tot_b, s0f, s0b)
    o_gla = _gla_call(qd_f, kd_f, kc_f, qd_b, kd_b, kc_b, v16, tot_f, tot_b, s_f, s_b)

    ncg = A_GROUPS - A_ROW_GROUPS
    bs_b = jnp.broadcast_to(a_bs[0][:, :, None], (A_GROUPS, A_CHUNK, A_GROUP_DIM))
    sv_col = _colmix_call(a_ws[0][A_ROW_GROUPS:], bs_b[A_ROW_GROUPS:],
                          vn_col.reshape(b, ncg, rows, GRID_W * A_GROUP_DIM), 2048)
    sv_col = sv_col.reshape(b, ncg, l, A_GROUP_DIM)

    w_main = jnp.concatenate([wi16[:, ZB0:VA0], wi16[:, ZA0:]], axis=1)
    return _main_call(x, mod3, ng, w_main, vn_row, sv_col, o_gla, a_ws[0], bs_b, b_norm_g[0][None, :],
                      w_proj_a[0].astype(BF16), w_proj_b[0].astype(BF16), w_out[0].astype(BF16),
                      final_norm_g[None, :], 512)
```

```python
import functools

import jax
import jax.numpy as jnp
from jax import lax
from jax.experimental import pallas as pl
from jax.experimental.pallas import tpu as pltpu

D_MODEL = 1024
GRID_W = 64
EPS = 1e-6

A_WIDTH = 512
A_GROUPS = 4
A_GROUP_DIM = 128
A_CHUNK = 128
A_ROW_GROUPS = 2

B_HEADS = 4
B_DK = 64
B_DV = 128
B_KEY_WIDTH = 256
B_VAL_WIDTH = 512
B_GATE_RANK = 16
B_GATE_TAU = 16.0
B_CHUNK = 64

Q0 = 0
K0 = Q0 + B_KEY_WIDTH
V0 = K0 + B_KEY_WIDTH
LR0 = V0 + B_VAL_WIDTH
ZB0 = LR0 + 2 * B_GATE_RANK
UA0 = ZB0 + B_VAL_WIDTH
VA0 = UA0 + A_WIDTH
ZA0 = VA0 + A_WIDTH
G0 = ZA0 + A_WIDTH
IN_WIDTH = G0 + 2 * D_MODEL

LANES = 128
GLA_BLOCK = 256
GLA_NC = GLA_BLOCK // B_CHUNK
SCAN_GROUP = 8

VMEM_LIMIT_BYTES = 56 * 1024 * 1024

BF16 = jnp.bfloat16
F32 = jnp.float32


def _dot(a, b):
    return jnp.dot(a.astype(BF16), b.astype(BF16), preferred_element_type=F32)


def _dot_nt(a, b):
    return lax.dot_general(a.astype(BF16), b.astype(BF16), (((1,), (1,)), ((), ())),
                           preferred_element_type=F32)


def _dot_tn(a, b):
    return lax.dot_general(a.astype(BF16), b.astype(BF16), (((0,), (0,)), ((), ())),
                           preferred_element_type=F32)


def _silu(x):
    return x * jax.nn.sigmoid(x)


def _rms_rows(x):
    return x * lax.rsqrt(jnp.mean(x * x, axis=-1, keepdims=True) + EPS)


def _head_lane_mask(parity, dtype):
    lane = lax.broadcasted_iota(jnp.int32, (1, B_KEY_WIDTH), 1)
    return (((lane // B_DK) % 2) == parity).astype(dtype)


def _mod_kernel(c_ref, w_ref, b_ref, o_ref):
    o_ref[...] = _dot(_silu(c_ref[...]), w_ref[...]) + b_ref[...]


def _mod_call(cc, wm, bm):
    n = wm.shape[1]
    tn = 1024
    return pl.pallas_call(
        _mod_kernel,
        grid=(n // tn,),
        in_specs=[pl.BlockSpec((8, D_MODEL), lambda j: (0, 0)),
                  pl.BlockSpec((D_MODEL, tn), lambda j: (0, j)),
                  pl.BlockSpec((1, tn), lambda j: (0, j))],
        out_specs=pl.BlockSpec((8, tn), lambda j: (0, j)),
        out_shape=jax.ShapeDtypeStruct((8, n), F32),
        name="mod",
    )(cc, wm, bm)


def _wprep_kernel(wi_ref, wpa_ref, wpb_ref, wo_ref, wpre_ref, wmain_ref, wpa16_ref, wpb16_ref, wo16_ref):
    wpre_ref[:, 0:LR0] = wi_ref[0, :, Q0:LR0].astype(BF16)
    wpre_ref[:, LR0:LR0 + A_WIDTH] = wi_ref[0, :, VA0:ZA0].astype(BF16)
    wpre_ref[:, LR0 + A_WIDTH:] = wi_ref[0, :, LR0:ZB0].astype(BF16)
    wmain_ref[:, 0:VA0 - ZB0] = wi_ref[0, :, ZB0:VA0].astype(BF16)
    wmain_ref[:, VA0 - ZB0:] = wi_ref[0, :, ZA0:].astype(BF16)
    wpa16_ref[...] = wpa_ref[0].astype(BF16)
    wpb16_ref[...] = wpb_ref[0].astype(BF16)
    wo16_ref[...] = wo_ref[0].astype(BF16)


def _wprep_call(w_in, w_proj_a, w_proj_b, w_out):
    steps = 8
    n_pre = ZB0 + A_WIDTH
    n_main = IN_WIDTH - n_pre

    def rows3(a):
        return pl.BlockSpec((1, a.shape[1] // steps, a.shape[2]), lambda i: (0, i, 0))

    def rows2(nrows, ncols):
        return (pl.BlockSpec((nrows // steps, ncols), lambda i: (i, 0)),
                jax.ShapeDtypeStruct((nrows, ncols), BF16))

    outs = [rows2(D_MODEL, n_pre), rows2(D_MODEL, n_main), rows2(A_WIDTH, D_MODEL),
            rows2(B_VAL_WIDTH, D_MODEL), rows2(D_MODEL, D_MODEL)]
    return pl.pallas_call(
        _wprep_kernel,
        grid=(steps,),
        in_specs=[rows3(w_in), rows3(w_proj_a), rows3(w_proj_b), rows3(w_out)],
        out_specs=[o[0] for o in outs],
        out_shape=[o[1] for o in outs],
        compiler_params=pltpu.CompilerParams(
            dimension_semantics=("parallel",), vmem_limit_bytes=VMEM_LIMIT_BYTES),
        name="wprep",
    )(w_in, w_proj_a, w_proj_b, w_out)


def _gate_logs(lr, w2_ref, gb_ref):
    out = []
    for r in range(2):
        logits = _dot(lr[:, r * B_GATE_RANK:(r + 1) * B_GATE_RANK], w2_ref[r]) + gb_ref[r:r + 1, :]
        out.append(jax.nn.log_sigmoid(logits) / B_GATE_TAU)
    return out


def _chunk_tri(reverse):
    i = lax.broadcasted_iota(jnp.int32, (GLA_BLOCK, GLA_BLOCK), 0)
    j = lax.broadcasted_iota(jnp.int32, (GLA_BLOCK, GLA_BLOCK), 1)
    same = (i // B_CHUNK) == (j // B_CHUNK)
    tri = (j >= i) if reverse else (j <= i)
    return (same & tri).astype(BF16)


def _sum_rows(rows):
    acc = rows[0]
    for r in rows[1:]:
        acc = acc + r
    return acc


def _block_operands(q, k, a, tri, reverse):
    hi = a.astype(BF16)
    lo = (a - hi.astype(F32)).astype(BF16)
    cum = (jnp.dot(tri, hi, preferred_element_type=F32) + jnp.dot(tri, lo, preferred_element_type=F32))
    last = 0 if reverse else B_CHUNK - 1
    tots = [cum[c * B_CHUNK + last:c * B_CHUNK + last + 1, :] for c in range(GLA_NC)]
    totb = jnp.concatenate([jnp.broadcast_to(t, (B_CHUNK, B_KEY_WIDTH)) for t in tots], axis=0)
    kdec = k * jnp.exp(totb - cum)
    later = []
    for c in range(GLA_NC):
        idx = list(range(0, c)) if reverse else list(range(c + 1, GLA_NC))
        if idx:
            later.append(jnp.broadcast_to(jnp.exp(_sum_rows([tots[m] for m in idx])), (B_CHUNK, B_KEY_WIDTH)))
        else:
            later.append(jnp.ones((B_CHUNK, B_KEY_WIDTH), F32))
    kblk = kdec * jnp.concatenate(later, axis=0)
    if q is None:
        return None, None, kdec, tots, kblk
    qd = q * jnp.exp(cum)
    kd = k * jnp.exp(-cum)
    return qd, kd, kdec, tots, kblk


def _block_kv_t(v16, kblk):
    k16 = kblk.astype(BF16)
    lane = lax.broadcasted_iota(jnp.int32, (1, B_KEY_WIDTH), 1)
    acc = jnp.zeros((B_DV, B_KEY_WIDTH), F32)
    for h in range(B_HEADS):
        full = _dot_tn(v16[:, h * B_DV:(h + 1) * B_DV], k16)
        acc = acc + jnp.where((lane // B_DK) == h, full, 0.0)
    return acc


def _ctx_kernel(ctx_ref, mod_ref, ng_ref, w_ref, w2_ref, gb_ref, sf_ref, sb_ref, *, ctx_row):
    xc = ctx_ref[0]
    shift = mod_ref[ctx_row:ctx_row + 1, 0:D_MODEL]
    scale = mod_ref[ctx_row:ctx_row + 1, D_MODEL:2 * D_MODEL]
    hc = _rms_rows(xc) * ng_ref[...] * (1.0 + scale) + shift
    pc = _dot(hc, w_ref[...])
    k = pc[:, K0:V0]
    v16 = pc[:, V0:LR0].astype(BF16)
    lr = pc[:, LR0 + A_WIDTH:]
    a_f, a_b = _gate_logs(lr, w2_ref, gb_ref)
    for a, reverse, out_ref in ((a_f, False, sf_ref), (a_b, True, sb_ref)):
        _, _, _, _, kblk = _block_operands(None, k, a, _chunk_tri(reverse), reverse)
        out_ref[0] = _block_kv_t(v16, kblk)


def _ctx_call(ctx, mod, ng, w_kvl, w2, gb, ctx_row):
    b, lc, _ = ctx.shape
    assert lc == GLA_BLOCK, "context length must be one GLA block"
    nw = w_kvl.shape[1]
    st = jax.ShapeDtypeStruct((b, B_DV, B_KEY_WIDTH), F32)
    st_spec = pl.BlockSpec((1, B_DV, B_KEY_WIDTH), lambda i: (i, 0, 0))
    return pl.pallas_call(
        functools.partial(_ctx_kernel, ctx_row=ctx_row),
        grid=(b,),
        in_specs=[pl.BlockSpec((1, lc, D_MODEL), lambda i: (i, 0, 0)),
                  pl.BlockSpec((8, 3 * D_MODEL), lambda i: (0, 0)),
                  pl.BlockSpec((1, D_MODEL), lambda i: (0, 0)),
                  pl.BlockSpec((D_MODEL, nw), lambda i: (0, 0)),
                  pl.BlockSpec((2, B_GATE_RANK, B_KEY_WIDTH), lambda i: (0, 0, 0)),
                  pl.BlockSpec((2, B_KEY_WIDTH), lambda i: (0, 0))],
        out_specs=[st_spec, st_spec],
        out_shape=[st, st],
        name="ctx",
    )(ctx, mod, ng, w_kvl, w2, gb)


def _latent_h(x, mod_ref, ng_ref):
    shift = mod_ref[0, 0:1, :]
    scale = mod_ref[0, 1:2, :]
    return _rms_rows(x) * ng_ref[...] * (1.0 + scale) + shift


def _pre_kernel(x_ref, mod_ref, ng_ref, w_ref, w2_ref, gb_ref, lg_ref, lb_ref,
                qdf_ref, kdf_ref, kcf_ref, qdb_ref, kdb_ref, kcb_ref, v_ref,
                totf_ref, totb_ref, kvf_ref, kvb_ref, vr_ref, vc_ref):
    h = _latent_h(x_ref[0], mod_ref, ng_ref)
    p = _dot(h, w_ref[...])
    q = p[:, Q0:K0] * (B_DK ** -0.5)
    k = p[:, K0:V0]
    v16 = p[:, V0:LR0].astype(BF16)
    v_ref[0] = v16
    a_f, a_b = _gate_logs(p[:, LR0 + A_WIDTH:], w2_ref, gb_ref)
    nblk = x_ref.shape[1] // GLA_BLOCK
    for a, reverse, qd_ref, kd_ref, kc_ref, tot_ref, kv_ref in (
            (a_f, False, qdf_ref, kdf_ref, kcf_ref, totf_ref, kvf_ref),
            (a_b, True, qdb_ref, kdb_ref, kcb_ref, totb_ref, kvb_ref)):
        tri = _chunk_tri(reverse)
        for blk in range(nblk):
            rs = slice(blk * GLA_BLOCK, (blk + 1) * GLA_BLOCK)
            qd, kd, kdec, tots, kblk = _block_operands(q[rs], k[rs], a[rs], tri, reverse)
            qd_ref[0, rs, :] = qd.astype(BF16)
            kd_ref[0, rs, :] = kd.astype(BF16)
            kc_ref[0, rs, :] = kdec.astype(BF16)
            for c in range(GLA_NC):
                tot_ref[0, blk, c:c + 1, :] = tots[c]
            kv_ref[0, blk] = _block_kv_t(v16[rs], kblk)

    va = p[:, LR0:LR0 + A_WIDTH]
    vc = va - jnp.mean(va, axis=-1, keepdims=True)
    vn = vc * lax.rsqrt(jnp.mean(vc * vc, axis=-1, keepdims=True) + EPS) * lg_ref[...] + lb_ref[...]
    nr = A_ROW_GROUPS * A_GROUP_DIM
    vr_ref[0] = vn[:, 0:nr].astype(BF16)
    for g in range(A_GROUPS - A_ROW_GROUPS):
        vc_ref[0, g] = vn[:, nr + g * A_GROUP_DIM:nr + (g + 1) * A_GROUP_DIM]


def _pre_call(x, mod3, ng, w_pre, w2, gb, lg, lb, t):
    b, l, _ = x.shape
    nw = w_pre.shape[1]
    ncg = A_GROUPS - A_ROW_GROUPS
    nblk = t // GLA_BLOCK

    def tok(width, dtype):
        return (pl.BlockSpec((1, t, width), lambda i, j: (i, j, 0)),
                jax.ShapeDtypeStruct((b, l, width), dtype))

    tot = (pl.BlockSpec((1, nblk, GLA_NC, B_KEY_WIDTH), lambda i, j: (i, j, 0, 0)),
           jax.ShapeDtypeStruct((b, l // GLA_BLOCK, GLA_NC, B_KEY_WIDTH), F32))
    kv = (pl.BlockSpec((1, nblk, B_DV, B_KEY_WIDTH), lambda i, j: (i, j, 0, 0)),
          jax.ShapeDtypeStruct((b, l // GLA_BLOCK, B_DV, B_KEY_WIDTH), F32))
    outs = [tok(B_KEY_WIDTH, BF16)] * 6 + [tok(B_VAL_WIDTH, BF16), tot, tot, kv, kv,
                                           tok(A_ROW_GROUPS * A_GROUP_DIM, BF16),
                                           (pl.BlockSpec((1, ncg, t, A_GROUP_DIM), lambda i, j: (i, 0, j, 0)),
                                            jax.ShapeDtypeStruct((b, ncg, l, A_GROUP_DIM), F32))]
    const2 = lambda i, j: (0, 0)
    return pl.pallas_call(
        _pre_kernel,
        grid=(b, l // t),
        in_specs=[pl.BlockSpec((1, t, D_MODEL), lambda i, j: (i, j, 0)),
                  pl.BlockSpec((1, 3, D_MODEL), lambda i, j: (i, 0, 0)),
                  pl.BlockSpec((1, D_MODEL), const2),
                  pl.BlockSpec((D_MODEL, nw), const2),
                  pl.BlockSpec((2, B_GATE_RANK, B_KEY_WIDTH), lambda i, j: (0, 0, 0)),
                  pl.BlockSpec((2, B_KEY_WIDTH), const2),
                  pl.BlockSpec((1, A_WIDTH), const2),
                  pl.BlockSpec((1, A_WIDTH), const2)],
        out_specs=[o[0] for o in outs],
        out_shape=[o[1] for o in outs],
        compiler_params=pltpu.CompilerParams(
            dimension_semantics=("parallel", "parallel"), vmem_limit_bytes=VMEM_LIMIT_BYTES),
        name="pre",
    )(x, mod3, ng, w_pre, w2, gb, lg, lb)


def _gscan_kernel(kvf_ref, kvb_ref, totf_ref, totb_ref, s0f_ref, s0b_ref, sf_ref, sb_ref, stf_scr, stb_scr):
    @pl.when(pl.program_id(1) == 0)
    def _():
        stf_scr[...] = s0f_ref[0]
        stb_scr[...] = s0b_ref[0]

    n = kvf_ref.shape[1]
    sf = stf_scr[...]
    sb = stb_scr[...]
    for i in range(n):
        sf_ref[0, i] = sf.astype(BF16)
        dec = jnp.exp(jnp.sum(totf_ref[0, i], axis=0, keepdims=True))
        sf = dec * sf + kvf_ref[0, i]
        ib = n - 1 - i
        sb_ref[0, ib] = sb.astype(BF16)
        dec = jnp.exp(jnp.sum(totb_ref[0, ib], axis=0, keepdims=True))
        sb = dec * sb + kvb_ref[0, ib]
    stf_scr[...] = sf
    stb_scr[...] = sb


def _gscan_call(kv_f, kv_b, tot_f, tot_b, s0f, s0b):
    b, nb = kv_f.shape[0], kv_f.shape[1]
    g = SCAN_GROUP
    nj = nb // g
    fwd = lambda i, j: (i, j, 0, 0)
    bwd = lambda i, j: (i, nj - 1 - j, 0, 0)
    kv_blk = (1, g, B_DV, B_KEY_WIDTH)
    tot_blk = (1, g, GLA_NC, B_KEY_WIDTH)
    st_spec = pl.BlockSpec((1, B_DV, B_KEY_WIDTH), lambda i, j: (i, 0, 0))
    s_shape = jax.ShapeDtypeStruct((b, nb, B_DV, B_KEY_WIDTH), BF16)
    return pl.pallas_call(
        _gscan_kernel,
        grid=(b, nj),
        in_specs=[pl.BlockSpec(kv_blk, fwd), pl.BlockSpec(kv_blk, bwd),
                  pl.BlockSpec(tot_blk, fwd), pl.BlockSpec(tot_blk, bwd), st_spec, st_spec],
        out_specs=[pl.BlockSpec(kv_blk, fwd), pl.BlockSpec(kv_blk, bwd)],
        out_shape=[s_shape, s_shape],
        scratch_shapes=[pltpu.VMEM((B_DV, B_KEY_WIDTH), F32), pltpu.VMEM((B_DV, B_KEY_WIDTH), F32)],
        compiler_params=pltpu.CompilerParams(
            dimension_semantics=("parallel", "arbitrary"), vmem_limit_bytes=VMEM_LIMIT_BYTES),
        name="gscan",
    )(kv_f, kv_b, tot_f, tot_b, s0f, s0b)


def _scale_rows(x16, scales):
    parts = []
    for c, s in enumerate(scales):
        xc = x16[c * B_CHUNK:(c + 1) * B_CHUNK, :]
        parts.append(xc if s is None else (xc.astype(F32) * s).astype(BF16))
    return parts[0] if len(parts) == 1 else jnp.concatenate(parts, axis=0)


def _exp_sum(tots, idx):
    return jnp.exp(_sum_rows([tots[m] for m in idx])) if idx else None


def _gla_kernel(qdf_ref, kdf_ref, kcf_ref, qdb_ref, kdb_ref, kcb_ref, v_ref, totf_ref, totb_ref,
                sf_ref, sb_ref, o_ref, p_scr):
    nc = GLA_NC
    ch = B_CHUNK
    qdf, kdf, kcf = qdf_ref[0], kdf_ref[0], kcf_ref[0]
    qdb, kdb, kcb = qdb_ref[0], kdb_ref[0], kcb_ref[0]
    tf = [totf_ref[0, 0, c:c + 1, :] for c in range(nc)]
    tb = [totb_ref[0, 0, c:c + 1, :] for c in range(nc)]
    hm = [_head_lane_mask(par, BF16) for par in range(2)]

    def lanes(h):
        return slice((h // 2) * LANES, (h // 2 + 1) * LANES)

    row = lax.broadcasted_iota(jnp.int32, (ch, 2 * ch), 0)
    col = lax.broadcasted_iota(jnp.int32, (ch, 2 * ch), 1)
    m_fe = (col < ch) & (col <= row)
    m_be = ((col < ch) & (col >= row)) | (col >= ch)
    m_fo = (col < ch) | (col - ch <= row)
    m_bo = (col >= ch) & (col - ch >= row)
    chunk_even = (lax.broadcasted_iota(jnp.int32, (GLA_BLOCK, 1), 0) // ch) % 2 == 0
    kmix_f = jnp.where(chunk_even, kcf, kdf)
    kmix_b = jnp.where(chunk_even, kdb, kcb)
    for par in range(2):
        kdf_m, kdb_m = kdf * hm[par], kdb * hm[par]
        kmf_m, kmb_m = kmix_f * hm[par], kmix_b * hm[par]
        for h in range(par, B_HEADS, 2):
            ls = lanes(h)
            for i in range(nc // 2):
                pr = slice(2 * i * ch, (2 * i + 2) * ch)
                re = slice(2 * i * ch, (2 * i + 1) * ch)
                ro = slice((2 * i + 1) * ch, (2 * i + 2) * ch)
                even = (jnp.where(m_fe, _dot_nt(qdf[re, ls], kdf_m[pr, ls]), 0.0)
                        + jnp.where(m_be, _dot_nt(qdb[re, ls], kmb_m[pr, ls]), 0.0))
                odd = (jnp.where(m_fo, _dot_nt(qdf[ro, ls], kmf_m[pr, ls]), 0.0)
                       + jnp.where(m_bo, _dot_nt(qdb[ro, ls], kdb_m[pr, ls]), 0.0))
                p_scr[h, re, pr] = even.astype(BF16)
                p_scr[h, ro, pr] = odd.astype(BF16)

    def cross(lo, hi):
        if hi - lo <= 2:
            return
        mid = (lo + hi) // 2
        cross(lo, mid)
        cross(mid, hi)
        rl = slice(lo * ch, mid * ch)
        rh = slice(mid * ch, hi * ch)
        qf = _scale_rows(qdf[rh], [_exp_sum(tf, range(mid, c)) for c in range(mid, hi)])
        kf = _scale_rows(kcf[rl], [_exp_sum(tf, range(c + 1, mid)) for c in range(lo, mid)])
        qb = _scale_rows(qdb[rl], [_exp_sum(tb, range(c + 1, mid)) for c in range(lo, mid)])
        kb = _scale_rows(kcb[rh], [_exp_sum(tb, range(mid, c)) for c in range(mid, hi)])
        for h in range(B_HEADS):
            ls = lanes(h)
            m = hm[h % 2]
            p_scr[h, rh, rl] = _dot_nt(qf[:, ls], (kf * m)[:, ls]).astype(BF16)
            p_scr[h, rl, rh] = _dot_nt(qb[:, ls], (kb * m)[:, ls]).astype(BF16)

    cross(0, nc)

    qsf = _scale_rows(qdf, [_exp_sum(tf, range(0, c)) for c in range(nc)])
    qsb = _scale_rows(qdb, [_exp_sum(tb, range(c + 1, nc)) for c in range(nc)])
    sf = sf_ref[0, 0]
    sb = sb_ref[0, 0]
    v16 = v_ref[0]
    for h in range(B_HEADS):
        ls = lanes(h)
        m = hm[h % 2]
        o_h = jnp.dot(p_scr[h], v16[:, h * B_DV:(h + 1) * B_DV], preferred_element_type=F32)
        o_h = o_h + _dot_nt(qsf[:, ls], (sf * m)[:, ls]) + _dot_nt(qsb[:, ls], (sb * m)[:, ls])
        o_ref[0, :, h * B_DV:(h + 1) * B_DV] = o_h


def _gla_call(qd_f, kd_f, kc_f, qd_b, kd_b, kc_b, v16, tot_f, tot_b, s_f, s_b):
    b, l, _ = v16.shape
    t = GLA_BLOCK
    tok = lambda width: pl.BlockSpec((1, t, width), lambda i, j: (i, j, 0))
    tot_spec = pl.BlockSpec((1, 1, GLA_NC, B_KEY_WIDTH), lambda i, j: (i, j, 0, 0))
    st_spec = pl.BlockSpec((1, 1, B_DV, B_KEY_WIDTH), lambda i, j: (i, j, 0, 0))
    return pl.pallas_call(
        _gla_kernel,
        grid=(b, l // t),
        in_specs=[tok(B_KEY_WIDTH)] * 6 + [tok(B_VAL_WIDTH), tot_spec, tot_spec, st_spec, st_spec],
        out_specs=tok(B_VAL_WIDTH),
        out_shape=jax.ShapeDtypeStruct((b, l, B_VAL_WIDTH), F32),
        scratch_shapes=[pltpu.VMEM((B_HEADS, t, t), BF16)],
        compiler_params=pltpu.CompilerParams(
            dimension_semantics=("parallel", "parallel"), vmem_limit_bytes=VMEM_LIMIT_BYTES),
        name="gla",
    )(qd_f, kd_f, kc_f, qd_b, kd_b, kc_b, v16, tot_f, tot_b, s_f, s_b)


def _colmix_kernel(ws_ref, bs_ref, vn_ref, o_ref):
    nw = vn_ref.shape[3]
    xs = jnp.concatenate([vn_ref[0, 0, :, w, :] for w in range(nw)], axis=1)
    y = _dot(ws_ref[0], xs) + bs_ref[0][:, 0:1]
    for w in range(nw):
        o_ref[0, 0, :, w, :] = y[:, w * A_GROUP_DIM:(w + 1) * A_GROUP_DIM]


def _colmix_call(ws_col, bs_col, vn_col, nw):
    b, g, rows, width, ch = vn_col.shape
    blk = pl.BlockSpec((1, 1, rows, nw, ch), lambda i, j, m: (i, j, 0, m, 0))
    return pl.pallas_call(
        _colmix_kernel,
        grid=(b, g, width // nw),
        in_specs=[pl.BlockSpec((1, rows, rows), lambda i, j, m: (j, 0, 0)),
                  pl.BlockSpec((1, rows, A_GROUP_DIM), lambda i, j, m: (j, 0, 0)),
                  blk],
        out_specs=blk,
        out_shape=jax.ShapeDtypeStruct(vn_col.shape, F32),
        compiler_params=pltpu.CompilerParams(
            dimension_semantics=("parallel", "parallel", "parallel"), vmem_limit_bytes=VMEM_LIMIT_BYTES),
        name="colmix",
    )(ws_col, bs_col, vn_col)


def _main_kernel(x_ref, mod_ref, ng_ref, w_ref, vr_ref, sc_ref, og_ref, ws_ref, bs_ref,
                 bg_ref, wpa_ref, wpb_ref, wo_ref, fg_ref, o_ref, acta_scr):
    x = x_ref[0]
    t = x.shape[0]
    h = _latent_h(x, mod_ref, ng_ref)
    p = _dot(h, w_ref[...])
    zb = p[:, 0:B_VAL_WIDTH]
    u = p[:, B_VAL_WIDTH:B_VAL_WIDTH + A_WIDTH]
    za = p[:, B_VAL_WIDTH + A_WIDTH:B_VAL_WIDTH + 2 * A_WIDTH]
    gates = jax.nn.sigmoid(p[:, B_VAL_WIDTH + 2 * A_WIDTH:])

    uz = u * _silu(za)
    for g in range(A_GROUPS):
        cs = slice(g * A_GROUP_DIM, (g + 1) * A_GROUP_DIM)
        if g < A_ROW_GROUPS:
            for c in range(t // A_CHUNK):
                rs = slice(c * A_CHUNK, (c + 1) * A_CHUNK)
                sv = _dot(ws_ref[g], vr_ref[0, rs, cs]) + bs_ref[g][:, 0:1]
                acta_scr[rs, cs] = (uz[rs, cs] * sv).astype(BF16)
        else:
            acta_scr[:, cs] = (uz[:, cs] * sc_ref[0, g - A_ROW_GROUPS]).astype(BF16)
    ya = jnp.dot(acta_scr[...], wpa_ref[...], preferred_element_type=F32)

    o = og_ref[0]
    on = jnp.concatenate(
        [_rms_rows(o[:, hd * B_DV:(hd + 1) * B_DV]) for hd in range(B_HEADS)], axis=1) * bg_ref[...]
    yb = _dot(on * _silu(zb), wpb_ref[...])

    m = gates[:, 0:D_MODEL] * ya + gates[:, D_MODEL:] * yb
    y = _dot(m, wo_ref[...])
    xo = x + mod_ref[0, 2:3, :] * y
    o_ref[0] = _rms_rows(xo) * fg_ref[...]


def _main_call(x, mod3, ng, w_main, vn_row, sv_col, o_gla, ws, bs, bg, wpa, wpb, wo, fg, t):
    b, l, _ = x.shape
    nw = w_main.shape[1]
    ncg = A_GROUPS - A_ROW_GROUPS
    const2 = lambda i, j: (0, 0)
    const3 = lambda i, j: (0, 0, 0)
    tok = lambda width: pl.BlockSpec((1, t, width), lambda i, j: (i, j, 0))
    return pl.pallas_call(
        _main_kernel,
        grid=(b, l // t),
        in_specs=[tok(D_MODEL),
                  pl.BlockSpec((1, 3, D_MODEL), lambda i, j: (i, 0, 0)),
                  pl.BlockSpec((1, D_MODEL), const2),
                  pl.BlockSpec((D_MODEL, nw), const2),
                  tok(A_ROW_GROUPS * A_GROUP_DIM),
                  pl.BlockSpec((1, ncg, t, A_GROUP_DIM), lambda i, j: (i, 0, j, 0)),
                  tok(B_VAL_WIDTH),
                  pl.BlockSpec((A_GROUPS, A_CHUNK, A_CHUNK), const3),
                  pl.BlockSpec((A_GROUPS, A_CHUNK, A_GROUP_DIM), const3),
                  pl.BlockSpec((1, B_VAL_WIDTH), const2),
                  pl.BlockSpec((A_WIDTH, D_MODEL), const2),
                  pl.BlockSpec((B_VAL_WIDTH, D_MODEL), const2),
                  pl.BlockSpec((D_MODEL, D_MODEL), const2),
                  pl.BlockSpec((1, D_MODEL), const2)],
        out_specs=tok(D_MODEL),
        out_shape=jax.ShapeDtypeStruct((b, l, D_MODEL), F32),
        scratch_shapes=[pltpu.VMEM((t, A_WIDTH), BF16)],
        compiler_params=pltpu.CompilerParams(
            dimension_semantics=("parallel", "parallel"), vmem_limit_bytes=VMEM_LIMIT_BYTES),
        name="main",
    )(x, mod3, ng, w_main, vn_row, sv_col, o_gla, ws, bs, bg, wpa, wpb, wo, fg)


def kernel(x, c, ctx, c_ctx, w_mod, b_mod, norm_g, w_in, a_ln_g, a_ln_b, a_ws, a_bs, b_gate_w2, b_gate_b,
           b_norm_g, w_proj_a, w_proj_b, w_out, final_norm_g):
    assert w_mod.shape[0] == 1, "single-layer block"
    b, l, _ = x.shape
    rows = l // GRID_W
    assert rows == A_CHUNK
    wm, bm, wi = w_mod[0], b_mod[0], w_in[0]
    ng = norm_g[0][None, :]

    cc = jnp.zeros((8, D_MODEL), F32).at[0:b].set(c).at[b].set(c_ctx)
    mod = _mod_call(cc, wm, bm[None, :])
    mod3 = mod[0:b].reshape(b, 3, D_MODEL)

    w_pre, w_main, wpa16, wpb16, wo16 = _wprep_call(w_in, w_proj_a, w_proj_b, w_out)
    w2, gb = b_gate_w2[0], b_gate_b[0]
    s0f, s0b = _ctx_call(ctx, mod, ng, w_pre, w2, gb, b)

    (qd_f, kd_f, kc_f, qd_b, kd_b, kc_b, v16, tot_f, tot_b, kv_f, kv_b, vn_row, vn_col) = _pre_call(
        x, mod3, ng, w_pre, w2, gb, a_ln_g[0][None, :], a_ln_b[0][None, :], 512)

    s_f, s_b = _gscan_call(kv_f, kv_b, tot_f, tot_b, s0f, s0b)
    o_gla = _gla_call(qd_f, kd_f, kc_f, qd_b, kd_b, kc_b, v16, tot_f, tot_b, s_f, s_b)

    ncg = A_GROUPS - A_ROW_GROUPS
    bs_b = jnp.broadcast_to(a_bs[0][:, :, None], (A_GROUPS, A_CHUNK, A_GROUP_DIM))
    sv_col = _colmix_call(a_ws[0][A_ROW_GROUPS:], bs_b[A_ROW_GROUPS:],
                          vn_col.reshape(b, ncg, rows, GRID_W, A_GROUP_DIM), 16)
    sv_col = sv_col.reshape(b, ncg, l, A_GROUP_DIM)

    return _main_call(x, mod3, ng, w_main, vn_row, sv_col, o_gla, a_ws[0], bs_b, b_norm_g[0][None, :],
                      wpa16, wpb16, wo16, final_norm_g[None, :], 512)
```

```python
import functools

import jax
import jax.numpy as jnp
from jax import lax
from jax.experimental import pallas as pl
from jax.experimental.pallas import tpu as pltpu

D_MODEL = 1024
GRID_W = 64
EPS = 1e-6

A_WIDTH = 512
A_GROUPS = 4
A_GROUP_DIM = 128
A_CHUNK = 128
A_ROW_GROUPS = 2

B_HEADS = 4
B_DK = 64
B_DV = 128
B_KEY_WIDTH = 256
B_VAL_WIDTH = 512
B_GATE_RANK = 16
B_GATE_TAU = 16.0
B_CHUNK = 64

Q0 = 0
K0 = Q0 + B_KEY_WIDTH
V0 = K0 + B_KEY_WIDTH
LR0 = V0 + B_VAL_WIDTH
ZB0 = LR0 + 2 * B_GATE_RANK
UA0 = ZB0 + B_VAL_WIDTH
VA0 = UA0 + A_WIDTH
ZA0 = VA0 + A_WIDTH
G0 = ZA0 + A_WIDTH
IN_WIDTH = G0 + 2 * D_MODEL

LANES = 128
GLA_BLOCK = 256
GLA_NC = GLA_BLOCK // B_CHUNK
SCAN_GROUP = 8

VMEM_LIMIT_BYTES = 56 * 1024 * 1024

BF16 = jnp.bfloat16
F32 = jnp.float32


def _dot(a, b):
    return jnp.dot(a.astype(BF16), b.astype(BF16), preferred_element_type=F32)


def _dot_nt(a, b):
    return lax.dot_general(a.astype(BF16), b.astype(BF16), (((1,), (1,)), ((), ())),
                           preferred_element_type=F32)


def _dot_tn(a, b):
    return lax.dot_general(a.astype(BF16), b.astype(BF16), (((0,), (0,)), ((), ())),
                           preferred_element_type=F32)


def _silu(x):
    return x * jax.nn.sigmoid(x)


def _rms_rows(x):
    return x * lax.rsqrt(jnp.mean(x * x, axis=-1, keepdims=True) + EPS)


def _head_lane_mask(parity, dtype):
    lane = lax.broadcasted_iota(jnp.int32, (1, B_KEY_WIDTH), 1)
    return (((lane // B_DK) % 2) == parity).astype(dtype)


def _mod_kernel(c_ref, w_ref, b_ref, o_ref):
    o_ref[...] = _dot(_silu(c_ref[...]), w_ref[...]) + b_ref[...]


def _mod_call(cc, wm, bm):
    n = wm.shape[1]
    tn = 1024
    return pl.pallas_call(
        _mod_kernel,
        grid=(n // tn,),
        in_specs=[pl.BlockSpec((8, D_MODEL), lambda j: (0, 0)),
                  pl.BlockSpec((D_MODEL, tn), lambda j: (0, j)),
                  pl.BlockSpec((1, tn), lambda j: (0, j))],
        out_specs=pl.BlockSpec((8, tn), lambda j: (0, j)),
        out_shape=jax.ShapeDtypeStruct((8, n), F32),
        name="mod",
    )(cc, wm, bm)


def _wprep_kernel(wt_ref, wpa_ref, wpb_ref, wo_ref, wpre_ref, wmain_ref, wpa16_ref, wpb16_ref, wo16_ref):
    def put(dst_ref, c0, r0, n):
        for s in range(0, n, 2 * LANES):
            m = min(2 * LANES, n - s)
            dst_ref[:, c0 + s:c0 + s + m] = wt_ref[r0 + s:r0 + s + m, :].T.astype(BF16)

    put(wpre_ref, 0, Q0, LR0)
    put(wpre_ref, LR0, VA0, A_WIDTH)
    lr = wt_ref[LR0:LR0 + LANES, :].T
    wpre_ref[:, LR0 + A_WIDTH:] = lr[:, 0:2 * B_GATE_RANK].astype(BF16)
    put(wmain_ref, 0, ZB0, VA0 - ZB0)
    put(wmain_ref, VA0 - ZB0, ZA0, IN_WIDTH - ZA0)
    wpa16_ref[...] = wpa_ref[0].astype(BF16)
    wpb16_ref[...] = wpb_ref[0].astype(BF16)
    wo16_ref[...] = wo_ref[0].astype(BF16)


def _wprep_call(w_in_t, w_proj_a, w_proj_b, w_out):
    steps = 4
    n_pre = ZB0 + A_WIDTH
    n_main = IN_WIDTH - n_pre

    def rows3(a):
        return pl.BlockSpec((1, a.shape[1] // steps, a.shape[2]), lambda i: (0, i, 0))

    def rows2(nrows, ncols):
        return (pl.BlockSpec((nrows // steps, ncols), lambda i: (i, 0)),
                jax.ShapeDtypeStruct((nrows, ncols), BF16))

    outs = [rows2(D_MODEL, n_pre), rows2(D_MODEL, n_main), rows2(A_WIDTH, D_MODEL),
            rows2(B_VAL_WIDTH, D_MODEL), rows2(D_MODEL, D_MODEL)]
    return pl.pallas_call(
        _wprep_kernel,
        grid=(steps,),
        in_specs=[pl.BlockSpec((IN_WIDTH, D_MODEL // steps), lambda i: (0, i)),
                  rows3(w_proj_a), rows3(w_proj_b), rows3(w_out)],
        out_specs=[o[0] for o in outs],
        out_shape=[o[1] for o in outs],
        compiler_params=pltpu.CompilerParams(
            dimension_semantics=("parallel",), vmem_limit_bytes=VMEM_LIMIT_BYTES),
        name="wprep",
    )(w_in_t, w_proj_a, w_proj_b, w_out)


def _gate_logs(lr, w2_ref, gb_ref):
    out = []
    for r in range(2):
        logits = _dot(lr[:, r * B_GATE_RANK:(r + 1) * B_GATE_RANK], w2_ref[r]) + gb_ref[r:r + 1, :]
        out.append(jax.nn.log_sigmoid(logits) / B_GATE_TAU)
    return out


def _chunk_tri(reverse):
    i = lax.broadcasted_iota(jnp.int32, (GLA_BLOCK, GLA_BLOCK), 0)
    j = lax.broadcasted_iota(jnp.int32, (GLA_BLOCK, GLA_BLOCK), 1)
    same = (i // B_CHUNK) == (j // B_CHUNK)
    tri = (j >= i) if reverse else (j <= i)
    return (same & tri).astype(BF16)


def _sum_rows(rows):
    acc = rows[0]
    for r in rows[1:]:
        acc = acc + r
    return acc


def _block_operands(q, k, a, tri, reverse):
    hi = a.astype(BF16)
    lo = (a - hi.astype(F32)).astype(BF16)
    cum = (jnp.dot(tri, hi, preferred_element_type=F32) + jnp.dot(tri, lo, preferred_element_type=F32))
    last = 0 if reverse else B_CHUNK - 1
    tots = [cum[c * B_CHUNK + last:c * B_CHUNK + last + 1, :] for c in range(GLA_NC)]
    totb = jnp.concatenate([jnp.broadcast_to(t, (B_CHUNK, B_KEY_WIDTH)) for t in tots], axis=0)
    kdec = k * jnp.exp(totb - cum)
    later = []
    for c in range(GLA_NC):
        idx = list(range(0, c)) if reverse else list(range(c + 1, GLA_NC))
        if idx:
            later.append(jnp.broadcast_to(jnp.exp(_sum_rows([tots[m] for m in idx])), (B_CHUNK, B_KEY_WIDTH)))
        else:
            later.append(jnp.ones((B_CHUNK, B_KEY_WIDTH), F32))
    kblk = kdec * jnp.concatenate(later, axis=0)
    if q is None:
        return None, None, kdec, tots, kblk
    qd = q * jnp.exp(cum)
    kd = k * jnp.exp(-cum)
    return qd, kd, kdec, tots, kblk


def _block_kv_t(v16, kblk):
    k16 = kblk.astype(BF16)
    lane = lax.broadcasted_iota(jnp.int32, (1, B_KEY_WIDTH), 1)
    acc = jnp.zeros((B_DV, B_KEY_WIDTH), F32)
    for h in range(B_HEADS):
        full = _dot_tn(v16[:, h * B_DV:(h + 1) * B_DV], k16)
        acc = acc + jnp.where((lane // B_DK) == h, full, 0.0)
    return acc


def _ctx_kernel(ctx_ref, mod_ref, ng_ref, w_ref, w2_ref, gb_ref, sf_ref, sb_ref, *, ctx_row):
    xc = ctx_ref[0]
    shift = mod_ref[ctx_row:ctx_row + 1, 0:D_MODEL]
    scale = mod_ref[ctx_row:ctx_row + 1, D_MODEL:2 * D_MODEL]
    hc = _rms_rows(xc) * ng_ref[...] * (1.0 + scale) + shift
    pc = _dot(hc, w_ref[...])
    k = pc[:, K0:V0]
    v16 = pc[:, V0:LR0].astype(BF16)
    lr = pc[:, LR0 + A_WIDTH:]
    a_f, a_b = _gate_logs(lr, w2_ref, gb_ref)
    for a, reverse, out_ref in ((a_f, False, sf_ref), (a_b, True, sb_ref)):
        _, _, _, _, kblk = _block_operands(None, k, a, _chunk_tri(reverse), reverse)
        out_ref[0] = _block_kv_t(v16, kblk)


def _ctx_call(ctx, mod, ng, w_kvl, w2, gb, ctx_row):
    b, lc, _ = ctx.shape
    assert lc == GLA_BLOCK, "context length must be one GLA block"
    nw = w_kvl.shape[1]
    st = jax.ShapeDtypeStruct((b, B_DV, B_KEY_WIDTH), F32)
    st_spec = pl.BlockSpec((1, B_DV, B_KEY_WIDTH), lambda i: (i, 0, 0))
    return pl.pallas_call(
        functools.partial(_ctx_kernel, ctx_row=ctx_row),
        grid=(b,),
        in_specs=[pl.BlockSpec((1, lc, D_MODEL), lambda i: (i, 0, 0)),
                  pl.BlockSpec((8, 3 * D_MODEL), lambda i: (0, 0)),
                  pl.BlockSpec((1, D_MODEL), lambda i: (0, 0)),
                  pl.BlockSpec((D_MODEL, nw), lambda i: (0, 0)),
                  pl.BlockSpec((2, B_GATE_RANK, B_KEY_WIDTH), lambda i: (0, 0, 0)),
                  pl.BlockSpec((2, B_KEY_WIDTH), lambda i: (0, 0))],
        out_specs=[st_spec, st_spec],
        out_shape=[st, st],
        name="ctx",
    )(ctx, mod, ng, w_kvl, w2, gb)


def _latent_h(x, mod_ref, ng_ref):
    shift = mod_ref[0, 0:1, :]
    scale = mod_ref[0, 1:2, :]
    return _rms_rows(x) * ng_ref[...] * (1.0 + scale) + shift


def _pre_kernel(x_ref, mod_ref, ng_ref, w_ref, w2_ref, gb_ref, lg_ref, lb_ref,
                qdf_ref, kdf_ref, kcf_ref, qdb_ref, kdb_ref, kcb_ref, v_ref,
                totf_ref, totb_ref, kvf_ref, kvb_ref, vr_ref, vc_ref):
    h = _latent_h(x_ref[0], mod_ref, ng_ref)
    p = _dot(h, w_ref[...])
    q = p[:, Q0:K0] * (B_DK ** -0.5)
    k = p[:, K0:V0]
    v16 = p[:, V0:LR0].astype(BF16)
    v_ref[0] = v16
    a_f, a_b = _gate_logs(p[:, LR0 + A_WIDTH:], w2_ref, gb_ref)
    nblk = x_ref.shape[1] // GLA_BLOCK
    for a, reverse, qd_ref, kd_ref, kc_ref, tot_ref, kv_ref in (
            (a_f, False, qdf_ref, kdf_ref, kcf_ref, totf_ref, kvf_ref),
            (a_b, True, qdb_ref, kdb_ref, kcb_ref, totb_ref, kvb_ref)):
        tri = _chunk_tri(reverse)
        for blk in range(nblk):
            rs = slice(blk * GLA_BLOCK, (blk + 1) * GLA_BLOCK)
            qd, kd, kdec, tots, kblk = _block_operands(q[rs], k[rs], a[rs], tri, reverse)
            qd_ref[0, rs, :] = qd.astype(BF16)
            kd_ref[0, rs, :] = kd.astype(BF16)
            kc_ref[0, rs, :] = kdec.astype(BF16)
            for c in range(GLA_NC):
                tot_ref[0, blk, c:c + 1, :] = tots[c]
            kv_ref[0, blk] = _block_kv_t(v16[rs], kblk)

    va = p[:, LR0:LR0 + A_WIDTH]
    vc = va - jnp.mean(va, axis=-1, keepdims=True)
    vn = vc * lax.rsqrt(jnp.mean(vc * vc, axis=-1, keepdims=True) + EPS) * lg_ref[...] + lb_ref[...]
    nr = A_ROW_GROUPS * A_GROUP_DIM
    vr_ref[0] = vn[:, 0:nr].astype(BF16)
    for g in range(A_GROUPS - A_ROW_GROUPS):
        vg = vn[:, nr + g * A_GROUP_DIM:nr + (g + 1) * A_GROUP_DIM]
        vc_ref[0, g] = jnp.swapaxes(vg.reshape(vg.shape[0] // GRID_W, GRID_W, A_GROUP_DIM), 0, 1)


def _pre_call(x, mod3, ng, w_pre, w2, gb, lg, lb, t):
    b, l, _ = x.shape
    nw = w_pre.shape[1]
    ncg = A_GROUPS - A_ROW_GROUPS
    nblk = t // GLA_BLOCK

    def tok(width, dtype):
        return (pl.BlockSpec((1, t, width), lambda i, j: (i, j, 0)),
                jax.ShapeDtypeStruct((b, l, width), dtype))

    tot = (pl.BlockSpec((1, nblk, GLA_NC, B_KEY_WIDTH), lambda i, j: (i, j, 0, 0)),
           jax.ShapeDtypeStruct((b, l // GLA_BLOCK, GLA_NC, B_KEY_WIDTH), F32))
    kv = (pl.BlockSpec((1, nblk, B_DV, B_KEY_WIDTH), lambda i, j: (i, j, 0, 0)),
          jax.ShapeDtypeStruct((b, l // GLA_BLOCK, B_DV, B_KEY_WIDTH), F32))
    outs = [tok(B_KEY_WIDTH, BF16)] * 6 + [tok(B_VAL_WIDTH, BF16), tot, tot, kv, kv,
                                           tok(A_ROW_GROUPS * A_GROUP_DIM, BF16),
                                           (pl.BlockSpec((1, ncg, GRID_W, t // GRID_W, A_GROUP_DIM),
                                                         lambda i, j: (i, 0, 0, j, 0)),
                                            jax.ShapeDtypeStruct((b, ncg, GRID_W, l // GRID_W, A_GROUP_DIM), F32))]
    const2 = lambda i, j: (0, 0)
    return pl.pallas_call(
        _pre_kernel,
        grid=(b, l // t),
        in_specs=[pl.BlockSpec((1, t, D_MODEL), lambda i, j: (i, j, 0)),
                  pl.BlockSpec((1, 3, D_MODEL), lambda i, j: (i, 0, 0)),
                  pl.BlockSpec((1, D_MODEL), const2),
                  pl.BlockSpec((D_MODEL, nw), const2),
                  pl.BlockSpec((2, B_GATE_RANK, B_KEY_WIDTH), lambda i, j: (0, 0, 0)),
                  pl.BlockSpec((2, B_KEY_WIDTH), const2),
                  pl.BlockSpec((1, A_WIDTH), const2),
                  pl.BlockSpec((1, A_WIDTH), const2)],
        out_specs=[o[0] for o in outs],
        out_shape=[o[1] for o in outs],
        compiler_params=pltpu.CompilerParams(
            dimension_semantics=("parallel", "parallel"), vmem_limit_bytes=VMEM_LIMIT_BYTES),
        name="pre",
    )(x, mod3, ng, w_pre, w2, gb, lg, lb)


def _gscan_kernel(kvf_ref, kvb_ref, totf_ref, totb_ref, s0f_ref, s0b_ref, sf_ref, sb_ref, stf_scr, stb_scr):
    @pl.when(pl.program_id(1) == 0)
    def _():
        stf_scr[...] = s0f_ref[0]
        stb_scr[...] = s0b_ref[0]

    n = kvf_ref.shape[1]
    sf = stf_scr[...]
    sb = stb_scr[...]
    for i in range(n):
        sf_ref[0, i] = sf.astype(BF16)
        dec = jnp.exp(jnp.sum(totf_ref[0, i], axis=0, keepdims=True))
        sf = dec * sf + kvf_ref[0, i]
        ib = n - 1 - i
        sb_ref[0, ib] = sb.astype(BF16)
        dec = jnp.exp(jnp.sum(totb_ref[0, ib], axis=0, keepdims=True))
        sb = dec * sb + kvb_ref[0, ib]
    stf_scr[...] = sf
    stb_scr[...] = sb


def _gscan_call(kv_f, kv_b, tot_f, tot_b, s0f, s0b):
    b, nb = kv_f.shape[0], kv_f.shape[1]
    g = SCAN_GROUP
    nj = nb // g
    fwd = lambda i, j: (i, j, 0, 0)
    bwd = lambda i, j: (i, nj - 1 - j, 0, 0)
    kv_blk = (1, g, B_DV, B_KEY_WIDTH)
    tot_blk = (1, g, GLA_NC, B_KEY_WIDTH)
    st_spec = pl.BlockSpec((1, B_DV, B_KEY_WIDTH), lambda i, j: (i, 0, 0))
    s_shape = jax.ShapeDtypeStruct((b, nb, B_DV, B_KEY_WIDTH), BF16)
    return pl.pallas_call(
        _gscan_kernel,
        grid=(b, nj),
        in_specs=[pl.BlockSpec(kv_blk, fwd), pl.BlockSpec(kv_blk, bwd),
                  pl.BlockSpec(tot_blk, fwd), pl.BlockSpec(tot_blk, bwd), st_spec, st_spec],
        out_specs=[pl.BlockSpec(kv_blk, fwd), pl.BlockSpec(kv_blk, bwd)],
        out_shape=[s_shape, s_shape],
        scratch_shapes=[pltpu.VMEM((B_DV, B_KEY_WIDTH), F32), pltpu.VMEM((B_DV, B_KEY_WIDTH), F32)],
        compiler_params=pltpu.CompilerParams(
            dimension_semantics=("parallel", "arbitrary"), vmem_limit_bytes=VMEM_LIMIT_BYTES),
        name="gscan",
    )(kv_f, kv_b, tot_f, tot_b, s0f, s0b)


def _scale_rows(x16, scales):
    parts = []
    for c, s in enumerate(scales):
        xc = x16[c * B_CHUNK:(c + 1) * B_CHUNK, :]
        parts.append(xc if s is None else (xc.astype(F32) * s).astype(BF16))
    return parts[0] if len(parts) == 1 else jnp.concatenate(parts, axis=0)


def _exp_sum(tots, idx):
    return jnp.exp(_sum_rows([tots[m] for m in idx])) if idx else None


def _gla_kernel(qdf_ref, kdf_ref, kcf_ref, qdb_ref, kdb_ref, kcb_ref, v_ref, totf_ref, totb_ref,
                sf_ref, sb_ref, o_ref, p_scr):
    nc = GLA_NC
    ch = B_CHUNK
    qdf, kdf, kcf = qdf_ref[0], kdf_ref[0], kcf_ref[0]
    qdb, kdb, kcb = qdb_ref[0], kdb_ref[0], kcb_ref[0]
    tf = [totf_ref[0, 0, c:c + 1, :] for c in range(nc)]
    tb = [totb_ref[0, 0, c:c + 1, :] for c in range(nc)]
    hm = [_head_lane_mask(par, BF16) for par in range(2)]

    def lanes(h):
        return slice((h // 2) * LANES, (h // 2 + 1) * LANES)

    row = lax.broadcasted_iota(jnp.int32, (ch, 2 * ch), 0)
    col = lax.broadcasted_iota(jnp.int32, (ch, 2 * ch), 1)
    m_fe = (col < ch) & (col <= row)
    m_be = ((col < ch) & (col >= row)) | (col >= ch)
    m_fo = (col < ch) | (col - ch <= row)
    m_bo = (col >= ch) & (col - ch >= row)
    chunk_even = (lax.broadcasted_iota(jnp.int32, (GLA_BLOCK, 1), 0) // ch) % 2 == 0
    kmix_f = jnp.where(chunk_even, kcf, kdf)
    kmix_b = jnp.where(chunk_even, kdb, kcb)
    for par in range(2):
        kdf_m, kdb_m = kdf * hm[par], kdb * hm[par]
        kmf_m, kmb_m = kmix_f * hm[par], kmix_b * hm[par]
        for h in range(par, B_HEADS, 2):
            ls = lanes(h)
            for i in range(nc // 2):
                pr = slice(2 * i * ch, (2 * i + 2) * ch)
                re = slice(2 * i * ch, (2 * i + 1) * ch)
                ro = slice((2 * i + 1) * ch, (2 * i + 2) * ch)
                even = (jnp.where(m_fe, _dot_nt(qdf[re, ls], kdf_m[pr, ls]), 0.0)
                        + jnp.where(m_be, _dot_nt(qdb[re, ls], kmb_m[pr, ls]), 0.0))
                odd = (jnp.where(m_fo, _dot_nt(qdf[ro, ls], kmf_m[pr, ls]), 0.0)
                       + jnp.where(m_bo, _dot_nt(qdb[ro, ls], kdb_m[pr, ls]), 0.0))
                p_scr[h, re, pr] = even.astype(BF16)
                p_scr[h, ro, pr] = odd.astype(BF16)

    def cross(lo, hi):
        if hi - lo <= 2:
            return
        mid = (lo + hi) // 2
        cross(lo, mid)
        cross(mid, hi)
        rl = slice(lo * ch, mid * ch)
        rh = slice(mid * ch, hi * ch)
        qf = _scale_rows(qdf[rh], [_exp_sum(tf, range(mid, c)) for c in range(mid, hi)])
        kf = _scale_rows(kcf[rl], [_exp_sum(tf, range(c + 1, mid)) for c in range(lo, mid)])
        qb = _scale_rows(qdb[rl], [_exp_sum(tb, range(c + 1, mid)) for c in range(lo, mid)])
        kb = _scale_rows(kcb[rh], [_exp_sum(tb, range(mid, c)) for c in range(mid, hi)])
        for h in range(B_HEADS):
            ls = lanes(h)
            m = hm[h % 2]
            p_scr[h, rh, rl] = _dot_nt(qf[:, ls], (kf * m)[:, ls]).astype(BF16)
            p_scr[h, rl, rh] = _dot_nt(qb[:, ls], (kb * m)[:, ls]).astype(BF16)

    cross(0, nc)

    qsf = _scale_rows(qdf, [_exp_sum(tf, range(0, c)) for c in range(nc)])
    qsb = _scale_rows(qdb, [_exp_sum(tb, range(c + 1, nc)) for c in range(nc)])
    sf = sf_ref[0, 0]
    sb = sb_ref[0, 0]
    v16 = v_ref[0]
    for h in range(B_HEADS):
        ls = lanes(h)
        m = hm[h % 2]
        o_h = jnp.dot(p_scr[h], v16[:, h * B_DV:(h + 1) * B_DV], preferred_element_type=F32)
        o_h = o_h + _dot_nt(qsf[:, ls], (sf * m)[:, ls]) + _dot_nt(qsb[:, ls], (sb * m)[:, ls])
        o_ref[0, :, h * B_DV:(h + 1) * B_DV] = o_h


def _gla_call(qd_f, kd_f, kc_f, qd_b, kd_b, kc_b, v16, tot_f, tot_b, s_f, s_b):
    b, l, _ = v16.shape
    t = GLA_BLOCK
    tok = lambda width: pl.BlockSpec((1, t, width), lambda i, j: (i, j, 0))
    tot_spec = pl.BlockSpec((1, 1, GLA_NC, B_KEY_WIDTH), lambda i, j: (i, j, 0, 0))
    st_spec = pl.BlockSpec((1, 1, B_DV, B_KEY_WIDTH), lambda i, j: (i, j, 0, 0))
    return pl.pallas_call(
        _gla_kernel,
        grid=(b, l // t),
        in_specs=[tok(B_KEY_WIDTH)] * 6 + [tok(B_VAL_WIDTH), tot_spec, tot_spec, st_spec, st_spec],
        out_specs=tok(B_VAL_WIDTH),
        out_shape=jax.ShapeDtypeStruct((b, l, B_VAL_WIDTH), F32),
        scratch_shapes=[pltpu.VMEM((B_HEADS, t, t), BF16)],
        compiler_params=pltpu.CompilerParams(
            dimension_semantics=("parallel", "parallel"), vmem_limit_bytes=VMEM_LIMIT_BYTES),
        name="gla",
    )(qd_f, kd_f, kc_f, qd_b, kd_b, kc_b, v16, tot_f, tot_b, s_f, s_b)


def _colmix_kernel(ws_ref, bs_ref, vn_ref, o_ref):
    nw = vn_ref.shape[2]
    xs = jnp.concatenate([vn_ref[0, 0, w] for w in range(nw)], axis=1)
    y = _dot(ws_ref[0], xs) + bs_ref[0][:, 0:1]
    for w in range(nw):
        o_ref[0, 0, w] = y[:, w * A_GROUP_DIM:(w + 1) * A_GROUP_DIM]


def _colmix_call(ws_col, bs_col, vn_col, nw):
    b, g, width, rows, ch = vn_col.shape
    blk = pl.BlockSpec((1, 1, nw, rows, ch), lambda i, j, m: (i, j, m, 0, 0))
    return pl.pallas_call(
        _colmix_kernel,
        grid=(b, g, width // nw),
        in_specs=[pl.BlockSpec((1, rows, rows), lambda i, j, m: (j, 0, 0)),
                  pl.BlockSpec((1, rows, A_GROUP_DIM), lambda i, j, m: (j, 0, 0)),
                  blk],
        out_specs=blk,
        out_shape=jax.ShapeDtypeStruct(vn_col.shape, F32),
        compiler_params=pltpu.CompilerParams(
            dimension_semantics=("parallel", "parallel", "parallel"), vmem_limit_bytes=VMEM_LIMIT_BYTES),
        name="colmix",
    )(ws_col, bs_col, vn_col)


def _main_kernel(x_ref, mod_ref, ng_ref, w_ref, vr_ref, sc_ref, og_ref, ws_ref, bs_ref,
                 bg_ref, wpa_ref, wpb_ref, wo_ref, fg_ref, o_ref, acta_scr):
    x = x_ref[0]
    t = x.shape[0]
    h = _latent_h(x, mod_ref, ng_ref)
    p = _dot(h, w_ref[...])
    zb = p[:, 0:B_VAL_WIDTH]
    u = p[:, B_VAL_WIDTH:B_VAL_WIDTH + A_WIDTH]
    za = p[:, B_VAL_WIDTH + A_WIDTH:B_VAL_WIDTH + 2 * A_WIDTH]
    gates = jax.nn.sigmoid(p[:, B_VAL_WIDTH + 2 * A_WIDTH:])

    uz = u * _silu(za)
    for g in range(A_GROUPS):
        cs = slice(g * A_GROUP_DIM, (g + 1) * A_GROUP_DIM)
        if g < A_ROW_GROUPS:
            for c in range(t // A_CHUNK):
                rs = slice(c * A_CHUNK, (c + 1) * A_CHUNK)
                sv = _dot(ws_ref[g], vr_ref[0, rs, cs]) + bs_ref[g][:, 0:1]
                acta_scr[rs, cs] = (uz[rs, cs] * sv).astype(BF16)
        else:
            sv = jnp.swapaxes(sc_ref[0, g - A_ROW_GROUPS], 0, 1).reshape(t, A_GROUP_DIM)
            acta_scr[:, cs] = (uz[:, cs] * sv).astype(BF16)
    ya = jnp.dot(acta_scr[...], wpa_ref[...], preferred_element_type=F32)

    o = og_ref[0]
    on = jnp.concatenate(
        [_rms_rows(o[:, hd * B_DV:(hd + 1) * B_DV]) for hd in range(B_HEADS)], axis=1) * bg_ref[...]
    yb = _dot(on * _silu(zb), wpb_ref[...])

    m = gates[:, 0:D_MODEL] * ya + gates[:, D_MODEL:] * yb
    y = _dot(m, wo_ref[...])
    xo = x + mod_ref[0, 2:3, :] * y
    o_ref[0] = _rms_rows(xo) * fg_ref[...]


def _main_call(x, mod3, ng, w_main, vn_row, sv_col, o_gla, ws, bs, bg, wpa, wpb, wo, fg, t):
    b, l, _ = x.shape
    nw = w_main.shape[1]
    ncg = A_GROUPS - A_ROW_GROUPS
    const2 = lambda i, j: (0, 0)
    const3 = lambda i, j: (0, 0, 0)
    tok = lambda width: pl.BlockSpec((1, t, width), lambda i, j: (i, j, 0))
    return pl.pallas_call(
        _main_kernel,
        grid=(b, l // t),
        in_specs=[tok(D_MODEL),
                  pl.BlockSpec((1, 3, D_MODEL), lambda i, j: (i, 0, 0)),
                  pl.BlockSpec((1, D_MODEL), const2),
                  pl.BlockSpec((D_MODEL, nw), const2),
                  tok(A_ROW_GROUPS * A_GROUP_DIM),
                  pl.BlockSpec((1, ncg, GRID_W, t // GRID_W, A_GROUP_DIM), lambda i, j: (i, 0, 0, j, 0)),
                  tok(B_VAL_WIDTH),
                  pl.BlockSpec((A_GROUPS, A_CHUNK, A_CHUNK), const3),
                  pl.BlockSpec((A_GROUPS, A_CHUNK, A_GROUP_DIM), const3),
                  pl.BlockSpec((1, B_VAL_WIDTH), const2),
                  pl.BlockSpec((A_WIDTH, D_MODEL), const2),
                  pl.BlockSpec((B_VAL_WIDTH, D_MODEL), const2),
                  pl.BlockSpec((D_MODEL, D_MODEL), const2),
                  pl.BlockSpec((1, D_MODEL), const2)],
        out_specs=tok(D_MODEL),
        out_shape=jax.ShapeDtypeStruct((b, l, D_MODEL), F32),
        scratch_shapes=[pltpu.VMEM((t, A_WIDTH), BF16)],
        compiler_params=pltpu.CompilerParams(
            dimension_semantics=("parallel", "parallel"), vmem_limit_bytes=VMEM_LIMIT_BYTES),
        name="main",
    )(x, mod3, ng, w_main, vn_row, sv_col, o_gla, ws, bs, bg, wpa, wpb, wo, fg)


def kernel(x, c, ctx, c_ctx, w_mod, b_mod, norm_g, w_in, a_ln_g, a_ln_b, a_ws, a_bs, b_gate_w2, b_gate_b,
           b_norm_g, w_proj_a, w_proj_b, w_out, final_norm_g):
    assert w_mod.shape[0] == 1, "single-layer block"
    b, l, _ = x.shape
    rows = l // GRID_W
    assert rows == A_CHUNK
    wm, bm, wi = w_mod[0], b_mod[0], w_in[0]
    ng = norm_g[0][None, :]

    cc = jnp.zeros((8, D_MODEL), F32).at[0:b].set(c).at[b].set(c_ctx)
    mod = _mod_call(cc, wm, bm[None, :])
    mod3 = mod[0:b].reshape(b, 3, D_MODEL)

    w_pre, w_main, wpa16, wpb16, wo16 = _wprep_call(w_in[0].T, w_proj_a, w_proj_b, w_out)
    w2, gb = b_gate_w2[0], b_gate_b[0]
    s0f, s0b = _ctx_call(ctx, mod, ng, w_pre, w2, gb, b)

    (qd_f, kd_f, kc_f, qd_b, kd_b, kc_b, v16, tot_f, tot_b, kv_f, kv_b, vn_row, vn_col) = _pre_call(
        x, mod3, ng, w_pre, w2, gb, a_ln_g[0][None, :], a_ln_b[0][None, :], 512)

    s_f, s_b = _gscan_call(kv_f, kv_b, tot_f, tot_b, s0f, s0b)
    o_gla = _gla_call(qd_f, kd_f, kc_f, qd_b, kd_b, kc_b, v16, tot_f, tot_b, s_f, s_b)

    ncg = A_GROUPS - A_ROW_GROUPS
    bs_b = jnp.broadcast_to(a_bs[0][:, :, None], (A_GROUPS, A_CHUNK, A_GROUP_DIM))
    sv_col = _colmix_call(a_ws[0][A_ROW_GROUPS:], bs_b[A_ROW_GROUPS:], vn_col, 16)

    return _main_call(x, mod3, ng, w_main, vn_row, sv_col, o_gla, a_ws[0], bs_b, b_norm_g[0][None, :],
                      wpa16, wpb16, wo16, final_norm_g[None, :], 512)
```

```python
import functools

import jax
import jax.numpy as jnp
from jax import lax
from jax.experimental import pallas as pl
from jax.experimental.pallas import tpu as pltpu

D_MODEL = 1024
GRID_W = 64
EPS = 1e-6

A_WIDTH = 512
A_GROUPS = 4
A_GROUP_DIM = 128
A_CHUNK = 128
A_ROW_GROUPS = 2

B_HEADS = 4
B_DK = 64
B_DV = 128
B_KEY_WIDTH = 256
B_VAL_WIDTH = 512
B_GATE_RANK = 16
B_GATE_TAU = 16.0
B_CHUNK = 64

Q0 = 0
K0 = Q0 + B_KEY_WIDTH
V0 = K0 + B_KEY_WIDTH
LR0 = V0 + B_VAL_WIDTH
ZB0 = LR0 + 2 * B_GATE_RANK
UA0 = ZB0 + B_VAL_WIDTH
VA0 = UA0 + A_WIDTH
ZA0 = VA0 + A_WIDTH
G0 = ZA0 + A_WIDTH
IN_WIDTH = G0 + 2 * D_MODEL

LANES = 128
GLA_BLOCK = 256
GLA_NC = GLA_BLOCK // B_CHUNK
SCAN_GROUP = 8

VMEM_LIMIT_BYTES = 56 * 1024 * 1024

BF16 = jnp.bfloat16
F32 = jnp.float32


def _dot(a, b):
    return jnp.dot(a.astype(BF16), b.astype(BF16), preferred_element_type=F32)


def _dot_nt(a, b):
    return lax.dot_general(a.astype(BF16), b.astype(BF16), (((1,), (1,)), ((), ())),
                           preferred_element_type=F32)


def _dot_tn(a, b):
    return lax.dot_general(a.astype(BF16), b.astype(BF16), (((0,), (0,)), ((), ())),
                           preferred_element_type=F32)


def _silu(x):
    return x * jax.nn.sigmoid(x)


def _rms_rows(x):
    return x * lax.rsqrt(jnp.mean(x * x, axis=-1, keepdims=True) + EPS)


def _head_lane_mask(parity, dtype):
    lane = lax.broadcasted_iota(jnp.int32, (1, B_KEY_WIDTH), 1)
    return (((lane // B_DK) % 2) == parity).astype(dtype)


def _mod_kernel(c_ref, w_ref, b_ref, o_ref):
    o_ref[...] = _dot(_silu(c_ref[...]), w_ref[...]) + b_ref[...]


def _mod_call(cc, wm, bm):
    n = wm.shape[1]
    tn = 1024
    return pl.pallas_call(
        _mod_kernel,
        grid=(n // tn,),
        in_specs=[pl.BlockSpec((8, D_MODEL), lambda j: (0, 0)),
                  pl.BlockSpec((D_MODEL, tn), lambda j: (0, j)),
                  pl.BlockSpec((1, tn), lambda j: (0, j))],
        out_specs=pl.BlockSpec((8, tn), lambda j: (0, j)),
        out_shape=jax.ShapeDtypeStruct((8, n), F32),
        name="mod",
    )(cc, wm, bm)


def _wprep_kernel(wt_ref, wpa_ref, wpb_ref, wo_ref, wpre_ref, wmain_ref, wpa16_ref, wpb16_ref, wo16_ref):
    def put(dst_ref, c0, r0, n):
        for s in range(0, n, 2 * LANES):
            m = min(2 * LANES, n - s)
            dst_ref[:, c0 + s:c0 + s + m] = wt_ref[r0 + s:r0 + s + m, :].T.astype(BF16)

    put(wpre_ref, 0, Q0, LR0)
    put(wpre_ref, LR0, VA0, A_WIDTH)
    lr = wt_ref[LR0:LR0 + LANES, :].T
    wpre_ref[:, LR0 + A_WIDTH:] = lr[:, 0:2 * B_GATE_RANK].astype(BF16)
    put(wmain_ref, 0, ZB0, VA0 - ZB0)
    put(wmain_ref, VA0 - ZB0, ZA0, IN_WIDTH - ZA0)
    wpa16_ref[...] = wpa_ref[0].astype(BF16)
    wpb16_ref[...] = wpb_ref[0].astype(BF16)
    wo16_ref[...] = wo_ref[0].astype(BF16)


def _wprep_call(w_in_t, w_proj_a, w_proj_b, w_out):
    steps = 4
    n_pre = ZB0 + A_WIDTH
    n_main = IN_WIDTH - n_pre

    def rows3(a):
        return pl.BlockSpec((1, a.shape[1] // steps, a.shape[2]), lambda i: (0, i, 0))

    def rows2(nrows, ncols):
        return (pl.BlockSpec((nrows // steps, ncols), lambda i: (i, 0)),
                jax.ShapeDtypeStruct((nrows, ncols), BF16))

    outs = [rows2(D_MODEL, n_pre), rows2(D_MODEL, n_main), rows2(A_WIDTH, D_MODEL),
            rows2(B_VAL_WIDTH, D_MODEL), rows2(D_MODEL, D_MODEL)]
    return pl.pallas_call(
        _wprep_kernel,
        grid=(steps,),
        in_specs=[pl.BlockSpec((IN_WIDTH, D_MODEL // steps), lambda i: (0, i)),
                  rows3(w_proj_a), rows3(w_proj_b), rows3(w_out)],
        out_specs=[o[0] for o in outs],
        out_shape=[o[1] for o in outs],
        compiler_params=pltpu.CompilerParams(
            dimension_semantics=("parallel",), vmem_limit_bytes=VMEM_LIMIT_BYTES),
        name="wprep",
    )(w_in_t, w_proj_a, w_proj_b, w_out)


def _gate_logs(lr, w2_ref, gb_ref):
    out = []
    for r in range(2):
        logits = _dot(lr[:, r * B_GATE_RANK:(r + 1) * B_GATE_RANK], w2_ref[r]) + gb_ref[r:r + 1, :]
        out.append(jax.nn.log_sigmoid(logits) / B_GATE_TAU)
    return out


def _chunk_tri(reverse):
    i = lax.broadcasted_iota(jnp.int32, (GLA_BLOCK, GLA_BLOCK), 0)
    j = lax.broadcasted_iota(jnp.int32, (GLA_BLOCK, GLA_BLOCK), 1)
    same = (i // B_CHUNK) == (j // B_CHUNK)
    tri = (j >= i) if reverse else (j <= i)
    return (same & tri).astype(BF16)


def _sum_rows(rows):
    acc = rows[0]
    for r in rows[1:]:
        acc = acc + r
    return acc


def _block_operands(q, k, a, tri, reverse):
    hi = a.astype(BF16)
    lo = (a - hi.astype(F32)).astype(BF16)
    cum = (jnp.dot(tri, hi, preferred_element_type=F32) + jnp.dot(tri, lo, preferred_element_type=F32))
    last = 0 if reverse else B_CHUNK - 1
    tots = [cum[c * B_CHUNK + last:c * B_CHUNK + last + 1, :] for c in range(GLA_NC)]
    totb = jnp.concatenate([jnp.broadcast_to(t, (B_CHUNK, B_KEY_WIDTH)) for t in tots], axis=0)
    kdec = k * jnp.exp(totb - cum)
    later = []
    for c in range(GLA_NC):
        idx = list(range(0, c)) if reverse else list(range(c + 1, GLA_NC))
        if idx:
            later.append(jnp.broadcast_to(jnp.exp(_sum_rows([tots[m] for m in idx])), (B_CHUNK, B_KEY_WIDTH)))
        else:
            later.append(jnp.ones((B_CHUNK, B_KEY_WIDTH), F32))
    kblk = kdec * jnp.concatenate(later, axis=0)
    if q is None:
        return None, None, kdec, tots, kblk
    qd = q * jnp.exp(cum)
    kd = k * jnp.exp(-cum)
    return qd, kd, kdec, tots, kblk


def _block_kv_t(v16, kblk):
    k16 = kblk.astype(BF16)
    lane = lax.broadcasted_iota(jnp.int32, (1, B_KEY_WIDTH), 1)
    acc = jnp.zeros((B_DV, B_KEY_WIDTH), F32)
    for h in range(B_HEADS):
        full = _dot_tn(v16[:, h * B_DV:(h + 1) * B_DV], k16)
        acc = acc + jnp.where((lane // B_DK) == h, full, 0.0)
    return acc


def _ctx_kernel(ctx_ref, mod_ref, ng_ref, w_ref, w2_ref, gb_ref, sf_ref, sb_ref, *, ctx_row):
    xc = ctx_ref[0]
    shift = mod_ref[ctx_row:ctx_row + 1, 0:D_MODEL]
    scale = mod_ref[ctx_row:ctx_row + 1, D_MODEL:2 * D_MODEL]
    hc = _rms_rows(xc) * ng_ref[...] * (1.0 + scale) + shift
    pc = _dot(hc, w_ref[...])
    k = pc[:, K0:V0]
    v16 = pc[:, V0:LR0].astype(BF16)
    lr = pc[:, LR0 + A_WIDTH:]
    a_f, a_b = _gate_logs(lr, w2_ref, gb_ref)
    for a, reverse, out_ref in ((a_f, False, sf_ref), (a_b, True, sb_ref)):
        _, _, _, _, kblk = _block_operands(None, k, a, _chunk_tri(reverse), reverse)
        out_ref[0] = _block_kv_t(v16, kblk)


def _ctx_call(ctx, mod, ng, w_kvl, w2, gb, ctx_row):
    b, lc, _ = ctx.shape
    assert lc == GLA_BLOCK, "context length must be one GLA block"
    nw = w_kvl.shape[1]
    st = jax.ShapeDtypeStruct((b, B_DV, B_KEY_WIDTH), F32)
    st_spec = pl.BlockSpec((1, B_DV, B_KEY_WIDTH), lambda i: (i, 0, 0))
    return pl.pallas_call(
        functools.partial(_ctx_kernel, ctx_row=ctx_row),
        grid=(b,),
        in_specs=[pl.BlockSpec((1, lc, D_MODEL), lambda i: (i, 0, 0)),
                  pl.BlockSpec((8, 3 * D_MODEL), lambda i: (0, 0)),
                  pl.BlockSpec((1, D_MODEL), lambda i: (0, 0)),
                  pl.BlockSpec((D_MODEL, nw), lambda i: (0, 0)),
                  pl.BlockSpec((2, B_GATE_RANK, B_KEY_WIDTH), lambda i: (0, 0, 0)),
                  pl.BlockSpec((2, B_KEY_WIDTH), lambda i: (0, 0))],
        out_specs=[st_spec, st_spec],
        out_shape=[st, st],
        name="ctx",
    )(ctx, mod, ng, w_kvl, w2, gb)


def _latent_h(x, mod_ref, ng_ref):
    shift = mod_ref[0, 0:1, :]
    scale = mod_ref[0, 1:2, :]
    return _rms_rows(x) * ng_ref[...] * (1.0 + scale) + shift


def _pre_kernel(x_ref, mod_ref, ng_ref, w_ref, w2_ref, gb_ref, lg_ref, lb_ref,
                qdf_ref, kdf_ref, kcf_ref, qdb_ref, kdb_ref, kcb_ref, v_ref,
                totf_ref, totb_ref, kvf_ref, kvb_ref, vr_ref, vc_ref):
    h = _latent_h(x_ref[0], mod_ref, ng_ref)
    p = _dot(h, w_ref[...])
    q = p[:, Q0:K0] * (B_DK ** -0.5)
    k = p[:, K0:V0]
    v16 = p[:, V0:LR0].astype(BF16)
    v_ref[0] = v16
    a_f, a_b = _gate_logs(p[:, LR0 + A_WIDTH:], w2_ref, gb_ref)
    nblk = x_ref.shape[1] // GLA_BLOCK
    for a, reverse, qd_ref, kd_ref, kc_ref, tot_ref, kv_ref in (
            (a_f, False, qdf_ref, kdf_ref, kcf_ref, totf_ref, kvf_ref),
            (a_b, True, qdb_ref, kdb_ref, kcb_ref, totb_ref, kvb_ref)):
        tri = _chunk_tri(reverse)
        for blk in range(nblk):
            rs = slice(blk * GLA_BLOCK, (blk + 1) * GLA_BLOCK)
            qd, kd, kdec, tots, kblk = _block_operands(q[rs], k[rs], a[rs], tri, reverse)
            qd_ref[0, rs, :] = qd.astype(BF16)
            kd_ref[0, rs, :] = kd.astype(BF16)
            kc_ref[0, rs, :] = kdec.astype(BF16)
            for c in range(GLA_NC):
                tot_ref[0, blk, c:c + 1, :] = tots[c]
            kv_ref[0, blk] = _block_kv_t(v16[rs], kblk)

    va = p[:, LR0:LR0 + A_WIDTH]
    vc = va - jnp.mean(va, axis=-1, keepdims=True)
    vn = vc * lax.rsqrt(jnp.mean(vc * vc, axis=-1, keepdims=True) + EPS) * lg_ref[...] + lb_ref[...]
    nr = A_ROW_GROUPS * A_GROUP_DIM
    vr_ref[0] = vn[:, 0:nr].astype(BF16)
    for g in range(A_GROUPS - A_ROW_GROUPS):
        vg = vn[:, nr + g * A_GROUP_DIM:nr + (g + 1) * A_GROUP_DIM]
        vc_ref[0, g] = jnp.swapaxes(vg.reshape(vg.shape[0] // GRID_W, GRID_W, A_GROUP_DIM), 0, 1)


def _pre_call(x, mod3, ng, w_pre, w2, gb, lg, lb, t):
    b, l, _ = x.shape
    nw = w_pre.shape[1]
    ncg = A_GROUPS - A_ROW_GROUPS
    nblk = t // GLA_BLOCK

    def tok(width, dtype):
        return (pl.BlockSpec((1, t, width), lambda i, j: (i, j, 0)),
                jax.ShapeDtypeStruct((b, l, width), dtype))

    tot = (pl.BlockSpec((1, nblk, GLA_NC, B_KEY_WIDTH), lambda i, j: (i, j, 0, 0)),
           jax.ShapeDtypeStruct((b, l // GLA_BLOCK, GLA_NC, B_KEY_WIDTH), F32))
    kv = (pl.BlockSpec((1, nblk, B_DV, B_KEY_WIDTH), lambda i, j: (i, j, 0, 0)),
          jax.ShapeDtypeStruct((b, l // GLA_BLOCK, B_DV, B_KEY_WIDTH), F32))
    outs = [tok(B_KEY_WIDTH, BF16)] * 6 + [tok(B_VAL_WIDTH, BF16), tot, tot, kv, kv,
                                           tok(A_ROW_GROUPS * A_GROUP_DIM, BF16),
                                           (pl.BlockSpec((1, ncg, GRID_W, t // GRID_W, A_GROUP_DIM),
                                                         lambda i, j: (i, 0, 0, j, 0)),
                                            jax.ShapeDtypeStruct((b, ncg, GRID_W, l // GRID_W, A_GROUP_DIM), F32))]
    const2 = lambda i, j: (0, 0)
    return pl.pallas_call(
        _pre_kernel,
        grid=(b, l // t),
        in_specs=[pl.BlockSpec((1, t, D_MODEL), lambda i, j: (i, j, 0)),
                  pl.BlockSpec((1, 3, D_MODEL), lambda i, j: (i, 0, 0)),
                  pl.BlockSpec((1, D_MODEL), const2),
                  pl.BlockSpec((D_MODEL, nw), const2),
                  pl.BlockSpec((2, B_GATE_RANK, B_KEY_WIDTH), lambda i, j: (0, 0, 0)),
                  pl.BlockSpec((2, B_KEY_WIDTH), const2),
                  pl.BlockSpec((1, A_WIDTH), const2),
                  pl.BlockSpec((1, A_WIDTH), const2)],
        out_specs=[o[0] for o in outs],
        out_shape=[o[1] for o in outs],
        compiler_params=pltpu.CompilerParams(
            dimension_semantics=("parallel", "parallel"), vmem_limit_bytes=VMEM_LIMIT_BYTES),
        name="pre",
    )(x, mod3, ng, w_pre, w2, gb, lg, lb)


def _gscan_kernel(kvf_ref, kvb_ref, totf_ref, totb_ref, s0f_ref, s0b_ref, sf_ref, sb_ref, stf_scr, stb_scr):
    @pl.when(pl.program_id(1) == 0)
    def _():
        stf_scr[...] = s0f_ref[0]
        stb_scr[...] = s0b_ref[0]

    n = kvf_ref.shape[1]
    sf = stf_scr[...]
    sb = stb_scr[...]
    for i in range(n):
        sf_ref[0, i] = sf.T.astype(BF16)
        dec = jnp.exp(jnp.sum(totf_ref[0, i], axis=0, keepdims=True))
        sf = dec * sf + kvf_ref[0, i]
        ib = n - 1 - i
        sb_ref[0, ib] = sb.T.astype(BF16)
        dec = jnp.exp(jnp.sum(totb_ref[0, ib], axis=0, keepdims=True))
        sb = dec * sb + kvb_ref[0, ib]
    stf_scr[...] = sf
    stb_scr[...] = sb


def _gscan_call(kv_f, kv_b, tot_f, tot_b, s0f, s0b):
    b, nb = kv_f.shape[0], kv_f.shape[1]
    g = SCAN_GROUP
    nj = nb // g
    fwd = lambda i, j: (i, j, 0, 0)
    bwd = lambda i, j: (i, nj - 1 - j, 0, 0)
    kv_blk = (1, g, B_DV, B_KEY_WIDTH)
    tot_blk = (1, g, GLA_NC, B_KEY_WIDTH)
    st_spec = pl.BlockSpec((1, B_DV, B_KEY_WIDTH), lambda i, j: (i, 0, 0))
    s_blk = (1, g, B_KEY_WIDTH, B_DV)
    s_shape = jax.ShapeDtypeStruct((b, nb, B_KEY_WIDTH, B_DV), BF16)
    return pl.pallas_call(
        _gscan_kernel,
        grid=(b, nj),
        in_specs=[pl.BlockSpec(kv_blk, fwd), pl.BlockSpec(kv_blk, bwd),
                  pl.BlockSpec(tot_blk, fwd), pl.BlockSpec(tot_blk, bwd), st_spec, st_spec],
        out_specs=[pl.BlockSpec(s_blk, fwd), pl.BlockSpec(s_blk, bwd)],
        out_shape=[s_shape, s_shape],
        scratch_shapes=[pltpu.VMEM((B_DV, B_KEY_WIDTH), F32), pltpu.VMEM((B_DV, B_KEY_WIDTH), F32)],
        compiler_params=pltpu.CompilerParams(
            dimension_semantics=("parallel", "arbitrary"), vmem_limit_bytes=VMEM_LIMIT_BYTES),
        name="gscan",
    )(kv_f, kv_b, tot_f, tot_b, s0f, s0b)


def _scale_rows(x16, scales):
    parts = []
    for c, s in enumerate(scales):
        xc = x16[c * B_CHUNK:(c + 1) * B_CHUNK, :]
        parts.append(xc if s is None else (xc.astype(F32) * s).astype(BF16))
    return parts[0] if len(parts) == 1 else jnp.concatenate(parts, axis=0)


def _exp_sum(tots, idx):
    return jnp.exp(_sum_rows([tots[m] for m in idx])) if idx else None


def _gla_block(qdf, kdf, kcf, qdb, kdb, kcb, v16, tf, tb, sf, sb, p_scr):
    nc = GLA_NC
    ch = B_CHUNK
    hm = [_head_lane_mask(par, BF16) for par in range(2)]

    def lanes(h):
        return slice((h // 2) * LANES, (h // 2 + 1) * LANES)

    row = lax.broadcasted_iota(jnp.int32, (ch, 2 * ch), 0)
    col = lax.broadcasted_iota(jnp.int32, (ch, 2 * ch), 1)
    m_fe = (col < ch) & (col <= row)
    m_be = ((col < ch) & (col >= row)) | (col >= ch)
    m_fo = (col < ch) | (col - ch <= row)
    m_bo = (col >= ch) & (col - ch >= row)
    chunk_even = (lax.broadcasted_iota(jnp.int32, (GLA_BLOCK, 1), 0) // ch) % 2 == 0
    kmix_f = jnp.where(chunk_even, kcf, kdf)
    kmix_b = jnp.where(chunk_even, kdb, kcb)
    for par in range(2):
        kdf_m, kdb_m = kdf * hm[par], kdb * hm[par]
        kmf_m, kmb_m = kmix_f * hm[par], kmix_b * hm[par]
        for h in range(par, B_HEADS, 2):
            ls = lanes(h)
            for i in range(nc // 2):
                pr = slice(2 * i * ch, (2 * i + 2) * ch)
                re = slice(2 * i * ch, (2 * i + 1) * ch)
                ro = slice((2 * i + 1) * ch, (2 * i + 2) * ch)
                even = (jnp.where(m_fe, _dot_nt(qdf[re, ls], kdf_m[pr, ls]), 0.0)
                        + jnp.where(m_be, _dot_nt(qdb[re, ls], kmb_m[pr, ls]), 0.0))
                odd = (jnp.where(m_fo, _dot_nt(qdf[ro, ls], kmf_m[pr, ls]), 0.0)
                       + jnp.where(m_bo, _dot_nt(qdb[ro, ls], kdb_m[pr, ls]), 0.0))
                p_scr[h, re, pr] = even.astype(BF16)
                p_scr[h, ro, pr] = odd.astype(BF16)

    def cross(lo, hi):
        if hi - lo <= 2:
            return
        mid = (lo + hi) // 2
        cross(lo, mid)
        cross(mid, hi)
        rl = slice(lo * ch, mid * ch)
        rh = slice(mid * ch, hi * ch)
        qf = _scale_rows(qdf[rh], [_exp_sum(tf, range(mid, c)) for c in range(mid, hi)])
        kf = _scale_rows(kcf[rl], [_exp_sum(tf, range(c + 1, mid)) for c in range(lo, mid)])
        qb = _scale_rows(qdb[rl], [_exp_sum(tb, range(c + 1, mid)) for c in range(lo, mid)])
        kb = _scale_rows(kcb[rh], [_exp_sum(tb, range(mid, c)) for c in range(mid, hi)])
        for h in range(B_HEADS):
            ls = lanes(h)
            m = hm[h % 2]
            p_scr[h, rh, rl] = _dot_nt(qf[:, ls], (kf * m)[:, ls]).astype(BF16)
            p_scr[h, rl, rh] = _dot_nt(qb[:, ls], (kb * m)[:, ls]).astype(BF16)

    cross(0, nc)

    qsf = _scale_rows(qdf, [_exp_sum(tf, range(0, c)) for c in range(nc)])
    qsb = _scale_rows(qdb, [_exp_sum(tb, range(c + 1, nc)) for c in range(nc)])
    head_row = lax.broadcasted_iota(jnp.int32, (LANES, 1), 0) // B_DK
    outs = []
    for h in range(B_HEADS):
        ls = lanes(h)
        own = (head_row == (h % 2)).astype(BF16)
        lhs = jnp.concatenate([p_scr[h], qsf[:, ls], qsb[:, ls]], axis=1)
        rhs = jnp.concatenate([v16[:, h * B_DV:(h + 1) * B_DV], sf[ls, :] * own, sb[ls, :] * own], axis=0)
        outs.append(jnp.dot(lhs, rhs, preferred_element_type=F32))
    return outs


def _colmix_kernel(ws_ref, bs_ref, vn_ref, o_ref):
    nw = vn_ref.shape[2]
    xs = jnp.concatenate([vn_ref[0, 0, w] for w in range(nw)], axis=1)
    y = _dot(ws_ref[0], xs) + bs_ref[0][:, 0:1]
    for w in range(nw):
        o_ref[0, 0, w] = y[:, w * A_GROUP_DIM:(w + 1) * A_GROUP_DIM]


def _colmix_call(ws_col, bs_col, vn_col, nw):
    b, g, width, rows, ch = vn_col.shape
    blk = pl.BlockSpec((1, 1, nw, rows, ch), lambda i, j, m: (i, j, m, 0, 0))
    return pl.pallas_call(
        _colmix_kernel,
        grid=(b, g, width // nw),
        in_specs=[pl.BlockSpec((1, rows, rows), lambda i, j, m: (j, 0, 0)),
                  pl.BlockSpec((1, rows, A_GROUP_DIM), lambda i, j, m: (j, 0, 0)),
                  blk],
        out_specs=blk,
        out_shape=jax.ShapeDtypeStruct(vn_col.shape, F32),
        compiler_params=pltpu.CompilerParams(
            dimension_semantics=("parallel", "parallel", "parallel"), vmem_limit_bytes=VMEM_LIMIT_BYTES),
        name="colmix",
    )(ws_col, bs_col, vn_col)


def _main_kernel(x_ref, mod_ref, ng_ref, w_ref, vr_ref, sc_ref,
                 qdf_ref, kdf_ref, kcf_ref, qdb_ref, kdb_ref, kcb_ref, v_ref, totf_ref, totb_ref, sf_ref, sb_ref,
                 ws_ref, bs_ref, bg_ref, wpa_ref, wpb_ref, wo_ref, fg_ref, o_ref, acta_scr, actb_scr, p_scr):
    x = x_ref[0]
    t = x.shape[0]
    h = _latent_h(x, mod_ref, ng_ref)
    p = _dot(h, w_ref[...])
    zb = p[:, 0:B_VAL_WIDTH]
    u = p[:, B_VAL_WIDTH:B_VAL_WIDTH + A_WIDTH]
    za = p[:, B_VAL_WIDTH + A_WIDTH:B_VAL_WIDTH + 2 * A_WIDTH]
    gates = jax.nn.sigmoid(p[:, B_VAL_WIDTH + 2 * A_WIDTH:])

    uz = u * _silu(za)
    for g in range(A_GROUPS):
        cs = slice(g * A_GROUP_DIM, (g + 1) * A_GROUP_DIM)
        if g < A_ROW_GROUPS:
            for c in range(t // A_CHUNK):
                rs = slice(c * A_CHUNK, (c + 1) * A_CHUNK)
                sv = _dot(ws_ref[g], vr_ref[0, rs, cs]) + bs_ref[g][:, 0:1]
                acta_scr[rs, cs] = (uz[rs, cs] * sv).astype(BF16)
        else:
            sv = jnp.swapaxes(sc_ref[0, g - A_ROW_GROUPS], 0, 1).reshape(t, A_GROUP_DIM)
            acta_scr[:, cs] = (uz[:, cs] * sv).astype(BF16)
    ya = jnp.dot(acta_scr[...], wpa_ref[...], preferred_element_type=F32)

    zg = _silu(zb) * bg_ref[...]
    for blk in range(t // GLA_BLOCK):
        rs = slice(blk * GLA_BLOCK, (blk + 1) * GLA_BLOCK)
        tf = [totf_ref[0, blk, c:c + 1, :] for c in range(GLA_NC)]
        tb = [totb_ref[0, blk, c:c + 1, :] for c in range(GLA_NC)]
        o_heads = _gla_block(qdf_ref[0, rs, :], kdf_ref[0, rs, :], kcf_ref[0, rs, :],
                             qdb_ref[0, rs, :], kdb_ref[0, rs, :], kcb_ref[0, rs, :], v_ref[0, rs, :],
                             tf, tb, sf_ref[0, blk], sb_ref[0, blk], p_scr.at[blk])
        for hd in range(B_HEADS):
            cs = slice(hd * B_DV, (hd + 1) * B_DV)
            actb_scr[rs, cs] = (_rms_rows(o_heads[hd]) * zg[rs, cs]).astype(BF16)
    yb = jnp.dot(actb_scr[...], wpb_ref[...], preferred_element_type=F32)

    m = gates[:, 0:D_MODEL] * ya + gates[:, D_MODEL:] * yb
    y = _dot(m, wo_ref[...])
    xo = x + mod_ref[0, 2:3, :] * y
    o_ref[0] = _rms_rows(xo) * fg_ref[...]


def _main_call(x, mod3, ng, w_main, vn_row, sv_col, gla_ops, ws, bs, bg, wpa, wpb, wo, fg, t):
    b, l, _ = x.shape
    nw = w_main.shape[1]
    ncg = A_GROUPS - A_ROW_GROUPS
    const2 = lambda i, j: (0, 0)
    const3 = lambda i, j: (0, 0, 0)
    tok = lambda width: pl.BlockSpec((1, t, width), lambda i, j: (i, j, 0))
    nblk = t // GLA_BLOCK
    tot_spec = pl.BlockSpec((1, nblk, GLA_NC, B_KEY_WIDTH), lambda i, j: (i, j, 0, 0))
    st_spec = pl.BlockSpec((1, nblk, B_KEY_WIDTH, B_DV), lambda i, j: (i, j, 0, 0))
    return pl.pallas_call(
        _main_kernel,
        grid=(b, l // t),
        in_specs=[tok(D_MODEL),
                  pl.BlockSpec((1, 3, D_MODEL), lambda i, j: (i, 0, 0)),
                  pl.BlockSpec((1, D_MODEL), const2),
                  pl.BlockSpec((D_MODEL, nw), const2),
                  tok(A_ROW_GROUPS * A_GROUP_DIM),
                  pl.BlockSpec((1, ncg, GRID_W, t // GRID_W, A_GROUP_DIM), lambda i, j: (i, 0, 0, j, 0)),
                  tok(B_KEY_WIDTH), tok(B_KEY_WIDTH), tok(B_KEY_WIDTH),
                  tok(B_KEY_WIDTH), tok(B_KEY_WIDTH), tok(B_KEY_WIDTH), tok(B_VAL_WIDTH),
                  tot_spec, tot_spec, st_spec, st_spec,
                  pl.BlockSpec((A_GROUPS, A_CHUNK, A_CHUNK), const3),
                  pl.BlockSpec((A_GROUPS, A_CHUNK, A_GROUP_DIM), const3),
                  pl.BlockSpec((1, B_VAL_WIDTH), const2),
                  pl.BlockSpec((A_WIDTH, D_MODEL), const2),
                  pl.BlockSpec((B_VAL_WIDTH, D_MODEL), const2),
                  pl.BlockSpec((D_MODEL, D_MODEL), const2),
                  pl.BlockSpec((1, D_MODEL), const2)],
        out_specs=tok(D_MODEL),
        out_shape=jax.ShapeDtypeStruct((b, l, D_MODEL), F32),
        scratch_shapes=[pltpu.VMEM((t, A_WIDTH), BF16), pltpu.VMEM((t, B_VAL_WIDTH), BF16),
                        pltpu.VMEM((nblk, B_HEADS, GLA_BLOCK, GLA_BLOCK), BF16)],
        compiler_params=pltpu.CompilerParams(
            dimension_semantics=("parallel", "parallel"), vmem_limit_bytes=VMEM_LIMIT_BYTES),
        name="main",
    )(x, mod3, ng, w_main, vn_row, sv_col, *gla_ops, ws, bs, bg, wpa, wpb, wo, fg)


def kernel(x, c, ctx, c_ctx, w_mod, b_mod, norm_g, w_in, a_ln_g, a_ln_b, a_ws, a_bs, b_gate_w2, b_gate_b,
           b_norm_g, w_proj_a, w_proj_b, w_out, final_norm_g):
    assert w_mod.shape[0] == 1, "single-layer block"
    b, l, _ = x.shape
    rows = l // GRID_W
    assert rows == A_CHUNK
    wm, bm = w_mod[0], b_mod[0]
    ng = norm_g[0][None, :]

    cc = jnp.zeros((8, D_MODEL), F32).at[0:b].set(c).at[b].set(c_ctx)
    mod = _mod_call(cc, wm, bm[None, :])
    mod3 = mod[0:b].reshape(b, 3, D_MODEL)

    w_pre, w_main, wpa16, wpb16, wo16 = _wprep_call(w_in[0].T, w_proj_a, w_proj_b, w_out)
    w2, gb = b_gate_w2[0], b_gate_b[0]
    s0f, s0b = _ctx_call(ctx, mod, ng, w_pre, w2, gb, b)

    (qd_f, kd_f, kc_f, qd_b, kd_b, kc_b, v16, tot_f, tot_b, kv_f, kv_b, vn_row, vn_col) = _pre_call(
        x, mod3, ng, w_pre, w2, gb, a_ln_g[0][None, :], a_ln_b[0][None, :], 512)

    s_f, s_b = _gscan_call(kv_f, kv_b, tot_f, tot_b, s0f, s0b)
    gla_ops = (qd_f, kd_f, kc_f, qd_b, kd_b, kc_b, v16, tot_f, tot_b, s_f, s_b)

    bs_b = jnp.broadcast_to(a_bs[0][:, :, None], (A_GROUPS, A_CHUNK, A_GROUP_DIM))
    sv_col = _colmix_call(a_ws[0][A_ROW_GROUPS:], bs_b[A_ROW_GROUPS:], vn_col, 16)

    return _main_call(x, mod3, ng, w_main, vn_row, sv_col, gla_ops, a_ws[0], bs_b, b_norm_g[0][None, :],
                      wpa16, wpb16, wo16, final_norm_g[None, :], 512)
```

```python
import functools

import jax
import jax.numpy as jnp
from jax import lax
from jax.experimental import pallas as pl
from jax.experimental.pallas import tpu as pltpu

D_MODEL = 1024
GRID_W = 64
EPS = 1e-6

A_WIDTH = 512
A_GROUPS = 4
A_GROUP_DIM = 128
A_CHUNK = 128
A_ROW_GROUPS = 2

B_HEADS = 4
B_DK = 64
B_DV = 128
B_KEY_WIDTH = 256
B_VAL_WIDTH = 512
B_GATE_RANK = 16
B_GATE_TAU = 16.0
B_CHUNK = 64

Q0 = 0
K0 = Q0 + B_KEY_WIDTH
V0 = K0 + B_KEY_WIDTH
LR0 = V0 + B_VAL_WIDTH
ZB0 = LR0 + 2 * B_GATE_RANK
UA0 = ZB0 + B_VAL_WIDTH
VA0 = UA0 + A_WIDTH
ZA0 = VA0 + A_WIDTH
G0 = ZA0 + A_WIDTH
IN_WIDTH = G0 + 2 * D_MODEL

LANES = 128

P_LR = 0
P_K = P_LR + LANES
P_Q = P_K + B_KEY_WIDTH
P_V = P_Q + B_KEY_WIDTH
P_VA = P_V + B_VAL_WIDTH
PRE_WIDTH = P_VA + A_WIDTH
GLA_BLOCK = 256
GLA_NC = GLA_BLOCK // B_CHUNK
SCAN_GROUP = 8

VMEM_LIMIT_BYTES = 56 * 1024 * 1024

BF16 = jnp.bfloat16
F32 = jnp.float32


def _dot(a, b):
    return jnp.dot(a.astype(BF16), b.astype(BF16), preferred_element_type=F32)


def _dot_nt(a, b):
    return lax.dot_general(a.astype(BF16), b.astype(BF16), (((1,), (1,)), ((), ())),
                           preferred_element_type=F32)


def _dot_tn(a, b):
    return lax.dot_general(a.astype(BF16), b.astype(BF16), (((0,), (0,)), ((), ())),
                           preferred_element_type=F32)


def _silu(x):
    return x * jax.nn.sigmoid(x)


def _rms_rows(x):
    return x * lax.rsqrt(jnp.mean(x * x, axis=-1, keepdims=True) + EPS)


def _head_lane_mask(parity, dtype):
    lane = lax.broadcasted_iota(jnp.int32, (1, B_KEY_WIDTH), 1)
    return (((lane // B_DK) % 2) == parity).astype(dtype)


def _mod_kernel(c_ref, w_ref, b_ref, o_ref):
    o_ref[...] = _dot(_silu(c_ref[...]), w_ref[...]) + b_ref[...]


def _mod_call(cc, wm, bm):
    n = wm.shape[1]
    tn = 1024
    return pl.pallas_call(
        _mod_kernel,
        grid=(n // tn,),
        in_specs=[pl.BlockSpec((8, D_MODEL), lambda j: (0, 0)),
                  pl.BlockSpec((D_MODEL, tn), lambda j: (0, j)),
                  pl.BlockSpec((1, tn), lambda j: (0, j))],
        out_specs=pl.BlockSpec((8, tn), lambda j: (0, j)),
        out_shape=jax.ShapeDtypeStruct((8, n), F32),
        name="mod",
    )(cc, wm, bm)


def _wprep_kernel(wt_ref, wpa_ref, wpb_ref, wo_ref, wpre_ref, wmain_ref, wpa16_ref, wpb16_ref, wo16_ref):
    def put(dst_ref, c0, r0, n):
        for s in range(0, n, 2 * LANES):
            m = min(2 * LANES, n - s)
            dst_ref[:, c0 + s:c0 + s + m] = wt_ref[r0 + s:r0 + s + m, :].T.astype(BF16)

    put(wpre_ref, P_LR, LR0, LANES)
    put(wpre_ref, P_K, K0, B_KEY_WIDTH)
    put(wpre_ref, P_Q, Q0, B_KEY_WIDTH)
    put(wpre_ref, P_V, V0, B_VAL_WIDTH)
    put(wpre_ref, P_VA, VA0, A_WIDTH)
    put(wmain_ref, 0, ZB0, VA0 - ZB0)
    put(wmain_ref, VA0 - ZB0, ZA0, IN_WIDTH - ZA0)
    wpa16_ref[...] = wpa_ref[0].astype(BF16)
    wpb16_ref[...] = wpb_ref[0].astype(BF16)
    wo16_ref[...] = wo_ref[0].astype(BF16)


def _wprep_call(w_in_t, w_proj_a, w_proj_b, w_out):
    steps = 4
    n_main = (VA0 - ZB0) + (IN_WIDTH - ZA0)

    def rows3(a):
        return pl.BlockSpec((1, a.shape[1] // steps, a.shape[2]), lambda i: (0, i, 0))

    def rows2(nrows, ncols):
        return (pl.BlockSpec((nrows // steps, ncols), lambda i: (i, 0)),
                jax.ShapeDtypeStruct((nrows, ncols), BF16))

    outs = [rows2(D_MODEL, PRE_WIDTH), rows2(D_MODEL, n_main), rows2(A_WIDTH, D_MODEL),
            rows2(B_VAL_WIDTH, D_MODEL), rows2(D_MODEL, D_MODEL)]
    return pl.pallas_call(
        _wprep_kernel,
        grid=(steps,),
        in_specs=[pl.BlockSpec((IN_WIDTH, D_MODEL // steps), lambda i: (0, i)),
                  rows3(w_proj_a), rows3(w_proj_b), rows3(w_out)],
        out_specs=[o[0] for o in outs],
        out_shape=[o[1] for o in outs],
        compiler_params=pltpu.CompilerParams(
            dimension_semantics=("parallel",), vmem_limit_bytes=VMEM_LIMIT_BYTES),
        name="wprep",
    )(w_in_t, w_proj_a, w_proj_b, w_out)


def _gate_logs(lr, w2_ref, gb_ref):
    out = []
    for r in range(2):
        logits = _dot(lr[:, r * B_GATE_RANK:(r + 1) * B_GATE_RANK], w2_ref[r]) + gb_ref[r:r + 1, :]
        out.append(jax.nn.log_sigmoid(logits) / B_GATE_TAU)
    return out


def _chunk_tri(reverse):
    i = lax.broadcasted_iota(jnp.int32, (GLA_BLOCK, GLA_BLOCK), 0)
    j = lax.broadcasted_iota(jnp.int32, (GLA_BLOCK, GLA_BLOCK), 1)
    same = (i // B_CHUNK) == (j // B_CHUNK)
    tri = (j >= i) if reverse else (j <= i)
    return (same & tri).astype(BF16)


def _sum_rows(rows):
    acc = rows[0]
    for r in rows[1:]:
        acc = acc + r
    return acc


def _block_cum(a, tri, reverse):
    hi = a.astype(BF16)
    lo = (a - hi.astype(F32)).astype(BF16)
    cum = (jnp.dot(tri, hi, preferred_element_type=F32) + jnp.dot(tri, lo, preferred_element_type=F32))
    last = 0 if reverse else B_CHUNK - 1
    tots = [cum[c * B_CHUNK + last:c * B_CHUNK + last + 1, :] for c in range(GLA_NC)]
    return cum, tots


def _block_keys(k, cum, tots, reverse):
    totb = jnp.concatenate([jnp.broadcast_to(t, (B_CHUNK, B_KEY_WIDTH)) for t in tots], axis=0)
    kdec = k * jnp.exp(totb - cum)
    later = []
    for c in range(GLA_NC):
        idx = list(range(0, c)) if reverse else list(range(c + 1, GLA_NC))
        if idx:
            later.append(jnp.broadcast_to(jnp.exp(_sum_rows([tots[m] for m in idx])), (B_CHUNK, B_KEY_WIDTH)))
        else:
            later.append(jnp.ones((B_CHUNK, B_KEY_WIDTH), F32))
    return kdec, kdec * jnp.concatenate(later, axis=0)


def _pre_project(h16, w_ref, want_q):
    lr = jnp.dot(h16, w_ref[:, P_LR:P_LR + LANES], preferred_element_type=F32)[:, 0:2 * B_GATE_RANK]
    k = jnp.dot(h16, w_ref[:, P_K:P_K + B_KEY_WIDTH], preferred_element_type=F32)
    q = jnp.dot(h16, w_ref[:, P_Q:P_Q + B_KEY_WIDTH], preferred_element_type=F32) if want_q else None
    v = jnp.dot(h16, w_ref[:, P_V:P_V + B_VAL_WIDTH], preferred_element_type=F32)
    return lr, k, q, v


def _block_kv_t(v16, kblk):
    k16 = kblk.astype(BF16)
    lane = lax.broadcasted_iota(jnp.int32, (1, B_KEY_WIDTH), 1)
    acc = jnp.zeros((B_DV, B_KEY_WIDTH), F32)
    for h in range(B_HEADS):
        full = _dot_tn(v16[:, h * B_DV:(h + 1) * B_DV], k16)
        acc = acc + jnp.where((lane // B_DK) == h, full, 0.0)
    return acc


def _ctx_kernel(ctx_ref, mod_ref, ng_ref, w_ref, w2_ref, gb_ref, sf_ref, sb_ref, *, ctx_row):
    xc = ctx_ref[0]
    shift = mod_ref[ctx_row:ctx_row + 1, 0:D_MODEL]
    scale = mod_ref[ctx_row:ctx_row + 1, D_MODEL:2 * D_MODEL]
    hc = _rms_rows(xc) * ng_ref[...] * (1.0 + scale) + shift
    lr, k, _, v = _pre_project(hc.astype(BF16), w_ref, False)
    v16 = v.astype(BF16)
    a_f, a_b = _gate_logs(lr, w2_ref, gb_ref)
    for a, reverse, out_ref in ((a_f, False, sf_ref), (a_b, True, sb_ref)):
        cum, tots = _block_cum(a, _chunk_tri(reverse), reverse)
        _, kblk = _block_keys(k, cum, tots, reverse)
        out_ref[0] = _block_kv_t(v16, kblk)


def _ctx_call(ctx, mod, ng, w_kvl, w2, gb, ctx_row):
    b, lc, _ = ctx.shape
    assert lc == GLA_BLOCK, "context length must be one GLA block"
    nw = w_kvl.shape[1]
    st = jax.ShapeDtypeStruct((b, B_DV, B_KEY_WIDTH), F32)
    st_spec = pl.BlockSpec((1, B_DV, B_KEY_WIDTH), lambda i: (i, 0, 0))
    return pl.pallas_call(
        functools.partial(_ctx_kernel, ctx_row=ctx_row),
        grid=(b,),
        in_specs=[pl.BlockSpec((1, lc, D_MODEL), lambda i: (i, 0, 0)),
                  pl.BlockSpec((8, 3 * D_MODEL), lambda i: (0, 0)),
                  pl.BlockSpec((1, D_MODEL), lambda i: (0, 0)),
                  pl.BlockSpec((D_MODEL, nw), lambda i: (0, 0)),
                  pl.BlockSpec((2, B_GATE_RANK, B_KEY_WIDTH), lambda i: (0, 0, 0)),
                  pl.BlockSpec((2, B_KEY_WIDTH), lambda i: (0, 0))],
        out_specs=[st_spec, st_spec],
        out_shape=[st, st],
        name="ctx",
    )(ctx, mod, ng, w_kvl, w2, gb)


def _latent_h(x, mod_ref, ng_ref):
    shift = mod_ref[0, 0:1, :]
    scale = mod_ref[0, 1:2, :]
    return _rms_rows(x) * ng_ref[...] * (1.0 + scale) + shift


def _pre_kernel(x_ref, mod_ref, ng_ref, w_ref, w2_ref, gb_ref, lg_ref, lb_ref,
                qdf_ref, kdf_ref, kcf_ref, qdb_ref, kdb_ref, kcb_ref, v_ref,
                totf_ref, totb_ref, kvf_ref, kvb_ref, vr_ref, vc_ref):
    h16 = _latent_h(x_ref[0], mod_ref, ng_ref).astype(BF16)
    nblk = x_ref.shape[1] // GLA_BLOCK
    blocks = [slice(blk * GLA_BLOCK, (blk + 1) * GLA_BLOCK) for blk in range(nblk)]
    dirs = ((False, qdf_ref, kdf_ref, kcf_ref, totf_ref, kvf_ref), (True, qdb_ref, kdb_ref, kcb_ref, totb_ref, kvb_ref))

    lr = jnp.dot(h16, w_ref[:, P_LR:P_LR + LANES], preferred_element_type=F32)[:, 0:2 * B_GATE_RANK]
    k = jnp.dot(h16, w_ref[:, P_K:P_K + B_KEY_WIDTH], preferred_element_type=F32)
    logs = _gate_logs(lr, w2_ref, gb_ref)
    q = jnp.dot(h16, w_ref[:, P_Q:P_Q + B_KEY_WIDTH], preferred_element_type=F32) * (B_DK ** -0.5)
    v16 = jnp.dot(h16, w_ref[:, P_V:P_V + B_VAL_WIDTH], preferred_element_type=F32).astype(BF16)
    v_ref[0] = v16

    cums = {}
    for d, (reverse, _, _, _, tot_ref, _) in enumerate(dirs):
        tri = _chunk_tri(reverse)
        for blk, rs in enumerate(blocks):
            cum, tots = _block_cum(logs[d][rs], tri, reverse)
            cums[d, blk] = (cum, tots)
            for c in range(GLA_NC):
                tot_ref[0, blk, c:c + 1, :] = tots[c]

    va = jnp.dot(h16, w_ref[:, P_VA:P_VA + A_WIDTH], preferred_element_type=F32)

    for d, (reverse, qd_ref, kd_ref, kc_ref, _, kv_ref) in enumerate(dirs):
        for blk, rs in enumerate(blocks):
            cum, tots = cums[d, blk]
            kdec, kblk = _block_keys(k[rs], cum, tots, reverse)
            kc_ref[0, rs, :] = kdec.astype(BF16)
            kd_ref[0, rs, :] = (k[rs] * jnp.exp(-cum)).astype(BF16)
            qd_ref[0, rs, :] = (q[rs] * jnp.exp(cum)).astype(BF16)
            kv_ref[0, blk] = _block_kv_t(v16[rs], kblk)

    vc = va - jnp.mean(va, axis=-1, keepdims=True)
    vn = vc * lax.rsqrt(jnp.mean(vc * vc, axis=-1, keepdims=True) + EPS) * lg_ref[...] + lb_ref[...]
    nr = A_ROW_GROUPS * A_GROUP_DIM
    vr_ref[0] = vn[:, 0:nr].astype(BF16)
    for g in range(A_GROUPS - A_ROW_GROUPS):
        vg = vn[:, nr + g * A_GROUP_DIM:nr + (g + 1) * A_GROUP_DIM]
        vc_ref[0, g] = jnp.swapaxes(vg.reshape(vg.shape[0] // GRID_W, GRID_W, A_GROUP_DIM), 0, 1)


def _pre_call(x, mod3, ng, w_pre, w2, gb, lg, lb, t):
    b, l, _ = x.shape
    nw = w_pre.shape[1]
    ncg = A_GROUPS - A_ROW_GROUPS
    nblk = t // GLA_BLOCK

    def tok(width, dtype):
        return (pl.BlockSpec((1, t, width), lambda i, j: (i, j, 0)),
                jax.ShapeDtypeStruct((b, l, width), dtype))

    tot = (pl.BlockSpec((1, nblk, GLA_NC, B_KEY_WIDTH), lambda i, j: (i, j, 0, 0)),
           jax.ShapeDtypeStruct((b, l // GLA_BLOCK, GLA_NC, B_KEY_WIDTH), F32))
    kv = (pl.BlockSpec((1, nblk, B_DV, B_KEY_WIDTH), lambda i, j: (i, j, 0, 0)),
          jax.ShapeDtypeStruct((b, l // GLA_BLOCK, B_DV, B_KEY_WIDTH), F32))
    outs = [tok(B_KEY_WIDTH, BF16)] * 6 + [tok(B_VAL_WIDTH, BF16), tot, tot, kv, kv,
                                           tok(A_ROW_GROUPS * A_GROUP_DIM, BF16),
                                           (pl.BlockSpec((1, ncg, GRID_W, t // GRID_W, A_GROUP_DIM),
                                                         lambda i, j: (i, 0, 0, j, 0)),
                                            jax.ShapeDtypeStruct((b, ncg, GRID_W, l // GRID_W, A_GROUP_DIM), F32))]
    const2 = lambda i, j: (0, 0)
    return pl.pallas_call(
        _pre_kernel,
        grid=(b, l // t),
        in_specs=[pl.BlockSpec((1, t, D_MODEL), lambda i, j: (i, j, 0)),
                  pl.BlockSpec((1, 3, D_MODEL), lambda i, j: (i, 0, 0)),
                  pl.BlockSpec((1, D_MODEL), const2),
                  pl.BlockSpec((D_MODEL, nw), const2),
                  pl.BlockSpec((2, B_GATE_RANK, B_KEY_WIDTH), lambda i, j: (0, 0, 0)),
                  pl.BlockSpec((2, B_KEY_WIDTH), const2),
                  pl.BlockSpec((1, A_WIDTH), const2),
                  pl.BlockSpec((1, A_WIDTH), const2)],
        out_specs=[o[0] for o in outs],
        out_shape=[o[1] for o in outs],
        compiler_params=pltpu.CompilerParams(
            dimension_semantics=("parallel", "parallel"), vmem_limit_bytes=VMEM_LIMIT_BYTES),
        name="pre",
    )(x, mod3, ng, w_pre, w2, gb, lg, lb)


def _gscan_kernel(kvf_ref, kvb_ref, totf_ref, totb_ref, s0f_ref, s0b_ref, sf_ref, sb_ref, stf_scr, stb_scr):
    @pl.when(pl.program_id(1) == 0)
    def _():
        stf_scr[...] = s0f_ref[0]
        stb_scr[...] = s0b_ref[0]

    n = kvf_ref.shape[1]
    sf = stf_scr[...]
    sb = stb_scr[...]
    for i in range(n):
        sf_ref[0, i] = sf.T.astype(BF16)
        dec = jnp.exp(jnp.sum(totf_ref[0, i], axis=0, keepdims=True))
        sf = dec * sf + kvf_ref[0, i]
        ib = n - 1 - i
        sb_ref[0, ib] = sb.T.astype(BF16)
        dec = jnp.exp(jnp.sum(totb_ref[0, ib], axis=0, keepdims=True))
        sb = dec * sb + kvb_ref[0, ib]
    stf_scr[...] = sf
    stb_scr[...] = sb


def _gscan_call(kv_f, kv_b, tot_f, tot_b, s0f, s0b):
    b, nb = kv_f.shape[0], kv_f.shape[1]
    g = SCAN_GROUP
    nj = nb // g
    fwd = lambda i, j: (i, j, 0, 0)
    bwd = lambda i, j: (i, nj - 1 - j, 0, 0)
    kv_blk = (1, g, B_DV, B_KEY_WIDTH)
    tot_blk = (1, g, GLA_NC, B_KEY_WIDTH)
    st_spec = pl.BlockSpec((1, B_DV, B_KEY_WIDTH), lambda i, j: (i, 0, 0))
    s_blk = (1, g, B_KEY_WIDTH, B_DV)
    s_shape = jax.ShapeDtypeStruct((b, nb, B_KEY_WIDTH, B_DV), BF16)
    return pl.pallas_call(
        _gscan_kernel,
        grid=(b, nj),
        in_specs=[pl.BlockSpec(kv_blk, fwd), pl.BlockSpec(kv_blk, bwd),
                  pl.BlockSpec(tot_blk, fwd), pl.BlockSpec(tot_blk, bwd), st_spec, st_spec],
        out_specs=[pl.BlockSpec(s_blk, fwd), pl.BlockSpec(s_blk, bwd)],
        out_shape=[s_shape, s_shape],
        scratch_shapes=[pltpu.VMEM((B_DV, B_KEY_WIDTH), F32), pltpu.VMEM((B_DV, B_KEY_WIDTH), F32)],
        compiler_params=pltpu.CompilerParams(
            dimension_semantics=("parallel", "arbitrary"), vmem_limit_bytes=VMEM_LIMIT_BYTES),
        name="gscan",
    )(kv_f, kv_b, tot_f, tot_b, s0f, s0b)


def _scale_rows(x16, scales):
    parts = []
    for c, s in enumerate(scales):
        xc = x16[c * B_CHUNK:(c + 1) * B_CHUNK, :]
        parts.append(xc if s is None else (xc.astype(F32) * s).astype(BF16))
    return parts[0] if len(parts) == 1 else jnp.concatenate(parts, axis=0)


def _exp_sum(tots, idx):
    return jnp.exp(_sum_rows([tots[m] for m in idx])) if idx else None


def _gla_block(qdf, kdf, kcf, qdb, kdb, kcb, v16, tf, tb, sf, sb, p_scr):
    nc = GLA_NC
    ch = B_CHUNK
    hm = [_head_lane_mask(par, BF16) for par in range(2)]

    def lanes(h):
        return slice((h // 2) * LANES, (h // 2 + 1) * LANES)

    row = lax.broadcasted_iota(jnp.int32, (ch, 2 * ch), 0)
    col = lax.broadcasted_iota(jnp.int32, (ch, 2 * ch), 1)
    m_fe = (col < ch) & (col <= row)
    m_be = ((col < ch) & (col >= row)) | (col >= ch)
    m_fo = (col < ch) | (col - ch <= row)
    m_bo = (col >= ch) & (col - ch >= row)
    chunk_even = (lax.broadcasted_iota(jnp.int32, (GLA_BLOCK, 1), 0) // ch) % 2 == 0
    kmix_f = jnp.where(chunk_even, kcf, kdf)
    kmix_b = jnp.where(chunk_even, kdb, kcb)
    for par in range(2):
        kdf_m, kdb_m = kdf * hm[par], kdb * hm[par]
        kmf_m, kmb_m = kmix_f * hm[par], kmix_b * hm[par]
        for h in range(par, B_HEADS, 2):
            ls = lanes(h)
            for i in range(nc // 2):
                pr = slice(2 * i * ch, (2 * i + 2) * ch)
                re = slice(2 * i * ch, (2 * i + 1) * ch)
                ro = slice((2 * i + 1) * ch, (2 * i + 2) * ch)
                even = (jnp.where(m_fe, _dot_nt(qdf[re, ls], kdf_m[pr, ls]), 0.0)
                        + jnp.where(m_be, _dot_nt(qdb[re, ls], kmb_m[pr, ls]), 0.0))
                odd = (jnp.where(m_fo, _dot_nt(qdf[ro, ls], kmf_m[pr, ls]), 0.0)
                       + jnp.where(m_bo, _dot_nt(qdb[ro, ls], kdb_m[pr, ls]), 0.0))
                p_scr[h, re, pr] = even.astype(BF16)
                p_scr[h, ro, pr] = odd.astype(BF16)

    def cross(lo, hi):
        if hi - lo <= 2:
            return
        mid = (lo + hi) // 2
        cross(lo, mid)
        cross(mid, hi)
        rl = slice(lo * ch, mid * ch)
        rh = slice(mid * ch, hi * ch)
        qf = _scale_rows(qdf[rh], [_exp_sum(tf, range(mid, c)) for c in range(mid, hi)])
        kf = _scale_rows(kcf[rl], [_exp_sum(tf, range(c + 1, mid)) for c in range(lo, mid)])
        qb = _scale_rows(qdb[rl], [_exp_sum(tb, range(c + 1, mid)) for c in range(lo, mid)])
        kb = _scale_rows(kcb[rh], [_exp_sum(tb, range(mid, c)) for c in range(mid, hi)])
        for h in range(B_HEADS):
            ls = lanes(h)
            m = hm[h % 2]
            p_scr[h, rh, rl] = _dot_nt(qf[:, ls], (kf * m)[:, ls]).astype(BF16)
            p_scr[h, rl, rh] = _dot_nt(qb[:, ls], (kb * m)[:, ls]).astype(BF16)

    cross(0, nc)

    qsf = _scale_rows(qdf, [_exp_sum(tf, range(0, c)) for c in range(nc)])
    qsb = _scale_rows(qdb, [_exp_sum(tb, range(c + 1, nc)) for c in range(nc)])
    head_row = lax.broadcasted_iota(jnp.int32, (LANES, 1), 0) // B_DK
    outs = []
    for h in range(B_HEADS):
        ls = lanes(h)
        own = (head_row == (h % 2)).astype(BF16)
        lhs = jnp.concatenate([p_scr[h], qsf[:, ls], qsb[:, ls]], axis=1)
        rhs = jnp.concatenate([v16[:, h * B_DV:(h + 1) * B_DV], sf[ls, :] * own, sb[ls, :] * own], axis=0)
        outs.append(jnp.dot(lhs, rhs, preferred_element_type=F32))
    return outs


def _colmix_kernel(ws_ref, bs_ref, vn_ref, o_ref):
    nw = vn_ref.shape[2]
    xs = jnp.concatenate([vn_ref[0, 0, w] for w in range(nw)], axis=1)
    y = _dot(ws_ref[0], xs) + bs_ref[0][:, 0:1]
    for w in range(nw):
        o_ref[0, 0, w] = y[:, w * A_GROUP_DIM:(w + 1) * A_GROUP_DIM]


def _colmix_call(ws_col, bs_col, vn_col, nw):
    b, g, width, rows, ch = vn_col.shape
    blk = pl.BlockSpec((1, 1, nw, rows, ch), lambda i, j, m: (i, j, m, 0, 0))
    return pl.pallas_call(
        _colmix_kernel,
        grid=(b, g, width // nw),
        in_specs=[pl.BlockSpec((1, rows, rows), lambda i, j, m: (j, 0, 0)),
                  pl.BlockSpec((1, rows, A_GROUP_DIM), lambda i, j, m: (j, 0, 0)),
                  blk],
        out_specs=blk,
        out_shape=jax.ShapeDtypeStruct(vn_col.shape, F32),
        compiler_params=pltpu.CompilerParams(
            dimension_semantics=("parallel", "parallel", "parallel"), vmem_limit_bytes=VMEM_LIMIT_BYTES),
        name="colmix",
    )(ws_col, bs_col, vn_col)


def _main_kernel(x_ref, mod_ref, ng_ref, w_ref, vr_ref, sc_ref,
                 qdf_ref, kdf_ref, kcf_ref, qdb_ref, kdb_ref, kcb_ref, v_ref, totf_ref, totb_ref, sf_ref, sb_ref,
                 ws_ref, bs_ref, bg_ref, wpa_ref, wpb_ref, wo_ref, fg_ref, o_ref, acta_scr, on_scr, p_scr):
    x = x_ref[0]
    t = x.shape[0]

    for blk in range(t // GLA_BLOCK):
        rs = slice(blk * GLA_BLOCK, (blk + 1) * GLA_BLOCK)
        tf = [totf_ref[0, blk, c:c + 1, :] for c in range(GLA_NC)]
        tb = [totb_ref[0, blk, c:c + 1, :] for c in range(GLA_NC)]
        o_heads = _gla_block(qdf_ref[0, rs, :], kdf_ref[0, rs, :], kcf_ref[0, rs, :],
                             qdb_ref[0, rs, :], kdb_ref[0, rs, :], kcb_ref[0, rs, :], v_ref[0, rs, :],
                             tf, tb, sf_ref[0, blk], sb_ref[0, blk], p_scr.at[blk])
        for hd in range(B_HEADS):
            on_scr[rs, hd * B_DV:(hd + 1) * B_DV] = _rms_rows(o_heads[hd])

    h = _latent_h(x, mod_ref, ng_ref)
    p = _dot(h, w_ref[...])
    zb = p[:, 0:B_VAL_WIDTH]
    u = p[:, B_VAL_WIDTH:B_VAL_WIDTH + A_WIDTH]
    za = p[:, B_VAL_WIDTH + A_WIDTH:B_VAL_WIDTH + 2 * A_WIDTH]
    gates = jax.nn.sigmoid(p[:, B_VAL_WIDTH + 2 * A_WIDTH:])

    uz = u * _silu(za)
    for g in range(A_GROUPS):
        cs = slice(g * A_GROUP_DIM, (g + 1) * A_GROUP_DIM)
        if g < A_ROW_GROUPS:
            for c in range(t // A_CHUNK):
                rs = slice(c * A_CHUNK, (c + 1) * A_CHUNK)
                sv = _dot(ws_ref[g], vr_ref[0, rs, cs]) + bs_ref[g][:, 0:1]
                acta_scr[rs, cs] = (uz[rs, cs] * sv).astype(BF16)
        else:
            sv = jnp.swapaxes(sc_ref[0, g - A_ROW_GROUPS], 0, 1).reshape(t, A_GROUP_DIM)
            acta_scr[:, cs] = (uz[:, cs] * sv).astype(BF16)
    ya = jnp.dot(acta_scr[...], wpa_ref[...], preferred_element_type=F32)

    yb = _dot(on_scr[...] * (_silu(zb) * bg_ref[...]), wpb_ref[...])

    m = gates[:, 0:D_MODEL] * ya + gates[:, D_MODEL:] * yb
    y = _dot(m, wo_ref[...])
    xo = x + mod_ref[0, 2:3, :] * y
    o_ref[0] = _rms_rows(xo) * fg_ref[...]


def _main_call(x, mod3, ng, w_main, vn_row, sv_col, gla_ops, ws, bs, bg, wpa, wpb, wo, fg, t):
    b, l, _ = x.shape
    nw = w_main.shape[1]
    ncg = A_GROUPS - A_ROW_GROUPS
    const2 = lambda i, j: (0, 0)
    const3 = lambda i, j: (0, 0, 0)
    tok = lambda width: pl.BlockSpec((1, t, width), lambda i, j: (i, j, 0))
    nblk = t // GLA_BLOCK
    tot_spec = pl.BlockSpec((1, nblk, GLA_NC, B_KEY_WIDTH), lambda i, j: (i, j, 0, 0))
    st_spec = pl.BlockSpec((1, nblk, B_KEY_WIDTH, B_DV), lambda i, j: (i, j, 0, 0))
    return pl.pallas_call(
        _main_kernel,
        grid=(b, l // t),
        in_specs=[tok(D_MODEL),
                  pl.BlockSpec((1, 3, D_MODEL), lambda i, j: (i, 0, 0)),
                  pl.BlockSpec((1, D_MODEL), const2),
                  pl.BlockSpec((D_MODEL, nw), const2),
                  tok(A_ROW_GROUPS * A_GROUP_DIM),
                  pl.BlockSpec((1, ncg, GRID_W, t // GRID_W, A_GROUP_DIM), lambda i, j: (i, 0, 0, j, 0)),
                  tok(B_KEY_WIDTH), tok(B_KEY_WIDTH), tok(B_KEY_WIDTH),
                  tok(B_KEY_WIDTH), tok(B_KEY_WIDTH), tok(B_KEY_WIDTH), tok(B_VAL_WIDTH),
                  tot_spec, tot_spec, st_spec, st_spec,
                  pl.BlockSpec((A_GROUPS, A_CHUNK, A_CHUNK), const3),
                  pl.BlockSpec((A_GROUPS, A_CHUNK, A_GROUP_DIM), const3),
                  pl.BlockSpec((1, B_VAL_WIDTH), const2),
                  pl.BlockSpec((A_WIDTH, D_MODEL), const2),
                  pl.BlockSpec((B_VAL_WIDTH, D_MODEL), const2),
                  pl.BlockSpec((D_MODEL, D_MODEL), const2),
                  pl.BlockSpec((1, D_MODEL), const2)],
        out_specs=tok(D_MODEL),
        out_shape=jax.ShapeDtypeStruct((b, l, D_MODEL), F32),
        scratch_shapes=[pltpu.VMEM((t, A_WIDTH), BF16), pltpu.VMEM((t, B_VAL_WIDTH), F32),
                        pltpu.VMEM((nblk, B_HEADS, GLA_BLOCK, GLA_BLOCK), BF16)],
        compiler_params=pltpu.CompilerParams(
            dimension_semantics=("parallel", "parallel"), vmem_limit_bytes=VMEM_LIMIT_BYTES),
        name="main",
    )(x, mod3, ng, w_main, vn_row, sv_col, *gla_ops, ws, bs, bg, wpa, wpb, wo, fg)


def kernel(x, c, ctx, c_ctx, w_mod, b_mod, norm_g, w_in, a_ln_g, a_ln_b, a_ws, a_bs, b_gate_w2, b_gate_b,
           b_norm_g, w_proj_a, w_proj_b, w_out, final_norm_g):
    assert w_mod.shape[0] == 1, "single-layer block"
    b, l, _ = x.shape
    rows = l // GRID_W
    assert rows == A_CHUNK
    wm, bm = w_mod[0], b_mod[0]
    ng = norm_g[0][None, :]

    cc = jnp.zeros((8, D_MODEL), F32).at[0:b].set(c).at[b].set(c_ctx)
    mod = _mod_call(cc, wm, bm[None, :])
    mod3 = mod[0:b].reshape(b, 3, D_MODEL)

    w_pre, w_main, wpa16, wpb16, wo16 = _wprep_call(w_in[0].T, w_proj_a, w_proj_b, w_out)
    w2, gb = b_gate_w2[0], b_gate_b[0]
    s0f, s0b = _ctx_call(ctx, mod, ng, w_pre, w2, gb, b)

    (qd_f, kd_f, kc_f, qd_b, kd_b, kc_b, v16, tot_f, tot_b, kv_f, kv_b, vn_row, vn_col) = _pre_call(
        x, mod3, ng, w_pre, w2, gb, a_ln_g[0][None, :], a_ln_b[0][None, :], 512)

    s_f, s_b = _gscan_call(kv_f, kv_b, tot_f, tot_b, s0f, s0b)
    gla_ops = (qd_f, kd_f, kc_f, qd_b, kd_b, kc_b, v16, tot_f, tot_b, s_f, s_b)

    bs_b = jnp.broadcast_to(a_bs[0][:, :, None], (A_GROUPS, A_CHUNK, A_GROUP_DIM))
    sv_col = _colmix_call(a_ws[0][A_ROW_GROUPS:], bs_b[A_ROW_GROUPS:], vn_col, 16)

    return _main_call(x, mod3, ng, w_main, vn_row, sv_col, gla_ops, a_ws[0], bs_b, b_norm_g[0][None, :],
                      wpa16, wpb16, wo16, final_norm_g[None, :], 512)
```

```python
import functools

import jax
import jax.numpy as jnp
from jax import lax
from jax.experimental import pallas as pl
from jax.experimental.pallas import tpu as pltpu

D_MODEL = 1024
GRID_W = 64
EPS = 1e-6

A_WIDTH = 512
A_GROUPS = 4
A_GROUP_DIM = 128
A_CHUNK = 128
A_ROW_GROUPS = 2

B_HEADS = 4
B_DK = 64
B_DV = 128
B_KEY_WIDTH = 256
B_VAL_WIDTH = 512
B_GATE_RANK = 16
B_GATE_TAU = 16.0
LOG2E = 1.4426950408889634
B_CHUNK = 64

Q0 = 0
K0 = Q0 + B_KEY_WIDTH
V0 = K0 + B_KEY_WIDTH
LR0 = V0 + B_VAL_WIDTH
ZB0 = LR0 + 2 * B_GATE_RANK
UA0 = ZB0 + B_VAL_WIDTH
VA0 = UA0 + A_WIDTH
ZA0 = VA0 + A_WIDTH
G0 = ZA0 + A_WIDTH
IN_WIDTH = G0 + 2 * D_MODEL

LANES = 128

P_LR = 0
P_K = P_LR + LANES
P_Q = P_K + B_KEY_WIDTH
P_V = P_Q + B_KEY_WIDTH
P_VA = P_V + B_VAL_WIDTH
PRE_WIDTH = P_VA + A_WIDTH
GLA_BLOCK = 256
GLA_NC = GLA_BLOCK // B_CHUNK
SCAN_GROUP = 8

VMEM_LIMIT_BYTES = 56 * 1024 * 1024

BF16 = jnp.bfloat16
F32 = jnp.float32


def _dot(a, b):
    return jnp.dot(a.astype(BF16), b.astype(BF16), preferred_element_type=F32)


def _dot_nt(a, b):
    return lax.dot_general(a.astype(BF16), b.astype(BF16), (((1,), (1,)), ((), ())),
                           preferred_element_type=F32)


def _dot_tn(a, b):
    return lax.dot_general(a.astype(BF16), b.astype(BF16), (((0,), (0,)), ((), ())),
                           preferred_element_type=F32)


def _silu(x):
    return x * jax.nn.sigmoid(x)


def _rms_rows(x):
    return x * lax.rsqrt(jnp.mean(x * x, axis=-1, keepdims=True) + EPS)


def _head_lane_mask(parity, dtype):
    lane = lax.broadcasted_iota(jnp.int32, (1, B_KEY_WIDTH), 1)
    return (((lane // B_DK) % 2) == parity).astype(dtype)


def _mod_kernel(c_ref, w_ref, b_ref, o_ref):
    o_ref[...] = _dot(_silu(c_ref[...]), w_ref[...]) + b_ref[...]


def _mod_call(cc, wm, bm):
    n = wm.shape[1]
    tn = 1024
    return pl.pallas_call(
        _mod_kernel,
        grid=(n // tn,),
        in_specs=[pl.BlockSpec((8, D_MODEL), lambda j: (0, 0)),
                  pl.BlockSpec((D_MODEL, tn), lambda j: (0, j)),
                  pl.BlockSpec((1, tn), lambda j: (0, j))],
        out_specs=pl.BlockSpec((8, tn), lambda j: (0, j)),
        out_shape=jax.ShapeDtypeStruct((8, n), F32),
        name="mod",
    )(cc, wm, bm)


def _wprep_kernel(wt_ref, wpa_ref, wpb_ref, wo_ref, wpre_ref, wmain_ref, wpa16_ref, wpb16_ref, wo16_ref):
    def put(dst_ref, c0, r0, n, scale=None):
        for s in range(0, n, 2 * LANES):
            m = min(2 * LANES, n - s)
            blk = wt_ref[r0 + s:r0 + s + m, :].T
            dst_ref[:, c0 + s:c0 + s + m] = (blk if scale is None else blk * scale).astype(BF16)

    put(wpre_ref, P_LR, LR0, LANES)
    put(wpre_ref, P_K, K0, B_KEY_WIDTH)
    put(wpre_ref, P_Q, Q0, B_KEY_WIDTH, B_DK ** -0.5)
    put(wpre_ref, P_V, V0, B_VAL_WIDTH)
    put(wpre_ref, P_VA, VA0, A_WIDTH)
    put(wmain_ref, 0, ZB0, VA0 - ZB0)
    put(wmain_ref, VA0 - ZB0, ZA0, IN_WIDTH - ZA0)
    wpa16_ref[...] = wpa_ref[0].astype(BF16)
    wpb16_ref[...] = wpb_ref[0].astype(BF16)
    wo16_ref[...] = wo_ref[0].astype(BF16)


def _wprep_call(w_in_t, w_proj_a, w_proj_b, w_out):
    steps = 4
    n_main = (VA0 - ZB0) + (IN_WIDTH - ZA0)

    def rows3(a):
        return pl.BlockSpec((1, a.shape[1] // steps, a.shape[2]), lambda i: (0, i, 0))

    def rows2(nrows, ncols):
        return (pl.BlockSpec((nrows // steps, ncols), lambda i: (i, 0)),
                jax.ShapeDtypeStruct((nrows, ncols), BF16))

    outs = [rows2(D_MODEL, PRE_WIDTH), rows2(D_MODEL, n_main), rows2(A_WIDTH, D_MODEL),
            rows2(B_VAL_WIDTH, D_MODEL), rows2(D_MODEL, D_MODEL)]
    return pl.pallas_call(
        _wprep_kernel,
        grid=(steps,),
        in_specs=[pl.BlockSpec((IN_WIDTH, D_MODEL // steps), lambda i: (0, i)),
                  rows3(w_proj_a), rows3(w_proj_b), rows3(w_out)],
        out_specs=[o[0] for o in outs],
        out_shape=[o[1] for o in outs],
        compiler_params=pltpu.CompilerParams(
            dimension_semantics=("parallel",), vmem_limit_bytes=VMEM_LIMIT_BYTES),
        name="wprep",
    )(w_in_t, w_proj_a, w_proj_b, w_out)


def _gate_logs(lr, w2_ref, gb_ref):
    out = []
    for r in range(2):
        logits = _dot(lr[:, r * B_GATE_RANK:(r + 1) * B_GATE_RANK], w2_ref[r]) + gb_ref[r:r + 1, :]
        log_sig = jnp.minimum(logits, 0.0) - jnp.log(1.0 + jnp.exp(-jnp.abs(logits)))
        out.append(log_sig * (LOG2E / B_GATE_TAU))
    return out


def _chunk_tri(reverse):
    i = lax.broadcasted_iota(jnp.int32, (GLA_BLOCK, GLA_BLOCK), 0)
    j = lax.broadcasted_iota(jnp.int32, (GLA_BLOCK, GLA_BLOCK), 1)
    same = (i // B_CHUNK) == (j // B_CHUNK)
    tri = (j >= i) if reverse else (j <= i)
    return (same & tri).astype(BF16)


def _sum_rows(rows):
    acc = rows[0]
    for r in rows[1:]:
        acc = acc + r
    return acc


def _block_cum(a, tri, reverse):
    hi = a.astype(BF16)
    lo = (a - hi.astype(F32)).astype(BF16)
    cum = (jnp.dot(tri, hi, preferred_element_type=F32) + jnp.dot(tri, lo, preferred_element_type=F32))
    last = 0 if reverse else B_CHUNK - 1
    tots = [cum[c * B_CHUNK + last:c * B_CHUNK + last + 1, :] for c in range(GLA_NC)]
    return cum, tots


def _block_keys(k, cum, tots, reverse):
    totb = jnp.concatenate([jnp.broadcast_to(t, (B_CHUNK, B_KEY_WIDTH)) for t in tots], axis=0)
    kdec = k * jnp.exp2(totb - cum)
    later = []
    for c in range(GLA_NC):
        idx = list(range(0, c)) if reverse else list(range(c + 1, GLA_NC))
        if idx:
            later.append(jnp.broadcast_to(jnp.exp2(_sum_rows([tots[m] for m in idx])), (B_CHUNK, B_KEY_WIDTH)))
        else:
            later.append(jnp.ones((B_CHUNK, B_KEY_WIDTH), F32))
    return kdec, kdec * jnp.concatenate(later, axis=0)


def _pre_project(h16, w_ref, want_q):
    lr = jnp.dot(h16, w_ref[:, P_LR:P_LR + LANES], preferred_element_type=F32)[:, 0:2 * B_GATE_RANK]
    k = jnp.dot(h16, w_ref[:, P_K:P_K + B_KEY_WIDTH], preferred_element_type=F32)
    q = jnp.dot(h16, w_ref[:, P_Q:P_Q + B_KEY_WIDTH], preferred_element_type=F32) if want_q else None
    v = jnp.dot(h16, w_ref[:, P_V:P_V + B_VAL_WIDTH], preferred_element_type=F32)
    return lr, k, q, v


def _block_kv_t(v16, kblk):
    k16 = kblk.astype(BF16)
    lane = lax.broadcasted_iota(jnp.int32, (1, B_KEY_WIDTH), 1)
    acc = jnp.zeros((B_DV, B_KEY_WIDTH), F32)
    for h in range(B_HEADS):
        full = _dot_tn(v16[:, h * B_DV:(h + 1) * B_DV], k16)
        acc = acc + jnp.where((lane // B_DK) == h, full, 0.0)
    return acc


def _ctx_kernel(ctx_ref, mod_ref, ng_ref, w_ref, w2_ref, gb_ref, sf_ref, sb_ref, *, ctx_row):
    xc = ctx_ref[0]
    shift = mod_ref[ctx_row:ctx_row + 1, 0:D_MODEL]
    scale = mod_ref[ctx_row:ctx_row + 1, D_MODEL:2 * D_MODEL]
    hc = _rms_rows(xc) * ng_ref[...] * (1.0 + scale) + shift
    lr, k, _, v = _pre_project(hc.astype(BF16), w_ref, False)
    v16 = v.astype(BF16)
    a_f, a_b = _gate_logs(lr, w2_ref, gb_ref)
    for a, reverse, out_ref in ((a_f, False, sf_ref), (a_b, True, sb_ref)):
        cum, tots = _block_cum(a, _chunk_tri(reverse), reverse)
        _, kblk = _block_keys(k, cum, tots, reverse)
        out_ref[0] = _block_kv_t(v16, kblk)


def _ctx_call(ctx, mod, ng, w_kvl, w2, gb, ctx_row):
    b, lc, _ = ctx.shape
    assert lc == GLA_BLOCK, "context length must be one GLA block"
    nw = w_kvl.shape[1]
    st = jax.ShapeDtypeStruct((b, B_DV, B_KEY_WIDTH), F32)
    st_spec = pl.BlockSpec((1, B_DV, B_KEY_WIDTH), lambda i: (i, 0, 0))
    return pl.pallas_call(
        functools.partial(_ctx_kernel, ctx_row=ctx_row),
        grid=(b,),
        in_specs=[pl.BlockSpec((1, lc, D_MODEL), lambda i: (i, 0, 0)),
                  pl.BlockSpec((8, 3 * D_MODEL), lambda i: (0, 0)),
                  pl.BlockSpec((1, D_MODEL), lambda i: (0, 0)),
                  pl.BlockSpec((D_MODEL, nw), lambda i: (0, 0)),
                  pl.BlockSpec((2, B_GATE_RANK, B_KEY_WIDTH), lambda i: (0, 0, 0)),
                  pl.BlockSpec((2, B_KEY_WIDTH), lambda i: (0, 0))],
        out_specs=[st_spec, st_spec],
        out_shape=[st, st],
        name="ctx",
    )(ctx, mod, ng, w_kvl, w2, gb)


def _latent_h(x, mod_ref, ng_ref):
    shift = mod_ref[0, 0:1, :]
    gain = ng_ref[...] * (1.0 + mod_ref[0, 1:2, :])
    return _rms_rows(x) * gain + shift


def _pre_kernel(x_ref, mod_ref, ng_ref, w_ref, w2_ref, gb_ref, lg_ref, lb_ref,
                qdf_ref, kdf_ref, kcf_ref, qdb_ref, kdb_ref, kcb_ref, v_ref,
                totf_ref, totb_ref, kvf_ref, kvb_ref, vr_ref, vc_ref):
    h16 = _latent_h(x_ref[0], mod_ref, ng_ref).astype(BF16)
    nblk = x_ref.shape[1] // GLA_BLOCK
    blocks = [slice(blk * GLA_BLOCK, (blk + 1) * GLA_BLOCK) for blk in range(nblk)]
    dirs = ((False, qdf_ref, kdf_ref, kcf_ref, totf_ref, kvf_ref), (True, qdb_ref, kdb_ref, kcb_ref, totb_ref, kvb_ref))

    lr = jnp.dot(h16, w_ref[:, P_LR:P_LR + LANES], preferred_element_type=F32)[:, 0:2 * B_GATE_RANK]
    k = jnp.dot(h16, w_ref[:, P_K:P_K + B_KEY_WIDTH], preferred_element_type=F32)
    logs = _gate_logs(lr, w2_ref, gb_ref)
    q = jnp.dot(h16, w_ref[:, P_Q:P_Q + B_KEY_WIDTH], preferred_element_type=F32)
    v16 = jnp.dot(h16, w_ref[:, P_V:P_V + B_VAL_WIDTH], preferred_element_type=F32).astype(BF16)
    v_ref[0] = v16

    cums = {}
    for d, (reverse, _, _, _, tot_ref, _) in enumerate(dirs):
        tri = _chunk_tri(reverse)
        for blk, rs in enumerate(blocks):
            cum, tots = _block_cum(logs[d][rs], tri, reverse)
            cums[d, blk] = (cum, tots)
            for c in range(GLA_NC):
                tot_ref[0, blk, c:c + 1, :] = tots[c]

    va = jnp.dot(h16, w_ref[:, P_VA:P_VA + A_WIDTH], preferred_element_type=F32)

    for d, (reverse, qd_ref, kd_ref, kc_ref, _, kv_ref) in enumerate(dirs):
        for blk, rs in enumerate(blocks):
            cum, tots = cums[d, blk]
            kdec, kblk = _block_keys(k[rs], cum, tots, reverse)
            kc_ref[0, rs, :] = kdec.astype(BF16)
            kd_ref[0, rs, :] = (k[rs] * jnp.exp2(-cum)).astype(BF16)
            qd_ref[0, rs, :] = (q[rs] * jnp.exp2(cum)).astype(BF16)
            kv_ref[0, blk] = _block_kv_t(v16[rs], kblk)

    vc = va - jnp.mean(va, axis=-1, keepdims=True)
    vn = vc * lax.rsqrt(jnp.mean(vc * vc, axis=-1, keepdims=True) + EPS) * lg_ref[...] + lb_ref[...]
    nr = A_ROW_GROUPS * A_GROUP_DIM
    vr_ref[0] = vn[:, 0:nr].astype(BF16)
    for g in range(A_GROUPS - A_ROW_GROUPS):
        vg = vn[:, nr + g * A_GROUP_DIM:nr + (g + 1) * A_GROUP_DIM]
        vc_ref[0, g] = jnp.swapaxes(vg.reshape(vg.shape[0] // GRID_W, GRID_W, A_GROUP_DIM), 0, 1)


def _pre_call(x, mod3, ng, w_pre, w2, gb, lg, lb, t):
    b, l, _ = x.shape
    nw = w_pre.shape[1]
    ncg = A_GROUPS - A_ROW_GROUPS
    nblk = t // GLA_BLOCK

    def tok(width, dtype):
        return (pl.BlockSpec((1, t, width), lambda i, j: (i, j, 0)),
                jax.ShapeDtypeStruct((b, l, width), dtype))

    tot = (pl.BlockSpec((1, nblk, GLA_NC, B_KEY_WIDTH), lambda i, j: (i, j, 0, 0)),
           jax.ShapeDtypeStruct((b, l // GLA_BLOCK, GLA_NC, B_KEY_WIDTH), F32))
    kv = (pl.BlockSpec((1, nblk, B_DV, B_KEY_WIDTH), lambda i, j: (i, j, 0, 0)),
          jax.ShapeDtypeStruct((b, l // GLA_BLOCK, B_DV, B_KEY_WIDTH), F32))
    outs = [tok(B_KEY_WIDTH, BF16)] * 6 + [tok(B_VAL_WIDTH, BF16), tot, tot, kv, kv,
                                           tok(A_ROW_GROUPS * A_GROUP_DIM, BF16),
                                           (pl.BlockSpec((1, ncg, GRID_W, t // GRID_W, A_GROUP_DIM),
                                                         lambda i, j: (i, 0, 0, j, 0)),
                                            jax.ShapeDtypeStruct((b, ncg, GRID_W, l // GRID_W, A_GROUP_DIM), F32))]
    const2 = lambda i, j: (0, 0)
    return pl.pallas_call(
        _pre_kernel,
        grid=(b, l // t),
        in_specs=[pl.BlockSpec((1, t, D_MODEL), lambda i, j: (i, j, 0)),
                  pl.BlockSpec((1, 3, D_MODEL), lambda i, j: (i, 0, 0)),
                  pl.BlockSpec((1, D_MODEL), const2),
                  pl.BlockSpec((D_MODEL, nw), const2),
                  pl.BlockSpec((2, B_GATE_RANK, B_KEY_WIDTH), lambda i, j: (0, 0, 0)),
                  pl.BlockSpec((2, B_KEY_WIDTH), const2),
                  pl.BlockSpec((1, A_WIDTH), const2),
                  pl.BlockSpec((1, A_WIDTH), const2)],
        out_specs=[o[0] for o in outs],
        out_shape=[o[1] for o in outs],
        compiler_params=pltpu.CompilerParams(
            dimension_semantics=("parallel", "parallel"), vmem_limit_bytes=VMEM_LIMIT_BYTES),
        name="pre",
    )(x, mod3, ng, w_pre, w2, gb, lg, lb)


def _gscan_kernel(kvf_ref, kvb_ref, totf_ref, totb_ref, s0f_ref, s0b_ref, sf_ref, sb_ref, stf_scr, stb_scr):
    @pl.when(pl.program_id(1) == 0)
    def _():
        stf_scr[...] = s0f_ref[0]
        stb_scr[...] = s0b_ref[0]

    n = kvf_ref.shape[1]
    sf = stf_scr[...]
    sb = stb_scr[...]
    for i in range(n):
        sf_ref[0, i] = sf.T.astype(BF16)
        dec = jnp.exp2(jnp.sum(totf_ref[0, i], axis=0, keepdims=True))
        sf = dec * sf + kvf_ref[0, i]
        ib = n - 1 - i
        sb_ref[0, ib] = sb.T.astype(BF16)
        dec = jnp.exp2(jnp.sum(totb_ref[0, ib], axis=0, keepdims=True))
        sb = dec * sb + kvb_ref[0, ib]
    stf_scr[...] = sf
    stb_scr[...] = sb


def _gscan_call(kv_f, kv_b, tot_f, tot_b, s0f, s0b):
    b, nb = kv_f.shape[0], kv_f.shape[1]
    g = SCAN_GROUP
    nj = nb // g
    fwd = lambda i, j: (i, j, 0, 0)
    bwd = lambda i, j: (i, nj - 1 - j, 0, 0)
    kv_blk = (1, g, B_DV, B_KEY_WIDTH)
    tot_blk = (1, g, GLA_NC, B_KEY_WIDTH)
    st_spec = pl.BlockSpec((1, B_DV, B_KEY_WIDTH), lambda i, j: (i, 0, 0))
    s_blk = (1, g, B_KEY_WIDTH, B_DV)
    s_shape = jax.ShapeDtypeStruct((b, nb, B_KEY_WIDTH, B_DV), BF16)
    return pl.pallas_call(
        _gscan_kernel,
        grid=(b, nj),
        in_specs=[pl.BlockSpec(kv_blk, fwd), pl.BlockSpec(kv_blk, bwd),
                  pl.BlockSpec(tot_blk, fwd), pl.BlockSpec(tot_blk, bwd), st_spec, st_spec],
        out_specs=[pl.BlockSpec(s_blk, fwd), pl.BlockSpec(s_blk, bwd)],
        out_shape=[s_shape, s_shape],
        scratch_shapes=[pltpu.VMEM((B_DV, B_KEY_WIDTH), F32), pltpu.VMEM((B_DV, B_KEY_WIDTH), F32)],
        compiler_params=pltpu.CompilerParams(
            dimension_semantics=("parallel", "arbitrary"), vmem_limit_bytes=VMEM_LIMIT_BYTES),
        name="gscan",
    )(kv_f, kv_b, tot_f, tot_b, s0f, s0b)


def _scale_rows(x16, scales):
    parts = []
    for c, s in enumerate(scales):
        xc = x16[c * B_CHUNK:(c + 1) * B_CHUNK, :]
        parts.append(xc if s is None else (xc.astype(F32) * s).astype(BF16))
    return parts[0] if len(parts) == 1 else jnp.concatenate(parts, axis=0)


def _exp_sum(tots, idx):
    return jnp.exp2(_sum_rows([tots[m] for m in idx])) if idx else None


def _gla_block(qdf, kdf, kcf, qdb, kdb, kcb, v16, tf, tb, sf, sb, p_scr):
    nc = GLA_NC
    ch = B_CHUNK
    hm = [_head_lane_mask(par, BF16) for par in range(2)]

    def lanes(h):
        return slice((h // 2) * LANES, (h // 2 + 1) * LANES)

    row = lax.broadcasted_iota(jnp.int32, (ch, 2 * ch), 0)
    col = lax.broadcasted_iota(jnp.int32, (ch, 2 * ch), 1)
    m_fe = (col < ch) & (col <= row)
    m_be = ((col < ch) & (col >= row)) | (col >= ch)
    m_fo = (col < ch) | (col - ch <= row)
    m_bo = (col >= ch) & (col - ch >= row)
    chunk_even = (lax.broadcasted_iota(jnp.int32, (GLA_BLOCK, 1), 0) // ch) % 2 == 0
    kmix_f = jnp.where(chunk_even, kcf, kdf)
    kmix_b = jnp.where(chunk_even, kdb, kcb)
    for par in range(2):
        kdf_m, kdb_m = kdf * hm[par], kdb * hm[par]
        kmf_m, kmb_m = kmix_f * hm[par], kmix_b * hm[par]
        for h in range(par, B_HEADS, 2):
            ls = lanes(h)
            for i in range(nc // 2):
                pr = slice(2 * i * ch, (2 * i + 2) * ch)
                re = slice(2 * i * ch, (2 * i + 1) * ch)
                ro = slice((2 * i + 1) * ch, (2 * i + 2) * ch)
                even = (jnp.where(m_fe, _dot_nt(qdf[re, ls], kdf_m[pr, ls]), 0.0)
                        + jnp.where(m_be, _dot_nt(qdb[re, ls], kmb_m[pr, ls]), 0.0))
                odd = (jnp.where(m_fo, _dot_nt(qdf[ro, ls], kmf_m[pr, ls]), 0.0)
                       + jnp.where(m_bo, _dot_nt(qdb[ro, ls], kdb_m[pr, ls]), 0.0))
                p_scr[h, re, pr] = even.astype(BF16)
                p_scr[h, ro, pr] = odd.astype(BF16)

    def cross(lo, hi):
        if hi - lo <= 2:
            return
        mid = (lo + hi) // 2
        cross(lo, mid)
        cross(mid, hi)
        rl = slice(lo * ch, mid * ch)
        rh = slice(mid * ch, hi * ch)
        qf = _scale_rows(qdf[rh], [_exp_sum(tf, range(mid, c)) for c in range(mid, hi)])
        kf = _scale_rows(kcf[rl], [_exp_sum(tf, range(c + 1, mid)) for c in range(lo, mid)])
        qb = _scale_rows(qdb[rl], [_exp_sum(tb, range(c + 1, mid)) for c in range(lo, mid)])
        kb = _scale_rows(kcb[rh], [_exp_sum(tb, range(mid, c)) for c in range(mid, hi)])
        for h in range(B_HEADS):
            ls = lanes(h)
            m = hm[h % 2]
            p_scr[h, rh, rl] = _dot_nt(qf[:, ls], (kf * m)[:, ls]).astype(BF16)
            p_scr[h, rl, rh] = _dot_nt(qb[:, ls], (kb * m)[:, ls]).astype(BF16)

    cross(0, nc)

    qsf = _scale_rows(qdf, [_exp_sum(tf, range(0, c)) for c in range(nc)])
    qsb = _scale_rows(qdb, [_exp_sum(tb, range(c + 1, nc)) for c in range(nc)])
    head_row = lax.broadcasted_iota(jnp.int32, (LANES, 1), 0) // B_DK
    outs = []
    for h in range(B_HEADS):
        ls = lanes(h)
        own = (head_row == (h % 2)).astype(BF16)
        lhs = jnp.concatenate([p_scr[h], qsf[:, ls], qsb[:, ls]], axis=1)
        rhs = jnp.concatenate([v16[:, h * B_DV:(h + 1) * B_DV], sf[ls, :] * own, sb[ls, :] * own], axis=0)
        outs.append(jnp.dot(lhs, rhs, preferred_element_type=F32))
    return outs


def _colmix_kernel(ws_ref, bs_ref, vn_ref, o_ref):
    nw = vn_ref.shape[2]
    xs = jnp.concatenate([vn_ref[0, 0, w] for w in range(nw)], axis=1)
    y = _dot(ws_ref[0], xs) + bs_ref[0][:, 0:1]
    for w in range(nw):
        o_ref[0, 0, w] = y[:, w * A_GROUP_DIM:(w + 1) * A_GROUP_DIM]


def _colmix_call(ws_col, bs_col, vn_col, nw):
    b, g, width, rows, ch = vn_col.shape
    blk = pl.BlockSpec((1, 1, nw, rows, ch), lambda i, j, m: (i, j, m, 0, 0))
    return pl.pallas_call(
        _colmix_kernel,
        grid=(b, g, width // nw),
        in_specs=[pl.BlockSpec((1, rows, rows), lambda i, j, m: (j, 0, 0)),
                  pl.BlockSpec((1, rows, A_GROUP_DIM), lambda i, j, m: (j, 0, 0)),
                  blk],
        out_specs=blk,
        out_shape=jax.ShapeDtypeStruct(vn_col.shape, F32),
        compiler_params=pltpu.CompilerParams(
            dimension_semantics=("parallel", "parallel", "parallel"), vmem_limit_bytes=VMEM_LIMIT_BYTES),
        name="colmix",
    )(ws_col, bs_col, vn_col)


def _main_kernel(x_ref, mod_ref, ng_ref, w_ref, vr_ref, sc_ref,
                 qdf_ref, kdf_ref, kcf_ref, qdb_ref, kdb_ref, kcb_ref, v_ref, totf_ref, totb_ref, sf_ref, sb_ref,
                 ws_ref, bs_ref, bg_ref, wpa_ref, wpb_ref, wo_ref, fg_ref, o_ref, acta_scr, on_scr, p_scr):
    t = x_ref.shape[1]
    blocks = [slice(blk * GLA_BLOCK, (blk + 1) * GLA_BLOCK) for blk in range(t // GLA_BLOCK)]

    def gla(blk):
        rs = blocks[blk]
        tf = [totf_ref[0, blk, c:c + 1, :] for c in range(GLA_NC)]
        tb = [totb_ref[0, blk, c:c + 1, :] for c in range(GLA_NC)]
        o_heads = _gla_block(qdf_ref[0, rs, :], kdf_ref[0, rs, :], kcf_ref[0, rs, :],
                             qdb_ref[0, rs, :], kdb_ref[0, rs, :], kcb_ref[0, rs, :], v_ref[0, rs, :],
                             tf, tb, sf_ref[0, blk], sb_ref[0, blk], p_scr.at[blk])
        for hd in range(B_HEADS):
            on_scr[rs, hd * B_DV:(hd + 1) * B_DV] = _rms_rows(o_heads[hd])

    def proj(c0, n):
        return jnp.dot(h16, w_ref[:, c0:c0 + n], preferred_element_type=F32)

    gla(0)
    h16 = _latent_h(x_ref[0], mod_ref, ng_ref).astype(BF16)
    zb = proj(0, B_VAL_WIDTH)
    u = proj(B_VAL_WIDTH, A_WIDTH)
    za = proj(B_VAL_WIDTH + A_WIDTH, A_WIDTH)
    for blk in range(1, len(blocks)):
        gla(blk)
    gate_a = jax.nn.sigmoid(proj(B_VAL_WIDTH + 2 * A_WIDTH, D_MODEL))

    uz = u * _silu(za)
    for g in range(A_GROUPS):
        cs = slice(g * A_GROUP_DIM, (g + 1) * A_GROUP_DIM)
        if g < A_ROW_GROUPS:
            for c in range(t // A_CHUNK):
                rs = slice(c * A_CHUNK, (c + 1) * A_CHUNK)
                sv = _dot(ws_ref[g], vr_ref[0, rs, cs]) + bs_ref[g][:, 0:1]
                acta_scr[rs, cs] = (uz[rs, cs] * sv).astype(BF16)
        else:
            sv = jnp.swapaxes(sc_ref[0, g - A_ROW_GROUPS], 0, 1).reshape(t, A_GROUP_DIM)
            acta_scr[:, cs] = (uz[:, cs] * sv).astype(BF16)
    ya = jnp.dot(acta_scr[...], wpa_ref[...], preferred_element_type=F32)

    gate_b = jax.nn.sigmoid(proj(B_VAL_WIDTH + 2 * A_WIDTH + D_MODEL, D_MODEL))
    yb = _dot(on_scr[...] * (_silu(zb) * bg_ref[...]), wpb_ref[...])

    m = (gate_a * ya + gate_b * yb).astype(BF16)
    for rs in blocks:
        y = jnp.dot(m[rs], wo_ref[...], preferred_element_type=F32)
        xo = x_ref[0, rs, :] + mod_ref[0, 2:3, :] * y
        o_ref[0, rs, :] = _rms_rows(xo) * fg_ref[...]


def _main_call(x, mod3, ng, w_main, vn_row, sv_col, gla_ops, ws, bs, bg, wpa, wpb, wo, fg, t):
    b, l, _ = x.shape
    nw = w_main.shape[1]
    ncg = A_GROUPS - A_ROW_GROUPS
    const2 = lambda i, j: (0, 0)
    const3 = lambda i, j: (0, 0, 0)
    tok = lambda width: pl.BlockSpec((1, t, width), lambda i, j: (i, j, 0))
    nblk = t // GLA_BLOCK
    tot_spec = pl.BlockSpec((1, nblk, GLA_NC, B_KEY_WIDTH), lambda i, j: (i, j, 0, 0))
    st_spec = pl.BlockSpec((1, nblk, B_KEY_WIDTH, B_DV), lambda i, j: (i, j, 0, 0))
    return pl.pallas_call(
        _main_kernel,
        grid=(b, l // t),
        in_specs=[tok(D_MODEL),
                  pl.BlockSpec((1, 3, D_MODEL), lambda i, j: (i, 0, 0)),
                  pl.BlockSpec((1, D_MODEL), const2),
                  pl.BlockSpec((D_MODEL, nw), const2),
                  tok(A_ROW_GROUPS * A_GROUP_DIM),
                  pl.BlockSpec((1, ncg, GRID_W, t // GRID_W, A_GROUP_DIM), lambda i, j: (i, 0, 0, j, 0)),
                  tok(B_KEY_WIDTH), tok(B_KEY_WIDTH), tok(B_KEY_WIDTH),
                  tok(B_KEY_WIDTH), tok(B_KEY_WIDTH), tok(B_KEY_WIDTH), tok(B_VAL_WIDTH),
                  tot_spec, tot_spec, st_spec, st_spec,
                  pl.BlockSpec((A_GROUPS, A_CHUNK, A_CHUNK), const3),
                  pl.BlockSpec((A_GROUPS, A_CHUNK, A_GROUP_DIM), const3),
                  pl.BlockSpec((1, B_VAL_WIDTH), const2),
                  pl.BlockSpec((A_WIDTH, D_MODEL), const2),
                  pl.BlockSpec((B_VAL_WIDTH, D_MODEL), const2),
                  pl.BlockSpec((D_MODEL, D_MODEL), const2),
                  pl.BlockSpec((1, D_MODEL), const2)],
        out_specs=tok(D_MODEL),
        out_shape=jax.ShapeDtypeStruct((b, l, D_MODEL), F32),
        scratch_shapes=[pltpu.VMEM((t, A_WIDTH), BF16), pltpu.VMEM((t, B_VAL_WIDTH), F32),
                        pltpu.VMEM((nblk, B_HEADS, GLA_BLOCK, GLA_BLOCK), BF16)],
        compiler_params=pltpu.CompilerParams(
            dimension_semantics=("parallel", "parallel"), vmem_limit_bytes=VMEM_LIMIT_BYTES),
        name="main",
    )(x, mod3, ng, w_main, vn_row, sv_col, *gla_ops, ws, bs, bg, wpa, wpb, wo, fg)


def kernel(x, c, ctx, c_ctx, w_mod, b_mod, norm_g, w_in, a_ln_g, a_ln_b, a_ws, a_bs, b_gate_w2, b_gate_b,
           b_norm_g, w_proj_a, w_proj_b, w_out, final_norm_g):
    assert w_mod.shape[0] == 1, "single-layer block"
    b, l, _ = x.shape
    rows = l // GRID_W
    assert rows == A_CHUNK
    wm, bm = w_mod[0], b_mod[0]
    ng = norm_g[0][None, :]

    cc = jnp.zeros((8, D_MODEL), F32).at[0:b].set(c).at[b].set(c_ctx)
    mod = _mod_call(cc, wm, bm[None, :])
    mod3 = mod[0:b].reshape(b, 3, D_MODEL)

    w_pre, w_main, wpa16, wpb16, wo16 = _wprep_call(w_in[0].T, w_proj_a, w_proj_b, w_out)
    w2, gb = b_gate_w2[0], b_gate_b[0]
    s0f, s0b = _ctx_call(ctx, mod, ng, w_pre, w2, gb, b)

    (qd_f, kd_f, kc_f, qd_b, kd_b, kc_b, v16, tot_f, tot_b, kv_f, kv_b, vn_row, vn_col) = _pre_call(
        x, mod3, ng, w_pre, w2, gb, a_ln_g[0][None, :], a_ln_b[0][None, :], 512)

    s_f, s_b = _gscan_call(kv_f, kv_b, tot_f, tot_b, s0f, s0b)
    gla_ops = (qd_f, kd_f, kc_f, qd_b, kd_b, kc_b, v16, tot_f, tot_b, s_f, s_b)

    bs_b = jnp.broadcast_to(a_bs[0][:, :, None], (A_GROUPS, A_CHUNK, A_GROUP_DIM))
    sv_col = _colmix_call(a_ws[0][A_ROW_GROUPS:], bs_b[A_ROW_GROUPS:], vn_col, 16)

    return _main_call(x, mod3, ng, w_main, vn_row, sv_col, gla_ops, a_ws[0], bs_b, b_norm_g[0][None, :],
                      wpa16, wpb16, wo16, final_norm_g[None, :], 512)
```

```python
import functools

import jax
import jax.numpy as jnp
from jax import lax
from jax.experimental import pallas as pl
from jax.experimental.pallas import tpu as pltpu

D_MODEL = 1024
GRID_W = 64
EPS = 1e-6

A_WIDTH = 512
A_GROUPS = 4
A_GROUP_DIM = 128
A_CHUNK = 128
A_ROW_GROUPS = 2

B_HEADS = 4
B_DK = 64
B_DV = 128
B_KEY_WIDTH = 256
B_VAL_WIDTH = 512
B_GATE_RANK = 16
B_GATE_TAU = 16.0
LOG2E = 1.4426950408889634
B_CHUNK = 64

Q0 = 0
K0 = Q0 + B_KEY_WIDTH
V0 = K0 + B_KEY_WIDTH
LR0 = V0 + B_VAL_WIDTH
ZB0 = LR0 + 2 * B_GATE_RANK
UA0 = ZB0 + B_VAL_WIDTH
VA0 = UA0 + A_WIDTH
ZA0 = VA0 + A_WIDTH
G0 = ZA0 + A_WIDTH
IN_WIDTH = G0 + 2 * D_MODEL

LANES = 128

P_LR = 0
P_K = P_LR + LANES
P_Q = P_K + B_KEY_WIDTH
P_V = P_Q + B_KEY_WIDTH
P_VA = P_V + B_VAL_WIDTH
PRE_WIDTH = P_VA + A_WIDTH
GLA_BLOCK = 256
GLA_NC = GLA_BLOCK // B_CHUNK
SCAN_GROUP = 8

VMEM_LIMIT_BYTES = 56 * 1024 * 1024

BF16 = jnp.bfloat16
F32 = jnp.float32


def _dot(a, b):
    return jnp.dot(a.astype(BF16), b.astype(BF16), preferred_element_type=F32)


def _dot_nt(a, b):
    return lax.dot_general(a.astype(BF16), b.astype(BF16), (((1,), (1,)), ((), ())),
                           preferred_element_type=F32)


def _dot_tn(a, b):
    return lax.dot_general(a.astype(BF16), b.astype(BF16), (((0,), (0,)), ((), ())),
                           preferred_element_type=F32)


def _silu(x):
    return x * jax.nn.sigmoid(x)


def _rms_rows(x):
    return x * lax.rsqrt(jnp.mean(x * x, axis=-1, keepdims=True) + EPS)


def _head_lane_mask(parity, dtype):
    lane = lax.broadcasted_iota(jnp.int32, (1, B_KEY_WIDTH), 1)
    return (((lane // B_DK) % 2) == parity).astype(dtype)


def _wprep_kernel(wt_ref, wpa_ref, wpb_ref, wo_ref, c_ref, wm_ref, bm_ref,
                  wpre_ref, wmain_ref, wpa16_ref, wpb16_ref, wo16_ref, mod_ref):
    def put(dst_ref, c0, r0, n, scale=None):
        for s in range(0, n, 2 * LANES):
            m = min(2 * LANES, n - s)
            blk = wt_ref[r0 + s:r0 + s + m, :].T
            dst_ref[:, c0 + s:c0 + s + m] = (blk if scale is None else blk * scale).astype(BF16)

    put(wpre_ref, P_LR, LR0, LANES)
    put(wpre_ref, P_K, K0, B_KEY_WIDTH)
    put(wpre_ref, P_Q, Q0, B_KEY_WIDTH, B_DK ** -0.5)
    put(wpre_ref, P_V, V0, B_VAL_WIDTH)
    put(wpre_ref, P_VA, VA0, A_WIDTH)
    put(wmain_ref, 0, ZB0, VA0 - ZB0)
    put(wmain_ref, VA0 - ZB0, ZA0, IN_WIDTH - ZA0)
    wpa16_ref[...] = wpa_ref[0].astype(BF16)
    wpb16_ref[...] = wpb_ref[0].astype(BF16)
    wo16_ref[...] = wo_ref[0].astype(BF16)
    mod_ref[...] = _dot(_silu(c_ref[...]), wm_ref[0]) + bm_ref[...]


def _wprep_call(w_in_t, w_proj_a, w_proj_b, w_out, cc, w_mod, b_mod):
    steps = 4
    n_main = (VA0 - ZB0) + (IN_WIDTH - ZA0)

    def rows3(a):
        return pl.BlockSpec((1, a.shape[1] // steps, a.shape[2]), lambda i: (0, i, 0))

    def rows2(nrows, ncols):
        return (pl.BlockSpec((nrows // steps, ncols), lambda i: (i, 0)),
                jax.ShapeDtypeStruct((nrows, ncols), BF16))

    n_mod = w_mod.shape[2]
    outs = [rows2(D_MODEL, PRE_WIDTH), rows2(D_MODEL, n_main), rows2(A_WIDTH, D_MODEL),
            rows2(B_VAL_WIDTH, D_MODEL), rows2(D_MODEL, D_MODEL),
            (pl.BlockSpec((8, n_mod // steps), lambda i: (0, i)), jax.ShapeDtypeStruct((8, n_mod), F32))]
    return pl.pallas_call(
        _wprep_kernel,
        grid=(steps,),
        in_specs=[pl.BlockSpec((IN_WIDTH, D_MODEL // steps), lambda i: (0, i)),
                  rows3(w_proj_a), rows3(w_proj_b), rows3(w_out),
                  pl.BlockSpec((8, D_MODEL), lambda i: (0, 0)),
                  pl.BlockSpec((1, D_MODEL, n_mod // steps), lambda i: (0, 0, i)),
                  pl.BlockSpec((1, n_mod // steps), lambda i: (0, i))],
        out_specs=[o[0] for o in outs],
        out_shape=[o[1] for o in outs],
        compiler_params=pltpu.CompilerParams(
            dimension_semantics=("parallel",), vmem_limit_bytes=VMEM_LIMIT_BYTES),
        name="wprep",
    )(w_in_t, w_proj_a, w_proj_b, w_out, cc, w_mod, b_mod)


def _gate_logs(lr, w2_ref, gb_ref):
    out = []
    for r in range(2):
        logits = _dot(lr[:, r * B_GATE_RANK:(r + 1) * B_GATE_RANK], w2_ref[r]) + gb_ref[r:r + 1, :]
        log_sig = jnp.minimum(logits, 0.0) - jnp.log(1.0 + jnp.exp(-jnp.abs(logits)))
        out.append(log_sig * (LOG2E / B_GATE_TAU))
    return out


def _chunk_tri(reverse):
    i = lax.broadcasted_iota(jnp.int32, (GLA_BLOCK, GLA_BLOCK), 0)
    j = lax.broadcasted_iota(jnp.int32, (GLA_BLOCK, GLA_BLOCK), 1)
    same = (i // B_CHUNK) == (j // B_CHUNK)
    tri = (j >= i) if reverse else (j <= i)
    return (same & tri).astype(BF16)


def _sum_rows(rows):
    acc = rows[0]
    for r in rows[1:]:
        acc = acc + r
    return acc


def _block_cum(a, tri, reverse):
    hi = a.astype(BF16)
    lo = (a - hi.astype(F32)).astype(BF16)
    cum = (jnp.dot(tri, hi, preferred_element_type=F32) + jnp.dot(tri, lo, preferred_element_type=F32))
    last = 0 if reverse else B_CHUNK - 1
    tots = [cum[c * B_CHUNK + last:c * B_CHUNK + last + 1, :] for c in range(GLA_NC)]
    return cum, tots


def _block_keys(k, cum, tots, reverse):
    totb = jnp.concatenate([jnp.broadcast_to(t, (B_CHUNK, B_KEY_WIDTH)) for t in tots], axis=0)
    kdec = k * jnp.exp2(totb - cum)
    later = []
    for c in range(GLA_NC):
        idx = list(range(0, c)) if reverse else list(range(c + 1, GLA_NC))
        if idx:
            later.append(jnp.broadcast_to(jnp.exp2(_sum_rows([tots[m] for m in idx])), (B_CHUNK, B_KEY_WIDTH)))
        else:
            later.append(jnp.ones((B_CHUNK, B_KEY_WIDTH), F32))
    return kdec, kdec * jnp.concatenate(later, axis=0)


def _pre_project(h16, w_ref, want_q):
    lr = jnp.dot(h16, w_ref[:, P_LR:P_LR + LANES], preferred_element_type=F32)[:, 0:2 * B_GATE_RANK]
    k = jnp.dot(h16, w_ref[:, P_K:P_K + B_KEY_WIDTH], preferred_element_type=F32)
    q = jnp.dot(h16, w_ref[:, P_Q:P_Q + B_KEY_WIDTH], preferred_element_type=F32) if want_q else None
    v = jnp.dot(h16, w_ref[:, P_V:P_V + B_VAL_WIDTH], preferred_element_type=F32)
    return lr, k, q, v


def _block_kv_t(v16, kblk):
    k16 = kblk.astype(BF16)
    lane = lax.broadcasted_iota(jnp.int32, (1, B_KEY_WIDTH), 1)
    acc = jnp.zeros((B_DV, B_KEY_WIDTH), F32)
    for h in range(B_HEADS):
        full = _dot_tn(v16[:, h * B_DV:(h + 1) * B_DV], k16)
        acc = acc + jnp.where((lane // B_DK) == h, full, 0.0)
    return acc


def _ctx_kernel(ctx_ref, mod_ref, ng_ref, w_ref, w2_ref, gb_ref, sf_ref, sb_ref, *, ctx_row):
    xc = ctx_ref[0]
    shift = mod_ref[ctx_row:ctx_row + 1, 0:D_MODEL]
    scale = mod_ref[ctx_row:ctx_row + 1, D_MODEL:2 * D_MODEL]
    hc = _rms_rows(xc) * ng_ref[...] * (1.0 + scale) + shift
    lr, k, _, v = _pre_project(hc.astype(BF16), w_ref, False)
    v16 = v.astype(BF16)
    a_f, a_b = _gate_logs(lr, w2_ref, gb_ref)
    for a, reverse, out_ref in ((a_f, False, sf_ref), (a_b, True, sb_ref)):
        cum, tots = _block_cum(a, _chunk_tri(reverse), reverse)
        _, kblk = _block_keys(k, cum, tots, reverse)
        out_ref[0] = _block_kv_t(v16, kblk)


def _ctx_call(ctx, mod, ng, w_kvl, w2, gb, ctx_row):
    b, lc, _ = ctx.shape
    assert lc == GLA_BLOCK, "context length must be one GLA block"
    nw = w_kvl.shape[1]
    st = jax.ShapeDtypeStruct((b, B_DV, B_KEY_WIDTH), F32)
    st_spec = pl.BlockSpec((1, B_DV, B_KEY_WIDTH), lambda i: (i, 0, 0))
    return pl.pallas_call(
        functools.partial(_ctx_kernel, ctx_row=ctx_row),
        grid=(b,),
        in_specs=[pl.BlockSpec((1, lc, D_MODEL), lambda i: (i, 0, 0)),
                  pl.BlockSpec((8, 3 * D_MODEL), lambda i: (0, 0)),
                  pl.BlockSpec((1, D_MODEL), lambda i: (0, 0)),
                  pl.BlockSpec((D_MODEL, nw), lambda i: (0, 0)),
                  pl.BlockSpec((2, B_GATE_RANK, B_KEY_WIDTH), lambda i: (0, 0, 0)),
                  pl.BlockSpec((2, B_KEY_WIDTH), lambda i: (0, 0))],
        out_specs=[st_spec, st_spec],
        out_shape=[st, st],
        name="ctx",
    )(ctx, mod, ng, w_kvl, w2, gb)


def _latent_h(x, mod_ref, ng_ref):
    shift = mod_ref[0, 0:1, :]
    gain = ng_ref[...] * (1.0 + mod_ref[0, 1:2, :])
    return _rms_rows(x) * gain + shift


def _pre_kernel(x_ref, mod_ref, ng_ref, w_ref, w2_ref, gb_ref, lg_ref, lb_ref,
                qdf_ref, kdf_ref, kcf_ref, qdb_ref, kdb_ref, kcb_ref, v_ref,
                totf_ref, totb_ref, kvf_ref, kvb_ref, vr_ref, vc_ref):
    h16 = _latent_h(x_ref[0], mod_ref, ng_ref).astype(BF16)
    nblk = x_ref.shape[1] // GLA_BLOCK
    blocks = [slice(blk * GLA_BLOCK, (blk + 1) * GLA_BLOCK) for blk in range(nblk)]
    dirs = ((False, qdf_ref, kdf_ref, kcf_ref, totf_ref, kvf_ref), (True, qdb_ref, kdb_ref, kcb_ref, totb_ref, kvb_ref))

    lr = jnp.dot(h16, w_ref[:, P_LR:P_LR + LANES], preferred_element_type=F32)[:, 0:2 * B_GATE_RANK]
    k = jnp.dot(h16, w_ref[:, P_K:P_K + B_KEY_WIDTH], preferred_element_type=F32)
    logs = _gate_logs(lr, w2_ref, gb_ref)
    q = jnp.dot(h16, w_ref[:, P_Q:P_Q + B_KEY_WIDTH], preferred_element_type=F32)
    v16 = jnp.dot(h16, w_ref[:, P_V:P_V + B_VAL_WIDTH], preferred_element_type=F32).astype(BF16)
    v_ref[0] = v16

    cums = {}
    for d, (reverse, _, _, _, tot_ref, _) in enumerate(dirs):
        tri = _chunk_tri(reverse)
        for blk, rs in enumerate(blocks):
            cum, tots = _block_cum(logs[d][rs], tri, reverse)
            cums[d, blk] = (cum, tots)
            for c in range(GLA_NC):
                tot_ref[0, blk, c:c + 1, :] = tots[c]

    va = jnp.dot(h16, w_ref[:, P_VA:P_VA + A_WIDTH], preferred_element_type=F32)

    for d, (reverse, qd_ref, kd_ref, kc_ref, _, kv_ref) in enumerate(dirs):
        for blk, rs in enumerate(blocks):
            cum, tots = cums[d, blk]
            kdec, kblk = _block_keys(k[rs], cum, tots, reverse)
            kc_ref[0, rs, :] = kdec.astype(BF16)
            kd_ref[0, rs, :] = (k[rs] * jnp.exp2(-cum)).astype(BF16)
            qd_ref[0, rs, :] = (q[rs] * jnp.exp2(cum)).astype(BF16)
            kv_ref[0, blk] = _block_kv_t(v16[rs], kblk)

    vc = va - jnp.mean(va, axis=-1, keepdims=True)
    vn = vc * lax.rsqrt(jnp.mean(vc * vc, axis=-1, keepdims=True) + EPS) * lg_ref[...] + lb_ref[...]
    nr = A_ROW_GROUPS * A_GROUP_DIM
    vr_ref[0] = vn[:, 0:nr].astype(BF16)
    for g in range(A_GROUPS - A_ROW_GROUPS):
        vg = vn[:, nr + g * A_GROUP_DIM:nr + (g + 1) * A_GROUP_DIM]
        vc_ref[0, g] = jnp.swapaxes(vg.reshape(vg.shape[0] // GRID_W, GRID_W, A_GROUP_DIM), 0, 1)


def _pre_call(x, mod3, ng, w_pre, w2, gb, lg, lb, t):
    b, l, _ = x.shape
    nw = w_pre.shape[1]
    ncg = A_GROUPS - A_ROW_GROUPS
    nblk = t // GLA_BLOCK

    def tok(width, dtype):
        return (pl.BlockSpec((1, t, width), lambda i, j: (i, j, 0)),
                jax.ShapeDtypeStruct((b, l, width), dtype))

    tot = (pl.BlockSpec((1, nblk, GLA_NC, B_KEY_WIDTH), lambda i, j: (i, j, 0, 0)),
           jax.ShapeDtypeStruct((b, l // GLA_BLOCK, GLA_NC, B_KEY_WIDTH), F32))
    kv = (pl.BlockSpec((1, nblk, B_DV, B_KEY_WIDTH), lambda i, j: (i, j, 0, 0)),
          jax.ShapeDtypeStruct((b, l // GLA_BLOCK, B_DV, B_KEY_WIDTH), F32))
    outs = [tok(B_KEY_WIDTH, BF16)] * 6 + [tok(B_VAL_WIDTH, BF16), tot, tot, kv, kv,
                                           tok(A_ROW_GROUPS * A_GROUP_DIM, BF16),
                                           (pl.BlockSpec((1, ncg, GRID_W, t // GRID_W, A_GROUP_DIM),
                                                         lambda i, j: (i, 0, 0, j, 0)),
                                            jax.ShapeDtypeStruct((b, ncg, GRID_W, l // GRID_W, A_GROUP_DIM), F32))]
    const2 = lambda i, j: (0, 0)
    return pl.pallas_call(
        _pre_kernel,
        grid=(b, l // t),
        in_specs=[pl.BlockSpec((1, t, D_MODEL), lambda i, j: (i, j, 0)),
                  pl.BlockSpec((1, 3, D_MODEL), lambda i, j: (i, 0, 0)),
                  pl.BlockSpec((1, D_MODEL), const2),
                  pl.BlockSpec((D_MODEL, nw), const2),
                  pl.BlockSpec((2, B_GATE_RANK, B_KEY_WIDTH), lambda i, j: (0, 0, 0)),
                  pl.BlockSpec((2, B_KEY_WIDTH), const2),
                  pl.BlockSpec((1, A_WIDTH), const2),
                  pl.BlockSpec((1, A_WIDTH), const2)],
        out_specs=[o[0] for o in outs],
        out_shape=[o[1] for o in outs],
        compiler_params=pltpu.CompilerParams(
            dimension_semantics=("parallel", "parallel"), vmem_limit_bytes=VMEM_LIMIT_BYTES),
        name="pre",
    )(x, mod3, ng, w_pre, w2, gb, lg, lb)


def _gscan_kernel(kvf_ref, kvb_ref, totf_ref, totb_ref, s0f_ref, s0b_ref, sf_ref, sb_ref, stf_scr, stb_scr):
    @pl.when(pl.program_id(1) == 0)
    def _():
        stf_scr[...] = s0f_ref[0]
        stb_scr[...] = s0b_ref[0]

    n = kvf_ref.shape[1]
    sf = stf_scr[...]
    sb = stb_scr[...]
    for i in range(n):
        sf_ref[0, i] = sf.T.astype(BF16)
        dec = jnp.exp2(jnp.sum(totf_ref[0, i], axis=0, keepdims=True))
        sf = dec * sf + kvf_ref[0, i]
        ib = n - 1 - i
        sb_ref[0, ib] = sb.T.astype(BF16)
        dec = jnp.exp2(jnp.sum(totb_ref[0, ib], axis=0, keepdims=True))
        sb = dec * sb + kvb_ref[0, ib]
    stf_scr[...] = sf
    stb_scr[...] = sb


def _gscan_call(kv_f, kv_b, tot_f, tot_b, s0f, s0b):
    b, nb = kv_f.shape[0], kv_f.shape[1]
    g = SCAN_GROUP
    nj = nb // g
    fwd = lambda i, j: (i, j, 0, 0)
    bwd = lambda i, j: (i, nj - 1 - j, 0, 0)
    kv_blk = (1, g, B_DV, B_KEY_WIDTH)
    tot_blk = (1, g, GLA_NC, B_KEY_WIDTH)
    st_spec = pl.BlockSpec((1, B_DV, B_KEY_WIDTH), lambda i, j: (i, 0, 0))
    s_blk = (1, g, B_KEY_WIDTH, B_DV)
    s_shape = jax.ShapeDtypeStruct((b, nb, B_KEY_WIDTH, B_DV), BF16)
    return pl.pallas_call(
        _gscan_kernel,
        grid=(b, nj),
        in_specs=[pl.BlockSpec(kv_blk, fwd), pl.BlockSpec(kv_blk, bwd),
                  pl.BlockSpec(tot_blk, fwd), pl.BlockSpec(tot_blk, bwd), st_spec, st_spec],
        out_specs=[pl.BlockSpec(s_blk, fwd), pl.BlockSpec(s_blk, bwd)],
        out_shape=[s_shape, s_shape],
        scratch_shapes=[pltpu.VMEM((B_DV, B_KEY_WIDTH), F32), pltpu.VMEM((B_DV, B_KEY_WIDTH), F32)],
        compiler_params=pltpu.CompilerParams(
            dimension_semantics=("parallel", "arbitrary"), vmem_limit_bytes=VMEM_LIMIT_BYTES),
        name="gscan",
    )(kv_f, kv_b, tot_f, tot_b, s0f, s0b)


def _scale_rows(x16, scales):
    parts = []
    for c, s in enumerate(scales):
        xc = x16[c * B_CHUNK:(c + 1) * B_CHUNK, :]
        parts.append(xc if s is None else (xc.astype(F32) * s).astype(BF16))
    return parts[0] if len(parts) == 1 else jnp.concatenate(parts, axis=0)


def _exp_sum(tots, idx):
    return jnp.exp2(_sum_rows([tots[m] for m in idx])) if idx else None


def _gla_block(qdf, kdf, kcf, qdb, kdb, kcb, v16, tf, tb, sf, sb, p_scr):
    nc = GLA_NC
    ch = B_CHUNK
    hm = [_head_lane_mask(par, BF16) for par in range(2)]

    def lanes(h):
        return slice((h // 2) * LANES, (h // 2 + 1) * LANES)

    row = lax.broadcasted_iota(jnp.int32, (ch, 2 * ch), 0)
    col = lax.broadcasted_iota(jnp.int32, (ch, 2 * ch), 1)
    m_fe = (col < ch) & (col <= row)
    m_be = ((col < ch) & (col >= row)) | (col >= ch)
    m_fo = (col < ch) | (col - ch <= row)
    m_bo = (col >= ch) & (col - ch >= row)
    chunk_even = (lax.broadcasted_iota(jnp.int32, (GLA_BLOCK, 1), 0) // ch) % 2 == 0
    kmix_f = jnp.where(chunk_even, kcf, kdf)
    kmix_b = jnp.where(chunk_even, kdb, kcb)
    for par in range(2):
        kdf_m, kdb_m = kdf * hm[par], kdb * hm[par]
        kmf_m, kmb_m = kmix_f * hm[par], kmix_b * hm[par]
        for h in range(par, B_HEADS, 2):
            ls = lanes(h)
            for i in range(nc // 2):
                pr = slice(2 * i * ch, (2 * i + 2) * ch)
                re = slice(2 * i * ch, (2 * i + 1) * ch)
                ro = slice((2 * i + 1) * ch, (2 * i + 2) * ch)
                even = (jnp.where(m_fe, _dot_nt(qdf[re, ls], kdf_m[pr, ls]), 0.0)
                        + jnp.where(m_be, _dot_nt(qdb[re, ls], kmb_m[pr, ls]), 0.0))
                odd = (jnp.where(m_fo, _dot_nt(qdf[ro, ls], kmf_m[pr, ls]), 0.0)
                       + jnp.where(m_bo, _dot_nt(qdb[ro, ls], kdb_m[pr, ls]), 0.0))
                p_scr[h, re, pr] = even.astype(BF16)
                p_scr[h, ro, pr] = odd.astype(BF16)

    def cross(lo, hi):
        if hi - lo <= 2:
            return
        mid = (lo + hi) // 2
        cross(lo, mid)
        cross(mid, hi)
        rl = slice(lo * ch, mid * ch)
        rh = slice(mid * ch, hi * ch)
        qf = _scale_rows(qdf[rh], [_exp_sum(tf, range(mid, c)) for c in range(mid, hi)])
        kf = _scale_rows(kcf[rl], [_exp_sum(tf, range(c + 1, mid)) for c in range(lo, mid)])
        qb = _scale_rows(qdb[rl], [_exp_sum(tb, range(c + 1, mid)) for c in range(lo, mid)])
        kb = _scale_rows(kcb[rh], [_exp_sum(tb, range(mid, c)) for c in range(mid, hi)])
        for h in range(B_HEADS):
            ls = lanes(h)
            m = hm[h % 2]
            p_scr[h, rh, rl] = _dot_nt(qf[:, ls], (kf * m)[:, ls]).astype(BF16)
            p_scr[h, rl, rh] = _dot_nt(qb[:, ls], (kb * m)[:, ls]).astype(BF16)

    cross(0, nc)

    qsf = _scale_rows(qdf, [_exp_sum(tf, range(0, c)) for c in range(nc)])
    qsb = _scale_rows(qdb, [_exp_sum(tb, range(c + 1, nc)) for c in range(nc)])
    head_row = lax.broadcasted_iota(jnp.int32, (LANES, 1), 0) // B_DK
    outs = []
    for h in range(B_HEADS):
        ls = lanes(h)
        own = (head_row == (h % 2)).astype(BF16)
        lhs = jnp.concatenate([p_scr[h], qsf[:, ls], qsb[:, ls]], axis=1)
        rhs = jnp.concatenate([v16[:, h * B_DV:(h + 1) * B_DV], sf[ls, :] * own, sb[ls, :] * own], axis=0)
        outs.append(jnp.dot(lhs, rhs, preferred_element_type=F32))
    return outs


def _colmix_kernel(ws_ref, bs_ref, vn_ref, o_ref):
    nw = vn_ref.shape[2]
    xs = jnp.concatenate([vn_ref[0, 0, w] for w in range(nw)], axis=1)
    y = _dot(ws_ref[0], xs) + bs_ref[0][:, 0:1]
    for w in range(nw):
        o_ref[0, 0, w] = y[:, w * A_GROUP_DIM:(w + 1) * A_GROUP_DIM]


def _colmix_call(ws_col, bs_col, vn_col, nw):
    b, g, width, rows, ch = vn_col.shape
    blk = pl.BlockSpec((1, 1, nw, rows, ch), lambda i, j, m: (i, j, m, 0, 0))
    return pl.pallas_call(
        _colmix_kernel,
        grid=(b, g, width // nw),
        in_specs=[pl.BlockSpec((1, rows, rows), lambda i, j, m: (j, 0, 0)),
                  pl.BlockSpec((1, rows, A_GROUP_DIM), lambda i, j, m: (j, 0, 0)),
                  blk],
        out_specs=blk,
        out_shape=jax.ShapeDtypeStruct(vn_col.shape, F32),
        compiler_params=pltpu.CompilerParams(
            dimension_semantics=("parallel", "parallel", "parallel"), vmem_limit_bytes=VMEM_LIMIT_BYTES),
        name="colmix",
    )(ws_col, bs_col, vn_col)


def _main_kernel(x_ref, mod_ref, ng_ref, w_ref, vr_ref, sc_ref,
                 qdf_ref, kdf_ref, kcf_ref, qdb_ref, kdb_ref, kcb_ref, v_ref, totf_ref, totb_ref, sf_ref, sb_ref,
                 ws_ref, bs_ref, bg_ref, wpa_ref, wpb_ref, wo_ref, fg_ref, o_ref, acta_scr, on_scr, p_scr):
    t = x_ref.shape[1]
    blocks = [slice(blk * GLA_BLOCK, (blk + 1) * GLA_BLOCK) for blk in range(t // GLA_BLOCK)]

    def gla(blk):
        rs = blocks[blk]
        tf = [totf_ref[0, blk, c:c + 1, :] for c in range(GLA_NC)]
        tb = [totb_ref[0, blk, c:c + 1, :] for c in range(GLA_NC)]
        o_heads = _gla_block(qdf_ref[0, rs, :], kdf_ref[0, rs, :], kcf_ref[0, rs, :],
                             qdb_ref[0, rs, :], kdb_ref[0, rs, :], kcb_ref[0, rs, :], v_ref[0, rs, :],
                             tf, tb, sf_ref[0, blk], sb_ref[0, blk], p_scr.at[blk])
        for hd in range(B_HEADS):
            on_scr[rs, hd * B_DV:(hd + 1) * B_DV] = _rms_rows(o_heads[hd])

    def proj(c0, n):
        return jnp.dot(h16, w_ref[:, c0:c0 + n], preferred_element_type=F32)

    gla(0)
    h16 = _latent_h(x_ref[0], mod_ref, ng_ref).astype(BF16)
    zb = proj(0, B_VAL_WIDTH)
    u = proj(B_VAL_WIDTH, A_WIDTH)
    za = proj(B_VAL_WIDTH + A_WIDTH, A_WIDTH)
    for blk in range(1, len(blocks)):
        gla(blk)
    gate_a = jax.nn.sigmoid(proj(B_VAL_WIDTH + 2 * A_WIDTH, D_MODEL))

    uz = u * _silu(za)
    for g in range(A_GROUPS):
        cs = slice(g * A_GROUP_DIM, (g + 1) * A_GROUP_DIM)
        if g < A_ROW_GROUPS:
            chunks = [slice(c * A_CHUNK, (c + 1) * A_CHUNK) for c in range(t // A_CHUNK)]
            sv_all = jnp.dot(ws_ref[g].astype(BF16), jnp.concatenate([vr_ref[0, rs, cs] for rs in chunks], axis=1),
                             preferred_element_type=F32)
            for c, rs in enumerate(chunks):
                sv = sv_all[:, c * A_GROUP_DIM:(c + 1) * A_GROUP_DIM] + bs_ref[g][:, 0:1]
                acta_scr[rs, cs] = (uz[rs, cs] * sv).astype(BF16)
        else:
            sv = jnp.swapaxes(sc_ref[0, g - A_ROW_GROUPS], 0, 1).reshape(t, A_GROUP_DIM)
            acta_scr[:, cs] = (uz[:, cs] * sv).astype(BF16)
    ya = jnp.dot(acta_scr[...], wpa_ref[...], preferred_element_type=F32)

    gate_b = jax.nn.sigmoid(proj(B_VAL_WIDTH + 2 * A_WIDTH + D_MODEL, D_MODEL))
    yb = _dot(on_scr[...] * (_silu(zb) * bg_ref[...]), wpb_ref[...])

    m = (gate_a * ya + gate_b * yb).astype(BF16)
    for rs in blocks:
        y = jnp.dot(m[rs], wo_ref[...], preferred_element_type=F32)
        xo = x_ref[0, rs, :] + mod_ref[0, 2:3, :] * y
        o_ref[0, rs, :] = _rms_rows(xo) * fg_ref[...]


def _main_call(x, mod3, ng, w_main, vn_row, sv_col, gla_ops, ws, bs, bg, wpa, wpb, wo, fg, t):
    b, l, _ = x.shape
    nw = w_main.shape[1]
    ncg = A_GROUPS - A_ROW_GROUPS
    const2 = lambda i, j: (0, 0)
    const3 = lambda i, j: (0, 0, 0)
    tok = lambda width: pl.BlockSpec((1, t, width), lambda i, j: (i, j, 0))
    nblk = t // GLA_BLOCK
    tot_spec = pl.BlockSpec((1, nblk, GLA_NC, B_KEY_WIDTH), lambda i, j: (i, j, 0, 0))
    st_spec = pl.BlockSpec((1, nblk, B_KEY_WIDTH, B_DV), lambda i, j: (i, j, 0, 0))
    return pl.pallas_call(
        _main_kernel,
        grid=(b, l // t),
        in_specs=[tok(D_MODEL),
                  pl.BlockSpec((1, 3, D_MODEL), lambda i, j: (i, 0, 0)),
                  pl.BlockSpec((1, D_MODEL), const2),
                  pl.BlockSpec((D_MODEL, nw), const2),
                  tok(A_ROW_GROUPS * A_GROUP_DIM),
                  pl.BlockSpec((1, ncg, GRID_W, t // GRID_W, A_GROUP_DIM), lambda i, j: (i, 0, 0, j, 0)),
                  tok(B_KEY_WIDTH), tok(B_KEY_WIDTH), tok(B_KEY_WIDTH),
                  tok(B_KEY_WIDTH), tok(B_KEY_WIDTH), tok(B_KEY_WIDTH), tok(B_VAL_WIDTH),
                  tot_spec, tot_spec, st_spec, st_spec,
                  pl.BlockSpec((A_GROUPS, A_CHUNK, A_CHUNK), const3),
                  pl.BlockSpec((A_GROUPS, A_CHUNK, A_GROUP_DIM), const3),
                  pl.BlockSpec((1, B_VAL_WIDTH), const2),
                  pl.BlockSpec((A_WIDTH, D_MODEL), const2),
                  pl.BlockSpec((B_VAL_WIDTH, D_MODEL), const2),
                  pl.BlockSpec((D_MODEL, D_MODEL), const2),
                  pl.BlockSpec((1, D_MODEL), const2)],
        out_specs=tok(D_MODEL),
        out_shape=jax.ShapeDtypeStruct((b, l, D_MODEL), F32),
        scratch_shapes=[pltpu.VMEM((t, A_WIDTH), BF16), pltpu.VMEM((t, B_VAL_WIDTH), F32),
                        pltpu.VMEM((nblk, B_HEADS, GLA_BLOCK, GLA_BLOCK), BF16)],
        compiler_params=pltpu.CompilerParams(
            dimension_semantics=("parallel", "parallel"), vmem_limit_bytes=VMEM_LIMIT_BYTES),
        name="main",
    )(x, mod3, ng, w_main, vn_row, sv_col, *gla_ops, ws, bs, bg, wpa, wpb, wo, fg)


def kernel(x, c, ctx, c_ctx, w_mod, b_mod, norm_g, w_in, a_ln_g, a_ln_b, a_ws, a_bs, b_gate_w2, b_gate_b,
           b_norm_g, w_proj_a, w_proj_b, w_out, final_norm_g):
    assert w_mod.shape[0] == 1, "single-layer block"
    b, l, _ = x.shape
    rows = l // GRID_W
    assert rows == A_CHUNK
    ng = norm_g[0][None, :]

    cc = jnp.zeros((8, D_MODEL), F32).at[0:b].set(c).at[b].set(c_ctx)
    w_pre, w_main, wpa16, wpb16, wo16, mod = _wprep_call(w_in[0].T, w_proj_a, w_proj_b, w_out, cc, w_mod, b_mod)
    mod3 = mod[0:b].reshape(b, 3, D_MODEL)
    w2, gb = b_gate_w2[0], b_gate_b[0]
    s0f, s0b = _ctx_call(ctx, mod, ng, w_pre, w2, gb, b)

    (qd_f, kd_f, kc_f, qd_b, kd_b, kc_b, v16, tot_f, tot_b, kv_f, kv_b, vn_row, vn_col) = _pre_call(
        x, mod3, ng, w_pre, w2, gb, a_ln_g[0][None, :], a_ln_b[0][None, :], 512)

    s_f, s_b = _gscan_call(kv_f, kv_b, tot_f, tot_b, s0f, s0b)
    gla_ops = (qd_f, kd_f, kc_f, qd_b, kd_b, kc_b, v16, tot_f, tot_b, s_f, s_b)

    bs_b = jnp.broadcast_to(a_bs[0][:, :, None], (A_GROUPS, A_CHUNK, A_GROUP_DIM))
    sv_col = _colmix_call(a_ws[0][A_ROW_GROUPS:], bs_b[A_ROW_GROUPS:], vn_col, 16)

    return _main_call(x, mod3, ng, w_main, vn_row, sv_col, gla_ops, a_ws[0], bs_b, b_norm_g[0][None, :],
                      wpa16, wpb16, wo16, final_norm_g[None, :], 512)
```

```python
import functools

import jax
import jax.numpy as jnp
from jax import lax
from jax.experimental import pallas as pl
from jax.experimental.pallas import tpu as pltpu

D_MODEL = 1024
GRID_W = 64
EPS = 1e-6

A_WIDTH = 512
A_GROUPS = 4
A_GROUP_DIM = 128
A_CHUNK = 128
A_ROW_GROUPS = 2

B_HEADS = 4
B_DK = 64
B_DV = 128
B_KEY_WIDTH = 256
B_VAL_WIDTH = 512
B_GATE_RANK = 16
B_GATE_TAU = 16.0
LOG2E = 1.4426950408889634
B_CHUNK = 64

Q0 = 0
K0 = Q0 + B_KEY_WIDTH
V0 = K0 + B_KEY_WIDTH
LR0 = V0 + B_VAL_WIDTH
ZB0 = LR0 + 2 * B_GATE_RANK
UA0 = ZB0 + B_VAL_WIDTH
VA0 = UA0 + A_WIDTH
ZA0 = VA0 + A_WIDTH
G0 = ZA0 + A_WIDTH
IN_WIDTH = G0 + 2 * D_MODEL

LANES = 128

P_LR = 0
P_K = P_LR + LANES
P_Q = P_K + B_KEY_WIDTH
P_V = P_Q + B_KEY_WIDTH
P_VA = P_V + B_VAL_WIDTH
PRE_WIDTH = P_VA + A_WIDTH
GLA_BLOCK = 256
GLA_NC = GLA_BLOCK // B_CHUNK
SCAN_GROUP = 8

VMEM_LIMIT_BYTES = 56 * 1024 * 1024

BF16 = jnp.bfloat16
F32 = jnp.float32


def _dot(a, b):
    return jnp.dot(a.astype(BF16), b.astype(BF16), preferred_element_type=F32)


def _dot_nt(a, b):
    return lax.dot_general(a.astype(BF16), b.astype(BF16), (((1,), (1,)), ((), ())),
                           preferred_element_type=F32)


def _dot_tn(a, b):
    return lax.dot_general(a.astype(BF16), b.astype(BF16), (((0,), (0,)), ((), ())),
                           preferred_element_type=F32)


def _silu(x):
    return x * jax.nn.sigmoid(x)


def _rms_rows(x):
    return x * lax.rsqrt(jnp.mean(x * x, axis=-1, keepdims=True) + EPS)


def _wprep_kernel(wt_ref, wpa_ref, wpb_ref, wo_ref, c_ref, wm_ref, bm_ref,
                  wpre_ref, wmain_ref, wpa16_ref, wpb16_ref, wo16_ref, mod_ref):
    def put(dst_ref, c0, r0, n, scale=None):
        for s in range(0, n, 2 * LANES):
            m = min(2 * LANES, n - s)
            blk = wt_ref[r0 + s:r0 + s + m, :].T
            dst_ref[:, c0 + s:c0 + s + m] = (blk if scale is None else blk * scale).astype(BF16)

    put(wpre_ref, P_LR, LR0, LANES)
    put(wpre_ref, P_K, K0, B_KEY_WIDTH)
    put(wpre_ref, P_Q, Q0, B_KEY_WIDTH, B_DK ** -0.5)
    put(wpre_ref, P_V, V0, B_VAL_WIDTH)
    put(wpre_ref, P_VA, VA0, A_WIDTH)
    put(wmain_ref, 0, ZB0, VA0 - ZB0)
    put(wmain_ref, VA0 - ZB0, ZA0, IN_WIDTH - ZA0)
    wpa16_ref[...] = wpa_ref[0].astype(BF16)
    wpb16_ref[...] = wpb_ref[0].astype(BF16)
    wo16_ref[...] = wo_ref[0].astype(BF16)
    mod_ref[...] = _dot(_silu(c_ref[...]), wm_ref[0]) + bm_ref[...]


def _wprep_call(w_in_t, w_proj_a, w_proj_b, w_out, cc, w_mod, b_mod):
    steps = 4
    n_main = (VA0 - ZB0) + (IN_WIDTH - ZA0)

    def rows3(a):
        return pl.BlockSpec((1, a.shape[1] // steps, a.shape[2]), lambda i: (0, i, 0))

    def rows2(nrows, ncols):
        return (pl.BlockSpec((nrows // steps, ncols), lambda i: (i, 0)),
                jax.ShapeDtypeStruct((nrows, ncols), BF16))

    n_mod = w_mod.shape[2]
    outs = [rows2(D_MODEL, PRE_WIDTH), rows2(D_MODEL, n_main), rows2(A_WIDTH, D_MODEL),
            rows2(B_VAL_WIDTH, D_MODEL), rows2(D_MODEL, D_MODEL),
            (pl.BlockSpec((8, n_mod // steps), lambda i: (0, i)), jax.ShapeDtypeStruct((8, n_mod), F32))]
    return pl.pallas_call(
        _wprep_kernel,
        grid=(steps,),
        in_specs=[pl.BlockSpec((IN_WIDTH, D_MODEL // steps), lambda i: (0, i)),
                  rows3(w_proj_a), rows3(w_proj_b), rows3(w_out),
                  pl.BlockSpec((8, D_MODEL), lambda i: (0, 0)),
                  pl.BlockSpec((1, D_MODEL, n_mod // steps), lambda i: (0, 0, i)),
                  pl.BlockSpec((1, n_mod // steps), lambda i: (0, i))],
        out_specs=[o[0] for o in outs],
        out_shape=[o[1] for o in outs],
        compiler_params=pltpu.CompilerParams(
            dimension_semantics=("parallel",), vmem_limit_bytes=VMEM_LIMIT_BYTES),
        name="wprep",
    )(w_in_t, w_proj_a, w_proj_b, w_out, cc, w_mod, b_mod)


def _gate_logs(lr, w2_ref, gb_ref):
    out = []
    for r in range(2):
        logits = _dot(lr[:, r * B_GATE_RANK:(r + 1) * B_GATE_RANK], w2_ref[r]) + gb_ref[r:r + 1, :]
        log_sig = jnp.minimum(logits, 0.0) - jnp.log(1.0 + jnp.exp(-jnp.abs(logits)))
        out.append(log_sig * (LOG2E / B_GATE_TAU))
    return out


def _chunk_tri(reverse):
    i = lax.broadcasted_iota(jnp.int32, (GLA_BLOCK, GLA_BLOCK), 0)
    j = lax.broadcasted_iota(jnp.int32, (GLA_BLOCK, GLA_BLOCK), 1)
    same = (i // B_CHUNK) == (j // B_CHUNK)
    tri = (j >= i) if reverse else (j <= i)
    return (same & tri).astype(BF16)


def _sum_rows(rows):
    acc = rows[0]
    for r in rows[1:]:
        acc = acc + r
    return acc


def _block_cum(a, tri, reverse):
    hi = a.astype(BF16)
    lo = (a - hi.astype(F32)).astype(BF16)
    cum = (jnp.dot(tri, hi, preferred_element_type=F32) + jnp.dot(tri, lo, preferred_element_type=F32))
    last = 0 if reverse else B_CHUNK - 1
    tots = [cum[c * B_CHUNK + last:c * B_CHUNK + last + 1, :] for c in range(GLA_NC)]
    return cum, tots


def _block_keys(k, cum, tots, reverse):
    totb = jnp.concatenate([jnp.broadcast_to(t, (B_CHUNK, B_KEY_WIDTH)) for t in tots], axis=0)
    kdec = k * jnp.exp2(totb - cum)
    later = []
    for c in range(GLA_NC):
        idx = list(range(0, c)) if reverse else list(range(c + 1, GLA_NC))
        if idx:
            later.append(jnp.broadcast_to(jnp.exp2(_sum_rows([tots[m] for m in idx])), (B_CHUNK, B_KEY_WIDTH)))
        else:
            later.append(jnp.ones((B_CHUNK, B_KEY_WIDTH), F32))
    return kdec, kdec * jnp.concatenate(later, axis=0)


def _pre_project(h16, w_ref, want_q):
    lr = jnp.dot(h16, w_ref[:, P_LR:P_LR + LANES], preferred_element_type=F32)[:, 0:2 * B_GATE_RANK]
    k = jnp.dot(h16, w_ref[:, P_K:P_K + B_KEY_WIDTH], preferred_element_type=F32)
    q = jnp.dot(h16, w_ref[:, P_Q:P_Q + B_KEY_WIDTH], preferred_element_type=F32) if want_q else None
    v = jnp.dot(h16, w_ref[:, P_V:P_V + B_VAL_WIDTH], preferred_element_type=F32)
    return lr, k, q, v


def _block_kv_t(v16, kblk):
    k16 = kblk.astype(BF16)
    lane = lax.broadcasted_iota(jnp.int32, (1, B_KEY_WIDTH), 1)
    acc = jnp.zeros((B_DV, B_KEY_WIDTH), F32)
    for h in range(B_HEADS):
        full = _dot_tn(v16[:, h * B_DV:(h + 1) * B_DV], k16)
        acc = acc + jnp.where((lane // B_DK) == h, full, 0.0)
    return acc


def _ctx_kernel(ctx_ref, mod_ref, ng_ref, w_ref, w2_ref, gb_ref, sf_ref, sb_ref, *, ctx_row):
    xc = ctx_ref[0]
    shift = mod_ref[ctx_row:ctx_row + 1, 0:D_MODEL]
    scale = mod_ref[ctx_row:ctx_row + 1, D_MODEL:2 * D_MODEL]
    hc = _rms_rows(xc) * ng_ref[...] * (1.0 + scale) + shift
    lr, k, _, v = _pre_project(hc.astype(BF16), w_ref, False)
    v16 = v.astype(BF16)
    a_f, a_b = _gate_logs(lr, w2_ref, gb_ref)
    for a, reverse, out_ref in ((a_f, False, sf_ref), (a_b, True, sb_ref)):
        cum, tots = _block_cum(a, _chunk_tri(reverse), reverse)
        _, kblk = _block_keys(k, cum, tots, reverse)
        out_ref[0] = _block_kv_t(v16, kblk)


def _ctx_call(ctx, mod, ng, w_kvl, w2, gb, ctx_row):
    b, lc, _ = ctx.shape
    assert lc == GLA_BLOCK, "context length must be one GLA block"
    nw = w_kvl.shape[1]
    st = jax.ShapeDtypeStruct((b, B_DV, B_KEY_WIDTH), F32)
    st_spec = pl.BlockSpec((1, B_DV, B_KEY_WIDTH), lambda i: (i, 0, 0))
    return pl.pallas_call(
        functools.partial(_ctx_kernel, ctx_row=ctx_row),
        grid=(b,),
        in_specs=[pl.BlockSpec((1, lc, D_MODEL), lambda i: (i, 0, 0)),
                  pl.BlockSpec((8, 3 * D_MODEL), lambda i: (0, 0)),
                  pl.BlockSpec((1, D_MODEL), lambda i: (0, 0)),
                  pl.BlockSpec((D_MODEL, nw), lambda i: (0, 0)),
                  pl.BlockSpec((2, B_GATE_RANK, B_KEY_WIDTH), lambda i: (0, 0, 0)),
                  pl.BlockSpec((2, B_KEY_WIDTH), lambda i: (0, 0))],
        out_specs=[st_spec, st_spec],
        out_shape=[st, st],
        name="ctx",
    )(ctx, mod, ng, w_kvl, w2, gb)


def _latent_h(x, mod_ref, ng_ref):
    shift = mod_ref[0, 0:1, :]
    gain = ng_ref[...] * (1.0 + mod_ref[0, 1:2, :])
    return _rms_rows(x) * gain + shift


def _pre_kernel(x_ref, mod_ref, ng_ref, w_ref, w2_ref, gb_ref, lg_ref, lb_ref,
                qdf_ref, kdf_ref, kcf_ref, qdb_ref, kdb_ref, kcb_ref, v_ref,
                totf_ref, totb_ref, kvf_ref, kvb_ref, vr_ref, vc_ref):
    h16 = _latent_h(x_ref[0], mod_ref, ng_ref).astype(BF16)
    nblk = x_ref.shape[1] // GLA_BLOCK
    blocks = [slice(blk * GLA_BLOCK, (blk + 1) * GLA_BLOCK) for blk in range(nblk)]
    dirs = ((False, qdf_ref, kdf_ref, kcf_ref, totf_ref, kvf_ref), (True, qdb_ref, kdb_ref, kcb_ref, totb_ref, kvb_ref))

    lr = jnp.dot(h16, w_ref[:, P_LR:P_LR + LANES], preferred_element_type=F32)[:, 0:2 * B_GATE_RANK]
    k = jnp.dot(h16, w_ref[:, P_K:P_K + B_KEY_WIDTH], preferred_element_type=F32)
    logs = _gate_logs(lr, w2_ref, gb_ref)
    q = jnp.dot(h16, w_ref[:, P_Q:P_Q + B_KEY_WIDTH], preferred_element_type=F32)
    v16 = jnp.dot(h16, w_ref[:, P_V:P_V + B_VAL_WIDTH], preferred_element_type=F32).astype(BF16)
    v_ref[0] = v16

    cums = {}
    for d, (reverse, _, _, _, tot_ref, _) in enumerate(dirs):
        tri = _chunk_tri(reverse)
        for blk, rs in enumerate(blocks):
            cum, tots = _block_cum(logs[d][rs], tri, reverse)
            cums[d, blk] = (cum, tots)
            for c in range(GLA_NC):
                tot_ref[0, blk, c:c + 1, :] = tots[c]

    va = jnp.dot(h16, w_ref[:, P_VA:P_VA + A_WIDTH], preferred_element_type=F32)

    for d, (reverse, qd_ref, kd_ref, kc_ref, _, kv_ref) in enumerate(dirs):
        for blk, rs in enumerate(blocks):
            cum, tots = cums[d, blk]
            kdec, kblk = _block_keys(k[rs], cum, tots, reverse)
            kc_ref[0, rs, :] = kdec.astype(BF16)
            kd_ref[0, rs, :] = (k[rs] * jnp.exp2(-cum)).astype(BF16)
            qd_ref[0, rs, :] = (q[rs] * jnp.exp2(cum)).astype(BF16)
            kv_ref[0, blk] = _block_kv_t(v16[rs], kblk)

    vc = va - jnp.mean(va, axis=-1, keepdims=True)
    vn = vc * lax.rsqrt(jnp.mean(vc * vc, axis=-1, keepdims=True) + EPS) * lg_ref[...] + lb_ref[...]
    nr = A_ROW_GROUPS * A_GROUP_DIM
    vr_ref[0] = vn[:, 0:nr].astype(BF16)
    for g in range(A_GROUPS - A_ROW_GROUPS):
        vg = vn[:, nr + g * A_GROUP_DIM:nr + (g + 1) * A_GROUP_DIM]
        vc_ref[0, g] = jnp.swapaxes(vg.reshape(vg.shape[0] // GRID_W, GRID_W, A_GROUP_DIM), 0, 1)


def _pre_call(x, mod3, ng, w_pre, w2, gb, lg, lb, t):
    b, l, _ = x.shape
    nw = w_pre.shape[1]
    ncg = A_GROUPS - A_ROW_GROUPS
    nblk = t // GLA_BLOCK

    def tok(width, dtype):
        return (pl.BlockSpec((1, t, width), lambda i, j: (i, j, 0)),
                jax.ShapeDtypeStruct((b, l, width), dtype))

    tot = (pl.BlockSpec((1, nblk, GLA_NC, B_KEY_WIDTH), lambda i, j: (i, j, 0, 0)),
           jax.ShapeDtypeStruct((b, l // GLA_BLOCK, GLA_NC, B_KEY_WIDTH), F32))
    kv = (pl.BlockSpec((1, nblk, B_DV, B_KEY_WIDTH), lambda i, j: (i, j, 0, 0)),
          jax.ShapeDtypeStruct((b, l // GLA_BLOCK, B_DV, B_KEY_WIDTH), F32))
    outs = [tok(B_KEY_WIDTH, BF16)] * 6 + [tok(B_VAL_WIDTH, BF16), tot, tot, kv, kv,
                                           tok(A_ROW_GROUPS * A_GROUP_DIM, BF16),
                                           (pl.BlockSpec((1, ncg, GRID_W, t // GRID_W, A_GROUP_DIM),
                                                         lambda i, j: (i, 0, 0, j, 0)),
                                            jax.ShapeDtypeStruct((b, ncg, GRID_W, l // GRID_W, A_GROUP_DIM), F32))]
    const2 = lambda i, j: (0, 0)
    return pl.pallas_call(
        _pre_kernel,
        grid=(b, l // t),
        in_specs=[pl.BlockSpec((1, t, D_MODEL), lambda i, j: (i, j, 0)),
                  pl.BlockSpec((1, 3, D_MODEL), lambda i, j: (i, 0, 0)),
                  pl.BlockSpec((1, D_MODEL), const2),
                  pl.BlockSpec((D_MODEL, nw), const2),
                  pl.BlockSpec((2, B_GATE_RANK, B_KEY_WIDTH), lambda i, j: (0, 0, 0)),
                  pl.BlockSpec((2, B_KEY_WIDTH), const2),
                  pl.BlockSpec((1, A_WIDTH), const2),
                  pl.BlockSpec((1, A_WIDTH), const2)],
        out_specs=[o[0] for o in outs],
        out_shape=[o[1] for o in outs],
        compiler_params=pltpu.CompilerParams(
            dimension_semantics=("parallel", "parallel"), vmem_limit_bytes=VMEM_LIMIT_BYTES),
        name="pre",
    )(x, mod3, ng, w_pre, w2, gb, lg, lb)


def _gscan_kernel(kvf_ref, kvb_ref, totf_ref, totb_ref, s0f_ref, s0b_ref, sf_ref, sb_ref, stf_scr, stb_scr):
    @pl.when(pl.program_id(1) == 0)
    def _():
        stf_scr[...] = s0f_ref[0]
        stb_scr[...] = s0b_ref[0]

    n = kvf_ref.shape[1]
    sf = stf_scr[...]
    sb = stb_scr[...]
    for i in range(n):
        sf_ref[0, i] = sf.T.astype(BF16)
        dec = jnp.exp2(jnp.sum(totf_ref[0, i], axis=0, keepdims=True))
        sf = dec * sf + kvf_ref[0, i]
        ib = n - 1 - i
        sb_ref[0, ib] = sb.T.astype(BF16)
        dec = jnp.exp2(jnp.sum(totb_ref[0, ib], axis=0, keepdims=True))
        sb = dec * sb + kvb_ref[0, ib]
    stf_scr[...] = sf
    stb_scr[...] = sb


def _gscan_call(kv_f, kv_b, tot_f, tot_b, s0f, s0b):
    b, nb = kv_f.shape[0], kv_f.shape[1]
    g = SCAN_GROUP
    nj = nb // g
    fwd = lambda i, j: (i, j, 0, 0)
    bwd = lambda i, j: (i, nj - 1 - j, 0, 0)
    kv_blk = (1, g, B_DV, B_KEY_WIDTH)
    tot_blk = (1, g, GLA_NC, B_KEY_WIDTH)
    st_spec = pl.BlockSpec((1, B_DV, B_KEY_WIDTH), lambda i, j: (i, 0, 0))
    s_blk = (1, g, B_KEY_WIDTH, B_DV)
    s_shape = jax.ShapeDtypeStruct((b, nb, B_KEY_WIDTH, B_DV), BF16)
    return pl.pallas_call(
        _gscan_kernel,
        grid=(b, nj),
        in_specs=[pl.BlockSpec(kv_blk, fwd), pl.BlockSpec(kv_blk, bwd),
                  pl.BlockSpec(tot_blk, fwd), pl.BlockSpec(tot_blk, bwd), st_spec, st_spec],
        out_specs=[pl.BlockSpec(s_blk, fwd), pl.BlockSpec(s_blk, bwd)],
        out_shape=[s_shape, s_shape],
        scratch_shapes=[pltpu.VMEM((B_DV, B_KEY_WIDTH), F32), pltpu.VMEM((B_DV, B_KEY_WIDTH), F32)],
        compiler_params=pltpu.CompilerParams(
            dimension_semantics=("parallel", "arbitrary"), vmem_limit_bytes=VMEM_LIMIT_BYTES),
        name="gscan",
    )(kv_f, kv_b, tot_f, tot_b, s0f, s0b)


def _scale_rows(x16, scales):
    parts = []
    for c, s in enumerate(scales):
        xc = x16[c * B_CHUNK:(c + 1) * B_CHUNK, :]
        parts.append(xc if s is None else (xc.astype(F32) * s).astype(BF16))
    return parts[0] if len(parts) == 1 else jnp.concatenate(parts, axis=0)


def _exp_sum(tots, idx):
    return jnp.exp2(_sum_rows([tots[m] for m in idx])) if idx else None


def _gla_block(qdf, kdf, kcf, qdb, kdb, kcb, v16, tf, tb, sf, sb, p_scr):
    nc = GLA_NC
    ch = B_CHUNK
    lane = lax.broadcasted_iota(jnp.int32, (1, LANES), 1)
    hm = [(lane < B_DK).astype(BF16), (lane >= B_DK).astype(BF16)]

    def pair_heads(x):
        return jnp.concatenate([x * hm[0], x * hm[1]], axis=0)

    row = lax.broadcasted_iota(jnp.int32, (ch, 2 * ch), 0)
    col = lax.broadcasted_iota(jnp.int32, (ch, 2 * ch), 1)
    first = col < ch
    zeros = jnp.zeros((ch, LANES), BF16)
    for g in range(B_HEADS // 2):
        ls = slice(g * LANES, (g + 1) * LANES)
        for i in range(nc // 2):
            pr = slice(2 * i * ch, (2 * i + 2) * ch)
            re = slice(2 * i * ch, (2 * i + 1) * ch)
            ro = slice((2 * i + 1) * ch, (2 * i + 2) * ch)
            sc_f = _dot_nt(pair_heads(qdf[pr, ls]), jnp.concatenate([kdf[pr, ls], kcf[re, ls], zeros], axis=0))
            sc_b = _dot_nt(pair_heads(qdb[pr, ls]), jnp.concatenate([kdb[pr, ls], zeros, kcb[ro, ls]], axis=0))
            for hh in range(2):
                he = slice(hh * 2 * ch, hh * 2 * ch + ch)
                ho = slice(hh * 2 * ch + ch, (hh + 1) * 2 * ch)
                even = jnp.where(first,
                                 jnp.where(col <= row, sc_f[he, 0:2 * ch], 0.0)
                                 + jnp.where(col >= row, sc_b[he, 0:2 * ch], 0.0),
                                 sc_b[he, 2 * ch:])
                odd = jnp.where(first, sc_f[ho, 2 * ch:],
                                jnp.where(col - ch <= row, sc_f[ho, 0:2 * ch], 0.0)
                                + jnp.where(col - ch >= row, sc_b[ho, 0:2 * ch], 0.0))
                p_scr[2 * g + hh, re, pr] = even.astype(BF16)
                p_scr[2 * g + hh, ro, pr] = odd.astype(BF16)

    def cross(lo, hi):
        if hi - lo <= 2:
            return
        mid = (lo + hi) // 2
        cross(lo, mid)
        cross(mid, hi)
        rl = slice(lo * ch, mid * ch)
        rh = slice(mid * ch, hi * ch)
        n = (hi - mid) * ch
        qf = _scale_rows(qdf[rh], [_exp_sum(tf, range(mid, c)) for c in range(mid, hi)])
        kf = _scale_rows(kcf[rl], [_exp_sum(tf, range(c + 1, mid)) for c in range(lo, mid)])
        qb = _scale_rows(qdb[rl], [_exp_sum(tb, range(c + 1, mid)) for c in range(lo, mid)])
        kb = _scale_rows(kcb[rh], [_exp_sum(tb, range(mid, c)) for c in range(mid, hi)])
        for g in range(B_HEADS // 2):
            ls = slice(g * LANES, (g + 1) * LANES)
            sc_f = _dot_nt(pair_heads(qf[:, ls]), kf[:, ls]).astype(BF16)
            sc_b = _dot_nt(pair_heads(qb[:, ls]), kb[:, ls]).astype(BF16)
            for hh in range(2):
                p_scr[2 * g + hh, rh, rl] = sc_f[hh * n:(hh + 1) * n]
                p_scr[2 * g + hh, rl, rh] = sc_b[hh * n:(hh + 1) * n]

    cross(0, nc)

    qsf = _scale_rows(qdf, [_exp_sum(tf, range(0, c)) for c in range(nc)])
    qsb = _scale_rows(qdb, [_exp_sum(tb, range(c + 1, nc)) for c in range(nc)])
    head_row = lax.broadcasted_iota(jnp.int32, (LANES, 1), 0) // B_DK
    outs = []
    for h in range(B_HEADS):
        ls = slice((h // 2) * LANES, (h // 2 + 1) * LANES)
        own = (head_row == (h % 2)).astype(BF16)
        lhs = jnp.concatenate([p_scr[h], qsf[:, ls], qsb[:, ls]], axis=1)
        rhs = jnp.concatenate([v16[:, h * B_DV:(h + 1) * B_DV], sf[ls, :] * own, sb[ls, :] * own], axis=0)
        outs.append(jnp.dot(lhs, rhs, preferred_element_type=F32))
    return outs


def _colmix_kernel(ws_ref, bs_ref, vn_ref, o_ref):
    nw = vn_ref.shape[2]
    xs = jnp.concatenate([vn_ref[0, 0, w] for w in range(nw)], axis=1)
    y = _dot(ws_ref[0], xs) + bs_ref[0][:, 0:1]
    for w in range(nw):
        o_ref[0, 0, w] = y[:, w * A_GROUP_DIM:(w + 1) * A_GROUP_DIM]


def _colmix_call(ws_col, bs_col, vn_col, nw):
    b, g, width, rows, ch = vn_col.shape
    blk = pl.BlockSpec((1, 1, nw, rows, ch), lambda i, j, m: (i, j, m, 0, 0))
    return pl.pallas_call(
        _colmix_kernel,
        grid=(b, g, width // nw),
        in_specs=[pl.BlockSpec((1, rows, rows), lambda i, j, m: (j, 0, 0)),
                  pl.BlockSpec((1, rows, A_GROUP_DIM), lambda i, j, m: (j, 0, 0)),
                  blk],
        out_specs=blk,
        out_shape=jax.ShapeDtypeStruct(vn_col.shape, F32),
        compiler_params=pltpu.CompilerParams(
            dimension_semantics=("parallel", "parallel", "parallel"), vmem_limit_bytes=VMEM_LIMIT_BYTES),
        name="colmix",
    )(ws_col, bs_col, vn_col)


def _main_kernel(x_ref, mod_ref, ng_ref, w_ref, vr_ref, sc_ref,
                 qdf_ref, kdf_ref, kcf_ref, qdb_ref, kdb_ref, kcb_ref, v_ref, totf_ref, totb_ref, sf_ref, sb_ref,
                 ws_ref, bs_ref, bg_ref, wpa_ref, wpb_ref, wo_ref, fg_ref, o_ref, acta_scr, on_scr, p_scr):
    t = x_ref.shape[1]
    blocks = [slice(blk * GLA_BLOCK, (blk + 1) * GLA_BLOCK) for blk in range(t // GLA_BLOCK)]

    def gla(blk):
        rs = blocks[blk]
        tf = [totf_ref[0, blk, c:c + 1, :] for c in range(GLA_NC)]
        tb = [totb_ref[0, blk, c:c + 1, :] for c in range(GLA_NC)]
        o_heads = _gla_block(qdf_ref[0, rs, :], kdf_ref[0, rs, :], kcf_ref[0, rs, :],
                             qdb_ref[0, rs, :], kdb_ref[0, rs, :], kcb_ref[0, rs, :], v_ref[0, rs, :],
                             tf, tb, sf_ref[0, blk], sb_ref[0, blk], p_scr.at[blk])
        for hd in range(B_HEADS):
            on_scr[rs, hd * B_DV:(hd + 1) * B_DV] = _rms_rows(o_heads[hd])

    def proj(c0, n):
        return jnp.dot(h16, w_ref[:, c0:c0 + n], preferred_element_type=F32)

    gla(0)
    h16 = _latent_h(x_ref[0], mod_ref, ng_ref).astype(BF16)
    zb = proj(0, B_VAL_WIDTH)
    u = proj(B_VAL_WIDTH, A_WIDTH)
    za = proj(B_VAL_WIDTH + A_WIDTH, A_WIDTH)
    for blk in range(1, len(blocks)):
        gla(blk)
    gate_a = jax.nn.sigmoid(proj(B_VAL_WIDTH + 2 * A_WIDTH, D_MODEL))

    uz = u * _silu(za)
    for g in range(A_GROUPS):
        cs = slice(g * A_GROUP_DIM, (g + 1) * A_GROUP_DIM)
        if g < A_ROW_GROUPS:
            chunks = [slice(c * A_CHUNK, (c + 1) * A_CHUNK) for c in range(t // A_CHUNK)]
            sv_all = jnp.dot(ws_ref[g].astype(BF16), jnp.concatenate([vr_ref[0, rs, cs] for rs in chunks], axis=1),
                             preferred_element_type=F32)
            for c, rs in enumerate(chunks):
                sv = sv_all[:, c * A_GROUP_DIM:(c + 1) * A_GROUP_DIM] + bs_ref[g][:, 0:1]
                acta_scr[rs, cs] = (uz[rs, cs] * sv).astype(BF16)
        else:
            sv = jnp.swapaxes(sc_ref[0, g - A_ROW_GROUPS], 0, 1).reshape(t, A_GROUP_DIM)
            acta_scr[:, cs] = (uz[:, cs] * sv).astype(BF16)
    ya = jnp.dot(acta_scr[...], wpa_ref[...], preferred_element_type=F32)

    gate_b = jax.nn.sigmoid(proj(B_VAL_WIDTH + 2 * A_WIDTH + D_MODEL, D_MODEL))
    yb = _dot(on_scr[...] * (_silu(zb) * bg_ref[...]), wpb_ref[...])

    m = (gate_a * ya + gate_b * yb).astype(BF16)
    for rs in blocks:
        y = jnp.dot(m[rs], wo_ref[...], preferred_element_type=F32)
        xo = x_ref[0, rs, :] + mod_ref[0, 2:3, :] * y
        o_ref[0, rs, :] = _rms_rows(xo) * fg_ref[...]


def _main_call(x, mod3, ng, w_main, vn_row, sv_col, gla_ops, ws, bs, bg, wpa, wpb, wo, fg, t):
    b, l, _ = x.shape
    nw = w_main.shape[1]
    ncg = A_GROUPS - A_ROW_GROUPS
    const2 = lambda i, j: (0, 0)
    const3 = lambda i, j: (0, 0, 0)
    tok = lambda width: pl.BlockSpec((1, t, width), lambda i, j: (i, j, 0))
    nblk = t // GLA_BLOCK
    tot_spec = pl.BlockSpec((1, nblk, GLA_NC, B_KEY_WIDTH), lambda i, j: (i, j, 0, 0))
    st_spec = pl.BlockSpec((1, nblk, B_KEY_WIDTH, B_DV), lambda i, j: (i, j, 0, 0))
    return pl.pallas_call(
        _main_kernel,
        grid=(b, l // t),
        in_specs=[tok(D_MODEL),
                  pl.BlockSpec((1, 3, D_MODEL), lambda i, j: (i, 0, 0)),
                  pl.BlockSpec((1, D_MODEL), const2),
                  pl.BlockSpec((D_MODEL, nw), const2),
                  tok(A_ROW_GROUPS * A_GROUP_DIM),
                  pl.BlockSpec((1, ncg, GRID_W, t // GRID_W, A_GROUP_DIM), lambda i, j: (i, 0, 0, j, 0)),
                  tok(B_KEY_WIDTH), tok(B_KEY_WIDTH), tok(B_KEY_WIDTH),
                  tok(B_KEY_WIDTH), tok(B_KEY_WIDTH), tok(B_KEY_WIDTH), tok(B_VAL_WIDTH),
                  tot_spec, tot_spec, st_spec, st_spec,
                  pl.BlockSpec((A_GROUPS, A_CHUNK, A_CHUNK), const3),
                  pl.BlockSpec((A_GROUPS, A_CHUNK, A_GROUP_DIM), const3),
                  pl.BlockSpec((1, B_VAL_WIDTH), const2),
                  pl.BlockSpec((A_WIDTH, D_MODEL), const2),
                  pl.BlockSpec((B_VAL_WIDTH, D_MODEL), const2),
                  pl.BlockSpec((D_MODEL, D_MODEL), const2),
                  pl.BlockSpec((1, D_MODEL), const2)],
        out_specs=tok(D_MODEL),
        out_shape=jax.ShapeDtypeStruct((b, l, D_MODEL), F32),
        scratch_shapes=[pltpu.VMEM((t, A_WIDTH), BF16), pltpu.VMEM((t, B_VAL_WIDTH), F32),
                        pltpu.VMEM((nblk, B_HEADS, GLA_BLOCK, GLA_BLOCK), BF16)],
        compiler_params=pltpu.CompilerParams(
            dimension_semantics=("parallel", "parallel"), vmem_limit_bytes=VMEM_LIMIT_BYTES),
        name="main",
    )(x, mod3, ng, w_main, vn_row, sv_col, *gla_ops, ws, bs, bg, wpa, wpb, wo, fg)


def kernel(x, c, ctx, c_ctx, w_mod, b_mod, norm_g, w_in, a_ln_g, a_ln_b, a_ws, a_bs, b_gate_w2, b_gate_b,
           b_norm_g, w_proj_a, w_proj_b, w_out, final_norm_g):
    assert w_mod.shape[0] == 1, "single-layer block"
    b, l, _ = x.shape
    rows = l // GRID_W
    assert rows == A_CHUNK
    ng = norm_g[0][None, :]

    cc = jnp.zeros((8, D_MODEL), F32).at[0:b].set(c).at[b].set(c_ctx)
    w_pre, w_main, wpa16, wpb16, wo16, mod = _wprep_call(w_in[0].T, w_proj_a, w_proj_b, w_out, cc, w_mod, b_mod)
    mod3 = mod[0:b].reshape(b, 3, D_MODEL)
    w2, gb = b_gate_w2[0], b_gate_b[0]
    s0f, s0b = _ctx_call(ctx, mod, ng, w_pre, w2, gb, b)

    (qd_f, kd_f, kc_f, qd_b, kd_b, kc_b, v16, tot_f, tot_b, kv_f, kv_b, vn_row, vn_col) = _pre_call(
        x, mod3, ng, w_pre, w2, gb, a_ln_g[0][None, :], a_ln_b[0][None, :], 512)

    s_f, s_b = _gscan_call(kv_f, kv_b, tot_f, tot_b, s0f, s0b)
    gla_ops = (qd_f, kd_f, kc_f, qd_b, kd_b, kc_b, v16, tot_f, tot_b, s_f, s_b)

    bs_b = jnp.broadcast_to(a_bs[0][:, :, None], (A_GROUPS, A_CHUNK, A_GROUP_DIM))
    sv_col = _colmix_call(a_ws[0][A_ROW_GROUPS:], bs_b[A_ROW_GROUPS:], vn_col, 16)

    return _main_call(x, mod3, ng, w_main, vn_row, sv_col, gla_ops, a_ws[0], bs_b, b_norm_g[0][None, :],
                      wpa16, wpb16, wo16, final_norm_g[None, :], 512)
```

```python
import functools

import jax
import jax.numpy as jnp
from jax import lax
from jax.experimental import pallas as pl
from jax.experimental.pallas import tpu as pltpu

D_MODEL = 1024
GRID_W = 64
EPS = 1e-6

A_WIDTH = 512
A_GROUPS = 4
A_GROUP_DIM = 128
A_CHUNK = 128
A_ROW_GROUPS = 2

B_HEADS = 4
B_DK = 64
B_DV = 128
B_KEY_WIDTH = 256
B_VAL_WIDTH = 512
B_GATE_RANK = 16
B_GATE_TAU = 16.0
LOG2E = 1.4426950408889634
B_CHUNK = 64

Q0 = 0
K0 = Q0 + B_KEY_WIDTH
V0 = K0 + B_KEY_WIDTH
LR0 = V0 + B_VAL_WIDTH
ZB0 = LR0 + 2 * B_GATE_RANK
UA0 = ZB0 + B_VAL_WIDTH
VA0 = UA0 + A_WIDTH
ZA0 = VA0 + A_WIDTH
G0 = ZA0 + A_WIDTH
IN_WIDTH = G0 + 2 * D_MODEL

LANES = 128

P_LR = 0
P_K = P_LR + LANES
P_Q = P_K + B_KEY_WIDTH
P_V = P_Q + B_KEY_WIDTH
P_VA = P_V + B_VAL_WIDTH
PRE_WIDTH = P_VA + A_WIDTH
GLA_BLOCK = 256
GLA_NC = GLA_BLOCK // B_CHUNK
SCAN_GROUP = 8

VMEM_LIMIT_BYTES = 56 * 1024 * 1024

BF16 = jnp.bfloat16
F32 = jnp.float32


def _dot(a, b):
    return jnp.dot(a.astype(BF16), b.astype(BF16), preferred_element_type=F32)


def _dot_nt(a, b):
    return lax.dot_general(a.astype(BF16), b.astype(BF16), (((1,), (1,)), ((), ())),
                           preferred_element_type=F32)


def _dot_tn(a, b):
    return lax.dot_general(a.astype(BF16), b.astype(BF16), (((0,), (0,)), ((), ())),
                           preferred_element_type=F32)


def _silu(x):
    return x * jax.nn.sigmoid(x)


def _rms_rows(x):
    return x * lax.rsqrt(jnp.mean(x * x, axis=-1, keepdims=True) + EPS)


def _wprep_kernel(wt_ref, wpa_ref, wpb_ref, wo_ref, c_ref, wm_ref, bm_ref,
                  wpre_ref, wmain_ref, wpa16_ref, wpb16_ref, wo16_ref, mod_ref):
    def put(dst_ref, c0, r0, n, scale=None):
        for s in range(0, n, 2 * LANES):
            m = min(2 * LANES, n - s)
            blk = wt_ref[r0 + s:r0 + s + m, :].T
            dst_ref[:, c0 + s:c0 + s + m] = (blk if scale is None else blk * scale).astype(BF16)

    put(wpre_ref, P_LR, LR0, LANES)
    put(wpre_ref, P_K, K0, B_KEY_WIDTH)
    put(wpre_ref, P_Q, Q0, B_KEY_WIDTH, B_DK ** -0.5)
    put(wpre_ref, P_V, V0, B_VAL_WIDTH)
    put(wpre_ref, P_VA, VA0, A_WIDTH)
    put(wmain_ref, 0, ZB0, VA0 - ZB0)
    put(wmain_ref, VA0 - ZB0, ZA0, IN_WIDTH - ZA0)
    wpa16_ref[...] = wpa_ref[0].astype(BF16)
    wpb16_ref[...] = wpb_ref[0].astype(BF16)
    wo16_ref[...] = wo_ref[0].astype(BF16)
    mod_ref[...] = _dot(_silu(c_ref[...]), wm_ref[0]) + bm_ref[...]


def _wprep_call(w_in_t, w_proj_a, w_proj_b, w_out, cc, w_mod, b_mod):
    steps = 4
    n_main = (VA0 - ZB0) + (IN_WIDTH - ZA0)

    def rows3(a):
        return pl.BlockSpec((1, a.shape[1] // steps, a.shape[2]), lambda i: (0, i, 0))

    def rows2(nrows, ncols):
        return (pl.BlockSpec((nrows // steps, ncols), lambda i: (i, 0)),
                jax.ShapeDtypeStruct((nrows, ncols), BF16))

    n_mod = w_mod.shape[2]
    outs = [rows2(D_MODEL, PRE_WIDTH), rows2(D_MODEL, n_main), rows2(A_WIDTH, D_MODEL),
            rows2(B_VAL_WIDTH, D_MODEL), rows2(D_MODEL, D_MODEL),
            (pl.BlockSpec((8, n_mod // steps), lambda i: (0, i)), jax.ShapeDtypeStruct((8, n_mod), F32))]
    return pl.pallas_call(
        _wprep_kernel,
        grid=(steps,),
        in_specs=[pl.BlockSpec((IN_WIDTH, D_MODEL // steps), lambda i: (0, i)),
                  rows3(w_proj_a), rows3(w_proj_b), rows3(w_out),
                  pl.BlockSpec((8, D_MODEL), lambda i: (0, 0)),
                  pl.BlockSpec((1, D_MODEL, n_mod // steps), lambda i: (0, 0, i)),
                  pl.BlockSpec((1, n_mod // steps), lambda i: (0, i))],
        out_specs=[o[0] for o in outs],
        out_shape=[o[1] for o in outs],
        compiler_params=pltpu.CompilerParams(
            dimension_semantics=("parallel",), vmem_limit_bytes=VMEM_LIMIT_BYTES),
        name="wprep",
    )(w_in_t, w_proj_a, w_proj_b, w_out, cc, w_mod, b_mod)


def _gate_logs(lr, w2_ref, gb_ref):
    out = []
    for r in range(2):
        logits = _dot(lr[:, r * B_GATE_RANK:(r + 1) * B_GATE_RANK], w2_ref[r]) + gb_ref[r:r + 1, :]
        log_sig = jnp.minimum(logits, 0.0) - jnp.log(1.0 + jnp.exp(-jnp.abs(logits)))
        out.append(log_sig * (LOG2E / B_GATE_TAU))
    return out


def _chunk_tri(reverse):
    i = lax.broadcasted_iota(jnp.int32, (GLA_BLOCK, GLA_BLOCK), 0)
    j = lax.broadcasted_iota(jnp.int32, (GLA_BLOCK, GLA_BLOCK), 1)
    same = (i // B_CHUNK) == (j // B_CHUNK)
    tri = (j >= i) if reverse else (j <= i)
    return (same & tri).astype(BF16)


def _sum_rows(rows):
    acc = rows[0]
    for r in rows[1:]:
        acc = acc + r
    return acc


def _block_cum(a, tri, reverse):
    hi = a.astype(BF16)
    lo = (a - hi.astype(F32)).astype(BF16)
    cum = (jnp.dot(tri, hi, preferred_element_type=F32) + jnp.dot(tri, lo, preferred_element_type=F32))
    last = 0 if reverse else B_CHUNK - 1
    tots = [cum[c * B_CHUNK + last:c * B_CHUNK + last + 1, :] for c in range(GLA_NC)]
    return cum, tots


def _block_keys(k, cum, tots, reverse):
    totb = jnp.concatenate([jnp.broadcast_to(t, (B_CHUNK, B_KEY_WIDTH)) for t in tots], axis=0)
    kdec = k * jnp.exp2(totb - cum)
    later = []
    for c in range(GLA_NC):
        idx = list(range(0, c)) if reverse else list(range(c + 1, GLA_NC))
        if idx:
            later.append(jnp.broadcast_to(jnp.exp2(_sum_rows([tots[m] for m in idx])), (B_CHUNK, B_KEY_WIDTH)))
        else:
            later.append(jnp.ones((B_CHUNK, B_KEY_WIDTH), F32))
    return kdec, kdec * jnp.concatenate(later, axis=0)


def _pre_project(h16, w_ref, want_q):
    lr = jnp.dot(h16, w_ref[:, P_LR:P_LR + LANES], preferred_element_type=F32)[:, 0:2 * B_GATE_RANK]
    k = jnp.dot(h16, w_ref[:, P_K:P_K + B_KEY_WIDTH], preferred_element_type=F32)
    q = jnp.dot(h16, w_ref[:, P_Q:P_Q + B_KEY_WIDTH], preferred_element_type=F32) if want_q else None
    v = jnp.dot(h16, w_ref[:, P_V:P_V + B_VAL_WIDTH], preferred_element_type=F32)
    return lr, k, q, v


def _block_kv_t(v16, kblk):
    k16 = kblk.astype(BF16)
    lane = lax.broadcasted_iota(jnp.int32, (1, B_KEY_WIDTH), 1)
    acc = jnp.zeros((B_DV, B_KEY_WIDTH), F32)
    for h in range(B_HEADS):
        full = _dot_tn(v16[:, h * B_DV:(h + 1) * B_DV], k16)
        acc = acc + jnp.where((lane // B_DK) == h, full, 0.0)
    return acc


def _ctx_kernel(ctx_ref, mod_ref, ng_ref, w_ref, w2_ref, gb_ref, sf_ref, sb_ref, *, ctx_row):
    nb, lc, _ = ctx_ref.shape
    xc = ctx_ref[...].reshape(nb * lc, D_MODEL)
    shift = mod_ref[ctx_row:ctx_row + 1, 0:D_MODEL]
    scale = mod_ref[ctx_row:ctx_row + 1, D_MODEL:2 * D_MODEL]
    hc = _rms_rows(xc) * ng_ref[...] * (1.0 + scale) + shift
    lr, k, _, v = _pre_project(hc.astype(BF16), w_ref, False)
    v16 = v.astype(BF16)
    a_f, a_b = _gate_logs(lr, w2_ref, gb_ref)
    for bi in range(nb):
        rs = slice(bi * lc, (bi + 1) * lc)
        for a, reverse, out_ref in ((a_f, False, sf_ref), (a_b, True, sb_ref)):
            cum, tots = _block_cum(a[rs], _chunk_tri(reverse), reverse)
            _, kblk = _block_keys(k[rs], cum, tots, reverse)
            out_ref[bi] = _block_kv_t(v16[rs], kblk)


def _ctx_call(ctx, mod, ng, w_kvl, w2, gb, ctx_row):
    b, lc, _ = ctx.shape
    assert lc == GLA_BLOCK, "context length must be one GLA block"
    nw = w_kvl.shape[1]
    st = jax.ShapeDtypeStruct((b, B_DV, B_KEY_WIDTH), F32)
    st_spec = pl.BlockSpec((b, B_DV, B_KEY_WIDTH), lambda i: (0, 0, 0))
    return pl.pallas_call(
        functools.partial(_ctx_kernel, ctx_row=ctx_row),
        grid=(1,),
        in_specs=[pl.BlockSpec((b, lc, D_MODEL), lambda i: (0, 0, 0)),
                  pl.BlockSpec((8, 3 * D_MODEL), lambda i: (0, 0)),
                  pl.BlockSpec((1, D_MODEL), lambda i: (0, 0)),
                  pl.BlockSpec((D_MODEL, nw), lambda i: (0, 0)),
                  pl.BlockSpec((2, B_GATE_RANK, B_KEY_WIDTH), lambda i: (0, 0, 0)),
                  pl.BlockSpec((2, B_KEY_WIDTH), lambda i: (0, 0))],
        out_specs=[st_spec, st_spec],
        out_shape=[st, st],
        name="ctx",
    )(ctx, mod, ng, w_kvl, w2, gb)


def _latent_h(x, mod_ref, ng_ref):
    shift = mod_ref[0, 0:1, :]
    gain = ng_ref[...] * (1.0 + mod_ref[0, 1:2, :])
    return _rms_rows(x) * gain + shift


def _pre_kernel(x_ref, mod_ref, ng_ref, w_ref, w2_ref, gb_ref, lg_ref, lb_ref,
                qdf_ref, kdf_ref, kcf_ref, qdb_ref, kdb_ref, kcb_ref, v_ref,
                totf_ref, totb_ref, kvf_ref, kvb_ref, vr_ref, vc_ref):
    h16 = _latent_h(x_ref[0], mod_ref, ng_ref).astype(BF16)
    nblk = x_ref.shape[1] // GLA_BLOCK
    blocks = [slice(blk * GLA_BLOCK, (blk + 1) * GLA_BLOCK) for blk in range(nblk)]
    dirs = ((False, qdf_ref, kdf_ref, kcf_ref, totf_ref, kvf_ref), (True, qdb_ref, kdb_ref, kcb_ref, totb_ref, kvb_ref))

    lrk = jnp.dot(h16, w_ref[:, P_LR:P_K + B_KEY_WIDTH], preferred_element_type=F32)
    lr = lrk[:, 0:2 * B_GATE_RANK]
    k = lrk[:, P_K:P_K + B_KEY_WIDTH]
    logs = _gate_logs(lr, w2_ref, gb_ref)
    q = jnp.dot(h16, w_ref[:, P_Q:P_Q + B_KEY_WIDTH], preferred_element_type=F32)
    v16 = jnp.dot(h16, w_ref[:, P_V:P_V + B_VAL_WIDTH], preferred_element_type=F32).astype(BF16)
    v_ref[0] = v16

    cums = {}
    for d, (reverse, _, _, _, tot_ref, _) in enumerate(dirs):
        tri = _chunk_tri(reverse)
        for blk, rs in enumerate(blocks):
            cum, tots = _block_cum(logs[d][rs], tri, reverse)
            cums[d, blk] = (cum, tots)
            for c in range(GLA_NC):
                tot_ref[0, blk, c:c + 1, :] = tots[c]

    va = jnp.dot(h16, w_ref[:, P_VA:P_VA + A_WIDTH], preferred_element_type=F32)

    for d, (reverse, qd_ref, kd_ref, kc_ref, _, kv_ref) in enumerate(dirs):
        for blk, rs in enumerate(blocks):
            cum, tots = cums[d, blk]
            kdec, kblk = _block_keys(k[rs], cum, tots, reverse)
            kc_ref[0, rs, :] = kdec.astype(BF16)
            kd_ref[0, rs, :] = (k[rs] * jnp.exp2(-cum)).astype(BF16)
            qd_ref[0, rs, :] = (q[rs] * jnp.exp2(cum)).astype(BF16)
            kv_ref[0, blk] = _block_kv_t(v16[rs], kblk)

    vc = va - jnp.mean(va, axis=-1, keepdims=True)
    vn = vc * lax.rsqrt(jnp.mean(vc * vc, axis=-1, keepdims=True) + EPS) * lg_ref[...] + lb_ref[...]
    nr = A_ROW_GROUPS * A_GROUP_DIM
    vr_ref[0] = vn[:, 0:nr].astype(BF16)
    for g in range(A_GROUPS - A_ROW_GROUPS):
        vg = vn[:, nr + g * A_GROUP_DIM:nr + (g + 1) * A_GROUP_DIM]
        vc_ref[0, g] = jnp.swapaxes(vg.reshape(vg.shape[0] // GRID_W, GRID_W, A_GROUP_DIM), 0, 1)


def _pre_call(x, mod3, ng, w_pre, w2, gb, lg, lb, t):
    b, l, _ = x.shape
    nw = w_pre.shape[1]
    ncg = A_GROUPS - A_ROW_GROUPS
    nblk = t // GLA_BLOCK

    def tok(width, dtype):
        return (pl.BlockSpec((1, t, width), lambda i, j: (i, j, 0)),
                jax.ShapeDtypeStruct((b, l, width), dtype))

    tot = (pl.BlockSpec((1, nblk, GLA_NC, B_KEY_WIDTH), lambda i, j: (i, j, 0, 0)),
           jax.ShapeDtypeStruct((b, l // GLA_BLOCK, GLA_NC, B_KEY_WIDTH), F32))
    kv = (pl.BlockSpec((1, nblk, B_DV, B_KEY_WIDTH), lambda i, j: (i, j, 0, 0)),
          jax.ShapeDtypeStruct((b, l // GLA_BLOCK, B_DV, B_KEY_WIDTH), F32))
    outs = [tok(B_KEY_WIDTH, BF16)] * 6 + [tok(B_VAL_WIDTH, BF16), tot, tot, kv, kv,
                                           tok(A_ROW_GROUPS * A_GROUP_DIM, BF16),
                                           (pl.BlockSpec((1, ncg, GRID_W, t // GRID_W, A_GROUP_DIM),
                                                         lambda i, j: (i, 0, 0, j, 0)),
                                            jax.ShapeDtypeStruct((b, ncg, GRID_W, l // GRID_W, A_GROUP_DIM), F32))]
    const2 = lambda i, j: (0, 0)
    return pl.pallas_call(
        _pre_kernel,
        grid=(b, l // t),
        in_specs=[pl.BlockSpec((1, t, D_MODEL), lambda i, j: (i, j, 0)),
                  pl.BlockSpec((1, 3, D_MODEL), lambda i, j: (i, 0, 0)),
                  pl.BlockSpec((1, D_MODEL), const2),
                  pl.BlockSpec((D_MODEL, nw), const2),
                  pl.BlockSpec((2, B_GATE_RANK, B_KEY_WIDTH), lambda i, j: (0, 0, 0)),
                  pl.BlockSpec((2, B_KEY_WIDTH), const2),
                  pl.BlockSpec((1, A_WIDTH), const2),
                  pl.BlockSpec((1, A_WIDTH), const2)],
        out_specs=[o[0] for o in outs],
        out_shape=[o[1] for o in outs],
        compiler_params=pltpu.CompilerParams(
            dimension_semantics=("parallel", "parallel"), vmem_limit_bytes=VMEM_LIMIT_BYTES),
        name="pre",
    )(x, mod3, ng, w_pre, w2, gb, lg, lb)


def _gscan_kernel(kvf_ref, kvb_ref, totf_ref, totb_ref, s0f_ref, s0b_ref, sf_ref, sb_ref, stf_scr, stb_scr):
    @pl.when(pl.program_id(1) == 0)
    def _():
        stf_scr[...] = s0f_ref[0]
        stb_scr[...] = s0b_ref[0]

    n = kvf_ref.shape[1]
    sf = stf_scr[...]
    sb = stb_scr[...]
    for i in range(n):
        sf_ref[0, i] = sf.T.astype(BF16)
        dec = jnp.exp2(jnp.sum(totf_ref[0, i], axis=0, keepdims=True))
        sf = dec * sf + kvf_ref[0, i]
        ib = n - 1 - i
        sb_ref[0, ib] = sb.T.astype(BF16)
        dec = jnp.exp2(jnp.sum(totb_ref[0, ib], axis=0, keepdims=True))
        sb = dec * sb + kvb_ref[0, ib]
    stf_scr[...] = sf
    stb_scr[...] = sb


def _gscan_call(kv_f, kv_b, tot_f, tot_b, s0f, s0b):
    b, nb = kv_f.shape[0], kv_f.shape[1]
    g = SCAN_GROUP
    nj = nb // g
    fwd = lambda i, j: (i, j, 0, 0)
    bwd = lambda i, j: (i, nj - 1 - j, 0, 0)
    kv_blk = (1, g, B_DV, B_KEY_WIDTH)
    tot_blk = (1, g, GLA_NC, B_KEY_WIDTH)
    st_spec = pl.BlockSpec((1, B_DV, B_KEY_WIDTH), lambda i, j: (i, 0, 0))
    s_blk = (1, g, B_KEY_WIDTH, B_DV)
    s_shape = jax.ShapeDtypeStruct((b, nb, B_KEY_WIDTH, B_DV), BF16)
    return pl.pallas_call(
        _gscan_kernel,
        grid=(b, nj),
        in_specs=[pl.BlockSpec(kv_blk, fwd), pl.BlockSpec(kv_blk, bwd),
                  pl.BlockSpec(tot_blk, fwd), pl.BlockSpec(tot_blk, bwd), st_spec, st_spec],
        out_specs=[pl.BlockSpec(s_blk, fwd), pl.BlockSpec(s_blk, bwd)],
        out_shape=[s_shape, s_shape],
        scratch_shapes=[pltpu.VMEM((B_DV, B_KEY_WIDTH), F32), pltpu.VMEM((B_DV, B_KEY_WIDTH), F32)],
        compiler_params=pltpu.CompilerParams(
            dimension_semantics=("parallel", "arbitrary"), vmem_limit_bytes=VMEM_LIMIT_BYTES),
        name="gscan",
    )(kv_f, kv_b, tot_f, tot_b, s0f, s0b)


def _scale_rows(x16, scales):
    parts = []
    for c, s in enumerate(scales):
        xc = x16[c * B_CHUNK:(c + 1) * B_CHUNK, :]
        parts.append(xc if s is None else (xc.astype(F32) * s).astype(BF16))
    return parts[0] if len(parts) == 1 else jnp.concatenate(parts, axis=0)


def _exp_sum(tots, idx):
    return jnp.exp2(_sum_rows([tots[m] for m in idx])) if idx else None


def _gla_block(qdf, kdf, kcf, qdb, kdb, kcb, v16, tf, tb, sf, sb, p_scr):
    nc = GLA_NC
    ch = B_CHUNK
    lane = lax.broadcasted_iota(jnp.int32, (1, LANES), 1)
    hm = [(lane < B_DK).astype(BF16), (lane >= B_DK).astype(BF16)]

    def pair_heads(x):
        return jnp.concatenate([x * hm[0], x * hm[1]], axis=0)

    row = lax.broadcasted_iota(jnp.int32, (ch, 2 * ch), 0)
    col = lax.broadcasted_iota(jnp.int32, (ch, 2 * ch), 1)
    first = col < ch
    zeros = jnp.zeros((ch, LANES), BF16)
    for g in range(B_HEADS // 2):
        ls = slice(g * LANES, (g + 1) * LANES)
        for i in range(nc // 2):
            pr = slice(2 * i * ch, (2 * i + 2) * ch)
            re = slice(2 * i * ch, (2 * i + 1) * ch)
            ro = slice((2 * i + 1) * ch, (2 * i + 2) * ch)
            sc_f = _dot_nt(pair_heads(qdf[pr, ls]), jnp.concatenate([kdf[pr, ls], kcf[re, ls], zeros], axis=0))
            sc_b = _dot_nt(pair_heads(qdb[pr, ls]), jnp.concatenate([kdb[pr, ls], zeros, kcb[ro, ls]], axis=0))
            for hh in range(2):
                he = slice(hh * 2 * ch, hh * 2 * ch + ch)
                ho = slice(hh * 2 * ch + ch, (hh + 1) * 2 * ch)
                even = jnp.where(first,
                                 jnp.where(col <= row, sc_f[he, 0:2 * ch], 0.0)
                                 + jnp.where(col >= row, sc_b[he, 0:2 * ch], 0.0),
                                 sc_b[he, 2 * ch:])
                odd = jnp.where(first, sc_f[ho, 2 * ch:],
                                jnp.where(col - ch <= row, sc_f[ho, 0:2 * ch], 0.0)
                                + jnp.where(col - ch >= row, sc_b[ho, 0:2 * ch], 0.0))
                p_scr[2 * g + hh, re, pr] = even.astype(BF16)
                p_scr[2 * g + hh, ro, pr] = odd.astype(BF16)

    def cross(lo, hi):
        if hi - lo <= 2:
            return
        mid = (lo + hi) // 2
        cross(lo, mid)
        cross(mid, hi)
        rl = slice(lo * ch, mid * ch)
        rh = slice(mid * ch, hi * ch)
        n = (hi - mid) * ch
        qf = _scale_rows(qdf[rh], [_exp_sum(tf, range(mid, c)) for c in range(mid, hi)])
        kf = _scale_rows(kcf[rl], [_exp_sum(tf, range(c + 1, mid)) for c in range(lo, mid)])
        qb = _scale_rows(qdb[rl], [_exp_sum(tb, range(c + 1, mid)) for c in range(lo, mid)])
        kb = _scale_rows(kcb[rh], [_exp_sum(tb, range(mid, c)) for c in range(mid, hi)])
        for g in range(B_HEADS // 2):
            ls = slice(g * LANES, (g + 1) * LANES)
            sc_f = _dot_nt(pair_heads(qf[:, ls]), kf[:, ls]).astype(BF16)
            sc_b = _dot_nt(pair_heads(qb[:, ls]), kb[:, ls]).astype(BF16)
            for hh in range(2):
                p_scr[2 * g + hh, rh, rl] = sc_f[hh * n:(hh + 1) * n]
                p_scr[2 * g + hh, rl, rh] = sc_b[hh * n:(hh + 1) * n]

    cross(0, nc)

    qsf = _scale_rows(qdf, [_exp_sum(tf, range(0, c)) for c in range(nc)])
    qsb = _scale_rows(qdb, [_exp_sum(tb, range(c + 1, nc)) for c in range(nc)])
    head_row = lax.broadcasted_iota(jnp.int32, (LANES, 1), 0) // B_DK
    outs = []
    for h in range(B_HEADS):
        ls = slice((h // 2) * LANES, (h // 2 + 1) * LANES)
        own = (head_row == (h % 2)).astype(BF16)
        lhs = jnp.concatenate([p_scr[h], qsf[:, ls], qsb[:, ls]], axis=1)
        rhs = jnp.concatenate([v16[:, h * B_DV:(h + 1) * B_DV], sf[ls, :] * own, sb[ls, :] * own], axis=0)
        outs.append(jnp.dot(lhs, rhs, preferred_element_type=F32))
    return outs


def _colmix_kernel(ws_ref, bs_ref, vn_ref, o_ref):
    nw = vn_ref.shape[2]
    xs = jnp.concatenate([vn_ref[0, 0, w] for w in range(nw)], axis=1)
    y = _dot(ws_ref[0], xs) + bs_ref[0][:, 0:1]
    for w in range(nw):
        o_ref[0, 0, w] = y[:, w * A_GROUP_DIM:(w + 1) * A_GROUP_DIM]


def _colmix_call(ws_col, bs_col, vn_col, nw):
    b, g, width, rows, ch = vn_col.shape
    blk = pl.BlockSpec((1, 1, nw, rows, ch), lambda i, j, m: (i, j, m, 0, 0))
    return pl.pallas_call(
        _colmix_kernel,
        grid=(b, g, width // nw),
        in_specs=[pl.BlockSpec((1, rows, rows), lambda i, j, m: (j, 0, 0)),
                  pl.BlockSpec((1, rows, A_GROUP_DIM), lambda i, j, m: (j, 0, 0)),
                  blk],
        out_specs=blk,
        out_shape=jax.ShapeDtypeStruct(vn_col.shape, F32),
        compiler_params=pltpu.CompilerParams(
            dimension_semantics=("parallel", "parallel", "parallel"), vmem_limit_bytes=VMEM_LIMIT_BYTES),
        name="colmix",
    )(ws_col, bs_col, vn_col)


def _main_kernel(x_ref, mod_ref, ng_ref, w_ref, vr_ref, sc_ref,
                 qdf_ref, kdf_ref, kcf_ref, qdb_ref, kdb_ref, kcb_ref, v_ref, totf_ref, totb_ref, sf_ref, sb_ref,
                 ws_ref, bs_ref, bg_ref, wpa_ref, wpb_ref, wo_ref, fg_ref, o_ref, acta_scr, on_scr, p_scr):
    t = x_ref.shape[1]
    blocks = [slice(blk * GLA_BLOCK, (blk + 1) * GLA_BLOCK) for blk in range(t // GLA_BLOCK)]

    def gla(blk):
        rs = blocks[blk]
        tf = [totf_ref[0, blk, c:c + 1, :] for c in range(GLA_NC)]
        tb = [totb_ref[0, blk, c:c + 1, :] for c in range(GLA_NC)]
        o_heads = _gla_block(qdf_ref[0, rs, :], kdf_ref[0, rs, :], kcf_ref[0, rs, :],
                             qdb_ref[0, rs, :], kdb_ref[0, rs, :], kcb_ref[0, rs, :], v_ref[0, rs, :],
                             tf, tb, sf_ref[0, blk], sb_ref[0, blk], p_scr.at[blk])
        for hd in range(B_HEADS):
            on_scr[rs, hd * B_DV:(hd + 1) * B_DV] = _rms_rows(o_heads[hd])

    def proj(c0, n):
        return jnp.dot(h16, w_ref[:, c0:c0 + n], preferred_element_type=F32)

    gla(0)
    h16 = _latent_h(x_ref[0], mod_ref, ng_ref).astype(BF16)
    zb = proj(0, B_VAL_WIDTH)
    u = proj(B_VAL_WIDTH, A_WIDTH)
    za = proj(B_VAL_WIDTH + A_WIDTH, A_WIDTH)
    for blk in range(1, len(blocks)):
        gla(blk)
    gate_a = jax.nn.sigmoid(proj(B_VAL_WIDTH + 2 * A_WIDTH, D_MODEL))

    uz = u * _silu(za)
    for g in range(A_GROUPS):
        cs = slice(g * A_GROUP_DIM, (g + 1) * A_GROUP_DIM)
        if g < A_ROW_GROUPS:
            chunks = [slice(c * A_CHUNK, (c + 1) * A_CHUNK) for c in range(t // A_CHUNK)]
            sv_all = jnp.dot(ws_ref[g].astype(BF16), jnp.concatenate([vr_ref[0, rs, cs] for rs in chunks], axis=1),
                             preferred_element_type=F32)
            for c, rs in enumerate(chunks):
                sv = sv_all[:, c * A_GROUP_DIM:(c + 1) * A_GROUP_DIM] + bs_ref[g][:, 0:1]
                acta_scr[rs, cs] = (uz[rs, cs] * sv).astype(BF16)
        else:
            sv = jnp.swapaxes(sc_ref[0, g - A_ROW_GROUPS], 0, 1).reshape(t, A_GROUP_DIM)
            acta_scr[:, cs] = (uz[:, cs] * sv).astype(BF16)
    ya = jnp.dot(acta_scr[...], wpa_ref[...], preferred_element_type=F32)

    gate_b = jax.nn.sigmoid(proj(B_VAL_WIDTH + 2 * A_WIDTH + D_MODEL, D_MODEL))
    yb = _dot(on_scr[...] * (_silu(zb) * bg_ref[...]), wpb_ref[...])

    m = (gate_a * ya + gate_b * yb).astype(BF16)
    for rs in blocks:
        y = jnp.dot(m[rs], wo_ref[...], preferred_element_type=F32)
        xo = x_ref[0, rs, :] + mod_ref[0, 2:3, :] * y
        o_ref[0, rs, :] = _rms_rows(xo) * fg_ref[...]


def _main_call(x, mod3, ng, w_main, vn_row, sv_col, gla_ops, ws, bs, bg, wpa, wpb, wo, fg, t):
    b, l, _ = x.shape
    nw = w_main.shape[1]
    ncg = A_GROUPS - A_ROW_GROUPS
    const2 = lambda i, j: (0, 0)
    const3 = lambda i, j: (0, 0, 0)
    tok = lambda width: pl.BlockSpec((1, t, width), lambda i, j: (i, j, 0))
    nblk = t // GLA_BLOCK
    tot_spec = pl.BlockSpec((1, nblk, GLA_NC, B_KEY_WIDTH), lambda i, j: (i, j, 0, 0))
    st_spec = pl.BlockSpec((1, nblk, B_KEY_WIDTH, B_DV), lambda i, j: (i, j, 0, 0))
    return pl.pallas_call(
        _main_kernel,
        grid=(b, l // t),
        in_specs=[tok(D_MODEL),
                  pl.BlockSpec((1, 3, D_MODEL), lambda i, j: (i, 0, 0)),
                  pl.BlockSpec((1, D_MODEL), const2),
                  pl.BlockSpec((D_MODEL, nw), const2),
                  tok(A_ROW_GROUPS * A_GROUP_DIM),
                  pl.BlockSpec((1, ncg, GRID_W, t // GRID_W, A_GROUP_DIM), lambda i, j: (i, 0, 0, j, 0)),
                  tok(B_KEY_WIDTH), tok(B_KEY_WIDTH), tok(B_KEY_WIDTH),
                  tok(B_KEY_WIDTH), tok(B_KEY_WIDTH), tok(B_KEY_WIDTH), tok(B_VAL_WIDTH),
                  tot_spec, tot_spec, st_spec, st_spec,
                  pl.BlockSpec((A_GROUPS, A_CHUNK, A_CHUNK), const3),
                  pl.BlockSpec((A_GROUPS, A_CHUNK, A_GROUP_DIM), const3),
                  pl.BlockSpec((1, B_VAL_WIDTH), const2),
                  pl.BlockSpec((A_WIDTH, D_MODEL), const2),
                  pl.BlockSpec((B_VAL_WIDTH, D_MODEL), const2),
                  pl.BlockSpec((D_MODEL, D_MODEL), const2),
                  pl.BlockSpec((1, D_MODEL), const2)],
        out_specs=tok(D_MODEL),
        out_shape=jax.ShapeDtypeStruct((b, l, D_MODEL), F32),
        scratch_shapes=[pltpu.VMEM((t, A_WIDTH), BF16), pltpu.VMEM((t, B_VAL_WIDTH), F32),
                        pltpu.VMEM((nblk, B_HEADS, GLA_BLOCK, GLA_BLOCK), BF16)],
        compiler_params=pltpu.CompilerParams(
            dimension_semantics=("parallel", "parallel"), vmem_limit_bytes=VMEM_LIMIT_BYTES),
        name="main",
    )(x, mod3, ng, w_main, vn_row, sv_col, *gla_ops, ws, bs, bg, wpa, wpb, wo, fg)


def kernel(x, c, ctx, c_ctx, w_mod, b_mod, norm_g, w_in, a_ln_g, a_ln_b, a_ws, a_bs, b_gate_w2, b_gate_b,
           b_norm_g, w_proj_a, w_proj_b, w_out, final_norm_g):
    assert w_mod.shape[0] == 1, "single-layer block"
    b, l, _ = x.shape
    rows = l // GRID_W
    assert rows == A_CHUNK
    ng = norm_g[0][None, :]

    cc = jnp.zeros((8, D_MODEL), F32).at[0:b].set(c).at[b].set(c_ctx)
    w_pre, w_main, wpa16, wpb16, wo16, mod = _wprep_call(w_in[0].T, w_proj_a, w_proj_b, w_out, cc, w_mod, b_mod)
    mod3 = mod[0:b].reshape(b, 3, D_MODEL)
    w2, gb = b_gate_w2[0], b_gate_b[0]
    s0f, s0b = _ctx_call(ctx, mod, ng, w_pre, w2, gb, b)

    (qd_f, kd_f, kc_f, qd_b, kd_b, kc_b, v16, tot_f, tot_b, kv_f, kv_b, vn_row, vn_col) = _pre_call(
        x, mod3, ng, w_pre, w2, gb, a_ln_g[0][None, :], a_ln_b[0][None, :], 1024)

    s_f, s_b = _gscan_call(kv_f, kv_b, tot_f, tot_b, s0f, s0b)
    gla_ops = (qd_f, kd_f, kc_f, qd_b, kd_b, kc_b, v16, tot_f, tot_b, s_f, s_b)

    bs_b = jnp.broadcast_to(a_bs[0][:, :, None], (A_GROUPS, A_CHUNK, A_GROUP_DIM))
    sv_col = _colmix_call(a_ws[0][A_ROW_GROUPS:], bs_b[A_ROW_GROUPS:], vn_col, 32)

    return _main_call(x, mod3, ng, w_main, vn_row, sv_col, gla_ops, a_ws[0], bs_b, b_norm_g[0][None, :],
                      wpa16, wpb16, wo16, final_norm_g[None, :], 512)
```

```python
import functools

import jax
import jax.numpy as jnp
from jax import lax
from jax.experimental import pallas as pl
from jax.experimental.pallas import tpu as pltpu

D_MODEL = 1024
GRID_W = 64
EPS = 1e-6

A_WIDTH = 512
A_GROUPS = 4
A_GROUP_DIM = 128
A_CHUNK = 128
A_ROW_GROUPS = 2

B_HEADS = 4
B_DK = 64
B_DV = 128
B_KEY_WIDTH = 256
B_VAL_WIDTH = 512
B_GATE_RANK = 16
B_GATE_TAU = 16.0
LOG2E = 1.4426950408889634
B_CHUNK = 64

Q0 = 0
K0 = Q0 + B_KEY_WIDTH
V0 = K0 + B_KEY_WIDTH
LR0 = V0 + B_VAL_WIDTH
ZB0 = LR0 + 2 * B_GATE_RANK
UA0 = ZB0 + B_VAL_WIDTH
VA0 = UA0 + A_WIDTH
ZA0 = VA0 + A_WIDTH
G0 = ZA0 + A_WIDTH
IN_WIDTH = G0 + 2 * D_MODEL

LANES = 128

P_LR = 0
P_K = P_LR + LANES
P_Q = P_K + B_KEY_WIDTH
P_V = P_Q + B_KEY_WIDTH
P_VA = P_V + B_VAL_WIDTH
PRE_WIDTH = P_VA + A_WIDTH
GLA_BLOCK = 256
GLA_NC = GLA_BLOCK // B_CHUNK
SCAN_GROUP = 8

VMEM_LIMIT_BYTES = 56 * 1024 * 1024

BF16 = jnp.bfloat16
F32 = jnp.float32


def _dot(a, b):
    return jnp.dot(a.astype(BF16), b.astype(BF16), preferred_element_type=F32)


def _dot_nt(a, b):
    return lax.dot_general(a.astype(BF16), b.astype(BF16), (((1,), (1,)), ((), ())),
                           preferred_element_type=F32)


def _dot_tn(a, b):
    return lax.dot_general(a.astype(BF16), b.astype(BF16), (((0,), (0,)), ((), ())),
                           preferred_element_type=F32)


def _silu(x):
    return x * jax.nn.sigmoid(x)


def _rms_rows(x):
    return x * lax.rsqrt(jnp.mean(x * x, axis=-1, keepdims=True) + EPS)


def _wprep_kernel(wt_ref, wpa_ref, wpb_ref, wo_ref, c_ref, wm_ref, bm_ref,
                  wpre_ref, wmain_ref, wpa16_ref, wpb16_ref, wo16_ref, mod_ref):
    def put(dst_ref, c0, r0, n, scale=None):
        for s in range(0, n, 2 * LANES):
            m = min(2 * LANES, n - s)
            blk = wt_ref[r0 + s:r0 + s + m, :].T
            dst_ref[:, c0 + s:c0 + s + m] = (blk if scale is None else blk * scale).astype(BF16)

    put(wpre_ref, P_LR, LR0, LANES)
    put(wpre_ref, P_K, K0, B_KEY_WIDTH)
    put(wpre_ref, P_Q, Q0, B_KEY_WIDTH, B_DK ** -0.5)
    put(wpre_ref, P_V, V0, B_VAL_WIDTH)
    put(wpre_ref, P_VA, VA0, A_WIDTH)
    put(wmain_ref, 0, ZB0, VA0 - ZB0)
    put(wmain_ref, VA0 - ZB0, ZA0, IN_WIDTH - ZA0)
    wpa16_ref[...] = wpa_ref[0].astype(BF16)
    wpb16_ref[...] = wpb_ref[0].astype(BF16)
    wo16_ref[...] = wo_ref[0].astype(BF16)
    mod_ref[...] = _dot(_silu(c_ref[...]), wm_ref[0]) + bm_ref[...]


def _wprep_call(w_in_t, w_proj_a, w_proj_b, w_out, cc, w_mod, b_mod):
    steps = 4
    n_main = (VA0 - ZB0) + (IN_WIDTH - ZA0)

    def rows3(a):
        return pl.BlockSpec((1, a.shape[1] // steps, a.shape[2]), lambda i: (0, i, 0))

    def rows2(nrows, ncols):
        return (pl.BlockSpec((nrows // steps, ncols), lambda i: (i, 0)),
                jax.ShapeDtypeStruct((nrows, ncols), BF16))

    n_mod = w_mod.shape[2]
    outs = [rows2(D_MODEL, PRE_WIDTH), rows2(D_MODEL, n_main), rows2(A_WIDTH, D_MODEL),
            rows2(B_VAL_WIDTH, D_MODEL), rows2(D_MODEL, D_MODEL),
            (pl.BlockSpec((8, n_mod // steps), lambda i: (0, i)), jax.ShapeDtypeStruct((8, n_mod), F32))]
    return pl.pallas_call(
        _wprep_kernel,
        grid=(steps,),
        in_specs=[pl.BlockSpec((IN_WIDTH, D_MODEL // steps), lambda i: (0, i)),
                  rows3(w_proj_a), rows3(w_proj_b), rows3(w_out),
                  pl.BlockSpec((8, D_MODEL), lambda i: (0, 0)),
                  pl.BlockSpec((1, D_MODEL, n_mod // steps), lambda i: (0, 0, i)),
                  pl.BlockSpec((1, n_mod // steps), lambda i: (0, i))],
        out_specs=[o[0] for o in outs],
        out_shape=[o[1] for o in outs],
        compiler_params=pltpu.CompilerParams(
            dimension_semantics=("parallel",), vmem_limit_bytes=VMEM_LIMIT_BYTES),
        name="wprep",
    )(w_in_t, w_proj_a, w_proj_b, w_out, cc, w_mod, b_mod)


def _gate_logs(lr, w2_ref, gb_ref):
    out = []
    for r in range(2):
        logits = _dot(lr[:, r * B_GATE_RANK:(r + 1) * B_GATE_RANK], w2_ref[r]) + gb_ref[r:r + 1, :]
        log_sig = jnp.minimum(logits, 0.0) - jnp.log(1.0 + jnp.exp(-jnp.abs(logits)))
        out.append(log_sig * (LOG2E / B_GATE_TAU))
    return out


def _chunk_tri(reverse):
    i = lax.broadcasted_iota(jnp.int32, (GLA_BLOCK, GLA_BLOCK), 0)
    j = lax.broadcasted_iota(jnp.int32, (GLA_BLOCK, GLA_BLOCK), 1)
    same = (i // B_CHUNK) == (j // B_CHUNK)
    tri = (j >= i) if reverse else (j <= i)
    return (same & tri).astype(BF16)


def _sum_rows(rows):
    acc = rows[0]
    for r in rows[1:]:
        acc = acc + r
    return acc


def _block_cum(a, tri, reverse):
    hi = a.astype(BF16)
    lo = (a - hi.astype(F32)).astype(BF16)
    cum = (jnp.dot(tri, hi, preferred_element_type=F32) + jnp.dot(tri, lo, preferred_element_type=F32))
    last = 0 if reverse else B_CHUNK - 1
    tots = [cum[c * B_CHUNK + last:c * B_CHUNK + last + 1, :] for c in range(GLA_NC)]
    return cum, tots


def _block_keys(k, cum, tots, reverse):
    totb = jnp.concatenate([jnp.broadcast_to(t, (B_CHUNK, B_KEY_WIDTH)) for t in tots], axis=0)
    kdec = k * jnp.exp2(totb - cum)
    later = []
    for c in range(GLA_NC):
        idx = list(range(0, c)) if reverse else list(range(c + 1, GLA_NC))
        if idx:
            later.append(jnp.broadcast_to(jnp.exp2(_sum_rows([tots[m] for m in idx])), (B_CHUNK, B_KEY_WIDTH)))
        else:
            later.append(jnp.ones((B_CHUNK, B_KEY_WIDTH), F32))
    return kdec, kdec * jnp.concatenate(later, axis=0)


def _pre_project(h16, w_ref, want_q):
    lr = jnp.dot(h16, w_ref[:, P_LR:P_LR + LANES], preferred_element_type=F32)[:, 0:2 * B_GATE_RANK]
    k = jnp.dot(h16, w_ref[:, P_K:P_K + B_KEY_WIDTH], preferred_element_type=F32)
    q = jnp.dot(h16, w_ref[:, P_Q:P_Q + B_KEY_WIDTH], preferred_element_type=F32) if want_q else None
    v = jnp.dot(h16, w_ref[:, P_V:P_V + B_VAL_WIDTH], preferred_element_type=F32)
    return lr, k, q, v


def _block_kv_t(v16, kblk):
    k16 = kblk.astype(BF16)
    lane = lax.broadcasted_iota(jnp.int32, (1, B_KEY_WIDTH), 1)
    acc = jnp.zeros((B_DV, B_KEY_WIDTH), F32)
    for h in range(B_HEADS):
        full = _dot_tn(v16[:, h * B_DV:(h + 1) * B_DV], k16)
        acc = acc + jnp.where((lane // B_DK) == h, full, 0.0)
    return acc


def _ctx_kernel(ctx_ref, mod_ref, ng_ref, w_ref, w2_ref, gb_ref, sf_ref, sb_ref, *, ctx_row):
    nb, lc, _ = ctx_ref.shape
    xc = ctx_ref[...].reshape(nb * lc, D_MODEL)
    shift = mod_ref[ctx_row:ctx_row + 1, 0:D_MODEL]
    scale = mod_ref[ctx_row:ctx_row + 1, D_MODEL:2 * D_MODEL]
    hc = _rms_rows(xc) * ng_ref[...] * (1.0 + scale) + shift
    lr, k, _, v = _pre_project(hc.astype(BF16), w_ref, False)
    v16 = v.astype(BF16)
    a_f, a_b = _gate_logs(lr, w2_ref, gb_ref)
    for bi in range(nb):
        rs = slice(bi * lc, (bi + 1) * lc)
        for a, reverse, out_ref in ((a_f, False, sf_ref), (a_b, True, sb_ref)):
            cum, tots = _block_cum(a[rs], _chunk_tri(reverse), reverse)
            _, kblk = _block_keys(k[rs], cum, tots, reverse)
            out_ref[bi] = _block_kv_t(v16[rs], kblk)


def _ctx_call(ctx, mod, ng, w_kvl, w2, gb, ctx_row):
    b, lc, _ = ctx.shape
    assert lc == GLA_BLOCK, "context length must be one GLA block"
    nw = w_kvl.shape[1]
    st = jax.ShapeDtypeStruct((b, B_DV, B_KEY_WIDTH), F32)
    st_spec = pl.BlockSpec((b, B_DV, B_KEY_WIDTH), lambda i: (0, 0, 0))
    return pl.pallas_call(
        functools.partial(_ctx_kernel, ctx_row=ctx_row),
        grid=(1,),
        in_specs=[pl.BlockSpec((b, lc, D_MODEL), lambda i: (0, 0, 0)),
                  pl.BlockSpec((8, 3 * D_MODEL), lambda i: (0, 0)),
                  pl.BlockSpec((1, D_MODEL), lambda i: (0, 0)),
                  pl.BlockSpec((D_MODEL, nw), lambda i: (0, 0)),
                  pl.BlockSpec((2, B_GATE_RANK, B_KEY_WIDTH), lambda i: (0, 0, 0)),
                  pl.BlockSpec((2, B_KEY_WIDTH), lambda i: (0, 0))],
        out_specs=[st_spec, st_spec],
        out_shape=[st, st],
        name="ctx",
    )(ctx, mod, ng, w_kvl, w2, gb)


def _latent_h(x, mod_ref, ng_ref):
    shift = mod_ref[0, 0:1, :]
    gain = ng_ref[...] * (1.0 + mod_ref[0, 1:2, :])
    return _rms_rows(x) * gain + shift


def _pre_kernel(x_ref, mod_ref, ng_ref, w_ref, w2_ref, gb_ref, lg_ref, lb_ref,
                qdf_ref, kdf_ref, kcf_ref, qdb_ref, kdb_ref, kcb_ref, v_ref,
                totf_ref, totb_ref, kvf_ref, kvb_ref, vr_ref, vc_ref):
    h16 = _latent_h(x_ref[0], mod_ref, ng_ref).astype(BF16)
    nblk = x_ref.shape[1] // GLA_BLOCK
    blocks = [slice(blk * GLA_BLOCK, (blk + 1) * GLA_BLOCK) for blk in range(nblk)]
    dirs = ((False, qdf_ref, kdf_ref, kcf_ref, totf_ref, kvf_ref), (True, qdb_ref, kdb_ref, kcb_ref, totb_ref, kvb_ref))

    lrk = jnp.dot(h16, w_ref[:, P_LR:P_K + B_KEY_WIDTH], preferred_element_type=F32)
    lr = lrk[:, 0:2 * B_GATE_RANK]
    k = lrk[:, P_K:P_K + B_KEY_WIDTH]
    logs = _gate_logs(lr, w2_ref, gb_ref)
    q = jnp.dot(h16, w_ref[:, P_Q:P_Q + B_KEY_WIDTH], preferred_element_type=F32)
    v16 = jnp.dot(h16, w_ref[:, P_V:P_V + B_VAL_WIDTH], preferred_element_type=F32).astype(BF16)
    v_ref[0] = v16

    cums = {}
    for d, (reverse, _, _, _, tot_ref, _) in enumerate(dirs):
        tri = _chunk_tri(reverse)
        for blk, rs in enumerate(blocks):
            cum, tots = _block_cum(logs[d][rs], tri, reverse)
            cums[d, blk] = (cum, tots)
            for c in range(GLA_NC):
                tot_ref[0, blk, c:c + 1, :] = tots[c]

    va = jnp.dot(h16, w_ref[:, P_VA:P_VA + A_WIDTH], preferred_element_type=F32)

    for d, (reverse, qd_ref, kd_ref, kc_ref, _, kv_ref) in enumerate(dirs):
        for blk, rs in enumerate(blocks):
            cum, tots = cums[d, blk]
            kdec, kblk = _block_keys(k[rs], cum, tots, reverse)
            kc_ref[0, rs, :] = kdec.astype(BF16)
            kd_ref[0, rs, :] = (k[rs] * jnp.exp2(-cum)).astype(BF16)
            qd_ref[0, rs, :] = (q[rs] * jnp.exp2(cum)).astype(BF16)
            kv_ref[0, blk] = _block_kv_t(v16[rs], kblk)

    vc = va - jnp.mean(va, axis=-1, keepdims=True)
    vn = vc * lax.rsqrt(jnp.mean(vc * vc, axis=-1, keepdims=True) + EPS) * lg_ref[...] + lb_ref[...]
    nr = A_ROW_GROUPS * A_GROUP_DIM
    vr_ref[0] = vn[:, 0:nr].astype(BF16)
    for g in range(A_GROUPS - A_ROW_GROUPS):
        vg = vn[:, nr + g * A_GROUP_DIM:nr + (g + 1) * A_GROUP_DIM]
        vc_ref[0, g] = jnp.swapaxes(vg.reshape(vg.shape[0] // GRID_W, GRID_W, A_GROUP_DIM), 0, 1)


def _pre_call(x, mod3, ng, w_pre, w2, gb, lg, lb, t):
    b, l, _ = x.shape
    nw = w_pre.shape[1]
    ncg = A_GROUPS - A_ROW_GROUPS
    nblk = t // GLA_BLOCK

    def tok(width, dtype):
        return (pl.BlockSpec((1, t, width), lambda i, j: (i, j, 0)),
                jax.ShapeDtypeStruct((b, l, width), dtype))

    tot = (pl.BlockSpec((1, nblk, GLA_NC, B_KEY_WIDTH), lambda i, j: (i, j, 0, 0)),
           jax.ShapeDtypeStruct((b, l // GLA_BLOCK, GLA_NC, B_KEY_WIDTH), F32))
    kv = (pl.BlockSpec((1, nblk, B_DV, B_KEY_WIDTH), lambda i, j: (i, j, 0, 0)),
          jax.ShapeDtypeStruct((b, l // GLA_BLOCK, B_DV, B_KEY_WIDTH), F32))
    outs = [tok(B_KEY_WIDTH, BF16)] * 6 + [tok(B_VAL_WIDTH, BF16), tot, tot, kv, kv,
                                           tok(A_ROW_GROUPS * A_GROUP_DIM, BF16),
                                           (pl.BlockSpec((1, ncg, GRID_W, t // GRID_W, A_GROUP_DIM),
                                                         lambda i, j: (i, 0, 0, j, 0)),
                                            jax.ShapeDtypeStruct((b, ncg, GRID_W, l // GRID_W, A_GROUP_DIM), F32))]
    const2 = lambda i, j: (0, 0)
    return pl.pallas_call(
        _pre_kernel,
        grid=(b, l // t),
        in_specs=[pl.BlockSpec((1, t, D_MODEL), lambda i, j: (i, j, 0)),
                  pl.BlockSpec((1, 3, D_MODEL), lambda i, j: (i, 0, 0)),
                  pl.BlockSpec((1, D_MODEL), const2),
                  pl.BlockSpec((D_MODEL, nw), const2),
                  pl.BlockSpec((2, B_GATE_RANK, B_KEY_WIDTH), lambda i, j: (0, 0, 0)),
                  pl.BlockSpec((2, B_KEY_WIDTH), const2),
                  pl.BlockSpec((1, A_WIDTH), const2),
                  pl.BlockSpec((1, A_WIDTH), const2)],
        out_specs=[o[0] for o in outs],
        out_shape=[o[1] for o in outs],
        compiler_params=pltpu.CompilerParams(
            dimension_semantics=("parallel", "parallel"), vmem_limit_bytes=VMEM_LIMIT_BYTES),
        name="pre",
    )(x, mod3, ng, w_pre, w2, gb, lg, lb)


def _gscan_kernel(kvf_ref, kvb_ref, totf_ref, totb_ref, s0f_ref, s0b_ref, sf_ref, sb_ref, stf_scr, stb_scr):
    @pl.when(pl.program_id(1) == 0)
    def _():
        stf_scr[...] = s0f_ref[0]
        stb_scr[...] = s0b_ref[0]

    n = kvf_ref.shape[1]
    sf = stf_scr[...]
    sb = stb_scr[...]
    for i in range(n):
        sf_ref[0, i] = sf.T.astype(BF16)
        dec = jnp.exp2(jnp.sum(totf_ref[0, i], axis=0, keepdims=True))
        sf = dec * sf + kvf_ref[0, i]
        ib = n - 1 - i
        sb_ref[0, ib] = sb.T.astype(BF16)
        dec = jnp.exp2(jnp.sum(totb_ref[0, ib], axis=0, keepdims=True))
        sb = dec * sb + kvb_ref[0, ib]
    stf_scr[...] = sf
    stb_scr[...] = sb


def _gscan_call(kv_f, kv_b, tot_f, tot_b, s0f, s0b):
    b, nb = kv_f.shape[0], kv_f.shape[1]
    g = SCAN_GROUP
    nj = nb // g
    fwd = lambda i, j: (i, j, 0, 0)
    bwd = lambda i, j: (i, nj - 1 - j, 0, 0)
    kv_blk = (1, g, B_DV, B_KEY_WIDTH)
    tot_blk = (1, g, GLA_NC, B_KEY_WIDTH)
    st_spec = pl.BlockSpec((1, B_DV, B_KEY_WIDTH), lambda i, j: (i, 0, 0))
    s_blk = (1, g, B_KEY_WIDTH, B_DV)
    s_shape = jax.ShapeDtypeStruct((b, nb, B_KEY_WIDTH, B_DV), BF16)
    return pl.pallas_call(
        _gscan_kernel,
        grid=(b, nj),
        in_specs=[pl.BlockSpec(kv_blk, fwd), pl.BlockSpec(kv_blk, bwd),
                  pl.BlockSpec(tot_blk, fwd), pl.BlockSpec(tot_blk, bwd), st_spec, st_spec],
        out_specs=[pl.BlockSpec(s_blk, fwd), pl.BlockSpec(s_blk, bwd)],
        out_shape=[s_shape, s_shape],
        scratch_shapes=[pltpu.VMEM((B_DV, B_KEY_WIDTH), F32), pltpu.VMEM((B_DV, B_KEY_WIDTH), F32)],
        compiler_params=pltpu.CompilerParams(
            dimension_semantics=("parallel", "arbitrary"), vmem_limit_bytes=VMEM_LIMIT_BYTES),
        name="gscan",
    )(kv_f, kv_b, tot_f, tot_b, s0f, s0b)


def _scale_rows(x16, scales):
    parts = []
    for c, s in enumerate(scales):
        xc = x16[c * B_CHUNK:(c + 1) * B_CHUNK, :]
        parts.append(xc if s is None else (xc.astype(F32) * s).astype(BF16))
    return parts[0] if len(parts) == 1 else jnp.concatenate(parts, axis=0)


def _exp_sum(tots, idx):
    return jnp.exp2(_sum_rows([tots[m] for m in idx])) if idx else None


def _gla_block(qdf, kdf, kcf, qdb, kdb, kcb, v16, tf, tb, sf, sb, p_scr):
    nc = GLA_NC
    ch = B_CHUNK
    lane = lax.broadcasted_iota(jnp.int32, (1, LANES), 1)
    hm = [(lane < B_DK).astype(BF16), (lane >= B_DK).astype(BF16)]

    def pair_heads(x):
        return jnp.concatenate([x * hm[0], x * hm[1]], axis=0)

    row = lax.broadcasted_iota(jnp.int32, (ch, 2 * ch), 0)
    col = lax.broadcasted_iota(jnp.int32, (ch, 2 * ch), 1)
    first = col < ch
    zeros = jnp.zeros((ch, LANES), BF16)
    for g in range(B_HEADS // 2):
        ls = slice(g * LANES, (g + 1) * LANES)
        for i in range(nc // 2):
            pr = slice(2 * i * ch, (2 * i + 2) * ch)
            re = slice(2 * i * ch, (2 * i + 1) * ch)
            ro = slice((2 * i + 1) * ch, (2 * i + 2) * ch)
            sc_f = _dot_nt(pair_heads(qdf[pr, ls]), jnp.concatenate([kdf[pr, ls], kcf[re, ls], zeros], axis=0))
            sc_b = _dot_nt(pair_heads(qdb[pr, ls]), jnp.concatenate([kdb[pr, ls], zeros, kcb[ro, ls]], axis=0))
            for hh in range(2):
                he = slice(hh * 2 * ch, hh * 2 * ch + ch)
                ho = slice(hh * 2 * ch + ch, (hh + 1) * 2 * ch)
                even = jnp.where(first,
                                 jnp.where(col <= row, sc_f[he, 0:2 * ch], 0.0)
                                 + jnp.where(col >= row, sc_b[he, 0:2 * ch], 0.0),
                                 sc_b[he, 2 * ch:])
                odd = jnp.where(first, sc_f[ho, 2 * ch:],
                                jnp.where(col - ch <= row, sc_f[ho, 0:2 * ch], 0.0)
                                + jnp.where(col - ch >= row, sc_b[ho, 0:2 * ch], 0.0))
                p_scr[2 * g + hh, re, pr] = even.astype(BF16)
                p_scr[2 * g + hh, ro, pr] = odd.astype(BF16)

    def cross(lo, hi):
        if hi - lo <= 2:
            return
        mid = (lo + hi) // 2
        cross(lo, mid)
        cross(mid, hi)
        rl = slice(lo * ch, mid * ch)
        rh = slice(mid * ch, hi * ch)
        n = (hi - mid) * ch
        qf = _scale_rows(qdf[rh], [_exp_sum(tf, range(mid, c)) for c in range(mid, hi)])
        kf = _scale_rows(kcf[rl], [_exp_sum(tf, range(c + 1, mid)) for c in range(lo, mid)])
        qb = _scale_rows(qdb[rl], [_exp_sum(tb, range(c + 1, mid)) for c in range(lo, mid)])
        kb = _scale_rows(kcb[rh], [_exp_sum(tb, range(mid, c)) for c in range(mid, hi)])
        for g in range(B_HEADS // 2):
            ls = slice(g * LANES, (g + 1) * LANES)
            sc_f = _dot_nt(pair_heads(qf[:, ls]), kf[:, ls]).astype(BF16)
            sc_b = _dot_nt(pair_heads(qb[:, ls]), kb[:, ls]).astype(BF16)
            for hh in range(2):
                p_scr[2 * g + hh, rh, rl] = sc_f[hh * n:(hh + 1) * n]
                p_scr[2 * g + hh, rl, rh] = sc_b[hh * n:(hh + 1) * n]

    cross(0, nc)

    qsf = _scale_rows(qdf, [_exp_sum(tf, range(0, c)) for c in range(nc)])
    qsb = _scale_rows(qdb, [_exp_sum(tb, range(c + 1, nc)) for c in range(nc)])
    head_row = lax.broadcasted_iota(jnp.int32, (LANES, 1), 0) // B_DK
    outs = []
    for h in range(B_HEADS):
        ls = slice((h // 2) * LANES, (h // 2 + 1) * LANES)
        own = (head_row == (h % 2)).astype(BF16)
        lhs = jnp.concatenate([p_scr[h], qsf[:, ls], qsb[:, ls]], axis=1)
        rhs = jnp.concatenate([v16[:, h * B_DV:(h + 1) * B_DV], sf[ls, :] * own, sb[ls, :] * own], axis=0)
        outs.append(jnp.dot(lhs, rhs, preferred_element_type=F32))
    return outs


def _colmix_kernel(ws_ref, bs_ref, vn_ref, o_ref):
    nw = vn_ref.shape[2]
    xs = jnp.concatenate([vn_ref[0, 0, w] for w in range(nw)], axis=1)
    y = _dot(ws_ref[0], xs) + bs_ref[0][:, 0:1]
    for w in range(nw):
        o_ref[0, 0, w] = y[:, w * A_GROUP_DIM:(w + 1) * A_GROUP_DIM]


def _colmix_call(ws_col, bs_col, vn_col, nw):
    b, g, width, rows, ch = vn_col.shape
    blk = pl.BlockSpec((1, 1, nw, rows, ch), lambda i, j, m: (i, j, m, 0, 0))
    return pl.pallas_call(
        _colmix_kernel,
        grid=(b, g, width // nw),
        in_specs=[pl.BlockSpec((1, rows, rows), lambda i, j, m: (j, 0, 0)),
                  pl.BlockSpec((1, rows, A_GROUP_DIM), lambda i, j, m: (j, 0, 0)),
                  blk],
        out_specs=blk,
        out_shape=jax.ShapeDtypeStruct(vn_col.shape, F32),
        compiler_params=pltpu.CompilerParams(
            dimension_semantics=("parallel", "parallel", "parallel"), vmem_limit_bytes=VMEM_LIMIT_BYTES),
        name="colmix",
    )(ws_col, bs_col, vn_col)


def _main_kernel(x_ref, mod_ref, ng_ref, w_ref, vr_ref, sc_ref,
                 qdf_ref, kdf_ref, kcf_ref, qdb_ref, kdb_ref, kcb_ref, v_ref, totf_ref, totb_ref, sf_ref, sb_ref,
                 ws_ref, bs_ref, bg_ref, wpa_ref, wpb_ref, wo_ref, fg_ref, o_ref, acta_scr, on_scr, p_scr):
    t = x_ref.shape[1]
    blocks = [slice(blk * GLA_BLOCK, (blk + 1) * GLA_BLOCK) for blk in range(t // GLA_BLOCK)]

    def gla(blk):
        rs = blocks[blk]
        tf = [totf_ref[0, blk, c:c + 1, :] for c in range(GLA_NC)]
        tb = [totb_ref[0, blk, c:c + 1, :] for c in range(GLA_NC)]
        o_heads = _gla_block(qdf_ref[0, rs, :], kdf_ref[0, rs, :], kcf_ref[0, rs, :],
                             qdb_ref[0, rs, :], kdb_ref[0, rs, :], kcb_ref[0, rs, :], v_ref[0, rs, :],
                             tf, tb, sf_ref[0, blk], sb_ref[0, blk], p_scr.at[blk])
        for hd in range(B_HEADS):
            on_scr[rs, hd * B_DV:(hd + 1) * B_DV] = _rms_rows(o_heads[hd])

    def proj(c0, n):
        return jnp.dot(h16, w_ref[:, c0:c0 + n], preferred_element_type=F32)

    gla(0)
    h16 = _latent_h(x_ref[0], mod_ref, ng_ref).astype(BF16)
    zb = proj(0, B_VAL_WIDTH)
    u = proj(B_VAL_WIDTH, A_WIDTH)
    za = proj(B_VAL_WIDTH + A_WIDTH, A_WIDTH)
    for blk in range(1, len(blocks)):
        gla(blk)
    gate_a = jax.nn.sigmoid(proj(B_VAL_WIDTH + 2 * A_WIDTH, D_MODEL))

    uz = u * _silu(za)
    for g in range(A_GROUPS):
        cs = slice(g * A_GROUP_DIM, (g + 1) * A_GROUP_DIM)
        if g < A_ROW_GROUPS:
            chunks = [slice(c * A_CHUNK, (c + 1) * A_CHUNK) for c in range(t // A_CHUNK)]
            sv_all = jnp.dot(ws_ref[g].astype(BF16), jnp.concatenate([vr_ref[0, rs, cs] for rs in chunks], axis=1),
                             preferred_element_type=F32)
            for c, rs in enumerate(chunks):
                sv = sv_all[:, c * A_GROUP_DIM:(c + 1) * A_GROUP_DIM] + bs_ref[g][:, 0:1]
                acta_scr[rs, cs] = (uz[rs, cs] * sv).astype(BF16)
        else:
            sv = jnp.swapaxes(sc_ref[0, g - A_ROW_GROUPS], 0, 1).reshape(t, A_GROUP_DIM)
            acta_scr[:, cs] = (uz[:, cs] * sv).astype(BF16)
    ya = jnp.dot(acta_scr[...], wpa_ref[...], preferred_element_type=F32)

    gate_b = jax.nn.sigmoid(proj(B_VAL_WIDTH + 2 * A_WIDTH + D_MODEL, D_MODEL))
    yb = _dot(on_scr[...] * (_silu(zb) * bg_ref[...]), wpb_ref[...])

    m = (gate_a * ya + gate_b * yb).astype(BF16)
    for rs in blocks:
        y = jnp.dot(m[rs], wo_ref[...], preferred_element_type=F32)
        xo = x_ref[0, rs, :] + mod_ref[0, 2:3, :] * y
        o_ref[0, rs, :] = _rms_rows(xo) * fg_ref[...]


def _main_call(x, mod3, ng, w_main, vn_row, sv_col, gla_ops, ws, bs, bg, wpa, wpb, wo, fg, t):
    b, l, _ = x.shape
    nw = w_main.shape[1]
    ncg = A_GROUPS - A_ROW_GROUPS
    const2 = lambda i, j: (0, 0)
    const3 = lambda i, j: (0, 0, 0)
    tok = lambda width: pl.BlockSpec((1, t, width), lambda i, j: (i, j, 0))
    nblk = t // GLA_BLOCK
    tot_spec = pl.BlockSpec((1, nblk, GLA_NC, B_KEY_WIDTH), lambda i, j: (i, j, 0, 0))
    st_spec = pl.BlockSpec((1, nblk, B_KEY_WIDTH, B_DV), lambda i, j: (i, j, 0, 0))
    return pl.pallas_call(
        _main_kernel,
        grid=(b, l // t),
        in_specs=[tok(D_MODEL),
                  pl.BlockSpec((1, 3, D_MODEL), lambda i, j: (i, 0, 0)),
                  pl.BlockSpec((1, D_MODEL), const2),
                  pl.BlockSpec((D_MODEL, nw), const2),
                  tok(A_ROW_GROUPS * A_GROUP_DIM),
                  pl.BlockSpec((1, ncg, GRID_W, t // GRID_W, A_GROUP_DIM), lambda i, j: (i, 0, 0, j, 0)),
                  tok(B_KEY_WIDTH), tok(B_KEY_WIDTH), tok(B_KEY_WIDTH),
                  tok(B_KEY_WIDTH), tok(B_KEY_WIDTH), tok(B_KEY_WIDTH), tok(B_VAL_WIDTH),
                  tot_spec, tot_spec, st_spec, st_spec,
                  pl.BlockSpec((A_GROUPS, A_CHUNK, A_CHUNK), const3),
                  pl.BlockSpec((A_GROUPS, A_CHUNK, A_GROUP_DIM), const3),
                  pl.BlockSpec((1, B_VAL_WIDTH), const2),
                  pl.BlockSpec((A_WIDTH, D_MODEL), const2),
                  pl.BlockSpec((B_VAL_WIDTH, D_MODEL), const2),
                  pl.BlockSpec((D_MODEL, D_MODEL), const2),
                  pl.BlockSpec((1, D_MODEL), const2)],
        out_specs=tok(D_MODEL),
        out_shape=jax.ShapeDtypeStruct((b, l, D_MODEL), F32),
        scratch_shapes=[pltpu.VMEM((t, A_WIDTH), BF16), pltpu.VMEM((t, B_VAL_WIDTH), F32),
                        pltpu.VMEM((nblk, B_HEADS, GLA_BLOCK, GLA_BLOCK), BF16)],
        compiler_params=pltpu.CompilerParams(
            dimension_semantics=("parallel", "parallel"), vmem_limit_bytes=VMEM_LIMIT_BYTES),
        name="main",
    )(x, mod3, ng, w_main, vn_row, sv_col, *gla_ops, ws, bs, bg, wpa, wpb, wo, fg)


def kernel(x, c, ctx, c_ctx, w_mod, b_mod, norm_g, w_in, a_ln_g, a_ln_b, a_ws, a_bs, b_gate_w2, b_gate_b,
           b_norm_g, w_proj_a, w_proj_b, w_out, final_norm_g):
    assert w_mod.shape[0] == 1, "single-layer block"
    b, l, _ = x.shape
    rows = l // GRID_W
    assert rows == A_CHUNK
    ng = norm_g[0][None, :]

    cc = jnp.zeros((8, D_MODEL), F32).at[0:b].set(c).at[b].set(c_ctx)
    w_pre, w_main, wpa16, wpb16, wo16, mod = _wprep_call(w_in[0].T, w_proj_a, w_proj_b, w_out, cc, w_mod, b_mod)
    mod3 = mod[0:b].reshape(b, 3, D_MODEL)
    w2, gb = b_gate_w2[0], b_gate_b[0]
    s0f, s0b = _ctx_call(ctx, mod, ng, w_pre, w2, gb, b)

    (qd_f, kd_f, kc_f, qd_b, kd_b, kc_b, v16, tot_f, tot_b, kv_f, kv_b, vn_row, vn_col) = _pre_call(
        x, mod3, ng, w_pre, w2, gb, a_ln_g[0][None, :], a_ln_b[0][None, :], 1024)

    s_f, s_b = _gscan_call(kv_f, kv_b, tot_f, tot_b, s0f, s0b)
    gla_ops = (qd_f, kd_f, kc_f, qd_b, kd_b, kc_b, v16, tot_f, tot_b, s_f, s_b)

    bs_b = jnp.broadcast_to(a_bs[0][:, :, None], (A_GROUPS, A_CHUNK, A_GROUP_DIM))
    sv_col = _colmix_call(a_ws[0][A_ROW_GROUPS:], bs_b[A_ROW_GROUPS:], vn_col, 32)

    return _main_call(x, mod3, ng, w_main, vn_row, sv_col, gla_ops, a_ws[0], bs_b, b_norm_g[0][None, :],
                      wpa16, wpb16, wo16, final_norm_g[None, :], 1024)
```

```python
import functools

import jax
import jax.numpy as jnp
from jax import lax
from jax.experimental import pallas as pl
from jax.experimental.pallas import tpu as pltpu

D_MODEL = 1024
GRID_W = 64
EPS = 1e-6

A_WIDTH = 512
A_GROUPS = 4
A_GROUP_DIM = 128
A_CHUNK = 128
A_ROW_GROUPS = 2

B_HEADS = 4
B_DK = 64
B_DV = 128
B_KEY_WIDTH = 256
B_VAL_WIDTH = 512
B_GATE_RANK = 16
B_GATE_TAU = 16.0
LOG2E = 1.4426950408889634
B_CHUNK = 64

Q0 = 0
K0 = Q0 + B_KEY_WIDTH
V0 = K0 + B_KEY_WIDTH
LR0 = V0 + B_VAL_WIDTH
ZB0 = LR0 + 2 * B_GATE_RANK
UA0 = ZB0 + B_VAL_WIDTH
VA0 = UA0 + A_WIDTH
ZA0 = VA0 + A_WIDTH
G0 = ZA0 + A_WIDTH
IN_WIDTH = G0 + 2 * D_MODEL

LANES = 128

P_LR = 0
P_K = P_LR + LANES
P_Q = P_K + B_KEY_WIDTH
P_V = P_Q + B_KEY_WIDTH
P_VA = P_V + B_VAL_WIDTH
PRE_WIDTH = P_VA + A_WIDTH
GLA_BLOCK = 256
GLA_NC = GLA_BLOCK // B_CHUNK

VMEM_LIMIT_BYTES = 56 * 1024 * 1024

BF16 = jnp.bfloat16
F32 = jnp.float32


def _dot(a, b):
    return jnp.dot(a.astype(BF16), b.astype(BF16), preferred_element_type=F32)


def _dot_nt(a, b):
    return lax.dot_general(a.astype(BF16), b.astype(BF16), (((1,), (1,)), ((), ())),
                           preferred_element_type=F32)


def _dot_tn(a, b):
    return lax.dot_general(a.astype(BF16), b.astype(BF16), (((0,), (0,)), ((), ())),
                           preferred_element_type=F32)


def _silu(x):
    return x * jax.nn.sigmoid(x)


def _rms_rows(x):
    return x * lax.rsqrt(jnp.mean(x * x, axis=-1, keepdims=True) + EPS)


def _wprep_kernel(wt_ref, wpa_ref, wpb_ref, wo_ref, c_ref, wm_ref, bm_ref,
                  wpre_ref, wmain_ref, wpa16_ref, wpb16_ref, wo16_ref, mod_ref):
    def put(dst_ref, c0, r0, n, scale=None):
        for s in range(0, n, 2 * LANES):
            m = min(2 * LANES, n - s)
            blk = wt_ref[r0 + s:r0 + s + m, :].T
            dst_ref[:, c0 + s:c0 + s + m] = (blk if scale is None else blk * scale).astype(BF16)

    put(wpre_ref, P_LR, LR0, LANES)
    put(wpre_ref, P_K, K0, B_KEY_WIDTH)
    put(wpre_ref, P_Q, Q0, B_KEY_WIDTH, B_DK ** -0.5)
    put(wpre_ref, P_V, V0, B_VAL_WIDTH)
    put(wpre_ref, P_VA, VA0, A_WIDTH)
    put(wmain_ref, 0, ZB0, VA0 - ZB0)
    put(wmain_ref, VA0 - ZB0, ZA0, IN_WIDTH - ZA0)
    wpa16_ref[...] = wpa_ref[0].astype(BF16)
    wpb16_ref[...] = wpb_ref[0].astype(BF16)
    wo16_ref[...] = wo_ref[0].astype(BF16)
    mod_ref[...] = _dot(_silu(c_ref[...]), wm_ref[0]) + bm_ref[...]


def _wprep_call(w_in_t, w_proj_a, w_proj_b, w_out, cc, w_mod, b_mod):
    steps = 4
    n_main = (VA0 - ZB0) + (IN_WIDTH - ZA0)

    def rows3(a):
        return pl.BlockSpec((1, a.shape[1] // steps, a.shape[2]), lambda i: (0, i, 0))

    def rows2(nrows, ncols):
        return (pl.BlockSpec((nrows // steps, ncols), lambda i: (i, 0)),
                jax.ShapeDtypeStruct((nrows, ncols), BF16))

    n_mod = w_mod.shape[2]
    outs = [rows2(D_MODEL, PRE_WIDTH), rows2(D_MODEL, n_main), rows2(A_WIDTH, D_MODEL),
            rows2(B_VAL_WIDTH, D_MODEL), rows2(D_MODEL, D_MODEL),
            (pl.BlockSpec((8, n_mod // steps), lambda i: (0, i)), jax.ShapeDtypeStruct((8, n_mod), F32))]
    return pl.pallas_call(
        _wprep_kernel,
        grid=(steps,),
        in_specs=[pl.BlockSpec((IN_WIDTH, D_MODEL // steps), lambda i: (0, i)),
                  rows3(w_proj_a), rows3(w_proj_b), rows3(w_out),
                  pl.BlockSpec((8, D_MODEL), lambda i: (0, 0)),
                  pl.BlockSpec((1, D_MODEL, n_mod // steps), lambda i: (0, 0, i)),
                  pl.BlockSpec((1, n_mod // steps), lambda i: (0, i))],
        out_specs=[o[0] for o in outs],
        out_shape=[o[1] for o in outs],
        compiler_params=pltpu.CompilerParams(
            dimension_semantics=("parallel",), vmem_limit_bytes=VMEM_LIMIT_BYTES),
        name="wprep",
    )(w_in_t, w_proj_a, w_proj_b, w_out, cc, w_mod, b_mod)


def _gate_logs(lr, w2_ref, gb_ref):
    out = []
    for r in range(2):
        logits = _dot(lr[:, r * B_GATE_RANK:(r + 1) * B_GATE_RANK], w2_ref[r]) + gb_ref[r:r + 1, :]
        log_sig = jnp.minimum(logits, 0.0) - jnp.log(1.0 + jnp.exp(-jnp.abs(logits)))
        out.append(log_sig * (LOG2E / B_GATE_TAU))
    return out


def _chunk_tri(reverse):
    i = lax.broadcasted_iota(jnp.int32, (GLA_BLOCK, GLA_BLOCK), 0)
    j = lax.broadcasted_iota(jnp.int32, (GLA_BLOCK, GLA_BLOCK), 1)
    same = (i // B_CHUNK) == (j // B_CHUNK)
    tri = (j >= i) if reverse else (j <= i)
    return (same & tri).astype(BF16)


def _sum_rows(rows):
    acc = rows[0]
    for r in rows[1:]:
        acc = acc + r
    return acc


def _block_cum(a, tri, reverse):
    hi = a.astype(BF16)
    lo = (a - hi.astype(F32)).astype(BF16)
    cum = (jnp.dot(tri, hi, preferred_element_type=F32) + jnp.dot(tri, lo, preferred_element_type=F32))
    last = 0 if reverse else B_CHUNK - 1
    tots = [cum[c * B_CHUNK + last:c * B_CHUNK + last + 1, :] for c in range(GLA_NC)]
    return cum, tots


def _block_keys(k, cum, tots, reverse):
    totb = jnp.concatenate([jnp.broadcast_to(t, (B_CHUNK, B_KEY_WIDTH)) for t in tots], axis=0)
    kdec = k * jnp.exp2(totb - cum)
    later = []
    for c in range(GLA_NC):
        idx = list(range(0, c)) if reverse else list(range(c + 1, GLA_NC))
        if idx:
            later.append(jnp.broadcast_to(jnp.exp2(_sum_rows([tots[m] for m in idx])), (B_CHUNK, B_KEY_WIDTH)))
        else:
            later.append(jnp.ones((B_CHUNK, B_KEY_WIDTH), F32))
    return kdec, kdec * jnp.concatenate(later, axis=0)


def _pre_project(h16, w_ref, want_q):
    lr = jnp.dot(h16, w_ref[:, P_LR:P_LR + LANES], preferred_element_type=F32)[:, 0:2 * B_GATE_RANK]
    k = jnp.dot(h16, w_ref[:, P_K:P_K + B_KEY_WIDTH], preferred_element_type=F32)
    q = jnp.dot(h16, w_ref[:, P_Q:P_Q + B_KEY_WIDTH], preferred_element_type=F32) if want_q else None
    v = jnp.dot(h16, w_ref[:, P_V:P_V + B_VAL_WIDTH], preferred_element_type=F32)
    return lr, k, q, v


def _block_kv_t(v16, kblk):
    k16 = kblk.astype(BF16)
    lane = lax.broadcasted_iota(jnp.int32, (1, B_KEY_WIDTH), 1)
    acc = jnp.zeros((B_DV, B_KEY_WIDTH), F32)
    for h in range(B_HEADS):
        full = _dot_tn(v16[:, h * B_DV:(h + 1) * B_DV], k16)
        acc = acc + jnp.where((lane // B_DK) == h, full, 0.0)
    return acc


def _ctx_kernel(ctx_ref, mod_ref, ng_ref, w_ref, w2_ref, gb_ref, sf_ref, sb_ref, *, ctx_row):
    nb, lc, _ = ctx_ref.shape
    xc = ctx_ref[...].reshape(nb * lc, D_MODEL)
    shift = mod_ref[ctx_row:ctx_row + 1, 0:D_MODEL]
    scale = mod_ref[ctx_row:ctx_row + 1, D_MODEL:2 * D_MODEL]
    hc = _rms_rows(xc) * ng_ref[...] * (1.0 + scale) + shift
    lr, k, _, v = _pre_project(hc.astype(BF16), w_ref, False)
    v16 = v.astype(BF16)
    a_f, a_b = _gate_logs(lr, w2_ref, gb_ref)
    for bi in range(nb):
        rs = slice(bi * lc, (bi + 1) * lc)
        for a, reverse, out_ref in ((a_f, False, sf_ref), (a_b, True, sb_ref)):
            cum, tots = _block_cum(a[rs], _chunk_tri(reverse), reverse)
            _, kblk = _block_keys(k[rs], cum, tots, reverse)
            out_ref[bi] = _block_kv_t(v16[rs], kblk)


def _ctx_call(ctx, mod, ng, w_kvl, w2, gb, ctx_row):
    b, lc, _ = ctx.shape
    assert lc == GLA_BLOCK, "context length must be one GLA block"
    nw = w_kvl.shape[1]
    st = jax.ShapeDtypeStruct((b, B_DV, B_KEY_WIDTH), F32)
    st_spec = pl.BlockSpec((b, B_DV, B_KEY_WIDTH), lambda i: (0, 0, 0))
    return pl.pallas_call(
        functools.partial(_ctx_kernel, ctx_row=ctx_row),
        grid=(1,),
        in_specs=[pl.BlockSpec((b, lc, D_MODEL), lambda i: (0, 0, 0)),
                  pl.BlockSpec((8, 3 * D_MODEL), lambda i: (0, 0)),
                  pl.BlockSpec((1, D_MODEL), lambda i: (0, 0)),
                  pl.BlockSpec((D_MODEL, nw), lambda i: (0, 0)),
                  pl.BlockSpec((2, B_GATE_RANK, B_KEY_WIDTH), lambda i: (0, 0, 0)),
                  pl.BlockSpec((2, B_KEY_WIDTH), lambda i: (0, 0))],
        out_specs=[st_spec, st_spec],
        out_shape=[st, st],
        name="ctx",
    )(ctx, mod, ng, w_kvl, w2, gb)


def _latent_h(x, mod_ref, ng_ref):
    shift = mod_ref[0, 0:1, :]
    gain = ng_ref[...] * (1.0 + mod_ref[0, 1:2, :])
    return _rms_rows(x) * gain + shift


def _pre_kernel(x_ref, mod_ref, ng_ref, w_ref, w2_ref, gb_ref, lg_ref, lb_ref, s0b_ref,
                qdf_ref, kdf_ref, kcf_ref, qdb_ref, kdb_ref, kcb_ref, v_ref,
                totf_ref, totb_ref, kvf_ref, sb_ref, vr_ref, vc_ref, sb_scr):
    @pl.when(pl.program_id(1) == 0)
    def _():
        sb_scr[...] = s0b_ref[0]

    h16 = _latent_h(x_ref[0], mod_ref, ng_ref).astype(BF16)
    nblk = x_ref.shape[1] // GLA_BLOCK
    blocks = [slice(blk * GLA_BLOCK, (blk + 1) * GLA_BLOCK) for blk in range(nblk)]
    dirs = ((False, qdf_ref, kdf_ref, kcf_ref, totf_ref), (True, qdb_ref, kdb_ref, kcb_ref, totb_ref))

    lrk = jnp.dot(h16, w_ref[:, P_LR:P_K + B_KEY_WIDTH], preferred_element_type=F32)
    lr = lrk[:, 0:2 * B_GATE_RANK]
    k = lrk[:, P_K:P_K + B_KEY_WIDTH]
    logs = _gate_logs(lr, w2_ref, gb_ref)
    q = jnp.dot(h16, w_ref[:, P_Q:P_Q + B_KEY_WIDTH], preferred_element_type=F32)
    v16 = jnp.dot(h16, w_ref[:, P_V:P_V + B_VAL_WIDTH], preferred_element_type=F32).astype(BF16)
    v_ref[0] = v16

    cums = {}
    for d, (reverse, _, _, _, tot_ref) in enumerate(dirs):
        tri = _chunk_tri(reverse)
        for blk, rs in enumerate(blocks):
            cum, tots = _block_cum(logs[d][rs], tri, reverse)
            cums[d, blk] = (cum, tots)
            for c in range(GLA_NC):
                tot_ref[0, blk, c:c + 1, :] = tots[c]

    va = jnp.dot(h16, w_ref[:, P_VA:P_VA + A_WIDTH], preferred_element_type=F32)

    kv_b = {}
    for d, (reverse, qd_ref, kd_ref, kc_ref, _) in enumerate(dirs):
        for blk, rs in enumerate(blocks):
            cum, tots = cums[d, blk]
            kdec, kblk = _block_keys(k[rs], cum, tots, reverse)
            kc_ref[0, rs, :] = kdec.astype(BF16)
            kd_ref[0, rs, :] = (k[rs] * jnp.exp2(-cum)).astype(BF16)
            qd_ref[0, rs, :] = (q[rs] * jnp.exp2(cum)).astype(BF16)
            if reverse:
                kv_b[blk] = _block_kv_t(v16[rs], kblk)
            else:
                kvf_ref[0, blk] = _block_kv_t(v16[rs], kblk)

    s = sb_scr[...]
    for blk in reversed(range(nblk)):
        sb_ref[0, blk] = s.T.astype(BF16)
        s = jnp.exp2(_sum_rows(cums[1, blk][1])) * s + kv_b[blk]
    sb_scr[...] = s

    vc = va - jnp.mean(va, axis=-1, keepdims=True)
    vn = vc * lax.rsqrt(jnp.mean(vc * vc, axis=-1, keepdims=True) + EPS) * lg_ref[...] + lb_ref[...]
    nr = A_ROW_GROUPS * A_GROUP_DIM
    vr_ref[0] = vn[:, 0:nr].astype(BF16)
    for g in range(A_GROUPS - A_ROW_GROUPS):
        vg = vn[:, nr + g * A_GROUP_DIM:nr + (g + 1) * A_GROUP_DIM]
        vc_ref[0, g] = jnp.swapaxes(vg.reshape(vg.shape[0] // GRID_W, GRID_W, A_GROUP_DIM), 0, 1)


def _pre_call(x, mod3, ng, w_pre, w2, gb, lg, lb, s0b, t):
    b, l, _ = x.shape
    nw = w_pre.shape[1]
    ncg = A_GROUPS - A_ROW_GROUPS
    nblk = t // GLA_BLOCK
    nj = l // t
    rev3 = lambda i, j: (i, nj - 1 - j, 0)
    rev4 = lambda i, j: (i, nj - 1 - j, 0, 0)

    def tok(width, dtype):
        return (pl.BlockSpec((1, t, width), rev3), jax.ShapeDtypeStruct((b, l, width), dtype))

    tot = (pl.BlockSpec((1, nblk, GLA_NC, B_KEY_WIDTH), rev4),
           jax.ShapeDtypeStruct((b, l // GLA_BLOCK, GLA_NC, B_KEY_WIDTH), F32))
    kv = (pl.BlockSpec((1, nblk, B_DV, B_KEY_WIDTH), rev4),
          jax.ShapeDtypeStruct((b, l // GLA_BLOCK, B_DV, B_KEY_WIDTH), F32))
    st = (pl.BlockSpec((1, nblk, B_KEY_WIDTH, B_DV), rev4),
          jax.ShapeDtypeStruct((b, l // GLA_BLOCK, B_KEY_WIDTH, B_DV), BF16))
    outs = [tok(B_KEY_WIDTH, BF16)] * 6 + [tok(B_VAL_WIDTH, BF16), tot, tot, kv, st,
                                           tok(A_ROW_GROUPS * A_GROUP_DIM, BF16),
                                           (pl.BlockSpec((1, ncg, GRID_W, t // GRID_W, A_GROUP_DIM),
                                                         lambda i, j: (i, 0, 0, nj - 1 - j, 0)),
                                            jax.ShapeDtypeStruct((b, ncg, GRID_W, l // GRID_W, A_GROUP_DIM), F32))]
    const2 = lambda i, j: (0, 0)
    return pl.pallas_call(
        _pre_kernel,
        grid=(b, nj),
        in_specs=[pl.BlockSpec((1, t, D_MODEL), rev3),
                  pl.BlockSpec((1, 3, D_MODEL), lambda i, j: (i, 0, 0)),
                  pl.BlockSpec((1, D_MODEL), const2),
                  pl.BlockSpec((D_MODEL, nw), const2),
                  pl.BlockSpec((2, B_GATE_RANK, B_KEY_WIDTH), lambda i, j: (0, 0, 0)),
                  pl.BlockSpec((2, B_KEY_WIDTH), const2),
                  pl.BlockSpec((1, A_WIDTH), const2),
                  pl.BlockSpec((1, A_WIDTH), const2),
                  pl.BlockSpec((1, B_DV, B_KEY_WIDTH), lambda i, j: (i, 0, 0))],
        out_specs=[o[0] for o in outs],
        out_shape=[o[1] for o in outs],
        scratch_shapes=[pltpu.VMEM((B_DV, B_KEY_WIDTH), F32)],
        compiler_params=pltpu.CompilerParams(
            dimension_semantics=("parallel", "arbitrary"), vmem_limit_bytes=VMEM_LIMIT_BYTES),
        name="pre",
    )(x, mod3, ng, w_pre, w2, gb, lg, lb, s0b)


def _scale_rows(x16, scales):
    parts = []
    for c, s in enumerate(scales):
        xc = x16[c * B_CHUNK:(c + 1) * B_CHUNK, :]
        parts.append(xc if s is None else (xc.astype(F32) * s).astype(BF16))
    return parts[0] if len(parts) == 1 else jnp.concatenate(parts, axis=0)


def _exp_sum(tots, idx):
    return jnp.exp2(_sum_rows([tots[m] for m in idx])) if idx else None


def _gla_block(qdf, kdf, kcf, qdb, kdb, kcb, v16, tf, tb, sf, sb, p_scr):
    nc = GLA_NC
    ch = B_CHUNK
    lane = lax.broadcasted_iota(jnp.int32, (1, LANES), 1)
    hm = [(lane < B_DK).astype(BF16), (lane >= B_DK).astype(BF16)]

    def pair_heads(x):
        return jnp.concatenate([x * hm[0], x * hm[1]], axis=0)

    row = lax.broadcasted_iota(jnp.int32, (ch, 2 * ch), 0)
    col = lax.broadcasted_iota(jnp.int32, (ch, 2 * ch), 1)
    first = col < ch
    zeros = jnp.zeros((ch, LANES), BF16)
    for g in range(B_HEADS // 2):
        ls = slice(g * LANES, (g + 1) * LANES)
        for i in range(nc // 2):
            pr = slice(2 * i * ch, (2 * i + 2) * ch)
            re = slice(2 * i * ch, (2 * i + 1) * ch)
            ro = slice((2 * i + 1) * ch, (2 * i + 2) * ch)
            sc_f = _dot_nt(pair_heads(qdf[pr, ls]), jnp.concatenate([kdf[pr, ls], kcf[re, ls], zeros], axis=0))
            sc_b = _dot_nt(pair_heads(qdb[pr, ls]), jnp.concatenate([kdb[pr, ls], zeros, kcb[ro, ls]], axis=0))
            for hh in range(2):
                he = slice(hh * 2 * ch, hh * 2 * ch + ch)
                ho = slice(hh * 2 * ch + ch, (hh + 1) * 2 * ch)
                even = jnp.where(first,
                                 jnp.where(col <= row, sc_f[he, 0:2 * ch], 0.0)
                                 + jnp.where(col >= row, sc_b[he, 0:2 * ch], 0.0),
                                 sc_b[he, 2 * ch:])
                odd = jnp.where(first, sc_f[ho, 2 * ch:],
                                jnp.where(col - ch <= row, sc_f[ho, 0:2 * ch], 0.0)
                                + jnp.where(col - ch >= row, sc_b[ho, 0:2 * ch], 0.0))
                p_scr[2 * g + hh, re, pr] = even.astype(BF16)
                p_scr[2 * g + hh, ro, pr] = odd.astype(BF16)

    def cross(lo, hi):
        if hi - lo <= 2:
            return
        mid = (lo + hi) // 2
        cross(lo, mid)
        cross(mid, hi)
        rl = slice(lo * ch, mid * ch)
        rh = slice(mid * ch, hi * ch)
        n = (hi - mid) * ch
        qf = _scale_rows(qdf[rh], [_exp_sum(tf, range(mid, c)) for c in range(mid, hi)])
        kf = _scale_rows(kcf[rl], [_exp_sum(tf, range(c + 1, mid)) for c in range(lo, mid)])
        qb = _scale_rows(qdb[rl], [_exp_sum(tb, range(c + 1, mid)) for c in range(lo, mid)])
        kb = _scale_rows(kcb[rh], [_exp_sum(tb, range(mid, c)) for c in range(mid, hi)])
        for g in range(B_HEADS // 2):
            ls = slice(g * LANES, (g + 1) * LANES)
            sc_f = _dot_nt(pair_heads(qf[:, ls]), kf[:, ls]).astype(BF16)
            sc_b = _dot_nt(pair_heads(qb[:, ls]), kb[:, ls]).astype(BF16)
            for hh in range(2):
                p_scr[2 * g + hh, rh, rl] = sc_f[hh * n:(hh + 1) * n]
                p_scr[2 * g + hh, rl, rh] = sc_b[hh * n:(hh + 1) * n]

    cross(0, nc)

    qsf = _scale_rows(qdf, [_exp_sum(tf, range(0, c)) for c in range(nc)])
    qsb = _scale_rows(qdb, [_exp_sum(tb, range(c + 1, nc)) for c in range(nc)])
    head_row = lax.broadcasted_iota(jnp.int32, (LANES, 1), 0) // B_DK
    outs = []
    for h in range(B_HEADS):
        ls = slice((h // 2) * LANES, (h // 2 + 1) * LANES)
        own = (head_row == (h % 2)).astype(BF16)
        lhs = jnp.concatenate([p_scr[h], qsf[:, ls], qsb[:, ls]], axis=1)
        rhs = jnp.concatenate([v16[:, h * B_DV:(h + 1) * B_DV], sf[ls, :] * own, sb[ls, :] * own], axis=0)
        outs.append(jnp.dot(lhs, rhs, preferred_element_type=F32))
    return outs


def _colmix_kernel(ws_ref, bs_ref, vn_ref, o_ref):
    nw = vn_ref.shape[2]
    xs = jnp.concatenate([vn_ref[0, 0, w] for w in range(nw)], axis=1)
    y = _dot(ws_ref[0], xs) + bs_ref[0][:, 0:1]
    for w in range(nw):
        o_ref[0, 0, w] = y[:, w * A_GROUP_DIM:(w + 1) * A_GROUP_DIM]


def _colmix_call(ws_col, bs_col, vn_col, nw):
    b, g, width, rows, ch = vn_col.shape
    blk = pl.BlockSpec((1, 1, nw, rows, ch), lambda i, j, m: (i, j, m, 0, 0))
    return pl.pallas_call(
        _colmix_kernel,
        grid=(b, g, width // nw),
        in_specs=[pl.BlockSpec((1, rows, rows), lambda i, j, m: (j, 0, 0)),
                  pl.BlockSpec((1, rows, A_GROUP_DIM), lambda i, j, m: (j, 0, 0)),
                  blk],
        out_specs=blk,
        out_shape=jax.ShapeDtypeStruct(vn_col.shape, F32),
        compiler_params=pltpu.CompilerParams(
            dimension_semantics=("parallel", "parallel", "parallel"), vmem_limit_bytes=VMEM_LIMIT_BYTES),
        name="colmix",
    )(ws_col, bs_col, vn_col)


def _main_kernel(x_ref, mod_ref, ng_ref, w_ref, vr_ref, sc_ref,
                 qdf_ref, kdf_ref, kcf_ref, qdb_ref, kdb_ref, kcb_ref, v_ref, totf_ref, totb_ref,
                 kvf_ref, s0f_ref, sb_ref,
                 ws_ref, bs_ref, bg_ref, wpa_ref, wpb_ref, wo_ref, fg_ref, o_ref, acta_scr, on_scr, p_scr, sf_scr):
    t = x_ref.shape[1]
    blocks = [slice(blk * GLA_BLOCK, (blk + 1) * GLA_BLOCK) for blk in range(t // GLA_BLOCK)]

    @pl.when(pl.program_id(1) == 0)
    def _():
        sf_scr[...] = s0f_ref[0]

    s = sf_scr[...]
    sf_in = []
    for blk in range(len(blocks)):
        sf_in.append(s.T.astype(BF16))
        s = jnp.exp2(_sum_rows([totf_ref[0, blk, c:c + 1, :] for c in range(GLA_NC)])) * s + kvf_ref[0, blk]
    sf_scr[...] = s

    def gla(blk):
        rs = blocks[blk]
        tf = [totf_ref[0, blk, c:c + 1, :] for c in range(GLA_NC)]
        tb = [totb_ref[0, blk, c:c + 1, :] for c in range(GLA_NC)]
        o_heads = _gla_block(qdf_ref[0, rs, :], kdf_ref[0, rs, :], kcf_ref[0, rs, :],
                             qdb_ref[0, rs, :], kdb_ref[0, rs, :], kcb_ref[0, rs, :], v_ref[0, rs, :],
                             tf, tb, sf_in[blk], sb_ref[0, blk], p_scr.at[blk])
        for hd in range(B_HEADS):
            on_scr[rs, hd * B_DV:(hd + 1) * B_DV] = _rms_rows(o_heads[hd])

    def proj(c0, n):
        return jnp.dot(h16, w_ref[:, c0:c0 + n], preferred_element_type=F32)

    gla(0)
    h16 = _latent_h(x_ref[0], mod_ref, ng_ref).astype(BF16)
    zb = proj(0, B_VAL_WIDTH)
    u = proj(B_VAL_WIDTH, A_WIDTH)
    za = proj(B_VAL_WIDTH + A_WIDTH, A_WIDTH)
    for blk in range(1, len(blocks)):
        gla(blk)
    gate_a = jax.nn.sigmoid(proj(B_VAL_WIDTH + 2 * A_WIDTH, D_MODEL))

    uz = u * _silu(za)
    for g in range(A_GROUPS):
        cs = slice(g * A_GROUP_DIM, (g + 1) * A_GROUP_DIM)
        if g < A_ROW_GROUPS:
            chunks = [slice(c * A_CHUNK, (c + 1) * A_CHUNK) for c in range(t // A_CHUNK)]
            sv_all = jnp.dot(ws_ref[g].astype(BF16), jnp.concatenate([vr_ref[0, rs, cs] for rs in chunks], axis=1),
                             preferred_element_type=F32)
            for c, rs in enumerate(chunks):
                sv = sv_all[:, c * A_GROUP_DIM:(c + 1) * A_GROUP_DIM] + bs_ref[g][:, 0:1]
                acta_scr[rs, cs] = (uz[rs, cs] * sv).astype(BF16)
        else:
            sv = jnp.swapaxes(sc_ref[0, g - A_ROW_GROUPS], 0, 1).reshape(t, A_GROUP_DIM)
            acta_scr[:, cs] = (uz[:, cs] * sv).astype(BF16)
    ya = jnp.dot(acta_scr[...], wpa_ref[...], preferred_element_type=F32)

    gate_b = jax.nn.sigmoid(proj(B_VAL_WIDTH + 2 * A_WIDTH + D_MODEL, D_MODEL))
    yb = _dot(on_scr[...] * (_silu(zb) * bg_ref[...]), wpb_ref[...])

    m = (gate_a * ya + gate_b * yb).astype(BF16)
    for rs in blocks:
        y = jnp.dot(m[rs], wo_ref[...], preferred_element_type=F32)
        xo = x_ref[0, rs, :] + mod_ref[0, 2:3, :] * y
        o_ref[0, rs, :] = _rms_rows(xo) * fg_ref[...]


def _main_call(x, mod3, ng, w_main, vn_row, sv_col, gla_ops, ws, bs, bg, wpa, wpb, wo, fg, t):
    b, l, _ = x.shape
    nw = w_main.shape[1]
    ncg = A_GROUPS - A_ROW_GROUPS
    const2 = lambda i, j: (0, 0)
    const3 = lambda i, j: (0, 0, 0)
    tok = lambda width: pl.BlockSpec((1, t, width), lambda i, j: (i, j, 0))
    nblk = t // GLA_BLOCK
    tot_spec = pl.BlockSpec((1, nblk, GLA_NC, B_KEY_WIDTH), lambda i, j: (i, j, 0, 0))
    st_spec = pl.BlockSpec((1, nblk, B_KEY_WIDTH, B_DV), lambda i, j: (i, j, 0, 0))
    kv_spec = pl.BlockSpec((1, nblk, B_DV, B_KEY_WIDTH), lambda i, j: (i, j, 0, 0))
    s0_spec = pl.BlockSpec((1, B_DV, B_KEY_WIDTH), lambda i, j: (i, 0, 0))
    return pl.pallas_call(
        _main_kernel,
        grid=(b, l // t),
        in_specs=[tok(D_MODEL),
                  pl.BlockSpec((1, 3, D_MODEL), lambda i, j: (i, 0, 0)),
                  pl.BlockSpec((1, D_MODEL), const2),
                  pl.BlockSpec((D_MODEL, nw), const2),
                  tok(A_ROW_GROUPS * A_GROUP_DIM),
                  pl.BlockSpec((1, ncg, GRID_W, t // GRID_W, A_GROUP_DIM), lambda i, j: (i, 0, 0, j, 0)),
                  tok(B_KEY_WIDTH), tok(B_KEY_WIDTH), tok(B_KEY_WIDTH),
                  tok(B_KEY_WIDTH), tok(B_KEY_WIDTH), tok(B_KEY_WIDTH), tok(B_VAL_WIDTH),
                  tot_spec, tot_spec, kv_spec, s0_spec, st_spec,
                  pl.BlockSpec((A_GROUPS, A_CHUNK, A_CHUNK), const3),
                  pl.BlockSpec((A_GROUPS, A_CHUNK, A_GROUP_DIM), const3),
                  pl.BlockSpec((1, B_VAL_WIDTH), const2),
                  pl.BlockSpec((A_WIDTH, D_MODEL), const2),
                  pl.BlockSpec((B_VAL_WIDTH, D_MODEL), const2),
                  pl.BlockSpec((D_MODEL, D_MODEL), const2),
                  pl.BlockSpec((1, D_MODEL), const2)],
        out_specs=tok(D_MODEL),
        out_shape=jax.ShapeDtypeStruct((b, l, D_MODEL), F32),
        scratch_shapes=[pltpu.VMEM((t, A_WIDTH), BF16), pltpu.VMEM((t, B_VAL_WIDTH), F32),
                        pltpu.VMEM((nblk, B_HEADS, GLA_BLOCK, GLA_BLOCK), BF16),
                        pltpu.VMEM((B_DV, B_KEY_WIDTH), F32)],
        compiler_params=pltpu.CompilerParams(
            dimension_semantics=("parallel", "arbitrary"), vmem_limit_bytes=VMEM_LIMIT_BYTES),
        name="main",
    )(x, mod3, ng, w_main, vn_row, sv_col, *gla_ops, ws, bs, bg, wpa, wpb, wo, fg)


def kernel(x, c, ctx, c_ctx, w_mod, b_mod, norm_g, w_in, a_ln_g, a_ln_b, a_ws, a_bs, b_gate_w2, b_gate_b,
           b_norm_g, w_proj_a, w_proj_b, w_out, final_norm_g):
    assert w_mod.shape[0] == 1, "single-layer block"
    b, l, _ = x.shape
    rows = l // GRID_W
    assert rows == A_CHUNK
    ng = norm_g[0][None, :]

    cc = jnp.zeros((8, D_MODEL), F32).at[0:b].set(c).at[b].set(c_ctx)
    w_pre, w_main, wpa16, wpb16, wo16, mod = _wprep_call(w_in[0].T, w_proj_a, w_proj_b, w_out, cc, w_mod, b_mod)
    mod3 = mod[0:b].reshape(b, 3, D_MODEL)
    w2, gb = b_gate_w2[0], b_gate_b[0]
    s0f, s0b = _ctx_call(ctx, mod, ng, w_pre, w2, gb, b)

    (qd_f, kd_f, kc_f, qd_b, kd_b, kc_b, v16, tot_f, tot_b, kv_f, s_b, vn_row, vn_col) = _pre_call(
        x, mod3, ng, w_pre, w2, gb, a_ln_g[0][None, :], a_ln_b[0][None, :], s0b, 1024)
    gla_ops = (qd_f, kd_f, kc_f, qd_b, kd_b, kc_b, v16, tot_f, tot_b, kv_f, s0f, s_b)

    bs_b = jnp.broadcast_to(a_bs[0][:, :, None], (A_GROUPS, A_CHUNK, A_GROUP_DIM))
    sv_col = _colmix_call(a_ws[0][A_ROW_GROUPS:], bs_b[A_ROW_GROUPS:], vn_col, 32)

    return _main_call(x, mod3, ng, w_main, vn_row, sv_col, gla_ops, a_ws[0], bs_b, b_norm_g[0][None, :],
                      wpa16, wpb16, wo16, final_norm_g[None, :], 1024)
```

```python
import functools

import jax
import jax.numpy as jnp
from jax import lax
from jax.experimental import pallas as pl
from jax.experimental.pallas import tpu as pltpu

D_MODEL = 1024
GRID_W = 64
EPS = 1e-6

A_WIDTH = 512
A_GROUPS = 4
A_GROUP_DIM = 128
A_CHUNK = 128
A_ROW_GROUPS = 2

B_HEADS = 4
B_DK = 64
B_DV = 128
B_KEY_WIDTH = 256
B_VAL_WIDTH = 512
B_GATE_RANK = 16
B_GATE_TAU = 16.0
LOG2E = 1.4426950408889634
B_CHUNK = 64

Q0 = 0
K0 = Q0 + B_KEY_WIDTH
V0 = K0 + B_KEY_WIDTH
LR0 = V0 + B_VAL_WIDTH
ZB0 = LR0 + 2 * B_GATE_RANK
UA0 = ZB0 + B_VAL_WIDTH
VA0 = UA0 + A_WIDTH
ZA0 = VA0 + A_WIDTH
G0 = ZA0 + A_WIDTH
IN_WIDTH = G0 + 2 * D_MODEL

LANES = 128

P_LR = 0
P_K = P_LR + LANES
P_Q = P_K + B_KEY_WIDTH
P_V = P_Q + B_KEY_WIDTH
P_VA = P_V + B_VAL_WIDTH
PRE_WIDTH = P_VA + A_WIDTH
GLA_BLOCK = 256
GLA_NC = GLA_BLOCK // B_CHUNK

VMEM_LIMIT_BYTES = 56 * 1024 * 1024

BF16 = jnp.bfloat16
F32 = jnp.float32


def _dot(a, b):
    return jnp.dot(a.astype(BF16), b.astype(BF16), preferred_element_type=F32)


def _dot_nt(a, b):
    return lax.dot_general(a.astype(BF16), b.astype(BF16), (((1,), (1,)), ((), ())),
                           preferred_element_type=F32)


def _dot_tn(a, b):
    return lax.dot_general(a.astype(BF16), b.astype(BF16), (((0,), (0,)), ((), ())),
                           preferred_element_type=F32)


def _silu(x):
    return x * jax.nn.sigmoid(x)


def _rms_rows(x):
    return x * lax.rsqrt(jnp.mean(x * x, axis=-1, keepdims=True) + EPS)


def _wprep_kernel(wt_ref, wpa_ref, wpb_ref, wo_ref, c_ref, cctx_ref, wm_ref, bm_ref, abs_ref,
                  wpre_ref, wmain_ref, wpa16_ref, wpb16_ref, wo16_ref, mod_ref, bst_ref):
    def put(dst_ref, c0, r0, n, scale=None):
        for s in range(0, n, 2 * LANES):
            m = min(2 * LANES, n - s)
            blk = wt_ref[r0 + s:r0 + s + m, :].T
            dst_ref[:, c0 + s:c0 + s + m] = (blk if scale is None else blk * scale).astype(BF16)

    put(wpre_ref, P_LR, LR0, LANES)
    put(wpre_ref, P_K, K0, B_KEY_WIDTH)
    put(wpre_ref, P_Q, Q0, B_KEY_WIDTH, B_DK ** -0.5)
    put(wpre_ref, P_V, V0, B_VAL_WIDTH)
    put(wpre_ref, P_VA, VA0, A_WIDTH)
    put(wmain_ref, 0, ZB0, VA0 - ZB0)
    put(wmain_ref, VA0 - ZB0, ZA0, IN_WIDTH - ZA0)
    wpa16_ref[...] = wpa_ref[0].astype(BF16)
    wpb16_ref[...] = wpb_ref[0].astype(BF16)
    wo16_ref[...] = wo_ref[0].astype(BF16)
    nb = c_ref.shape[0]
    cc = jnp.concatenate([c_ref[...], cctx_ref[...], jnp.zeros((8 - nb - 1, D_MODEL), F32)], axis=0)
    mod_ref[...] = _dot(_silu(cc), wm_ref[0]) + bm_ref[...]
    g = pl.program_id(0)
    bst_ref[0] = jnp.broadcast_to(abs_ref[0, pl.ds(g, 1), :], (A_GROUP_DIM, A_CHUNK)).T


def _wprep_call(w_in_t, w_proj_a, w_proj_b, w_out, c, c_ctx, w_mod, b_mod, a_bs):
    steps = A_GROUPS
    n_main = (VA0 - ZB0) + (IN_WIDTH - ZA0)

    def rows3(a):
        return pl.BlockSpec((1, a.shape[1] // steps, a.shape[2]), lambda i: (0, i, 0))

    def rows2(nrows, ncols):
        return (pl.BlockSpec((nrows // steps, ncols), lambda i: (i, 0)),
                jax.ShapeDtypeStruct((nrows, ncols), BF16))

    n_mod = w_mod.shape[2]
    outs = [rows2(D_MODEL, PRE_WIDTH), rows2(D_MODEL, n_main), rows2(A_WIDTH, D_MODEL),
            rows2(B_VAL_WIDTH, D_MODEL), rows2(D_MODEL, D_MODEL),
            (pl.BlockSpec((8, n_mod // steps), lambda i: (0, i)), jax.ShapeDtypeStruct((8, n_mod), F32)),
            (pl.BlockSpec((1, A_CHUNK, A_GROUP_DIM), lambda i: (i, 0, 0)),
             jax.ShapeDtypeStruct((A_GROUPS, A_CHUNK, A_GROUP_DIM), F32))]
    return pl.pallas_call(
        _wprep_kernel,
        grid=(steps,),
        in_specs=[pl.BlockSpec((IN_WIDTH, D_MODEL // steps), lambda i: (0, i)),
                  rows3(w_proj_a), rows3(w_proj_b), rows3(w_out),
                  pl.BlockSpec(c.shape, lambda i: (0, 0)),
                  pl.BlockSpec((1, D_MODEL), lambda i: (0, 0)),
                  pl.BlockSpec((1, D_MODEL, n_mod // steps), lambda i: (0, 0, i)),
                  pl.BlockSpec((1, n_mod // steps), lambda i: (0, i)),
                  pl.BlockSpec(a_bs.shape, lambda i: (0, 0, 0))],
        out_specs=[o[0] for o in outs],
        out_shape=[o[1] for o in outs],
        compiler_params=pltpu.CompilerParams(
            dimension_semantics=("parallel",), vmem_limit_bytes=VMEM_LIMIT_BYTES),
        name="wprep",
    )(w_in_t, w_proj_a, w_proj_b, w_out, c, c_ctx[None, :], w_mod, b_mod, a_bs)


def _gate_logs(lr, w2_ref, gb_ref):
    out = []
    for r in range(2):
        logits = _dot(lr[:, r * B_GATE_RANK:(r + 1) * B_GATE_RANK], w2_ref[r]) + gb_ref[r:r + 1, :]
        log_sig = jnp.minimum(logits, 0.0) - jnp.log(1.0 + jnp.exp(-jnp.abs(logits)))
        out.append(log_sig * (LOG2E / B_GATE_TAU))
    return out


def _chunk_tri(reverse):
    i = lax.broadcasted_iota(jnp.int32, (GLA_BLOCK, GLA_BLOCK), 0)
    j = lax.broadcasted_iota(jnp.int32, (GLA_BLOCK, GLA_BLOCK), 1)
    same = (i // B_CHUNK) == (j // B_CHUNK)
    tri = (j >= i) if reverse else (j <= i)
    return (same & tri).astype(BF16)


def _sum_rows(rows):
    acc = rows[0]
    for r in rows[1:]:
        acc = acc + r
    return acc


def _block_cum(a, tri, reverse):
    hi = a.astype(BF16)
    lo = (a - hi.astype(F32)).astype(BF16)
    cum = (jnp.dot(tri, hi, preferred_element_type=F32) + jnp.dot(tri, lo, preferred_element_type=F32))
    last = 0 if reverse else B_CHUNK - 1
    tots = [cum[c * B_CHUNK + last:c * B_CHUNK + last + 1, :] for c in range(GLA_NC)]
    return cum, tots


def _block_keys(k, cum, tots, reverse):
    totb = jnp.concatenate([jnp.broadcast_to(t, (B_CHUNK, B_KEY_WIDTH)) for t in tots], axis=0)
    kdec = k * jnp.exp2(totb - cum)
    later = []
    for c in range(GLA_NC):
        idx = list(range(0, c)) if reverse else list(range(c + 1, GLA_NC))
        if idx:
            later.append(jnp.broadcast_to(jnp.exp2(_sum_rows([tots[m] for m in idx])), (B_CHUNK, B_KEY_WIDTH)))
        else:
            later.append(jnp.ones((B_CHUNK, B_KEY_WIDTH), F32))
    return kdec, kdec * jnp.concatenate(later, axis=0)


def _pre_project(h16, w_ref, want_q):
    lr = jnp.dot(h16, w_ref[:, P_LR:P_LR + LANES], preferred_element_type=F32)[:, 0:2 * B_GATE_RANK]
    k = jnp.dot(h16, w_ref[:, P_K:P_K + B_KEY_WIDTH], preferred_element_type=F32)
    q = jnp.dot(h16, w_ref[:, P_Q:P_Q + B_KEY_WIDTH], preferred_element_type=F32) if want_q else None
    v = jnp.dot(h16, w_ref[:, P_V:P_V + B_VAL_WIDTH], preferred_element_type=F32)
    return lr, k, q, v


def _block_kv_t(v16, kblk):
    k16 = kblk.astype(BF16)
    lane = lax.broadcasted_iota(jnp.int32, (1, B_KEY_WIDTH), 1)
    acc = jnp.zeros((B_DV, B_KEY_WIDTH), F32)
    for h in range(B_HEADS):
        full = _dot_tn(v16[:, h * B_DV:(h + 1) * B_DV], k16)
        acc = acc + jnp.where((lane // B_DK) == h, full, 0.0)
    return acc


def _ctx_kernel(ctx_ref, mod_ref, ng_ref, w_ref, w2_ref, gb_ref, sf_ref, sb_ref, *, ctx_row):
    nb, lc, _ = ctx_ref.shape
    xc = ctx_ref[...].reshape(nb * lc, D_MODEL)
    shift = mod_ref[ctx_row:ctx_row + 1, 0:D_MODEL]
    scale = mod_ref[ctx_row:ctx_row + 1, D_MODEL:2 * D_MODEL]
    hc = _rms_rows(xc) * ng_ref[...] * (1.0 + scale) + shift
    lr, k, _, v = _pre_project(hc.astype(BF16), w_ref, False)
    v16 = v.astype(BF16)
    a_f, a_b = _gate_logs(lr, w2_ref, gb_ref)
    for bi in range(nb):
        rs = slice(bi * lc, (bi + 1) * lc)
        for a, reverse, out_ref in ((a_f, False, sf_ref), (a_b, True, sb_ref)):
            cum, tots = _block_cum(a[rs], _chunk_tri(reverse), reverse)
            _, kblk = _block_keys(k[rs], cum, tots, reverse)
            out_ref[bi] = _block_kv_t(v16[rs], kblk)


def _ctx_call(ctx, mod, ng, w_kvl, w2, gb, ctx_row):
    b, lc, _ = ctx.shape
    assert lc == GLA_BLOCK, "context length must be one GLA block"
    nw = w_kvl.shape[1]
    st = jax.ShapeDtypeStruct((b, B_DV, B_KEY_WIDTH), F32)
    st_spec = pl.BlockSpec((b, B_DV, B_KEY_WIDTH), lambda i: (0, 0, 0))
    return pl.pallas_call(
        functools.partial(_ctx_kernel, ctx_row=ctx_row),
        grid=(1,),
        in_specs=[pl.BlockSpec((b, lc, D_MODEL), lambda i: (0, 0, 0)),
                  pl.BlockSpec((8, 3 * D_MODEL), lambda i: (0, 0)),
                  pl.BlockSpec((1, D_MODEL), lambda i: (0, 0)),
                  pl.BlockSpec((D_MODEL, nw), lambda i: (0, 0)),
                  pl.BlockSpec((2, B_GATE_RANK, B_KEY_WIDTH), lambda i: (0, 0, 0)),
                  pl.BlockSpec((2, B_KEY_WIDTH), lambda i: (0, 0))],
        out_specs=[st_spec, st_spec],
        out_shape=[st, st],
        name="ctx",
    )(ctx, mod, ng, w_kvl, w2, gb)


def _latent_h(x, mod_ref, ng_ref):
    row = pl.ds(pl.program_id(0), 1)
    shift = mod_ref[row, 0:D_MODEL]
    gain = ng_ref[...] * (1.0 + mod_ref[row, D_MODEL:2 * D_MODEL])
    return _rms_rows(x) * gain + shift


def _pre_kernel(x_ref, mod_ref, ng_ref, w_ref, w2_ref, gb_ref, lg_ref, lb_ref, s0b_ref,
                qdf_ref, kdf_ref, kcf_ref, qdb_ref, kdb_ref, kcb_ref, v_ref,
                totf_ref, totb_ref, kvf_ref, sb_ref, vr_ref, vc_ref, sb_scr):
    @pl.when(pl.program_id(1) == 0)
    def _():
        sb_scr[...] = s0b_ref[0]

    h16 = _latent_h(x_ref[0], mod_ref, ng_ref).astype(BF16)
    nblk = x_ref.shape[1] // GLA_BLOCK
    blocks = [slice(blk * GLA_BLOCK, (blk + 1) * GLA_BLOCK) for blk in range(nblk)]
    dirs = ((False, qdf_ref, kdf_ref, kcf_ref, totf_ref), (True, qdb_ref, kdb_ref, kcb_ref, totb_ref))

    lrk = jnp.dot(h16, w_ref[:, P_LR:P_K + B_KEY_WIDTH], preferred_element_type=F32)
    lr = lrk[:, 0:2 * B_GATE_RANK]
    k = lrk[:, P_K:P_K + B_KEY_WIDTH]
    logs = _gate_logs(lr, w2_ref, gb_ref)
    q = jnp.dot(h16, w_ref[:, P_Q:P_Q + B_KEY_WIDTH], preferred_element_type=F32)
    v16 = jnp.dot(h16, w_ref[:, P_V:P_V + B_VAL_WIDTH], preferred_element_type=F32).astype(BF16)
    v_ref[0] = v16

    cums = {}
    for d, (reverse, _, _, _, tot_ref) in enumerate(dirs):
        tri = _chunk_tri(reverse)
        for blk, rs in enumerate(blocks):
            cum, tots = _block_cum(logs[d][rs], tri, reverse)
            cums[d, blk] = (cum, tots)
            for c in range(GLA_NC):
                tot_ref[0, blk, c:c + 1, :] = tots[c]

    va = jnp.dot(h16, w_ref[:, P_VA:P_VA + A_WIDTH], preferred_element_type=F32)

    kv_b = {}
    for d, (reverse, qd_ref, kd_ref, kc_ref, _) in enumerate(dirs):
        for blk, rs in enumerate(blocks):
            cum, tots = cums[d, blk]
            kdec, kblk = _block_keys(k[rs], cum, tots, reverse)
            kc_ref[0, rs, :] = kdec.astype(BF16)
            kd_ref[0, rs, :] = (k[rs] * jnp.exp2(-cum)).astype(BF16)
            qd_ref[0, rs, :] = (q[rs] * jnp.exp2(cum)).astype(BF16)
            if reverse:
                kv_b[blk] = _block_kv_t(v16[rs], kblk)
            else:
                kvf_ref[0, blk] = _block_kv_t(v16[rs], kblk)

    s = sb_scr[...]
    for blk in reversed(range(nblk)):
        sb_ref[0, blk] = s.T.astype(BF16)
        s = jnp.exp2(_sum_rows(cums[1, blk][1])) * s + kv_b[blk]
    sb_scr[...] = s

    vc = va - jnp.mean(va, axis=-1, keepdims=True)
    vn = vc * lax.rsqrt(jnp.mean(vc * vc, axis=-1, keepdims=True) + EPS) * lg_ref[...] + lb_ref[...]
    nr = A_ROW_GROUPS * A_GROUP_DIM
    vr_ref[0] = vn[:, 0:nr].astype(BF16)
    for g in range(A_GROUPS - A_ROW_GROUPS):
        vg = vn[:, nr + g * A_GROUP_DIM:nr + (g + 1) * A_GROUP_DIM]
        vc_ref[0, g] = jnp.swapaxes(vg.reshape(vg.shape[0] // GRID_W, GRID_W, A_GROUP_DIM), 0, 1)


def _pre_call(x, mod, ng, w_pre, w2, gb, lg, lb, s0b, t):
    b, l, _ = x.shape
    nw = w_pre.shape[1]
    ncg = A_GROUPS - A_ROW_GROUPS
    nblk = t // GLA_BLOCK
    nj = l // t
    rev3 = lambda i, j: (i, nj - 1 - j, 0)
    rev4 = lambda i, j: (i, nj - 1 - j, 0, 0)

    def tok(width, dtype):
        return (pl.BlockSpec((1, t, width), rev3), jax.ShapeDtypeStruct((b, l, width), dtype))

    tot = (pl.BlockSpec((1, nblk, GLA_NC, B_KEY_WIDTH), rev4),
           jax.ShapeDtypeStruct((b, l // GLA_BLOCK, GLA_NC, B_KEY_WIDTH), F32))
    kv = (pl.BlockSpec((1, nblk, B_DV, B_KEY_WIDTH), rev4),
          jax.ShapeDtypeStruct((b, l // GLA_BLOCK, B_DV, B_KEY_WIDTH), F32))
    st = (pl.BlockSpec((1, nblk, B_KEY_WIDTH, B_DV), rev4),
          jax.ShapeDtypeStruct((b, l // GLA_BLOCK, B_KEY_WIDTH, B_DV), BF16))
    outs = [tok(B_KEY_WIDTH, BF16)] * 6 + [tok(B_VAL_WIDTH, BF16), tot, tot, kv, st,
                                           tok(A_ROW_GROUPS * A_GROUP_DIM, BF16),
                                           (pl.BlockSpec((1, ncg, GRID_W, t // GRID_W, A_GROUP_DIM),
                                                         lambda i, j: (i, 0, 0, nj - 1 - j, 0)),
                                            jax.ShapeDtypeStruct((b, ncg, GRID_W, l // GRID_W, A_GROUP_DIM), F32))]
    const2 = lambda i, j: (0, 0)
    return pl.pallas_call(
        _pre_kernel,
        grid=(b, nj),
        in_specs=[pl.BlockSpec((1, t, D_MODEL), rev3),
                  pl.BlockSpec((8, 3 * D_MODEL), const2),
                  pl.BlockSpec((1, D_MODEL), const2),
                  pl.BlockSpec((D_MODEL, nw), const2),
                  pl.BlockSpec((2, B_GATE_RANK, B_KEY_WIDTH), lambda i, j: (0, 0, 0)),
                  pl.BlockSpec((2, B_KEY_WIDTH), const2),
                  pl.BlockSpec((1, A_WIDTH), const2),
                  pl.BlockSpec((1, A_WIDTH), const2),
                  pl.BlockSpec((1, B_DV, B_KEY_WIDTH), lambda i, j: (i, 0, 0))],
        out_specs=[o[0] for o in outs],
        out_shape=[o[1] for o in outs],
        scratch_shapes=[pltpu.VMEM((B_DV, B_KEY_WIDTH), F32)],
        compiler_params=pltpu.CompilerParams(
            dimension_semantics=("parallel", "arbitrary"), vmem_limit_bytes=VMEM_LIMIT_BYTES),
        name="pre",
    )(x, mod, ng, w_pre, w2, gb, lg, lb, s0b)


def _scale_rows(x16, scales):
    parts = []
    for c, s in enumerate(scales):
        xc = x16[c * B_CHUNK:(c + 1) * B_CHUNK, :]
        parts.append(xc if s is None else (xc.astype(F32) * s).astype(BF16))
    return parts[0] if len(parts) == 1 else jnp.concatenate(parts, axis=0)


def _exp_sum(tots, idx):
    return jnp.exp2(_sum_rows([tots[m] for m in idx])) if idx else None


def _gla_block(qdf, kdf, kcf, qdb, kdb, kcb, v16, tf, tb, sf, sb, p_scr):
    nc = GLA_NC
    ch = B_CHUNK
    lane = lax.broadcasted_iota(jnp.int32, (1, LANES), 1)
    hm = [(lane < B_DK).astype(BF16), (lane >= B_DK).astype(BF16)]

    def pair_heads(x):
        return jnp.concatenate([x * hm[0], x * hm[1]], axis=0)

    row = lax.broadcasted_iota(jnp.int32, (ch, 2 * ch), 0)
    col = lax.broadcasted_iota(jnp.int32, (ch, 2 * ch), 1)
    first = col < ch
    zeros = jnp.zeros((ch, LANES), BF16)
    for g in range(B_HEADS // 2):
        ls = slice(g * LANES, (g + 1) * LANES)
        for i in range(nc // 2):
            pr = slice(2 * i * ch, (2 * i + 2) * ch)
            re = slice(2 * i * ch, (2 * i + 1) * ch)
            ro = slice((2 * i + 1) * ch, (2 * i + 2) * ch)
            sc_f = _dot_nt(pair_heads(qdf[pr, ls]), jnp.concatenate([kdf[pr, ls], kcf[re, ls], zeros], axis=0))
            sc_b = _dot_nt(pair_heads(qdb[pr, ls]), jnp.concatenate([kdb[pr, ls], zeros, kcb[ro, ls]], axis=0))
            for hh in range(2):
                he = slice(hh * 2 * ch, hh * 2 * ch + ch)
                ho = slice(hh * 2 * ch + ch, (hh + 1) * 2 * ch)
                even = jnp.where(first,
                                 jnp.where(col <= row, sc_f[he, 0:2 * ch], 0.0)
                                 + jnp.where(col >= row, sc_b[he, 0:2 * ch], 0.0),
                                 sc_b[he, 2 * ch:])
                odd = jnp.where(first, sc_f[ho, 2 * ch:],
                                jnp.where(col - ch <= row, sc_f[ho, 0:2 * ch], 0.0)
                                + jnp.where(col - ch >= row, sc_b[ho, 0:2 * ch], 0.0))
                p_scr[2 * g + hh, re, pr] = even.astype(BF16)
                p_scr[2 * g + hh, ro, pr] = odd.astype(BF16)

    def cross(lo, hi):
        if hi - lo <= 2:
            return
        mid = (lo + hi) // 2
        cross(lo, mid)
        cross(mid, hi)
        rl = slice(lo * ch, mid * ch)
        rh = slice(mid * ch, hi * ch)
        n = (hi - mid) * ch
        qf = _scale_rows(qdf[rh], [_exp_sum(tf, range(mid, c)) for c in range(mid, hi)])
        kf = _scale_rows(kcf[rl], [_exp_sum(tf, range(c + 1, mid)) for c in range(lo, mid)])
        qb = _scale_rows(qdb[rl], [_exp_sum(tb, range(c + 1, mid)) for c in range(lo, mid)])
        kb = _scale_rows(kcb[rh], [_exp_sum(tb, range(mid, c)) for c in range(mid, hi)])
        for g in range(B_HEADS // 2):
            ls = slice(g * LANES, (g + 1) * LANES)
            sc_f = _dot_nt(pair_heads(qf[:, ls]), kf[:, ls]).astype(BF16)
            sc_b = _dot_nt(pair_heads(qb[:, ls]), kb[:, ls]).astype(BF16)
            for hh in range(2):
                p_scr[2 * g + hh, rh, rl] = sc_f[hh * n:(hh + 1) * n]
                p_scr[2 * g + hh, rl, rh] = sc_b[hh * n:(hh + 1) * n]

    cross(0, nc)

    qsf = _scale_rows(qdf, [_exp_sum(tf, range(0, c)) for c in range(nc)])
    qsb = _scale_rows(qdb, [_exp_sum(tb, range(c + 1, nc)) for c in range(nc)])
    head_row = lax.broadcasted_iota(jnp.int32, (LANES, 1), 0) // B_DK
    outs = []
    for h in range(B_HEADS):
        ls = slice((h // 2) * LANES, (h // 2 + 1) * LANES)
        own = (head_row == (h % 2)).astype(BF16)
        lhs = jnp.concatenate([p_scr[h], qsf[:, ls], qsb[:, ls]], axis=1)
        rhs = jnp.concatenate([v16[:, h * B_DV:(h + 1) * B_DV], sf[ls, :] * own, sb[ls, :] * own], axis=0)
        outs.append(jnp.dot(lhs, rhs, preferred_element_type=F32))
    return outs


def _colmix_kernel(ws_ref, bs_ref, vn_ref, o_ref):
    nw = vn_ref.shape[2]
    xs = jnp.concatenate([vn_ref[0, 0, w] for w in range(nw)], axis=1)
    y = _dot(ws_ref[0], xs) + bs_ref[0][:, 0:1]
    for w in range(nw):
        o_ref[0, 0, w] = y[:, w * A_GROUP_DIM:(w + 1) * A_GROUP_DIM]


def _colmix_call(ws, bs, vn_col, nw):
    b, g, width, rows, ch = vn_col.shape
    blk = pl.BlockSpec((1, 1, nw, rows, ch), lambda i, j, m: (i, j, m, 0, 0))
    return pl.pallas_call(
        _colmix_kernel,
        grid=(b, g, width // nw),
        in_specs=[pl.BlockSpec((1, rows, rows), lambda i, j, m: (j + A_ROW_GROUPS, 0, 0)),
                  pl.BlockSpec((1, rows, A_GROUP_DIM), lambda i, j, m: (j + A_ROW_GROUPS, 0, 0)),
                  blk],
        out_specs=blk,
        out_shape=jax.ShapeDtypeStruct(vn_col.shape, F32),
        compiler_params=pltpu.CompilerParams(
            dimension_semantics=("parallel", "parallel", "parallel"), vmem_limit_bytes=VMEM_LIMIT_BYTES),
        name="colmix",
    )(ws, bs, vn_col)


def _main_kernel(x_ref, mod_ref, ng_ref, w_ref, vr_ref, sc_ref,
                 qdf_ref, kdf_ref, kcf_ref, qdb_ref, kdb_ref, kcb_ref, v_ref, totf_ref, totb_ref,
                 kvf_ref, s0f_ref, sb_ref,
                 ws_ref, bs_ref, bg_ref, wpa_ref, wpb_ref, wo_ref, fg_ref, o_ref, acta_scr, on_scr, p_scr, sf_scr):
    t = x_ref.shape[1]
    blocks = [slice(blk * GLA_BLOCK, (blk + 1) * GLA_BLOCK) for blk in range(t // GLA_BLOCK)]

    @pl.when(pl.program_id(1) == 0)
    def _():
        sf_scr[...] = s0f_ref[0]

    s = sf_scr[...]
    sf_in = []
    for blk in range(len(blocks)):
        sf_in.append(s.T.astype(BF16))
        s = jnp.exp2(_sum_rows([totf_ref[0, blk, c:c + 1, :] for c in range(GLA_NC)])) * s + kvf_ref[0, blk]
    sf_scr[...] = s

    def gla(blk):
        rs = blocks[blk]
        tf = [totf_ref[0, blk, c:c + 1, :] for c in range(GLA_NC)]
        tb = [totb_ref[0, blk, c:c + 1, :] for c in range(GLA_NC)]
        o_heads = _gla_block(qdf_ref[0, rs, :], kdf_ref[0, rs, :], kcf_ref[0, rs, :],
                             qdb_ref[0, rs, :], kdb_ref[0, rs, :], kcb_ref[0, rs, :], v_ref[0, rs, :],
                             tf, tb, sf_in[blk], sb_ref[0, blk], p_scr.at[blk])
        for hd in range(B_HEADS):
            on_scr[rs, hd * B_DV:(hd + 1) * B_DV] = _rms_rows(o_heads[hd])

    def proj(c0, n):
        return jnp.dot(h16, w_ref[:, c0:c0 + n], preferred_element_type=F32)

    gla(0)
    h16 = _latent_h(x_ref[0], mod_ref, ng_ref).astype(BF16)
    zb = proj(0, B_VAL_WIDTH)
    u = proj(B_VAL_WIDTH, A_WIDTH)
    za = proj(B_VAL_WIDTH + A_WIDTH, A_WIDTH)
    for blk in range(1, len(blocks)):
        gla(blk)
    gate_a = jax.nn.sigmoid(proj(B_VAL_WIDTH + 2 * A_WIDTH, D_MODEL))

    uz = u * _silu(za)
    for g in range(A_GROUPS):
        cs = slice(g * A_GROUP_DIM, (g + 1) * A_GROUP_DIM)
        if g < A_ROW_GROUPS:
            chunks = [slice(c * A_CHUNK, (c + 1) * A_CHUNK) for c in range(t // A_CHUNK)]
            sv_all = jnp.dot(ws_ref[g].astype(BF16), jnp.concatenate([vr_ref[0, rs, cs] for rs in chunks], axis=1),
                             preferred_element_type=F32)
            for c, rs in enumerate(chunks):
                sv = sv_all[:, c * A_GROUP_DIM:(c + 1) * A_GROUP_DIM] + bs_ref[g][:, 0:1]
                acta_scr[rs, cs] = (uz[rs, cs] * sv).astype(BF16)
        else:
            sv = jnp.swapaxes(sc_ref[0, g - A_ROW_GROUPS], 0, 1).reshape(t, A_GROUP_DIM)
            acta_scr[:, cs] = (uz[:, cs] * sv).astype(BF16)
    ya = jnp.dot(acta_scr[...], wpa_ref[...], preferred_element_type=F32)

    gate_b = jax.nn.sigmoid(proj(B_VAL_WIDTH + 2 * A_WIDTH + D_MODEL, D_MODEL))
    yb = _dot(on_scr[...] * (_silu(zb) * bg_ref[...]), wpb_ref[...])

    m = (gate_a * ya + gate_b * yb).astype(BF16)
    for rs in blocks:
        y = jnp.dot(m[rs], wo_ref[...], preferred_element_type=F32)
        xo = x_ref[0, rs, :] + mod_ref[pl.ds(pl.program_id(0), 1), 2 * D_MODEL:] * y
        o_ref[0, rs, :] = _rms_rows(xo) * fg_ref[...]


def _main_call(x, mod, ng, w_main, vn_row, sv_col, gla_ops, ws, bs, bg, wpa, wpb, wo, fg, t):
    b, l, _ = x.shape
    nw = w_main.shape[1]
    ncg = A_GROUPS - A_ROW_GROUPS
    const2 = lambda i, j: (0, 0)
    const3 = lambda i, j: (0, 0, 0)
    tok = lambda width: pl.BlockSpec((1, t, width), lambda i, j: (i, j, 0))
    nblk = t // GLA_BLOCK
    tot_spec = pl.BlockSpec((1, nblk, GLA_NC, B_KEY_WIDTH), lambda i, j: (i, j, 0, 0))
    st_spec = pl.BlockSpec((1, nblk, B_KEY_WIDTH, B_DV), lambda i, j: (i, j, 0, 0))
    kv_spec = pl.BlockSpec((1, nblk, B_DV, B_KEY_WIDTH), lambda i, j: (i, j, 0, 0))
    s0_spec = pl.BlockSpec((1, B_DV, B_KEY_WIDTH), lambda i, j: (i, 0, 0))
    return pl.pallas_call(
        _main_kernel,
        grid=(b, l // t),
        in_specs=[tok(D_MODEL),
                  pl.BlockSpec((8, 3 * D_MODEL), const2),
                  pl.BlockSpec((1, D_MODEL), const2),
                  pl.BlockSpec((D_MODEL, nw), const2),
                  tok(A_ROW_GROUPS * A_GROUP_DIM),
                  pl.BlockSpec((1, ncg, GRID_W, t // GRID_W, A_GROUP_DIM), lambda i, j: (i, 0, 0, j, 0)),
                  tok(B_KEY_WIDTH), tok(B_KEY_WIDTH), tok(B_KEY_WIDTH),
                  tok(B_KEY_WIDTH), tok(B_KEY_WIDTH), tok(B_KEY_WIDTH), tok(B_VAL_WIDTH),
                  tot_spec, tot_spec, kv_spec, s0_spec, st_spec,
                  pl.BlockSpec((A_GROUPS, A_CHUNK, A_CHUNK), const3),
                  pl.BlockSpec((A_GROUPS, A_CHUNK, A_GROUP_DIM), const3),
                  pl.BlockSpec((1, B_VAL_WIDTH), const2),
                  pl.BlockSpec((A_WIDTH, D_MODEL), const2),
                  pl.BlockSpec((B_VAL_WIDTH, D_MODEL), const2),
                  pl.BlockSpec((D_MODEL, D_MODEL), const2),
                  pl.BlockSpec((1, D_MODEL), const2)],
        out_specs=tok(D_MODEL),
        out_shape=jax.ShapeDtypeStruct((b, l, D_MODEL), F32),
        scratch_shapes=[pltpu.VMEM((t, A_WIDTH), BF16), pltpu.VMEM((t, B_VAL_WIDTH), F32),
                        pltpu.VMEM((nblk, B_HEADS, GLA_BLOCK, GLA_BLOCK), BF16),
                        pltpu.VMEM((B_DV, B_KEY_WIDTH), F32)],
        compiler_params=pltpu.CompilerParams(
            dimension_semantics=("parallel", "arbitrary"), vmem_limit_bytes=VMEM_LIMIT_BYTES),
        name="main",
    )(x, mod, ng, w_main, vn_row, sv_col, *gla_ops, ws, bs, bg, wpa, wpb, wo, fg)


def kernel(x, c, ctx, c_ctx, w_mod, b_mod, norm_g, w_in, a_ln_g, a_ln_b, a_ws, a_bs, b_gate_w2, b_gate_b,
           b_norm_g, w_proj_a, w_proj_b, w_out, final_norm_g):
    assert w_mod.shape[0] == 1, "single-layer block"
    b, l, _ = x.shape
    rows = l // GRID_W
    assert rows == A_CHUNK
    ng = norm_g[0][None, :]

    w_pre, w_main, wpa16, wpb16, wo16, mod, bs_t = _wprep_call(
        w_in[0].T, w_proj_a, w_proj_b, w_out, c, c_ctx, w_mod, b_mod, a_bs)
    w2, gb = b_gate_w2[0], b_gate_b[0]
    s0f, s0b = _ctx_call(ctx, mod, ng, w_pre, w2, gb, b)

    (qd_f, kd_f, kc_f, qd_b, kd_b, kc_b, v16, tot_f, tot_b, kv_f, s_b, vn_row, vn_col) = _pre_call(
        x, mod, ng, w_pre, w2, gb, a_ln_g[0][None, :], a_ln_b[0][None, :], s0b, 1024)
    gla_ops = (qd_f, kd_f, kc_f, qd_b, kd_b, kc_b, v16, tot_f, tot_b, kv_f, s0f, s_b)

    sv_col = _colmix_call(a_ws[0], bs_t, vn_col, 32)

    return _main_call(x, mod, ng, w_main, vn_row, sv_col, gla_ops, a_ws[0], bs_t, b_norm_g[0][None, :],
                      wpa16, wpb16, wo16, final_norm_g[None, :], 1024)
```

```python
import functools

import jax
import jax.numpy as jnp
from jax import lax
from jax.experimental import pallas as pl
from jax.experimental.pallas import tpu as pltpu

D_MODEL = 1024
GRID_W = 64
EPS = 1e-6

A_WIDTH = 512
A_GROUPS = 4
A_GROUP_DIM = 128
A_CHUNK = 128
A_ROW_GROUPS = 2

B_HEADS = 4
B_DK = 64
B_DV = 128
B_KEY_WIDTH = 256
B_VAL_WIDTH = 512
B_GATE_RANK = 16
B_GATE_TAU = 16.0
LOG2E = 1.4426950408889634
B_CHUNK = 64

Q0 = 0
K0 = Q0 + B_KEY_WIDTH
V0 = K0 + B_KEY_WIDTH
LR0 = V0 + B_VAL_WIDTH
ZB0 = LR0 + 2 * B_GATE_RANK
UA0 = ZB0 + B_VAL_WIDTH
VA0 = UA0 + A_WIDTH
ZA0 = VA0 + A_WIDTH
G0 = ZA0 + A_WIDTH
IN_WIDTH = G0 + 2 * D_MODEL

LANES = 128

P_LR = 0
P_K = P_LR + LANES
P_Q = P_K + B_KEY_WIDTH
P_V = P_Q + B_KEY_WIDTH
P_VA = P_V + B_VAL_WIDTH
PRE_WIDTH = P_VA + A_WIDTH
GLA_BLOCK = 256
GLA_NC = GLA_BLOCK // B_CHUNK

VMEM_LIMIT_BYTES = 56 * 1024 * 1024

BF16 = jnp.bfloat16
F32 = jnp.float32


def _dot(a, b):
    return jnp.dot(a.astype(BF16), b.astype(BF16), preferred_element_type=F32)


def _dot_nt(a, b):
    return lax.dot_general(a.astype(BF16), b.astype(BF16), (((1,), (1,)), ((), ())),
                           preferred_element_type=F32)


def _dot_tn(a, b):
    return lax.dot_general(a.astype(BF16), b.astype(BF16), (((0,), (0,)), ((), ())),
                           preferred_element_type=F32)


def _silu(x):
    return x * jax.nn.sigmoid(x)


def _rms_rows(x):
    return x * lax.rsqrt(jnp.mean(x * x, axis=-1, keepdims=True) + EPS)


def _wprep_kernel(wt_ref, wpa_ref, wpb_ref, wo_ref, c_ref, cctx_ref, wm_ref, bm_ref, abs_ref,
                  wpre_ref, wmain_ref, wpa16_ref, wpb16_ref, wo16_ref, mod_ref, bst_ref):
    def put(dst_ref, c0, r0, n, scale=None):
        for s in range(0, n, 2 * LANES):
            m = min(2 * LANES, n - s)
            blk = wt_ref[r0 + s:r0 + s + m, :].T
            dst_ref[:, c0 + s:c0 + s + m] = (blk if scale is None else blk * scale).astype(BF16)

    put(wpre_ref, P_LR, LR0, LANES)
    put(wpre_ref, P_K, K0, B_KEY_WIDTH)
    put(wpre_ref, P_Q, Q0, B_KEY_WIDTH, B_DK ** -0.5)
    put(wpre_ref, P_V, V0, B_VAL_WIDTH)
    put(wpre_ref, P_VA, VA0, A_WIDTH)
    put(wmain_ref, 0, ZB0, VA0 - ZB0)
    put(wmain_ref, VA0 - ZB0, ZA0, IN_WIDTH - ZA0)
    wpa16_ref[...] = wpa_ref[0].astype(BF16)
    wpb16_ref[...] = wpb_ref[0].astype(BF16)
    wo16_ref[...] = wo_ref[0].astype(BF16)
    nb = c_ref.shape[0]
    cc = jnp.concatenate([c_ref[...], cctx_ref[...], jnp.zeros((8 - nb - 1, D_MODEL), F32)], axis=0)
    mod_ref[...] = _dot(_silu(cc), wm_ref[0]) + bm_ref[...]
    g = pl.program_id(0)
    bst_ref[0] = jnp.broadcast_to(abs_ref[0, pl.ds(g, 1), :], (A_GROUP_DIM, A_CHUNK)).T


def _wprep_call(w_in_t, w_proj_a, w_proj_b, w_out, c, c_ctx, w_mod, b_mod, a_bs):
    steps = A_GROUPS
    n_main = (VA0 - ZB0) + (IN_WIDTH - ZA0)

    def rows3(a):
        return pl.BlockSpec((1, a.shape[1] // steps, a.shape[2]), lambda i: (0, i, 0))

    def rows2(nrows, ncols):
        return (pl.BlockSpec((nrows // steps, ncols), lambda i: (i, 0)),
                jax.ShapeDtypeStruct((nrows, ncols), BF16))

    n_mod = w_mod.shape[2]
    outs = [rows2(D_MODEL, PRE_WIDTH), rows2(D_MODEL, n_main), rows2(A_WIDTH, D_MODEL),
            rows2(B_VAL_WIDTH, D_MODEL), rows2(D_MODEL, D_MODEL),
            (pl.BlockSpec((8, n_mod // steps), lambda i: (0, i)), jax.ShapeDtypeStruct((8, n_mod), F32)),
            (pl.BlockSpec((1, A_CHUNK, A_GROUP_DIM), lambda i: (i, 0, 0)),
             jax.ShapeDtypeStruct((A_GROUPS, A_CHUNK, A_GROUP_DIM), F32))]
    return pl.pallas_call(
        _wprep_kernel,
        grid=(steps,),
        in_specs=[pl.BlockSpec((IN_WIDTH, D_MODEL // steps), lambda i: (0, i)),
                  rows3(w_proj_a), rows3(w_proj_b), rows3(w_out),
                  pl.BlockSpec(c.shape, lambda i: (0, 0)),
                  pl.BlockSpec((1, D_MODEL), lambda i: (0, 0)),
                  pl.BlockSpec((1, D_MODEL, n_mod // steps), lambda i: (0, 0, i)),
                  pl.BlockSpec((1, n_mod // steps), lambda i: (0, i)),
                  pl.BlockSpec(a_bs.shape, lambda i: (0, 0, 0))],
        out_specs=[o[0] for o in outs],
        out_shape=[o[1] for o in outs],
        compiler_params=pltpu.CompilerParams(
            dimension_semantics=("parallel",), vmem_limit_bytes=VMEM_LIMIT_BYTES),
        name="wprep",
    )(w_in_t, w_proj_a, w_proj_b, w_out, c, c_ctx[None, :], w_mod, b_mod, a_bs)


def _gate_logs(lr, w2_ref, gb_ref):
    out = []
    for r in range(2):
        logits = _dot(lr[:, r * B_GATE_RANK:(r + 1) * B_GATE_RANK], w2_ref[r]) + gb_ref[r:r + 1, :]
        log_sig = jnp.minimum(logits, 0.0) - jnp.log(1.0 + jnp.exp(-jnp.abs(logits)))
        out.append(log_sig * (LOG2E / B_GATE_TAU))
    return out


def _chunk_tri(reverse):
    i = lax.broadcasted_iota(jnp.int32, (GLA_BLOCK, GLA_BLOCK), 0)
    j = lax.broadcasted_iota(jnp.int32, (GLA_BLOCK, GLA_BLOCK), 1)
    same = (i // B_CHUNK) == (j // B_CHUNK)
    tri = (j >= i) if reverse else (j <= i)
    return (same & tri).astype(BF16)


def _sum_rows(rows):
    acc = rows[0]
    for r in rows[1:]:
        acc = acc + r
    return acc


def _block_cum(a, tri, reverse):
    hi = a.astype(BF16)
    lo = (a - hi.astype(F32)).astype(BF16)
    cum = (jnp.dot(tri, hi, preferred_element_type=F32) + jnp.dot(tri, lo, preferred_element_type=F32))
    last = 0 if reverse else B_CHUNK - 1
    tots = [cum[c * B_CHUNK + last:c * B_CHUNK + last + 1, :] for c in range(GLA_NC)]
    return cum, tots


def _block_keys(k, cum, tots, reverse):
    totb = jnp.concatenate([jnp.broadcast_to(t, (B_CHUNK, B_KEY_WIDTH)) for t in tots], axis=0)
    kdec = k * jnp.exp2(totb - cum)
    later = []
    for c in range(GLA_NC):
        idx = list(range(0, c)) if reverse else list(range(c + 1, GLA_NC))
        if idx:
            later.append(jnp.broadcast_to(jnp.exp2(_sum_rows([tots[m] for m in idx])), (B_CHUNK, B_KEY_WIDTH)))
        else:
            later.append(jnp.ones((B_CHUNK, B_KEY_WIDTH), F32))
    return kdec, kdec * jnp.concatenate(later, axis=0)


def _pre_project(h16, w_ref, want_q):
    lr = jnp.dot(h16, w_ref[:, P_LR:P_LR + LANES], preferred_element_type=F32)[:, 0:2 * B_GATE_RANK]
    k = jnp.dot(h16, w_ref[:, P_K:P_K + B_KEY_WIDTH], preferred_element_type=F32)
    q = jnp.dot(h16, w_ref[:, P_Q:P_Q + B_KEY_WIDTH], preferred_element_type=F32) if want_q else None
    v = jnp.dot(h16, w_ref[:, P_V:P_V + B_VAL_WIDTH], preferred_element_type=F32)
    return lr, k, q, v


def _block_kv_t(v16, kblk):
    k16 = kblk.astype(BF16)
    lane = lax.broadcasted_iota(jnp.int32, (1, B_KEY_WIDTH), 1)
    acc = jnp.zeros((B_DV, B_KEY_WIDTH), F32)
    for h in range(B_HEADS):
        full = _dot_tn(v16[:, h * B_DV:(h + 1) * B_DV], k16)
        acc = acc + jnp.where((lane // B_DK) == h, full, 0.0)
    return acc


def _ctx_kernel(ctx_ref, mod_ref, ng_ref, w_ref, w2_ref, gb_ref, sf_ref, sb_ref, *, ctx_row):
    nb, lc, _ = ctx_ref.shape
    xc = ctx_ref[...].reshape(nb * lc, D_MODEL)
    shift = mod_ref[ctx_row:ctx_row + 1, 0:D_MODEL]
    scale = mod_ref[ctx_row:ctx_row + 1, D_MODEL:2 * D_MODEL]
    hc = _rms_rows(xc) * ng_ref[...] * (1.0 + scale) + shift
    lr, k, _, v = _pre_project(hc.astype(BF16), w_ref, False)
    v16 = v.astype(BF16)
    a_f, a_b = _gate_logs(lr, w2_ref, gb_ref)
    for bi in range(nb):
        rs = slice(bi * lc, (bi + 1) * lc)
        for a, reverse, out_ref in ((a_f, False, sf_ref), (a_b, True, sb_ref)):
            cum, tots = _block_cum(a[rs], _chunk_tri(reverse), reverse)
            _, kblk = _block_keys(k[rs], cum, tots, reverse)
            out_ref[bi] = _block_kv_t(v16[rs], kblk)


def _ctx_call(ctx, mod, ng, w_kvl, w2, gb, ctx_row):
    b, lc, _ = ctx.shape
    assert lc == GLA_BLOCK, "context length must be one GLA block"
    nw = w_kvl.shape[1]
    st = jax.ShapeDtypeStruct((b, B_DV, B_KEY_WIDTH), F32)
    st_spec = pl.BlockSpec((b, B_DV, B_KEY_WIDTH), lambda i: (0, 0, 0))
    return pl.pallas_call(
        functools.partial(_ctx_kernel, ctx_row=ctx_row),
        grid=(1,),
        in_specs=[pl.BlockSpec((b, lc, D_MODEL), lambda i: (0, 0, 0)),
                  pl.BlockSpec((8, 3 * D_MODEL), lambda i: (0, 0)),
                  pl.BlockSpec((1, D_MODEL), lambda i: (0, 0)),
                  pl.BlockSpec((D_MODEL, nw), lambda i: (0, 0)),
                  pl.BlockSpec((2, B_GATE_RANK, B_KEY_WIDTH), lambda i: (0, 0, 0)),
                  pl.BlockSpec((2, B_KEY_WIDTH), lambda i: (0, 0))],
        out_specs=[st_spec, st_spec],
        out_shape=[st, st],
        name="ctx",
    )(ctx, mod, ng, w_kvl, w2, gb)


def _latent_h(x, mod_ref, ng_ref):
    row = pl.ds(pl.program_id(0), 1)
    shift = mod_ref[row, 0:D_MODEL]
    gain = ng_ref[...] * (1.0 + mod_ref[row, D_MODEL:2 * D_MODEL])
    return _rms_rows(x) * gain + shift


def _pre_kernel(x_ref, mod_ref, ng_ref, w_ref, w2_ref, gb_ref, lg_ref, lb_ref, s0b_ref,
                qdf_ref, kdf_ref, kcf_ref, qdb_ref, kdb_ref, kcb_ref, v_ref,
                totf_ref, totb_ref, kvf_ref, sb_ref, vr_ref, vc_ref, sb_scr):
    @pl.when(pl.program_id(1) == 0)
    def _():
        sb_scr[...] = s0b_ref[0]

    h16 = _latent_h(x_ref[0], mod_ref, ng_ref).astype(BF16)
    nblk = x_ref.shape[1] // GLA_BLOCK
    blocks = [slice(blk * GLA_BLOCK, (blk + 1) * GLA_BLOCK) for blk in range(nblk)]
    dirs = ((False, qdf_ref, kdf_ref, kcf_ref, totf_ref), (True, qdb_ref, kdb_ref, kcb_ref, totb_ref))

    lrk = jnp.dot(h16, w_ref[:, P_LR:P_K + B_KEY_WIDTH], preferred_element_type=F32)
    lr = lrk[:, 0:2 * B_GATE_RANK]
    k = lrk[:, P_K:P_K + B_KEY_WIDTH]
    logs = _gate_logs(lr, w2_ref, gb_ref)
    q = jnp.dot(h16, w_ref[:, P_Q:P_Q + B_KEY_WIDTH], preferred_element_type=F32)
    v16 = jnp.dot(h16, w_ref[:, P_V:P_V + B_VAL_WIDTH], preferred_element_type=F32).astype(BF16)
    v_ref[0] = v16

    cums = {}
    for d, (reverse, _, _, _, tot_ref) in enumerate(dirs):
        tri = _chunk_tri(reverse)
        for blk, rs in enumerate(blocks):
            cum, tots = _block_cum(logs[d][rs], tri, reverse)
            cums[d, blk] = (cum, tots)
            for c in range(GLA_NC):
                tot_ref[0, blk, c:c + 1, :] = tots[c]

    va = jnp.dot(h16, w_ref[:, P_VA:P_VA + A_WIDTH], preferred_element_type=F32)
    vc = va - jnp.mean(va, axis=-1, keepdims=True)
    vn = vc * lax.rsqrt(jnp.mean(vc * vc, axis=-1, keepdims=True) + EPS) * lg_ref[...] + lb_ref[...]
    nr = A_ROW_GROUPS * A_GROUP_DIM
    vr_ref[0] = vn[:, 0:nr].astype(BF16)
    for g in range(A_GROUPS - A_ROW_GROUPS):
        vg = vn[:, nr + g * A_GROUP_DIM:nr + (g + 1) * A_GROUP_DIM]
        vc_ref[0, g] = jnp.swapaxes(vg.reshape(vg.shape[0] // GRID_W, GRID_W, A_GROUP_DIM), 0, 1).astype(BF16)

    kv_b = {}
    for d, (reverse, qd_ref, kd_ref, kc_ref, _) in enumerate(dirs):
        for blk, rs in enumerate(blocks):
            cum, tots = cums[d, blk]
            kdec, kblk = _block_keys(k[rs], cum, tots, reverse)
            kc_ref[0, rs, :] = kdec.astype(BF16)
            kd_ref[0, rs, :] = (k[rs] * jnp.exp2(-cum)).astype(BF16)
            qd_ref[0, rs, :] = (q[rs] * jnp.exp2(cum)).astype(BF16)
            if reverse:
                kv_b[blk] = _block_kv_t(v16[rs], kblk)
            else:
                kvf_ref[0, blk] = _block_kv_t(v16[rs], kblk)

    s = sb_scr[...]
    for blk in reversed(range(nblk)):
        sb_ref[0, blk] = s.T.astype(BF16)
        s = jnp.exp2(_sum_rows(cums[1, blk][1])) * s + kv_b[blk]
    sb_scr[...] = s


def _pre_call(x, mod, ng, w_pre, w2, gb, lg, lb, s0b, t):
    b, l, _ = x.shape
    nw = w_pre.shape[1]
    ncg = A_GROUPS - A_ROW_GROUPS
    nblk = t // GLA_BLOCK
    nj = l // t
    rev3 = lambda i, j: (i, nj - 1 - j, 0)
    rev4 = lambda i, j: (i, nj - 1 - j, 0, 0)

    def tok(width, dtype):
        return (pl.BlockSpec((1, t, width), rev3), jax.ShapeDtypeStruct((b, l, width), dtype))

    tot = (pl.BlockSpec((1, nblk, GLA_NC, B_KEY_WIDTH), rev4),
           jax.ShapeDtypeStruct((b, l // GLA_BLOCK, GLA_NC, B_KEY_WIDTH), F32))
    kv = (pl.BlockSpec((1, nblk, B_DV, B_KEY_WIDTH), rev4),
          jax.ShapeDtypeStruct((b, l // GLA_BLOCK, B_DV, B_KEY_WIDTH), F32))
    st = (pl.BlockSpec((1, nblk, B_KEY_WIDTH, B_DV), rev4),
          jax.ShapeDtypeStruct((b, l // GLA_BLOCK, B_KEY_WIDTH, B_DV), BF16))
    outs = [tok(B_KEY_WIDTH, BF16)] * 6 + [tok(B_VAL_WIDTH, BF16), tot, tot, kv, st,
                                           tok(A_ROW_GROUPS * A_GROUP_DIM, BF16),
                                           (pl.BlockSpec((1, ncg, GRID_W, t // GRID_W, A_GROUP_DIM),
                                                         lambda i, j: (i, 0, 0, nj - 1 - j, 0)),
                                            jax.ShapeDtypeStruct((b, ncg, GRID_W, l // GRID_W, A_GROUP_DIM), BF16))]
    const2 = lambda i, j: (0, 0)
    return pl.pallas_call(
        _pre_kernel,
        grid=(b, nj),
        in_specs=[pl.BlockSpec((1, t, D_MODEL), rev3),
                  pl.BlockSpec((8, 3 * D_MODEL), const2),
                  pl.BlockSpec((1, D_MODEL), const2),
                  pl.BlockSpec((D_MODEL, nw), const2),
                  pl.BlockSpec((2, B_GATE_RANK, B_KEY_WIDTH), lambda i, j: (0, 0, 0)),
                  pl.BlockSpec((2, B_KEY_WIDTH), const2),
                  pl.BlockSpec((1, A_WIDTH), const2),
                  pl.BlockSpec((1, A_WIDTH), const2),
                  pl.BlockSpec((1, B_DV, B_KEY_WIDTH), lambda i, j: (i, 0, 0))],
        out_specs=[o[0] for o in outs],
        out_shape=[o[1] for o in outs],
        scratch_shapes=[pltpu.VMEM((B_DV, B_KEY_WIDTH), F32)],
        compiler_params=pltpu.CompilerParams(
            dimension_semantics=("parallel", "arbitrary"), vmem_limit_bytes=VMEM_LIMIT_BYTES),
        name="pre",
    )(x, mod, ng, w_pre, w2, gb, lg, lb, s0b)


def _scale_rows(x16, scales):
    parts = []
    for c, s in enumerate(scales):
        xc = x16[c * B_CHUNK:(c + 1) * B_CHUNK, :]
        parts.append(xc if s is None else (xc.astype(F32) * s).astype(BF16))
    return parts[0] if len(parts) == 1 else jnp.concatenate(parts, axis=0)


def _exp_sum(tots, idx):
    return jnp.exp2(_sum_rows([tots[m] for m in idx])) if idx else None


def _gla_block(qdf, kdf, kcf, qdb, kdb, kcb, v16, tf, tb, sf, sb, p_scr):
    nc = GLA_NC
    ch = B_CHUNK
    lane = lax.broadcasted_iota(jnp.int32, (1, LANES), 1)
    hm = [(lane < B_DK).astype(BF16), (lane >= B_DK).astype(BF16)]

    def pair_heads(x):
        return jnp.concatenate([x * hm[0], x * hm[1]], axis=0)

    row = lax.broadcasted_iota(jnp.int32, (ch, 2 * ch), 0)
    col = lax.broadcasted_iota(jnp.int32, (ch, 2 * ch), 1)
    first = col < ch
    zeros = jnp.zeros((ch, LANES), BF16)
    for g in range(B_HEADS // 2):
        ls = slice(g * LANES, (g + 1) * LANES)
        for i in range(nc // 2):
            pr = slice(2 * i * ch, (2 * i + 2) * ch)
            re = slice(2 * i * ch, (2 * i + 1) * ch)
            ro = slice((2 * i + 1) * ch, (2 * i + 2) * ch)
            sc_f = _dot_nt(pair_heads(qdf[pr, ls]), jnp.concatenate([kdf[pr, ls], kcf[re, ls], zeros], axis=0))
            sc_b = _dot_nt(pair_heads(qdb[pr, ls]), jnp.concatenate([kdb[pr, ls], zeros, kcb[ro, ls]], axis=0))
            for hh in range(2):
                he = slice(hh * 2 * ch, hh * 2 * ch + ch)
                ho = slice(hh * 2 * ch + ch, (hh + 1) * 2 * ch)
                even = jnp.where(first,
                                 jnp.where(col <= row, sc_f[he, 0:2 * ch], 0.0)
                                 + jnp.where(col >= row, sc_b[he, 0:2 * ch], 0.0),
                                 sc_b[he, 2 * ch:])
                odd = jnp.where(first, sc_f[ho, 2 * ch:],
                                jnp.where(col - ch <= row, sc_f[ho, 0:2 * ch], 0.0)
                                + jnp.where(col - ch >= row, sc_b[ho, 0:2 * ch], 0.0))
                p_scr[2 * g + hh, re, pr] = even.astype(BF16)
                p_scr[2 * g + hh, ro, pr] = odd.astype(BF16)

    def cross(lo, hi):
        if hi - lo <= 2:
            return
        mid = (lo + hi) // 2
        cross(lo, mid)
        cross(mid, hi)
        rl = slice(lo * ch, mid * ch)
        rh = slice(mid * ch, hi * ch)
        n = (hi - mid) * ch
        qf = _scale_rows(qdf[rh], [_exp_sum(tf, range(mid, c)) for c in range(mid, hi)])
        kf = _scale_rows(kcf[rl], [_exp_sum(tf, range(c + 1, mid)) for c in range(lo, mid)])
        qb = _scale_rows(qdb[rl], [_exp_sum(tb, range(c + 1, mid)) for c in range(lo, mid)])
        kb = _scale_rows(kcb[rh], [_exp_sum(tb, range(mid, c)) for c in range(mid, hi)])
        for g in range(B_HEADS // 2):
            ls = slice(g * LANES, (g + 1) * LANES)
            sc_f = _dot_nt(pair_heads(qf[:, ls]), kf[:, ls]).astype(BF16)
            sc_b = _dot_nt(pair_heads(qb[:, ls]), kb[:, ls]).astype(BF16)
            for hh in range(2):
                p_scr[2 * g + hh, rh, rl] = sc_f[hh * n:(hh + 1) * n]
                p_scr[2 * g + hh, rl, rh] = sc_b[hh * n:(hh + 1) * n]

    cross(0, nc)

    qsf = _scale_rows(qdf, [_exp_sum(tf, range(0, c)) for c in range(nc)])
    qsb = _scale_rows(qdb, [_exp_sum(tb, range(c + 1, nc)) for c in range(nc)])
    head_row = lax.broadcasted_iota(jnp.int32, (LANES, 1), 0) // B_DK
    outs = []
    for h in range(B_HEADS):
        ls = slice((h // 2) * LANES, (h // 2 + 1) * LANES)
        own = (head_row == (h % 2)).astype(BF16)
        lhs = jnp.concatenate([p_scr[h], qsf[:, ls], qsb[:, ls]], axis=1)
        rhs = jnp.concatenate([v16[:, h * B_DV:(h + 1) * B_DV], sf[ls, :] * own, sb[ls, :] * own], axis=0)
        outs.append(jnp.dot(lhs, rhs, preferred_element_type=F32))
    return outs


def _colmix_kernel(ws_ref, bs_ref, vn_ref, o_ref):
    nw = vn_ref.shape[2]
    xs = jnp.concatenate([vn_ref[0, 0, w] for w in range(nw)], axis=1)
    y = _dot(ws_ref[0], xs) + bs_ref[0][:, 0:1]
    for w in range(nw):
        o_ref[0, 0, w] = y[:, w * A_GROUP_DIM:(w + 1) * A_GROUP_DIM]


def _colmix_call(ws, bs, vn_col, nw):
    b, g, width, rows, ch = vn_col.shape
    blk = pl.BlockSpec((1, 1, nw, rows, ch), lambda i, j, m: (i, j, m, 0, 0))
    return pl.pallas_call(
        _colmix_kernel,
        grid=(b, g, width // nw),
        in_specs=[pl.BlockSpec((1, rows, rows), lambda i, j, m: (j + A_ROW_GROUPS, 0, 0)),
                  pl.BlockSpec((1, rows, A_GROUP_DIM), lambda i, j, m: (j + A_ROW_GROUPS, 0, 0)),
                  blk],
        out_specs=blk,
        out_shape=jax.ShapeDtypeStruct(vn_col.shape, F32),
        compiler_params=pltpu.CompilerParams(
            dimension_semantics=("parallel", "parallel", "parallel"), vmem_limit_bytes=VMEM_LIMIT_BYTES),
        name="colmix",
    )(ws, bs, vn_col)


def _main_kernel(x_ref, mod_ref, ng_ref, w_ref, vr_ref, sc_ref,
                 qdf_ref, kdf_ref, kcf_ref, qdb_ref, kdb_ref, kcb_ref, v_ref, totf_ref, totb_ref,
                 kvf_ref, s0f_ref, sb_ref,
                 ws_ref, bs_ref, bg_ref, wpa_ref, wpb_ref, wo_ref, fg_ref, o_ref, acta_scr, on_scr, p_scr, sf_scr):
    t = x_ref.shape[1]
    blocks = [slice(blk * GLA_BLOCK, (blk + 1) * GLA_BLOCK) for blk in range(t // GLA_BLOCK)]

    @pl.when(pl.program_id(1) == 0)
    def _():
        sf_scr[...] = s0f_ref[0]

    s = sf_scr[...]
    sf_in = []
    for blk in range(len(blocks)):
        sf_in.append(s.T.astype(BF16))
        s = jnp.exp2(_sum_rows([totf_ref[0, blk, c:c + 1, :] for c in range(GLA_NC)])) * s + kvf_ref[0, blk]
    sf_scr[...] = s

    def gla(blk):
        rs = blocks[blk]
        tf = [totf_ref[0, blk, c:c + 1, :] for c in range(GLA_NC)]
        tb = [totb_ref[0, blk, c:c + 1, :] for c in range(GLA_NC)]
        o_heads = _gla_block(qdf_ref[0, rs, :], kdf_ref[0, rs, :], kcf_ref[0, rs, :],
                             qdb_ref[0, rs, :], kdb_ref[0, rs, :], kcb_ref[0, rs, :], v_ref[0, rs, :],
                             tf, tb, sf_in[blk], sb_ref[0, blk], p_scr.at[blk])
        for hd in range(B_HEADS):
            on_scr[rs, hd * B_DV:(hd + 1) * B_DV] = _rms_rows(o_heads[hd])

    def proj(c0, n):
        return jnp.dot(h16, w_ref[:, c0:c0 + n], preferred_element_type=F32)

    gla(0)
    h16 = _latent_h(x_ref[0], mod_ref, ng_ref).astype(BF16)
    zb = proj(0, B_VAL_WIDTH)
    u = proj(B_VAL_WIDTH, A_WIDTH)
    za = proj(B_VAL_WIDTH + A_WIDTH, A_WIDTH)
    for blk in range(1, len(blocks)):
        gla(blk)
    gate_a = jax.nn.sigmoid(proj(B_VAL_WIDTH + 2 * A_WIDTH, D_MODEL))

    uz = u * _silu(za)
    for g in range(A_GROUPS):
        cs = slice(g * A_GROUP_DIM, (g + 1) * A_GROUP_DIM)
        if g < A_ROW_GROUPS:
            chunks = [slice(c * A_CHUNK, (c + 1) * A_CHUNK) for c in range(t // A_CHUNK)]
            sv_all = jnp.dot(ws_ref[g].astype(BF16), jnp.concatenate([vr_ref[0, rs, cs] for rs in chunks], axis=1),
                             preferred_element_type=F32)
            for c, rs in enumerate(chunks):
                sv = sv_all[:, c * A_GROUP_DIM:(c + 1) * A_GROUP_DIM] + bs_ref[g][:, 0:1]
                acta_scr[rs, cs] = (uz[rs, cs] * sv).astype(BF16)
        else:
            sv = jnp.swapaxes(sc_ref[0, g - A_ROW_GROUPS], 0, 1).reshape(t, A_GROUP_DIM)
            acta_scr[:, cs] = (uz[:, cs] * sv).astype(BF16)
    ya = jnp.dot(acta_scr[...], wpa_ref[...], preferred_element_type=F32)

    gate_b = jax.nn.sigmoid(proj(B_VAL_WIDTH + 2 * A_WIDTH + D_MODEL, D_MODEL))
    yb = _dot(on_scr[...] * (_silu(zb) * bg_ref[...]), wpb_ref[...])

    m = (gate_a * ya + gate_b * yb).astype(BF16)
    for rs in blocks:
        y = jnp.dot(m[rs], wo_ref[...], preferred_element_type=F32)
        xo = x_ref[0, rs, :] + mod_ref[pl.ds(pl.program_id(0), 1), 2 * D_MODEL:] * y
        o_ref[0, rs, :] = _rms_rows(xo) * fg_ref[...]


def _main_call(x, mod, ng, w_main, vn_row, sv_col, gla_ops, ws, bs, bg, wpa, wpb, wo, fg, t):
    b, l, _ = x.shape
    nw = w_main.shape[1]
    ncg = A_GROUPS - A_ROW_GROUPS
    const2 = lambda i, j: (0, 0)
    const3 = lambda i, j: (0, 0, 0)
    tok = lambda width: pl.BlockSpec((1, t, width), lambda i, j: (i, j, 0))
    nblk = t // GLA_BLOCK
    tot_spec = pl.BlockSpec((1, nblk, GLA_NC, B_KEY_WIDTH), lambda i, j: (i, j, 0, 0))
    st_spec = pl.BlockSpec((1, nblk, B_KEY_WIDTH, B_DV), lambda i, j: (i, j, 0, 0))
    kv_spec = pl.BlockSpec((1, nblk, B_DV, B_KEY_WIDTH), lambda i, j: (i, j, 0, 0))
    s0_spec = pl.BlockSpec((1, B_DV, B_KEY_WIDTH), lambda i, j: (i, 0, 0))
    return pl.pallas_call(
        _main_kernel,
        grid=(b, l // t),
        in_specs=[tok(D_MODEL),
                  pl.BlockSpec((8, 3 * D_MODEL), const2),
                  pl.BlockSpec((1, D_MODEL), const2),
                  pl.BlockSpec((D_MODEL, nw), const2),
                  tok(A_ROW_GROUPS * A_GROUP_DIM),
                  pl.BlockSpec((1, ncg, GRID_W, t // GRID_W, A_GROUP_DIM), lambda i, j: (i, 0, 0, j, 0)),
                  tok(B_KEY_WIDTH), tok(B_KEY_WIDTH), tok(B_KEY_WIDTH),
                  tok(B_KEY_WIDTH), tok(B_KEY_WIDTH), tok(B_KEY_WIDTH), tok(B_VAL_WIDTH),
                  tot_spec, tot_spec, kv_spec, s0_spec, st_spec,
                  pl.BlockSpec((A_GROUPS, A_CHUNK, A_CHUNK), const3),
                  pl.BlockSpec((A_GROUPS, A_CHUNK, A_GROUP_DIM), const3),
                  pl.BlockSpec((1, B_VAL_WIDTH), const2),
                  pl.BlockSpec((A_WIDTH, D_MODEL), const2),
                  pl.BlockSpec((B_VAL_WIDTH, D_MODEL), const2),
                  pl.BlockSpec((D_MODEL, D_MODEL), const2),
                  pl.BlockSpec((1, D_MODEL), const2)],
        out_specs=tok(D_MODEL),
        out_shape=jax.ShapeDtypeStruct((b, l, D_MODEL), F32),
        scratch_shapes=[pltpu.VMEM((t, A_WIDTH), BF16), pltpu.VMEM((t, B_VAL_WIDTH), F32),
                        pltpu.VMEM((nblk, B_HEADS, GLA_BLOCK, GLA_BLOCK), BF16),
                        pltpu.VMEM((B_DV, B_KEY_WIDTH), F32)],
        compiler_params=pltpu.CompilerParams(
            dimension_semantics=("parallel", "arbitrary"), vmem_limit_bytes=VMEM_LIMIT_BYTES),
        name="main",
    )(x, mod, ng, w_main, vn_row, sv_col, *gla_ops, ws, bs, bg, wpa, wpb, wo, fg)


def kernel(x, c, ctx, c_ctx, w_mod, b_mod, norm_g, w_in, a_ln_g, a_ln_b, a_ws, a_bs, b_gate_w2, b_gate_b,
           b_norm_g, w_proj_a, w_proj_b, w_out, final_norm_g):
    assert w_mod.shape[0] == 1, "single-layer block"
    b, l, _ = x.shape
    rows = l // GRID_W
    assert rows == A_CHUNK
    ng = norm_g[0][None, :]

    w_pre, w_main, wpa16, wpb16, wo16, mod, bs_t = _wprep_call(
        w_in[0].T, w_proj_a, w_proj_b, w_out, c, c_ctx, w_mod, b_mod, a_bs)
    w2, gb = b_gate_w2[0], b_gate_b[0]
    s0f, s0b = _ctx_call(ctx, mod, ng, w_pre, w2, gb, b)

    (qd_f, kd_f, kc_f, qd_b, kd_b, kc_b, v16, tot_f, tot_b, kv_f, s_b, vn_row, vn_col) = _pre_call(
        x, mod, ng, w_pre, w2, gb, a_ln_g[0][None, :], a_ln_b[0][None, :], s0b, 1024)
    gla_ops = (qd_f, kd_f, kc_f, qd_b, kd_b, kc_b, v16, tot_f, tot_b, kv_f, s0f, s_b)

    sv_col = _colmix_call(a_ws[0], bs_t, vn_col, 32)

    return _main_call(x, mod, ng, w_main, vn_row, sv_col, gla_ops, a_ws[0], bs_t, b_norm_g[0][None, :],
                      wpa16, wpb16, wo16, final_norm_g[None, :], 1024)
```

```python
import functools

import jax
import jax.numpy as jnp
from jax import lax
from jax.experimental import pallas as pl
from jax.experimental.pallas import tpu as pltpu

D_MODEL = 1024
GRID_W = 64
EPS = 1e-6

A_WIDTH = 512
A_GROUPS = 4
A_GROUP_DIM = 128
A_CHUNK = 128
A_ROW_GROUPS = 2

B_HEADS = 4
B_DK = 64
B_DV = 128
B_KEY_WIDTH = 256
B_VAL_WIDTH = 512
B_GATE_RANK = 16
B_GATE_TAU = 16.0
LOG2E = 1.4426950408889634
B_CHUNK = 64

Q0 = 0
K0 = Q0 + B_KEY_WIDTH
V0 = K0 + B_KEY_WIDTH
LR0 = V0 + B_VAL_WIDTH
ZB0 = LR0 + 2 * B_GATE_RANK
UA0 = ZB0 + B_VAL_WIDTH
VA0 = UA0 + A_WIDTH
ZA0 = VA0 + A_WIDTH
G0 = ZA0 + A_WIDTH
IN_WIDTH = G0 + 2 * D_MODEL

LANES = 128

P_LR = 0
P_K = P_LR + LANES
P_Q = P_K + B_KEY_WIDTH
P_V = P_Q + B_KEY_WIDTH
P_VA = P_V + B_VAL_WIDTH
PRE_WIDTH = P_VA + A_WIDTH
GLA_BLOCK = 256
TOKEN_TILE = 1024
COLMIX_COLUMNS = 32
GLA_NC = GLA_BLOCK // B_CHUNK

VMEM_LIMIT_BYTES = 56 * 1024 * 1024

BF16 = jnp.bfloat16
F32 = jnp.float32


def _dot(a, b):
    return jnp.dot(a.astype(BF16), b.astype(BF16), preferred_element_type=F32)


def _dot_nt(a, b):
    return lax.dot_general(a.astype(BF16), b.astype(BF16), (((1,), (1,)), ((), ())),
                           preferred_element_type=F32)


def _dot_tn(a, b):
    return lax.dot_general(a.astype(BF16), b.astype(BF16), (((0,), (0,)), ((), ())),
                           preferred_element_type=F32)


def _silu(x):
    return x * jax.nn.sigmoid(x)


def _rms_rows(x):
    return x * lax.rsqrt(jnp.mean(x * x, axis=-1, keepdims=True) + EPS)


def _wprep_kernel(wt_ref, wpa_ref, wpb_ref, wo_ref, c_ref, cctx_ref, wm_ref, bm_ref, abs_ref,
                  wpre_ref, wmain_ref, wpa16_ref, wpb16_ref, wo16_ref, mod_ref, bst_ref):
    def put(dst_ref, c0, r0, n, scale=None):
        for s in range(0, n, 2 * LANES):
            m = min(2 * LANES, n - s)
            blk = wt_ref[r0 + s:r0 + s + m, :].T
            dst_ref[:, c0 + s:c0 + s + m] = (blk if scale is None else blk * scale).astype(BF16)

    put(wpre_ref, P_LR, LR0, LANES)
    put(wpre_ref, P_K, K0, B_KEY_WIDTH)
    put(wpre_ref, P_Q, Q0, B_KEY_WIDTH, B_DK ** -0.5)
    put(wpre_ref, P_V, V0, B_VAL_WIDTH)
    put(wpre_ref, P_VA, VA0, A_WIDTH)
    put(wmain_ref, 0, ZB0, VA0 - ZB0)
    put(wmain_ref, VA0 - ZB0, ZA0, IN_WIDTH - ZA0)
    wpa16_ref[...] = wpa_ref[0].astype(BF16)
    wpb16_ref[...] = wpb_ref[0].astype(BF16)
    wo16_ref[...] = wo_ref[0].astype(BF16)
    nb = c_ref.shape[0]
    cc = jnp.concatenate([c_ref[...], cctx_ref[...], jnp.zeros((8 - nb - 1, D_MODEL), F32)], axis=0)
    mod_ref[...] = _dot(_silu(cc), wm_ref[0]) + bm_ref[...]
    g = pl.program_id(0)
    bst_ref[0] = jnp.broadcast_to(abs_ref[0, pl.ds(g, 1), :], (A_GROUP_DIM, A_CHUNK)).T


def _wprep_call(w_in_t, w_proj_a, w_proj_b, w_out, c, c_ctx, w_mod, b_mod, a_bs):
    steps = A_GROUPS
    n_main = (VA0 - ZB0) + (IN_WIDTH - ZA0)

    def rows3(a):
        return pl.BlockSpec((1, a.shape[1] // steps, a.shape[2]), lambda i: (0, i, 0))

    def rows2(nrows, ncols):
        return (pl.BlockSpec((nrows // steps, ncols), lambda i: (i, 0)),
                jax.ShapeDtypeStruct((nrows, ncols), BF16))

    n_mod = w_mod.shape[2]
    outs = [rows2(D_MODEL, PRE_WIDTH), rows2(D_MODEL, n_main), rows2(A_WIDTH, D_MODEL),
            rows2(B_VAL_WIDTH, D_MODEL), rows2(D_MODEL, D_MODEL),
            (pl.BlockSpec((8, n_mod // steps), lambda i: (0, i)), jax.ShapeDtypeStruct((8, n_mod), F32)),
            (pl.BlockSpec((1, A_CHUNK, A_GROUP_DIM), lambda i: (i, 0, 0)),
             jax.ShapeDtypeStruct((A_GROUPS, A_CHUNK, A_GROUP_DIM), F32))]
    return pl.pallas_call(
        _wprep_kernel,
        grid=(steps,),
        in_specs=[pl.BlockSpec((IN_WIDTH, D_MODEL // steps), lambda i: (0, i)),
                  rows3(w_proj_a), rows3(w_proj_b), rows3(w_out),
                  pl.BlockSpec(c.shape, lambda i: (0, 0)),
                  pl.BlockSpec((1, D_MODEL), lambda i: (0, 0)),
                  pl.BlockSpec((1, D_MODEL, n_mod // steps), lambda i: (0, 0, i)),
                  pl.BlockSpec((1, n_mod // steps), lambda i: (0, i)),
                  pl.BlockSpec(a_bs.shape, lambda i: (0, 0, 0))],
        out_specs=[o[0] for o in outs],
        out_shape=[o[1] for o in outs],
        compiler_params=pltpu.CompilerParams(
            dimension_semantics=("parallel",), vmem_limit_bytes=VMEM_LIMIT_BYTES),
        name="wprep",
    )(w_in_t, w_proj_a, w_proj_b, w_out, c, c_ctx[None, :], w_mod, b_mod, a_bs)


def _gate_logs(lr, w2_ref, gb_ref):
    out = []
    for r in range(2):
        logits = _dot(lr[:, r * B_GATE_RANK:(r + 1) * B_GATE_RANK], w2_ref[r]) + gb_ref[r:r + 1, :]
        log_sig = jnp.minimum(logits, 0.0) - jnp.log(1.0 + jnp.exp(-jnp.abs(logits)))
        out.append(log_sig * (LOG2E / B_GATE_TAU))
    return out


def _chunk_tri(reverse):
    i = lax.broadcasted_iota(jnp.int32, (GLA_BLOCK, GLA_BLOCK), 0)
    j = lax.broadcasted_iota(jnp.int32, (GLA_BLOCK, GLA_BLOCK), 1)
    same = (i // B_CHUNK) == (j // B_CHUNK)
    tri = (j >= i) if reverse else (j <= i)
    return (same & tri).astype(BF16)


def _sum_rows(rows):
    acc = rows[0]
    for r in rows[1:]:
        acc = acc + r
    return acc


def _block_cum(a, tri, reverse):
    hi = a.astype(BF16)
    lo = (a - hi.astype(F32)).astype(BF16)
    cum = (jnp.dot(tri, hi, preferred_element_type=F32) + jnp.dot(tri, lo, preferred_element_type=F32))
    last = 0 if reverse else B_CHUNK - 1
    tots = [cum[c * B_CHUNK + last:c * B_CHUNK + last + 1, :] for c in range(GLA_NC)]
    return cum, tots


def _block_keys(k, cum, tots, reverse):
    totb = jnp.concatenate([jnp.broadcast_to(t, (B_CHUNK, B_KEY_WIDTH)) for t in tots], axis=0)
    kdec = k * jnp.exp2(totb - cum)
    later = []
    for c in range(GLA_NC):
        idx = list(range(0, c)) if reverse else list(range(c + 1, GLA_NC))
        if idx:
            later.append(jnp.broadcast_to(jnp.exp2(_sum_rows([tots[m] for m in idx])), (B_CHUNK, B_KEY_WIDTH)))
        else:
            later.append(jnp.ones((B_CHUNK, B_KEY_WIDTH), F32))
    return kdec, kdec * jnp.concatenate(later, axis=0)


def _pre_project(h16, w_ref, want_q):
    lr = jnp.dot(h16, w_ref[:, P_LR:P_LR + LANES], preferred_element_type=F32)[:, 0:2 * B_GATE_RANK]
    k = jnp.dot(h16, w_ref[:, P_K:P_K + B_KEY_WIDTH], preferred_element_type=F32)
    q = jnp.dot(h16, w_ref[:, P_Q:P_Q + B_KEY_WIDTH], preferred_element_type=F32) if want_q else None
    v = jnp.dot(h16, w_ref[:, P_V:P_V + B_VAL_WIDTH], preferred_element_type=F32)
    return lr, k, q, v


def _block_kv_t(v16, kblk):
    k16 = kblk.astype(BF16)
    lane = lax.broadcasted_iota(jnp.int32, (1, B_KEY_WIDTH), 1)
    acc = jnp.zeros((B_DV, B_KEY_WIDTH), F32)
    for h in range(B_HEADS):
        full = _dot_tn(v16[:, h * B_DV:(h + 1) * B_DV], k16)
        acc = acc + jnp.where((lane // B_DK) == h, full, 0.0)
    return acc


def _ctx_kernel(ctx_ref, mod_ref, ng_ref, w_ref, w2_ref, gb_ref, sf_ref, sb_ref, *, ctx_row):
    nb, lc, _ = ctx_ref.shape
    xc = ctx_ref[...].reshape(nb * lc, D_MODEL)
    shift = mod_ref[ctx_row:ctx_row + 1, 0:D_MODEL]
    scale = mod_ref[ctx_row:ctx_row + 1, D_MODEL:2 * D_MODEL]
    hc = _rms_rows(xc) * ng_ref[...] * (1.0 + scale) + shift
    lr, k, _, v = _pre_project(hc.astype(BF16), w_ref, False)
    v16 = v.astype(BF16)
    a_f, a_b = _gate_logs(lr, w2_ref, gb_ref)
    for bi in range(nb):
        rs = slice(bi * lc, (bi + 1) * lc)
        for a, reverse, out_ref in ((a_f, False, sf_ref), (a_b, True, sb_ref)):
            cum, tots = _block_cum(a[rs], _chunk_tri(reverse), reverse)
            _, kblk = _block_keys(k[rs], cum, tots, reverse)
            out_ref[bi] = _block_kv_t(v16[rs], kblk)


def _ctx_call(ctx, mod, ng, w_kvl, w2, gb, ctx_row):
    b, lc, _ = ctx.shape
    assert lc == GLA_BLOCK, "context length must be one GLA block"
    nw = w_kvl.shape[1]
    st = jax.ShapeDtypeStruct((b, B_DV, B_KEY_WIDTH), F32)
    st_spec = pl.BlockSpec((b, B_DV, B_KEY_WIDTH), lambda i: (0, 0, 0))
    return pl.pallas_call(
        functools.partial(_ctx_kernel, ctx_row=ctx_row),
        grid=(1,),
        in_specs=[pl.BlockSpec((b, lc, D_MODEL), lambda i: (0, 0, 0)),
                  pl.BlockSpec((8, 3 * D_MODEL), lambda i: (0, 0)),
                  pl.BlockSpec((1, D_MODEL), lambda i: (0, 0)),
                  pl.BlockSpec((D_MODEL, nw), lambda i: (0, 0)),
                  pl.BlockSpec((2, B_GATE_RANK, B_KEY_WIDTH), lambda i: (0, 0, 0)),
                  pl.BlockSpec((2, B_KEY_WIDTH), lambda i: (0, 0))],
        out_specs=[st_spec, st_spec],
        out_shape=[st, st],
        name="ctx",
    )(ctx, mod, ng, w_kvl, w2, gb)


def _latent_h(x, mod_ref, ng_ref):
    row = pl.ds(pl.program_id(0), 1)
    shift = mod_ref[row, 0:D_MODEL]
    gain = ng_ref[...] * (1.0 + mod_ref[row, D_MODEL:2 * D_MODEL])
    return _rms_rows(x) * gain + shift


def _pre_kernel(x_ref, mod_ref, ng_ref, w_ref, w2_ref, gb_ref, lg_ref, lb_ref, s0b_ref,
                qdf_ref, kdf_ref, kcf_ref, qdb_ref, kdb_ref, kcb_ref, v_ref,
                totf_ref, totb_ref, kvf_ref, sb_ref, vr_ref, vc_ref, sb_scr):
    @pl.when(pl.program_id(1) == 0)
    def _():
        sb_scr[...] = s0b_ref[0]

    h16 = _latent_h(x_ref[0], mod_ref, ng_ref).astype(BF16)
    nblk = x_ref.shape[1] // GLA_BLOCK
    blocks = [slice(blk * GLA_BLOCK, (blk + 1) * GLA_BLOCK) for blk in range(nblk)]
    dirs = ((False, qdf_ref, kdf_ref, kcf_ref, totf_ref), (True, qdb_ref, kdb_ref, kcb_ref, totb_ref))

    lrk = jnp.dot(h16, w_ref[:, P_LR:P_K + B_KEY_WIDTH], preferred_element_type=F32)
    lr = lrk[:, 0:2 * B_GATE_RANK]
    k = lrk[:, P_K:P_K + B_KEY_WIDTH]
    logs = _gate_logs(lr, w2_ref, gb_ref)
    q = jnp.dot(h16, w_ref[:, P_Q:P_Q + B_KEY_WIDTH], preferred_element_type=F32)
    v16 = jnp.dot(h16, w_ref[:, P_V:P_V + B_VAL_WIDTH], preferred_element_type=F32).astype(BF16)
    v_ref[0] = v16

    cums = {}
    for d, (reverse, _, _, _, tot_ref) in enumerate(dirs):
        tri = _chunk_tri(reverse)
        for blk, rs in enumerate(blocks):
            cum, tots = _block_cum(logs[d][rs], tri, reverse)
            cums[d, blk] = (cum, tots)
            for c in range(GLA_NC):
                tot_ref[0, blk, c:c + 1, :] = tots[c]

    va = jnp.dot(h16, w_ref[:, P_VA:P_VA + A_WIDTH], preferred_element_type=F32)
    vc = va - jnp.mean(va, axis=-1, keepdims=True)
    vn = vc * lax.rsqrt(jnp.mean(vc * vc, axis=-1, keepdims=True) + EPS) * lg_ref[...] + lb_ref[...]
    nr = A_ROW_GROUPS * A_GROUP_DIM
    vr_ref[0] = vn[:, 0:nr].astype(BF16)
    for g in range(A_GROUPS - A_ROW_GROUPS):
        vg = vn[:, nr + g * A_GROUP_DIM:nr + (g + 1) * A_GROUP_DIM]
        vc_ref[0, g] = jnp.swapaxes(vg.reshape(vg.shape[0] // GRID_W, GRID_W, A_GROUP_DIM), 0, 1).astype(BF16)

    kv_b = {}
    for d, (reverse, qd_ref, kd_ref, kc_ref, _) in enumerate(dirs):
        for blk, rs in enumerate(blocks):
            cum, tots = cums[d, blk]
            kdec, kblk = _block_keys(k[rs], cum, tots, reverse)
            kc_ref[0, rs, :] = kdec.astype(BF16)
            kd_ref[0, rs, :] = (k[rs] * jnp.exp2(-cum)).astype(BF16)
            qd_ref[0, rs, :] = (q[rs] * jnp.exp2(cum)).astype(BF16)
            if reverse:
                kv_b[blk] = _block_kv_t(v16[rs], kblk)
            else:
                kvf_ref[0, blk] = _block_kv_t(v16[rs], kblk)

    s = sb_scr[...]
    for blk in reversed(range(nblk)):
        sb_ref[0, blk] = s.T.astype(BF16)
        s = jnp.exp2(_sum_rows(cums[1, blk][1])) * s + kv_b[blk]
    sb_scr[...] = s


def _pre_call(x, mod, ng, w_pre, w2, gb, lg, lb, s0b, t):
    b, l, _ = x.shape
    nw = w_pre.shape[1]
    ncg = A_GROUPS - A_ROW_GROUPS
    nblk = t // GLA_BLOCK
    nj = l // t
    rev3 = lambda i, j: (i, nj - 1 - j, 0)
    rev4 = lambda i, j: (i, nj - 1 - j, 0, 0)

    def tok(width, dtype):
        return (pl.BlockSpec((1, t, width), rev3), jax.ShapeDtypeStruct((b, l, width), dtype))

    tot = (pl.BlockSpec((1, nblk, GLA_NC, B_KEY_WIDTH), rev4),
           jax.ShapeDtypeStruct((b, l // GLA_BLOCK, GLA_NC, B_KEY_WIDTH), F32))
    kv = (pl.BlockSpec((1, nblk, B_DV, B_KEY_WIDTH), rev4),
          jax.ShapeDtypeStruct((b, l // GLA_BLOCK, B_DV, B_KEY_WIDTH), F32))
    st = (pl.BlockSpec((1, nblk, B_KEY_WIDTH, B_DV), rev4),
          jax.ShapeDtypeStruct((b, l // GLA_BLOCK, B_KEY_WIDTH, B_DV), BF16))
    outs = [tok(B_KEY_WIDTH, BF16)] * 6 + [tok(B_VAL_WIDTH, BF16), tot, tot, kv, st,
                                           tok(A_ROW_GROUPS * A_GROUP_DIM, BF16),
                                           (pl.BlockSpec((1, ncg, GRID_W, t // GRID_W, A_GROUP_DIM),
                                                         lambda i, j: (i, 0, 0, nj - 1 - j, 0)),
                                            jax.ShapeDtypeStruct((b, ncg, GRID_W, l // GRID_W, A_GROUP_DIM), BF16))]
    const2 = lambda i, j: (0, 0)
    return pl.pallas_call(
        _pre_kernel,
        grid=(b, nj),
        in_specs=[pl.BlockSpec((1, t, D_MODEL), rev3),
                  pl.BlockSpec((8, 3 * D_MODEL), const2),
                  pl.BlockSpec((1, D_MODEL), const2),
                  pl.BlockSpec((D_MODEL, nw), const2),
                  pl.BlockSpec((2, B_GATE_RANK, B_KEY_WIDTH), lambda i, j: (0, 0, 0)),
                  pl.BlockSpec((2, B_KEY_WIDTH), const2),
                  pl.BlockSpec((1, A_WIDTH), const2),
                  pl.BlockSpec((1, A_WIDTH), const2),
                  pl.BlockSpec((1, B_DV, B_KEY_WIDTH), lambda i, j: (i, 0, 0))],
        out_specs=[o[0] for o in outs],
        out_shape=[o[1] for o in outs],
        scratch_shapes=[pltpu.VMEM((B_DV, B_KEY_WIDTH), F32)],
        compiler_params=pltpu.CompilerParams(
            dimension_semantics=("parallel", "arbitrary"), vmem_limit_bytes=VMEM_LIMIT_BYTES),
        name="pre",
    )(x, mod, ng, w_pre, w2, gb, lg, lb, s0b)


def _scale_rows(x16, scales):
    parts = []
    for c, s in enumerate(scales):
        xc = x16[c * B_CHUNK:(c + 1) * B_CHUNK, :]
        parts.append(xc if s is None else (xc.astype(F32) * s).astype(BF16))
    return parts[0] if len(parts) == 1 else jnp.concatenate(parts, axis=0)


def _exp_sum(tots, idx):
    return jnp.exp2(_sum_rows([tots[m] for m in idx])) if idx else None


def _gla_block(qdf, kdf, kcf, qdb, kdb, kcb, v16, tf, tb, sf, sb, p_scr):
    nc = GLA_NC
    ch = B_CHUNK
    lane = lax.broadcasted_iota(jnp.int32, (1, LANES), 1)
    hm = [(lane < B_DK).astype(BF16), (lane >= B_DK).astype(BF16)]

    def pair_heads(x):
        return jnp.concatenate([x * hm[0], x * hm[1]], axis=0)

    row = lax.broadcasted_iota(jnp.int32, (ch, 2 * ch), 0)
    col = lax.broadcasted_iota(jnp.int32, (ch, 2 * ch), 1)
    first = col < ch
    zeros = jnp.zeros((ch, LANES), BF16)
    for g in range(B_HEADS // 2):
        ls = slice(g * LANES, (g + 1) * LANES)
        for i in range(nc // 2):
            pr = slice(2 * i * ch, (2 * i + 2) * ch)
            re = slice(2 * i * ch, (2 * i + 1) * ch)
            ro = slice((2 * i + 1) * ch, (2 * i + 2) * ch)
            sc_f = _dot_nt(pair_heads(qdf[pr, ls]), jnp.concatenate([kdf[pr, ls], kcf[re, ls], zeros], axis=0))
            sc_b = _dot_nt(pair_heads(qdb[pr, ls]), jnp.concatenate([kdb[pr, ls], zeros, kcb[ro, ls]], axis=0))
            for hh in range(2):
                he = slice(hh * 2 * ch, hh * 2 * ch + ch)
                ho = slice(hh * 2 * ch + ch, (hh + 1) * 2 * ch)
                even = jnp.where(first,
                                 jnp.where(col <= row, sc_f[he, 0:2 * ch], 0.0)
                                 + jnp.where(col >= row, sc_b[he, 0:2 * ch], 0.0),
                                 sc_b[he, 2 * ch:])
                odd = jnp.where(first, sc_f[ho, 2 * ch:],
                                jnp.where(col - ch <= row, sc_f[ho, 0:2 * ch], 0.0)
                                + jnp.where(col - ch >= row, sc_b[ho, 0:2 * ch], 0.0))
                p_scr[2 * g + hh, re, pr] = even.astype(BF16)
                p_scr[2 * g + hh, ro, pr] = odd.astype(BF16)

    def cross(lo, hi):
        if hi - lo <= 2:
            return
        mid = (lo + hi) // 2
        cross(lo, mid)
        cross(mid, hi)
        rl = slice(lo * ch, mid * ch)
        rh = slice(mid * ch, hi * ch)
        n = (hi - mid) * ch
        qf = _scale_rows(qdf[rh], [_exp_sum(tf, range(mid, c)) for c in range(mid, hi)])
        kf = _scale_rows(kcf[rl], [_exp_sum(tf, range(c + 1, mid)) for c in range(lo, mid)])
        qb = _scale_rows(qdb[rl], [_exp_sum(tb, range(c + 1, mid)) for c in range(lo, mid)])
        kb = _scale_rows(kcb[rh], [_exp_sum(tb, range(mid, c)) for c in range(mid, hi)])
        for g in range(B_HEADS // 2):
            ls = slice(g * LANES, (g + 1) * LANES)
            sc_f = _dot_nt(pair_heads(qf[:, ls]), kf[:, ls]).astype(BF16)
            sc_b = _dot_nt(pair_heads(qb[:, ls]), kb[:, ls]).astype(BF16)
            for hh in range(2):
                p_scr[2 * g + hh, rh, rl] = sc_f[hh * n:(hh + 1) * n]
                p_scr[2 * g + hh, rl, rh] = sc_b[hh * n:(hh + 1) * n]

    cross(0, nc)

    qsf = _scale_rows(qdf, [_exp_sum(tf, range(0, c)) for c in range(nc)])
    qsb = _scale_rows(qdb, [_exp_sum(tb, range(c + 1, nc)) for c in range(nc)])
    head_row = lax.broadcasted_iota(jnp.int32, (LANES, 1), 0) // B_DK
    outs = []
    for h in range(B_HEADS):
        ls = slice((h // 2) * LANES, (h // 2 + 1) * LANES)
        own = (head_row == (h % 2)).astype(BF16)
        lhs = jnp.concatenate([p_scr[h], qsf[:, ls], qsb[:, ls]], axis=1)
        rhs = jnp.concatenate([v16[:, h * B_DV:(h + 1) * B_DV], sf[ls, :] * own, sb[ls, :] * own], axis=0)
        outs.append(jnp.dot(lhs, rhs, preferred_element_type=F32))
    return outs


def _colmix_kernel(ws_ref, bs_ref, vn_ref, o_ref):
    nw = vn_ref.shape[2]
    xs = jnp.concatenate([vn_ref[0, 0, w] for w in range(nw)], axis=1)
    y = _dot(ws_ref[0], xs) + bs_ref[0][:, 0:1]
    for w in range(nw):
        o_ref[0, 0, w] = y[:, w * A_GROUP_DIM:(w + 1) * A_GROUP_DIM].astype(o_ref.dtype)


def _colmix_call(ws, bs, vn_col, nw):
    b, g, width, rows, ch = vn_col.shape
    blk = pl.BlockSpec((1, 1, nw, rows, ch), lambda i, j, m: (i, j, m, 0, 0))
    return pl.pallas_call(
        _colmix_kernel,
        grid=(b, g, width // nw),
        in_specs=[pl.BlockSpec((1, rows, rows), lambda i, j, m: (j + A_ROW_GROUPS, 0, 0)),
                  pl.BlockSpec((1, rows, A_GROUP_DIM), lambda i, j, m: (j + A_ROW_GROUPS, 0, 0)),
                  blk],
        out_specs=blk,
        out_shape=jax.ShapeDtypeStruct(vn_col.shape, BF16),
        compiler_params=pltpu.CompilerParams(
            dimension_semantics=("parallel", "parallel", "parallel"), vmem_limit_bytes=VMEM_LIMIT_BYTES),
        name="colmix",
    )(ws, bs, vn_col)


def _main_kernel(x_ref, mod_ref, ng_ref, w_ref, vr_ref, sc_ref,
                 qdf_ref, kdf_ref, kcf_ref, qdb_ref, kdb_ref, kcb_ref, v_ref, totf_ref, totb_ref,
                 kvf_ref, s0f_ref, sb_ref,
                 ws_ref, bs_ref, bg_ref, wpa_ref, wpb_ref, wo_ref, fg_ref, o_ref, acta_scr, on_scr, p_scr, sf_scr):
    t = x_ref.shape[1]
    blocks = [slice(blk * GLA_BLOCK, (blk + 1) * GLA_BLOCK) for blk in range(t // GLA_BLOCK)]

    @pl.when(pl.program_id(1) == 0)
    def _():
        sf_scr[...] = s0f_ref[0]

    s = sf_scr[...]
    sf_in = []
    for blk in range(len(blocks)):
        sf_in.append(s.T.astype(BF16))
        s = jnp.exp2(_sum_rows([totf_ref[0, blk, c:c + 1, :] for c in range(GLA_NC)])) * s + kvf_ref[0, blk]
    sf_scr[...] = s

    def gla(blk):
        rs = blocks[blk]
        tf = [totf_ref[0, blk, c:c + 1, :] for c in range(GLA_NC)]
        tb = [totb_ref[0, blk, c:c + 1, :] for c in range(GLA_NC)]
        o_heads = _gla_block(qdf_ref[0, rs, :], kdf_ref[0, rs, :], kcf_ref[0, rs, :],
                             qdb_ref[0, rs, :], kdb_ref[0, rs, :], kcb_ref[0, rs, :], v_ref[0, rs, :],
                             tf, tb, sf_in[blk], sb_ref[0, blk], p_scr.at[blk])
        for hd in range(B_HEADS):
            on_scr[rs, hd * B_DV:(hd + 1) * B_DV] = _rms_rows(o_heads[hd])

    def proj(c0, n):
        return jnp.dot(h16, w_ref[:, c0:c0 + n], preferred_element_type=F32)

    gla(0)
    h16 = _latent_h(x_ref[0], mod_ref, ng_ref).astype(BF16)
    zb = proj(0, B_VAL_WIDTH)
    u = proj(B_VAL_WIDTH, A_WIDTH)
    za = proj(B_VAL_WIDTH + A_WIDTH, A_WIDTH)
    for blk in range(1, len(blocks)):
        gla(blk)
    gate_a = jax.nn.sigmoid(proj(B_VAL_WIDTH + 2 * A_WIDTH, D_MODEL))

    uz = u * _silu(za)
    for g in range(A_GROUPS):
        cs = slice(g * A_GROUP_DIM, (g + 1) * A_GROUP_DIM)
        if g < A_ROW_GROUPS:
            chunks = [slice(c * A_CHUNK, (c + 1) * A_CHUNK) for c in range(t // A_CHUNK)]
            sv_all = jnp.dot(ws_ref[g].astype(BF16), jnp.concatenate([vr_ref[0, rs, cs] for rs in chunks], axis=1),
                             preferred_element_type=F32)
            for c, rs in enumerate(chunks):
                sv = sv_all[:, c * A_GROUP_DIM:(c + 1) * A_GROUP_DIM] + bs_ref[g][:, 0:1]
                acta_scr[rs, cs] = (uz[rs, cs] * sv).astype(BF16)
        else:
            sv = jnp.swapaxes(sc_ref[0, g - A_ROW_GROUPS].astype(F32), 0, 1).reshape(t, A_GROUP_DIM)
            acta_scr[:, cs] = (uz[:, cs] * sv).astype(BF16)
    ya = jnp.dot(acta_scr[...], wpa_ref[...], preferred_element_type=F32)

    gate_b = jax.nn.sigmoid(proj(B_VAL_WIDTH + 2 * A_WIDTH + D_MODEL, D_MODEL))
    yb = _dot(on_scr[...] * (_silu(zb) * bg_ref[...]), wpb_ref[...])

    m = (gate_a * ya + gate_b * yb).astype(BF16)
    for rs in blocks:
        y = jnp.dot(m[rs], wo_ref[...], preferred_element_type=F32)
        xo = x_ref[0, rs, :] + mod_ref[pl.ds(pl.program_id(0), 1), 2 * D_MODEL:] * y
        o_ref[0, rs, :] = _rms_rows(xo) * fg_ref[...]


def _main_call(x, mod, ng, w_main, vn_row, sv_col, gla_ops, ws, bs, bg, wpa, wpb, wo, fg, t):
    b, l, _ = x.shape
    nw = w_main.shape[1]
    ncg = A_GROUPS - A_ROW_GROUPS
    const2 = lambda i, j: (0, 0)
    const3 = lambda i, j: (0, 0, 0)
    tok = lambda width: pl.BlockSpec((1, t, width), lambda i, j: (i, j, 0))
    nblk = t // GLA_BLOCK
    tot_spec = pl.BlockSpec((1, nblk, GLA_NC, B_KEY_WIDTH), lambda i, j: (i, j, 0, 0))
    st_spec = pl.BlockSpec((1, nblk, B_KEY_WIDTH, B_DV), lambda i, j: (i, j, 0, 0))
    kv_spec = pl.BlockSpec((1, nblk, B_DV, B_KEY_WIDTH), lambda i, j: (i, j, 0, 0))
    s0_spec = pl.BlockSpec((1, B_DV, B_KEY_WIDTH), lambda i, j: (i, 0, 0))
    return pl.pallas_call(
        _main_kernel,
        grid=(b, l // t),
        in_specs=[tok(D_MODEL),
                  pl.BlockSpec((8, 3 * D_MODEL), const2),
                  pl.BlockSpec((1, D_MODEL), const2),
                  pl.BlockSpec((D_MODEL, nw), const2),
                  tok(A_ROW_GROUPS * A_GROUP_DIM),
                  pl.BlockSpec((1, ncg, GRID_W, t // GRID_W, A_GROUP_DIM), lambda i, j: (i, 0, 0, j, 0)),
                  tok(B_KEY_WIDTH), tok(B_KEY_WIDTH), tok(B_KEY_WIDTH),
                  tok(B_KEY_WIDTH), tok(B_KEY_WIDTH), tok(B_KEY_WIDTH), tok(B_VAL_WIDTH),
                  tot_spec, tot_spec, kv_spec, s0_spec, st_spec,
                  pl.BlockSpec((A_GROUPS, A_CHUNK, A_CHUNK), const3),
                  pl.BlockSpec((A_GROUPS, A_CHUNK, A_GROUP_DIM), const3),
                  pl.BlockSpec((1, B_VAL_WIDTH), const2),
                  pl.BlockSpec((A_WIDTH, D_MODEL), const2),
                  pl.BlockSpec((B_VAL_WIDTH, D_MODEL), const2),
                  pl.BlockSpec((D_MODEL, D_MODEL), const2),
                  pl.BlockSpec((1, D_MODEL), const2)],
        out_specs=tok(D_MODEL),
        out_shape=jax.ShapeDtypeStruct((b, l, D_MODEL), F32),
        scratch_shapes=[pltpu.VMEM((t, A_WIDTH), BF16), pltpu.VMEM((t, B_VAL_WIDTH), F32),
                        pltpu.VMEM((nblk, B_HEADS, GLA_BLOCK, GLA_BLOCK), BF16),
                        pltpu.VMEM((B_DV, B_KEY_WIDTH), F32)],
        compiler_params=pltpu.CompilerParams(
            dimension_semantics=("parallel", "arbitrary"), vmem_limit_bytes=VMEM_LIMIT_BYTES),
        name="main",
    )(x, mod, ng, w_main, vn_row, sv_col, *gla_ops, ws, bs, bg, wpa, wpb, wo, fg)


def kernel(x, c, ctx, c_ctx, w_mod, b_mod, norm_g, w_in, a_ln_g, a_ln_b, a_ws, a_bs, b_gate_w2, b_gate_b,
           b_norm_g, w_proj_a, w_proj_b, w_out, final_norm_g):
    assert w_mod.shape[0] == 1, "single-layer block"
    b, l, _ = x.shape
    rows = l // GRID_W
    assert rows == A_CHUNK
    ng = norm_g[0][None, :]

    w_pre, w_main, wpa16, wpb16, wo16, mod, bs_t = _wprep_call(
        w_in[0].T, w_proj_a, w_proj_b, w_out, c, c_ctx, w_mod, b_mod, a_bs)
    w2, gb = b_gate_w2[0], b_gate_b[0]
    s0f, s0b = _ctx_call(ctx, mod, ng, w_pre, w2, gb, b)

    (qd_f, kd_f, kc_f, qd_b, kd_b, kc_b, v16, tot_f, tot_b, kv_f, s_b, vn_row, vn_col) = _pre_call(
        x, mod, ng, w_pre, w2, gb, a_ln_g[0][None, :], a_ln_b[0][None, :], s0b, TOKEN_TILE)
    gla_ops = (qd_f, kd_f, kc_f, qd_b, kd_b, kc_b, v16, tot_f, tot_b, kv_f, s0f, s_b)

    sv_col = _colmix_call(a_ws[0], bs_t, vn_col, COLMIX_COLUMNS)

    return _main_call(x, mod, ng, w_main, vn_row, sv_col, gla_ops, a_ws[0], bs_t, b_norm_g[0][None, :],
                      wpa16, wpb16, wo16, final_norm_g[None, :], TOKEN_TILE)
```

```python
import functools

import jax
import jax.numpy as jnp
from jax import lax
from jax.experimental import pallas as pl
from jax.experimental.pallas import tpu as pltpu

D_MODEL = 1024
GRID_W = 64
EPS = 1e-6

A_WIDTH = 512
A_GROUPS = 4
A_GROUP_DIM = 128
A_CHUNK = 128
A_ROW_GROUPS = 2

B_HEADS = 4
B_DK = 64
B_DV = 128
B_KEY_WIDTH = 256
B_VAL_WIDTH = 512
B_GATE_RANK = 16
B_GATE_TAU = 16.0
LOG2E = 1.4426950408889634
B_CHUNK = 64

Q0 = 0
K0 = Q0 + B_KEY_WIDTH
V0 = K0 + B_KEY_WIDTH
LR0 = V0 + B_VAL_WIDTH
ZB0 = LR0 + 2 * B_GATE_RANK
UA0 = ZB0 + B_VAL_WIDTH
VA0 = UA0 + A_WIDTH
ZA0 = VA0 + A_WIDTH
G0 = ZA0 + A_WIDTH
IN_WIDTH = G0 + 2 * D_MODEL

LANES = 128

P_LR = 0
P_K = P_LR + LANES
P_Q = P_K + B_KEY_WIDTH
P_V = P_Q + B_KEY_WIDTH
P_VA = P_V + B_VAL_WIDTH
PRE_WIDTH = P_VA + A_WIDTH
GLA_BLOCK = 256
TOKEN_TILE = 1024
COLMIX_COLUMNS = 32
GLA_NC = GLA_BLOCK // B_CHUNK

VMEM_LIMIT_BYTES = 56 * 1024 * 1024

BF16 = jnp.bfloat16
F32 = jnp.float32


def _dot(a, b):
    return jnp.dot(a.astype(BF16), b.astype(BF16), preferred_element_type=F32)


def _dot_nt(a, b):
    return lax.dot_general(a.astype(BF16), b.astype(BF16), (((1,), (1,)), ((), ())),
                           preferred_element_type=F32)


def _dot_tn(a, b):
    return lax.dot_general(a.astype(BF16), b.astype(BF16), (((0,), (0,)), ((), ())),
                           preferred_element_type=F32)


def _silu(x):
    return x * jax.nn.sigmoid(x)


def _rms_rows(x):
    return x * lax.rsqrt(jnp.mean(x * x, axis=-1, keepdims=True) + EPS)


def _put_transposed(dst_ref, c0, wt_ref, r0, n, scale=None):
    for s in range(0, n, 2 * LANES):
        m = min(2 * LANES, n - s)
        blk = wt_ref[r0 + s:r0 + s + m, :].T
        dst_ref[:, c0 + s:c0 + s + m] = (blk if scale is None else blk * scale).astype(BF16)


def _wprep_kernel(wt_ref, c_ref, cctx_ref, wm_ref, bm_ref, abs_ref, wpre_ref, mod_ref, bst_ref):
    _put_transposed(wpre_ref, P_LR, wt_ref, LR0, LANES)
    _put_transposed(wpre_ref, P_K, wt_ref, K0, B_KEY_WIDTH)
    _put_transposed(wpre_ref, P_Q, wt_ref, Q0, B_KEY_WIDTH, B_DK ** -0.5)
    _put_transposed(wpre_ref, P_V, wt_ref, V0, B_VAL_WIDTH)
    _put_transposed(wpre_ref, P_VA, wt_ref, VA0, A_WIDTH)
    nb = c_ref.shape[0]
    cc = jnp.concatenate([c_ref[...], cctx_ref[...], jnp.zeros((8 - nb - 1, D_MODEL), F32)], axis=0)
    mod_ref[...] = _dot(_silu(cc), wm_ref[0]) + bm_ref[...]
    g = pl.program_id(0)
    bst_ref[0] = jnp.broadcast_to(abs_ref[0, pl.ds(g, 1), :], (A_GROUP_DIM, A_CHUNK)).T


def _wprep_call(w_in_t, c, c_ctx, w_mod, b_mod, a_bs):
    steps = A_GROUPS
    n_mod = w_mod.shape[2]
    outs = [(pl.BlockSpec((D_MODEL // steps, PRE_WIDTH), lambda i: (i, 0)),
             jax.ShapeDtypeStruct((D_MODEL, PRE_WIDTH), BF16)),
            (pl.BlockSpec((8, n_mod // steps), lambda i: (0, i)), jax.ShapeDtypeStruct((8, n_mod), F32)),
            (pl.BlockSpec((1, A_CHUNK, A_GROUP_DIM), lambda i: (i, 0, 0)),
             jax.ShapeDtypeStruct((A_GROUPS, A_CHUNK, A_GROUP_DIM), F32))]
    return pl.pallas_call(
        _wprep_kernel,
        grid=(steps,),
        in_specs=[pl.BlockSpec((VA0 + A_WIDTH, D_MODEL // steps), lambda i: (0, i)),
                  pl.BlockSpec(c.shape, lambda i: (0, 0)),
                  pl.BlockSpec((1, D_MODEL), lambda i: (0, 0)),
                  pl.BlockSpec((1, D_MODEL, n_mod // steps), lambda i: (0, 0, i)),
                  pl.BlockSpec((1, n_mod // steps), lambda i: (0, i)),
                  pl.BlockSpec(a_bs.shape, lambda i: (0, 0, 0))],
        out_specs=[o[0] for o in outs],
        out_shape=[o[1] for o in outs],
        compiler_params=pltpu.CompilerParams(
            dimension_semantics=("parallel",), vmem_limit_bytes=VMEM_LIMIT_BYTES),
        name="wprep",
    )(w_in_t, c, c_ctx[None, :], w_mod, b_mod, a_bs)


def _gate_logs(lr, w2_ref, gb_ref):
    out = []
    for r in range(2):
        logits = _dot(lr[:, r * B_GATE_RANK:(r + 1) * B_GATE_RANK], w2_ref[r]) + gb_ref[r:r + 1, :]
        log_sig = jnp.minimum(logits, 0.0) - jnp.log(1.0 + jnp.exp(-jnp.abs(logits)))
        out.append(log_sig * (LOG2E / B_GATE_TAU))
    return out


def _chunk_tri(reverse):
    i = lax.broadcasted_iota(jnp.int32, (GLA_BLOCK, GLA_BLOCK), 0)
    j = lax.broadcasted_iota(jnp.int32, (GLA_BLOCK, GLA_BLOCK), 1)
    same = (i // B_CHUNK) == (j // B_CHUNK)
    tri = (j >= i) if reverse else (j <= i)
    return (same & tri).astype(BF16)


def _sum_rows(rows):
    acc = rows[0]
    for r in rows[1:]:
        acc = acc + r
    return acc


def _block_cum(a, tri, reverse):
    hi = a.astype(BF16)
    lo = (a - hi.astype(F32)).astype(BF16)
    cum = (jnp.dot(tri, hi, preferred_element_type=F32) + jnp.dot(tri, lo, preferred_element_type=F32))
    last = 0 if reverse else B_CHUNK - 1
    tots = [cum[c * B_CHUNK + last:c * B_CHUNK + last + 1, :] for c in range(GLA_NC)]
    return cum, tots


def _block_keys(k, cum, tots, reverse):
    totb = jnp.concatenate([jnp.broadcast_to(t, (B_CHUNK, B_KEY_WIDTH)) for t in tots], axis=0)
    kdec = k * jnp.exp2(totb - cum)
    later = []
    for c in range(GLA_NC):
        idx = list(range(0, c)) if reverse else list(range(c + 1, GLA_NC))
        if idx:
            later.append(jnp.broadcast_to(jnp.exp2(_sum_rows([tots[m] for m in idx])), (B_CHUNK, B_KEY_WIDTH)))
        else:
            later.append(jnp.ones((B_CHUNK, B_KEY_WIDTH), F32))
    return kdec, kdec * jnp.concatenate(later, axis=0)


def _pre_project(h16, w_ref, want_q):
    lr = jnp.dot(h16, w_ref[:, P_LR:P_LR + LANES], preferred_element_type=F32)[:, 0:2 * B_GATE_RANK]
    k = jnp.dot(h16, w_ref[:, P_K:P_K + B_KEY_WIDTH], preferred_element_type=F32)
    q = jnp.dot(h16, w_ref[:, P_Q:P_Q + B_KEY_WIDTH], preferred_element_type=F32) if want_q else None
    v = jnp.dot(h16, w_ref[:, P_V:P_V + B_VAL_WIDTH], preferred_element_type=F32)
    return lr, k, q, v


def _block_kv_t(v16, kblk):
    k16 = kblk.astype(BF16)
    lane = lax.broadcasted_iota(jnp.int32, (1, B_KEY_WIDTH), 1)
    acc = jnp.zeros((B_DV, B_KEY_WIDTH), F32)
    for h in range(B_HEADS):
        full = _dot_tn(v16[:, h * B_DV:(h + 1) * B_DV], k16)
        acc = acc + jnp.where((lane // B_DK) == h, full, 0.0)
    return acc


def _ctx_kernel(ctx_ref, mod_ref, ng_ref, w_ref, w2_ref, gb_ref, sf_ref, sb_ref, *, ctx_row):
    nb, lc, _ = ctx_ref.shape
    xc = ctx_ref[...].reshape(nb * lc, D_MODEL)
    shift = mod_ref[ctx_row:ctx_row + 1, 0:D_MODEL]
    scale = mod_ref[ctx_row:ctx_row + 1, D_MODEL:2 * D_MODEL]
    hc = _rms_rows(xc) * ng_ref[...] * (1.0 + scale) + shift
    lr, k, _, v = _pre_project(hc.astype(BF16), w_ref, False)
    v16 = v.astype(BF16)
    a_f, a_b = _gate_logs(lr, w2_ref, gb_ref)
    for bi in range(nb):
        rs = slice(bi * lc, (bi + 1) * lc)
        for a, reverse, out_ref in ((a_f, False, sf_ref), (a_b, True, sb_ref)):
            cum, tots = _block_cum(a[rs], _chunk_tri(reverse), reverse)
            _, kblk = _block_keys(k[rs], cum, tots, reverse)
            out_ref[bi] = _block_kv_t(v16[rs], kblk)


def _ctx_call(ctx, mod, ng, w_kvl, w2, gb, ctx_row):
    b, lc, _ = ctx.shape
    assert lc == GLA_BLOCK, "context length must be one GLA block"
    nw = w_kvl.shape[1]
    st = jax.ShapeDtypeStruct((b, B_DV, B_KEY_WIDTH), F32)
    st_spec = pl.BlockSpec((b, B_DV, B_KEY_WIDTH), lambda i: (0, 0, 0))
    return pl.pallas_call(
        functools.partial(_ctx_kernel, ctx_row=ctx_row),
        grid=(1,),
        in_specs=[pl.BlockSpec((b, lc, D_MODEL), lambda i: (0, 0, 0)),
                  pl.BlockSpec((8, 3 * D_MODEL), lambda i: (0, 0)),
                  pl.BlockSpec((1, D_MODEL), lambda i: (0, 0)),
                  pl.BlockSpec((D_MODEL, nw), lambda i: (0, 0)),
                  pl.BlockSpec((2, B_GATE_RANK, B_KEY_WIDTH), lambda i: (0, 0, 0)),
                  pl.BlockSpec((2, B_KEY_WIDTH), lambda i: (0, 0))],
        out_specs=[st_spec, st_spec],
        out_shape=[st, st],
        name="ctx",
    )(ctx, mod, ng, w_kvl, w2, gb)


def _latent_h(x, mod_ref, ng_ref):
    row = pl.ds(pl.program_id(0), 1)
    shift = mod_ref[row, 0:D_MODEL]
    gain = ng_ref[...] * (1.0 + mod_ref[row, D_MODEL:2 * D_MODEL])
    return _rms_rows(x) * gain + shift


def _pre_kernel(x_ref, mod_ref, ng_ref, w_ref, w2_ref, gb_ref, lg_ref, lb_ref, s0b_ref,
                wt_ref, wpa_ref, wpb_ref, wo_ref,
                qdf_ref, kdf_ref, kcf_ref, qdb_ref, kdb_ref, kcb_ref, v_ref,
                totf_ref, totb_ref, kvf_ref, sb_ref, vr_ref, vc_ref,
                wmain_ref, wpa16_ref, wpb16_ref, wo16_ref, sb_scr):
    @pl.when(pl.program_id(0) == 0)
    def _():
        _put_transposed(wmain_ref, 0, wt_ref, ZB0, VA0 - ZB0)
        _put_transposed(wmain_ref, VA0 - ZB0, wt_ref, ZA0, IN_WIDTH - ZA0)
        wpa16_ref[...] = wpa_ref[0].astype(BF16)
        wpb16_ref[...] = wpb_ref[0].astype(BF16)
        wo16_ref[...] = wo_ref[0].astype(BF16)

    @pl.when(pl.program_id(1) == 0)
    def _():
        sb_scr[...] = s0b_ref[0]

    h16 = _latent_h(x_ref[0], mod_ref, ng_ref).astype(BF16)
    nblk = x_ref.shape[1] // GLA_BLOCK
    blocks = [slice(blk * GLA_BLOCK, (blk + 1) * GLA_BLOCK) for blk in range(nblk)]
    dirs = ((False, qdf_ref, kdf_ref, kcf_ref, totf_ref), (True, qdb_ref, kdb_ref, kcb_ref, totb_ref))

    lrk = jnp.dot(h16, w_ref[:, P_LR:P_K + B_KEY_WIDTH], preferred_element_type=F32)
    lr = lrk[:, 0:2 * B_GATE_RANK]
    k = lrk[:, P_K:P_K + B_KEY_WIDTH]
    logs = _gate_logs(lr, w2_ref, gb_ref)
    q = jnp.dot(h16, w_ref[:, P_Q:P_Q + B_KEY_WIDTH], preferred_element_type=F32)
    v16 = jnp.dot(h16, w_ref[:, P_V:P_V + B_VAL_WIDTH], preferred_element_type=F32).astype(BF16)
    v_ref[0] = v16

    cums = {}
    for d, (reverse, _, _, _, tot_ref) in enumerate(dirs):
        tri = _chunk_tri(reverse)
        for blk, rs in enumerate(blocks):
            cum, tots = _block_cum(logs[d][rs], tri, reverse)
            cums[d, blk] = (cum, tots)
            for c in range(GLA_NC):
                tot_ref[0, blk, c:c + 1, :] = tots[c]

    va = jnp.dot(h16, w_ref[:, P_VA:P_VA + A_WIDTH], preferred_element_type=F32)
    vc = va - jnp.mean(va, axis=-1, keepdims=True)
    vn = vc * lax.rsqrt(jnp.mean(vc * vc, axis=-1, keepdims=True) + EPS) * lg_ref[...] + lb_ref[...]
    nr = A_ROW_GROUPS * A_GROUP_DIM
    vr_ref[0] = vn[:, 0:nr].astype(BF16)
    for g in range(A_GROUPS - A_ROW_GROUPS):
        vg = vn[:, nr + g * A_GROUP_DIM:nr + (g + 1) * A_GROUP_DIM]
        vc_ref[0, g] = jnp.swapaxes(vg.reshape(vg.shape[0] // GRID_W, GRID_W, A_GROUP_DIM), 0, 1).astype(BF16)

    kv_b = {}
    for d, (reverse, qd_ref, kd_ref, kc_ref, _) in enumerate(dirs):
        for blk, rs in enumerate(blocks):
            cum, tots = cums[d, blk]
            kdec, kblk = _block_keys(k[rs], cum, tots, reverse)
            kc_ref[0, rs, :] = kdec.astype(BF16)
            kd_ref[0, rs, :] = (k[rs] * jnp.exp2(-cum)).astype(BF16)
            qd_ref[0, rs, :] = (q[rs] * jnp.exp2(cum)).astype(BF16)
            if reverse:
                kv_b[blk] = _block_kv_t(v16[rs], kblk)
            else:
                kvf_ref[0, blk] = _block_kv_t(v16[rs], kblk)

    s = sb_scr[...]
    for blk in reversed(range(nblk)):
        sb_ref[0, blk] = s.T.astype(BF16)
        s = jnp.exp2(_sum_rows(cums[1, blk][1])) * s + kv_b[blk]
    sb_scr[...] = s


def _pre_call(x, mod, ng, w_pre, w2, gb, lg, lb, s0b, w_in_t, w_proj_a, w_proj_b, w_out, t):
    b, l, _ = x.shape
    nw = w_pre.shape[1]
    ncg = A_GROUPS - A_ROW_GROUPS
    nblk = t // GLA_BLOCK
    nj = l // t
    n_main = (VA0 - ZB0) + (IN_WIDTH - ZA0)
    assert D_MODEL % nj == 0 and (D_MODEL // nj) % LANES == 0, "one weight strip per sequence step"
    strip = lambda i, j: jnp.where(i == 0, j, nj - 1)

    def wrows(a):
        return pl.BlockSpec((1, a.shape[1] // nj, a.shape[2]), lambda i, j: (0, strip(i, j), 0))

    def wout(nrows, ncols):
        return (pl.BlockSpec((nrows // nj, ncols), lambda i, j: (strip(i, j), 0)),
                jax.ShapeDtypeStruct((nrows, ncols), BF16))
    rev3 = lambda i, j: (i, nj - 1 - j, 0)
    rev4 = lambda i, j: (i, nj - 1 - j, 0, 0)

    def tok(width, dtype):
        return (pl.BlockSpec((1, t, width), rev3), jax.ShapeDtypeStruct((b, l, width), dtype))

    tot = (pl.BlockSpec((1, nblk, GLA_NC, B_KEY_WIDTH), rev4),
           jax.ShapeDtypeStruct((b, l // GLA_BLOCK, GLA_NC, B_KEY_WIDTH), F32))
    kv = (pl.BlockSpec((1, nblk, B_DV, B_KEY_WIDTH), rev4),
          jax.ShapeDtypeStruct((b, l // GLA_BLOCK, B_DV, B_KEY_WIDTH), F32))
    st = (pl.BlockSpec((1, nblk, B_KEY_WIDTH, B_DV), rev4),
          jax.ShapeDtypeStruct((b, l // GLA_BLOCK, B_KEY_WIDTH, B_DV), BF16))
    outs = [tok(B_KEY_WIDTH, BF16)] * 6 + [tok(B_VAL_WIDTH, BF16), tot, tot, kv, st,
                                           tok(A_ROW_GROUPS * A_GROUP_DIM, BF16),
                                           (pl.BlockSpec((1, ncg, GRID_W, t // GRID_W, A_GROUP_DIM),
                                                         lambda i, j: (i, 0, 0, nj - 1 - j, 0)),
                                            jax.ShapeDtypeStruct((b, ncg, GRID_W, l // GRID_W, A_GROUP_DIM), BF16)),
                                           wout(D_MODEL, n_main), wout(A_WIDTH, D_MODEL),
                                           wout(B_VAL_WIDTH, D_MODEL), wout(D_MODEL, D_MODEL)]
    const2 = lambda i, j: (0, 0)
    return pl.pallas_call(
        _pre_kernel,
        grid=(b, nj),
        in_specs=[pl.BlockSpec((1, t, D_MODEL), rev3),
                  pl.BlockSpec((8, 3 * D_MODEL), const2),
                  pl.BlockSpec((1, D_MODEL), const2),
                  pl.BlockSpec((D_MODEL, nw), const2),
                  pl.BlockSpec((2, B_GATE_RANK, B_KEY_WIDTH), lambda i, j: (0, 0, 0)),
                  pl.BlockSpec((2, B_KEY_WIDTH), const2),
                  pl.BlockSpec((1, A_WIDTH), const2),
                  pl.BlockSpec((1, A_WIDTH), const2),
                  pl.BlockSpec((1, B_DV, B_KEY_WIDTH), lambda i, j: (i, 0, 0)),
                  pl.BlockSpec((IN_WIDTH, D_MODEL // nj), lambda i, j: (0, strip(i, j))),
                  wrows(w_proj_a), wrows(w_proj_b), wrows(w_out)],
        out_specs=[o[0] for o in outs],
        out_shape=[o[1] for o in outs],
        scratch_shapes=[pltpu.VMEM((B_DV, B_KEY_WIDTH), F32)],
        compiler_params=pltpu.CompilerParams(
            dimension_semantics=("arbitrary", "arbitrary"), vmem_limit_bytes=VMEM_LIMIT_BYTES),
        name="pre",
    )(x, mod, ng, w_pre, w2, gb, lg, lb, s0b, w_in_t, w_proj_a, w_proj_b, w_out)


def _scale_rows(x16, scales):
    parts = []
    for c, s in enumerate(scales):
        xc = x16[c * B_CHUNK:(c + 1) * B_CHUNK, :]
        parts.append(xc if s is None else (xc.astype(F32) * s).astype(BF16))
    return parts[0] if len(parts) == 1 else jnp.concatenate(parts, axis=0)


def _exp_sum(tots, idx):
    return jnp.exp2(_sum_rows([tots[m] for m in idx])) if idx else None


def _gla_block(qdf, kdf, kcf, qdb, kdb, kcb, v16, tf, tb, sf, sb, p_scr):
    nc = GLA_NC
    ch = B_CHUNK
    lane = lax.broadcasted_iota(jnp.int32, (1, LANES), 1)
    hm = [(lane < B_DK).astype(BF16), (lane >= B_DK).astype(BF16)]

    def pair_heads(x):
        return jnp.concatenate([x * hm[0], x * hm[1]], axis=0)

    row = lax.broadcasted_iota(jnp.int32, (ch, 2 * ch), 0)
    col = lax.broadcasted_iota(jnp.int32, (ch, 2 * ch), 1)
    first = col < ch
    zeros = jnp.zeros((ch, LANES), BF16)
    for g in range(B_HEADS // 2):
        ls = slice(g * LANES, (g + 1) * LANES)
        for i in range(nc // 2):
            pr = slice(2 * i * ch, (2 * i + 2) * ch)
            re = slice(2 * i * ch, (2 * i + 1) * ch)
            ro = slice((2 * i + 1) * ch, (2 * i + 2) * ch)
            sc_f = _dot_nt(pair_heads(qdf[pr, ls]), jnp.concatenate([kdf[pr, ls], kcf[re, ls], zeros], axis=0))
            sc_b = _dot_nt(pair_heads(qdb[pr, ls]), jnp.concatenate([kdb[pr, ls], zeros, kcb[ro, ls]], axis=0))
            for hh in range(2):
                he = slice(hh * 2 * ch, hh * 2 * ch + ch)
                ho = slice(hh * 2 * ch + ch, (hh + 1) * 2 * ch)
                even = jnp.where(first,
                                 jnp.where(col <= row, sc_f[he, 0:2 * ch], 0.0)
                                 + jnp.where(col >= row, sc_b[he, 0:2 * ch], 0.0),
                                 sc_b[he, 2 * ch:])
                odd = jnp.where(first, sc_f[ho, 2 * ch:],
                                jnp.where(col - ch <= row, sc_f[ho, 0:2 * ch], 0.0)
                                + jnp.where(col - ch >= row, sc_b[ho, 0:2 * ch], 0.0))
                p_scr[2 * g + hh, re, pr] = even.astype(BF16)
                p_scr[2 * g + hh, ro, pr] = odd.astype(BF16)

    def cross(lo, hi):
        if hi - lo <= 2:
            return
        mid = (lo + hi) // 2
        cross(lo, mid)
        cross(mid, hi)
        rl = slice(lo * ch, mid * ch)
        rh = slice(mid * ch, hi * ch)
        n = (hi - mid) * ch
        qf = _scale_rows(qdf[rh], [_exp_sum(tf, range(mid, c)) for c in range(mid, hi)])
        kf = _scale_rows(kcf[rl], [_exp_sum(tf, range(c + 1, mid)) for c in range(lo, mid)])
        qb = _scale_rows(qdb[rl], [_exp_sum(tb, range(c + 1, mid)) for c in range(lo, mid)])
        kb = _scale_rows(kcb[rh], [_exp_sum(tb, range(mid, c)) for c in range(mid, hi)])
        for g in range(B_HEADS // 2):
            ls = slice(g * LANES, (g + 1) * LANES)
            sc_f = _dot_nt(pair_heads(qf[:, ls]), kf[:, ls]).astype(BF16)
            sc_b = _dot_nt(pair_heads(qb[:, ls]), kb[:, ls]).astype(BF16)
            for hh in range(2):
                p_scr[2 * g + hh, rh, rl] = sc_f[hh * n:(hh + 1) * n]
                p_scr[2 * g + hh, rl, rh] = sc_b[hh * n:(hh + 1) * n]

    cross(0, nc)

    qsf = _scale_rows(qdf, [_exp_sum(tf, range(0, c)) for c in range(nc)])
    qsb = _scale_rows(qdb, [_exp_sum(tb, range(c + 1, nc)) for c in range(nc)])
    head_row = lax.broadcasted_iota(jnp.int32, (LANES, 1), 0) // B_DK
    outs = []
    for h in range(B_HEADS):
        ls = slice((h // 2) * LANES, (h // 2 + 1) * LANES)
        own = (head_row == (h % 2)).astype(BF16)
        lhs = jnp.concatenate([p_scr[h], qsf[:, ls], qsb[:, ls]], axis=1)
        rhs = jnp.concatenate([v16[:, h * B_DV:(h + 1) * B_DV], sf[ls, :] * own, sb[ls, :] * own], axis=0)
        outs.append(jnp.dot(lhs, rhs, preferred_element_type=F32))
    return outs


def _colmix_kernel(ws_ref, bs_ref, vn_ref, o_ref):
    nw = vn_ref.shape[2]
    xs = jnp.concatenate([vn_ref[0, 0, w] for w in range(nw)], axis=1)
    y = _dot(ws_ref[0], xs) + bs_ref[0][:, 0:1]
    for w in range(nw):
        o_ref[0, 0, w] = y[:, w * A_GROUP_DIM:(w + 1) * A_GROUP_DIM]


def _colmix_call(ws, bs, vn_col, nw):
    b, g, width, rows, ch = vn_col.shape
    blk = pl.BlockSpec((1, 1, nw, rows, ch), lambda i, j, m: (i, j, m, 0, 0))
    return pl.pallas_call(
        _colmix_kernel,
        grid=(b, g, width // nw),
        in_specs=[pl.BlockSpec((1, rows, rows), lambda i, j, m: (j + A_ROW_GROUPS, 0, 0)),
                  pl.BlockSpec((1, rows, A_GROUP_DIM), lambda i, j, m: (j + A_ROW_GROUPS, 0, 0)),
                  blk],
        out_specs=blk,
        out_shape=jax.ShapeDtypeStruct(vn_col.shape, F32),
        compiler_params=pltpu.CompilerParams(
            dimension_semantics=("parallel", "parallel", "parallel"), vmem_limit_bytes=VMEM_LIMIT_BYTES),
        name="colmix",
    )(ws, bs, vn_col)


def _main_kernel(x_ref, mod_ref, ng_ref, w_ref, vr_ref, sc_ref,
                 qdf_ref, kdf_ref, kcf_ref, qdb_ref, kdb_ref, kcb_ref, v_ref, totf_ref, totb_ref,
                 kvf_ref, s0f_ref, sb_ref,
                 ws_ref, bs_ref, bg_ref, wpa_ref, wpb_ref, wo_ref, fg_ref, o_ref, acta_scr, on_scr, p_scr, sf_scr):
    t = x_ref.shape[1]
    blocks = [slice(blk * GLA_BLOCK, (blk + 1) * GLA_BLOCK) for blk in range(t // GLA_BLOCK)]

    @pl.when(pl.program_id(1) == 0)
    def _():
        sf_scr[...] = s0f_ref[0]

    s = sf_scr[...]
    sf_in = []
    for blk in range(len(blocks)):
        sf_in.append(s.T.astype(BF16))
        s = jnp.exp2(_sum_rows([totf_ref[0, blk, c:c + 1, :] for c in range(GLA_NC)])) * s + kvf_ref[0, blk]
    sf_scr[...] = s

    def gla(blk):
        rs = blocks[blk]
        tf = [totf_ref[0, blk, c:c + 1, :] for c in range(GLA_NC)]
        tb = [totb_ref[0, blk, c:c + 1, :] for c in range(GLA_NC)]
        o_heads = _gla_block(qdf_ref[0, rs, :], kdf_ref[0, rs, :], kcf_ref[0, rs, :],
                             qdb_ref[0, rs, :], kdb_ref[0, rs, :], kcb_ref[0, rs, :], v_ref[0, rs, :],
                             tf, tb, sf_in[blk], sb_ref[0, blk], p_scr.at[blk])
        for hd in range(B_HEADS):
            on_scr[rs, hd * B_DV:(hd + 1) * B_DV] = _rms_rows(o_heads[hd])

    def proj(c0, n):
        return jnp.dot(h16, w_ref[:, c0:c0 + n], preferred_element_type=F32)

    gla(0)
    h16 = _latent_h(x_ref[0], mod_ref, ng_ref).astype(BF16)
    zb = proj(0, B_VAL_WIDTH)
    u = proj(B_VAL_WIDTH, A_WIDTH)
    za = proj(B_VAL_WIDTH + A_WIDTH, A_WIDTH)
    for blk in range(1, len(blocks)):
        gla(blk)
    gate_a = jax.nn.sigmoid(proj(B_VAL_WIDTH + 2 * A_WIDTH, D_MODEL))

    uz = u * _silu(za)
    for g in range(A_GROUPS):
        cs = slice(g * A_GROUP_DIM, (g + 1) * A_GROUP_DIM)
        if g < A_ROW_GROUPS:
            chunks = [slice(c * A_CHUNK, (c + 1) * A_CHUNK) for c in range(t // A_CHUNK)]
            sv_all = jnp.dot(ws_ref[g].astype(BF16), jnp.concatenate([vr_ref[0, rs, cs] for rs in chunks], axis=1),
                             preferred_element_type=F32)
            for c, rs in enumerate(chunks):
                sv = sv_all[:, c * A_GROUP_DIM:(c + 1) * A_GROUP_DIM] + bs_ref[g][:, 0:1]
                acta_scr[rs, cs] = (uz[rs, cs] * sv).astype(BF16)
        else:
            sv = jnp.swapaxes(sc_ref[0, g - A_ROW_GROUPS], 0, 1).reshape(t, A_GROUP_DIM)
            acta_scr[:, cs] = (uz[:, cs] * sv).astype(BF16)
    ya = jnp.dot(acta_scr[...], wpa_ref[...], preferred_element_type=F32)

    gate_b = jax.nn.sigmoid(proj(B_VAL_WIDTH + 2 * A_WIDTH + D_MODEL, D_MODEL))
    yb = _dot(on_scr[...] * (_silu(zb) * bg_ref[...]), wpb_ref[...])

    m = (gate_a * ya + gate_b * yb).astype(BF16)
    for rs in blocks:
        y = jnp.dot(m[rs], wo_ref[...], preferred_element_type=F32)
        xo = x_ref[0, rs, :] + mod_ref[pl.ds(pl.program_id(0), 1), 2 * D_MODEL:] * y
        o_ref[0, rs, :] = _rms_rows(xo) * fg_ref[...]


def _main_call(x, mod, ng, w_main, vn_row, sv_col, gla_ops, ws, bs, bg, wpa, wpb, wo, fg, t):
    b, l, _ = x.shape
    nw = w_main.shape[1]
    ncg = A_GROUPS - A_ROW_GROUPS
    const2 = lambda i, j: (0, 0)
    const3 = lambda i, j: (0, 0, 0)
    tok = lambda width: pl.BlockSpec((1, t, width), lambda i, j: (i, j, 0))
    nblk = t // GLA_BLOCK
    tot_spec = pl.BlockSpec((1, nblk, GLA_NC, B_KEY_WIDTH), lambda i, j: (i, j, 0, 0))
    st_spec = pl.BlockSpec((1, nblk, B_KEY_WIDTH, B_DV), lambda i, j: (i, j, 0, 0))
    kv_spec = pl.BlockSpec((1, nblk, B_DV, B_KEY_WIDTH), lambda i, j: (i, j, 0, 0))
    s0_spec = pl.BlockSpec((1, B_DV, B_KEY_WIDTH), lambda i, j: (i, 0, 0))
    return pl.pallas_call(
        _main_kernel,
        grid=(b, l // t),
        in_specs=[tok(D_MODEL),
                  pl.BlockSpec((8, 3 * D_MODEL), const2),
                  pl.BlockSpec((1, D_MODEL), const2),
                  pl.BlockSpec((D_MODEL, nw), const2),
                  tok(A_ROW_GROUPS * A_GROUP_DIM),
                  pl.BlockSpec((1, ncg, GRID_W, t // GRID_W, A_GROUP_DIM), lambda i, j: (i, 0, 0, j, 0)),
                  tok(B_KEY_WIDTH), tok(B_KEY_WIDTH), tok(B_KEY_WIDTH),
                  tok(B_KEY_WIDTH), tok(B_KEY_WIDTH), tok(B_KEY_WIDTH), tok(B_VAL_WIDTH),
                  tot_spec, tot_spec, kv_spec, s0_spec, st_spec,
                  pl.BlockSpec((A_GROUPS, A_CHUNK, A_CHUNK), const3),
                  pl.BlockSpec((A_GROUPS, A_CHUNK, A_GROUP_DIM), const3),
                  pl.BlockSpec((1, B_VAL_WIDTH), const2),
                  pl.BlockSpec((A_WIDTH, D_MODEL), const2),
                  pl.BlockSpec((B_VAL_WIDTH, D_MODEL), const2),
                  pl.BlockSpec((D_MODEL, D_MODEL), const2),
                  pl.BlockSpec((1, D_MODEL), const2)],
        out_specs=tok(D_MODEL),
        out_shape=jax.ShapeDtypeStruct((b, l, D_MODEL), F32),
        scratch_shapes=[pltpu.VMEM((t, A_WIDTH), BF16), pltpu.VMEM((t, B_VAL_WIDTH), F32),
                        pltpu.VMEM((nblk, B_HEADS, GLA_BLOCK, GLA_BLOCK), BF16),
                        pltpu.VMEM((B_DV, B_KEY_WIDTH), F32)],
        compiler_params=pltpu.CompilerParams(
            dimension_semantics=("parallel", "arbitrary"), vmem_limit_bytes=VMEM_LIMIT_BYTES),
        name="main",
    )(x, mod, ng, w_main, vn_row, sv_col, *gla_ops, ws, bs, bg, wpa, wpb, wo, fg)


def kernel(x, c, ctx, c_ctx, w_mod, b_mod, norm_g, w_in, a_ln_g, a_ln_b, a_ws, a_bs, b_gate_w2, b_gate_b,
           b_norm_g, w_proj_a, w_proj_b, w_out, final_norm_g):
    assert w_mod.shape[0] == 1, "single-layer block"
    b, l, _ = x.shape
    rows = l // GRID_W
    assert rows == A_CHUNK
    ng = norm_g[0][None, :]

    w_in_t = w_in[0].T
    w_pre, mod, bs_t = _wprep_call(w_in_t, c, c_ctx, w_mod, b_mod, a_bs)
    w2, gb = b_gate_w2[0], b_gate_b[0]
    s0f, s0b = _ctx_call(ctx, mod, ng, w_pre, w2, gb, b)

    (qd_f, kd_f, kc_f, qd_b, kd_b, kc_b, v16, tot_f, tot_b, kv_f, s_b, vn_row, vn_col,
     w_main, wpa16, wpb16, wo16) = _pre_call(
        x, mod, ng, w_pre, w2, gb, a_ln_g[0][None, :], a_ln_b[0][None, :], s0b,
        w_in_t, w_proj_a, w_proj_b, w_out, TOKEN_TILE)
    gla_ops = (qd_f, kd_f, kc_f, qd_b, kd_b, kc_b, v16, tot_f, tot_b, kv_f, s0f, s_b)

    sv_col = _colmix_call(a_ws[0], bs_t, vn_col, COLMIX_COLUMNS)

    return _main_call(x, mod, ng, w_main, vn_row, sv_col, gla_ops, a_ws[0], bs_t, b_norm_g[0][None, :],
                      wpa16, wpb16, wo16, final_norm_g[None, :], TOKEN_TILE)
```

```python
import functools

import jax
import jax.numpy as jnp
from jax import lax
from jax.experimental import pallas as pl
from jax.experimental.pallas import tpu as pltpu

D_MODEL = 1024
GRID_W = 64
EPS = 1e-6

A_WIDTH = 512
A_GROUPS = 4
A_GROUP_DIM = 128
A_CHUNK = 128
A_ROW_GROUPS = 2

B_HEADS = 4
B_DK = 64
B_DV = 128
B_KEY_WIDTH = 256
B_VAL_WIDTH = 512
B_GATE_RANK = 16
B_GATE_TAU = 16.0
LOG2E = 1.4426950408889634
B_CHUNK = 64

Q0 = 0
K0 = Q0 + B_KEY_WIDTH
V0 = K0 + B_KEY_WIDTH
LR0 = V0 + B_VAL_WIDTH
ZB0 = LR0 + 2 * B_GATE_RANK
UA0 = ZB0 + B_VAL_WIDTH
VA0 = UA0 + A_WIDTH
ZA0 = VA0 + A_WIDTH
G0 = ZA0 + A_WIDTH
IN_WIDTH = G0 + 2 * D_MODEL

LANES = 128

P_LR = 0
P_K = P_LR + LANES
P_Q = P_K + B_KEY_WIDTH
P_V = P_Q + B_KEY_WIDTH
P_VA = P_V + B_VAL_WIDTH
PRE_WIDTH = P_VA + A_WIDTH
GLA_BLOCK = 256
TOKEN_TILE = 1024
COLMIX_COLUMNS = 32
GLA_NC = GLA_BLOCK // B_CHUNK

VMEM_LIMIT_BYTES = 56 * 1024 * 1024

BF16 = jnp.bfloat16
F32 = jnp.float32


def _dot(a, b):
    return jnp.dot(a.astype(BF16), b.astype(BF16), preferred_element_type=F32)


def _dot_nt(a, b):
    return lax.dot_general(a.astype(BF16), b.astype(BF16), (((1,), (1,)), ((), ())),
                           preferred_element_type=F32)


def _dot_tn(a, b):
    return lax.dot_general(a.astype(BF16), b.astype(BF16), (((0,), (0,)), ((), ())),
                           preferred_element_type=F32)


def _silu(x):
    return x * jax.nn.sigmoid(x)


def _rms_rows(x):
    return x * lax.rsqrt(jnp.mean(x * x, axis=-1, keepdims=True) + EPS)


def _put_transposed(dst_ref, c0, wt_ref, r0, n, scale=None):
    for s in range(0, n, 2 * LANES):
        m = min(2 * LANES, n - s)
        blk = wt_ref[r0 + s:r0 + s + m, :].T
        dst_ref[:, c0 + s:c0 + s + m] = (blk if scale is None else blk * scale).astype(BF16)


def _wprep_kernel(wt_ref, c_ref, cctx_ref, wm_ref, bm_ref, abs_ref, wpre_ref, mod_ref, bst_ref):
    _put_transposed(wpre_ref, P_LR, wt_ref, LR0, LANES)
    _put_transposed(wpre_ref, P_K, wt_ref, K0, B_KEY_WIDTH)
    _put_transposed(wpre_ref, P_Q, wt_ref, Q0, B_KEY_WIDTH, B_DK ** -0.5)
    _put_transposed(wpre_ref, P_V, wt_ref, V0, B_VAL_WIDTH)
    _put_transposed(wpre_ref, P_VA, wt_ref, VA0, A_WIDTH)
    nb = c_ref.shape[0]
    cc = jnp.concatenate([c_ref[...], cctx_ref[...], jnp.zeros((8 - nb - 1, D_MODEL), F32)], axis=0)
    mod_ref[...] = _dot(_silu(cc), wm_ref[0]) + bm_ref[...]
    g = pl.program_id(0)
    bst_ref[0] = jnp.broadcast_to(abs_ref[0, pl.ds(g, 1), :], (A_GROUP_DIM, A_CHUNK)).T


def _wprep_call(w_in_t, c, c_ctx, w_mod, b_mod, a_bs):
    steps = A_GROUPS
    n_mod = w_mod.shape[2]
    outs = [(pl.BlockSpec((D_MODEL // steps, PRE_WIDTH), lambda i: (i, 0)),
             jax.ShapeDtypeStruct((D_MODEL, PRE_WIDTH), BF16)),
            (pl.BlockSpec((8, n_mod // steps), lambda i: (0, i)), jax.ShapeDtypeStruct((8, n_mod), F32)),
            (pl.BlockSpec((1, A_CHUNK, A_GROUP_DIM), lambda i: (i, 0, 0)),
             jax.ShapeDtypeStruct((A_GROUPS, A_CHUNK, A_GROUP_DIM), F32))]
    return pl.pallas_call(
        _wprep_kernel,
        grid=(steps,),
        in_specs=[pl.BlockSpec((VA0 + A_WIDTH, D_MODEL // steps), lambda i: (0, i)),
                  pl.BlockSpec(c.shape, lambda i: (0, 0)),
                  pl.BlockSpec((1, D_MODEL), lambda i: (0, 0)),
                  pl.BlockSpec((1, D_MODEL, n_mod // steps), lambda i: (0, 0, i)),
                  pl.BlockSpec((1, n_mod // steps), lambda i: (0, i)),
                  pl.BlockSpec(a_bs.shape, lambda i: (0, 0, 0))],
        out_specs=[o[0] for o in outs],
        out_shape=[o[1] for o in outs],
        compiler_params=pltpu.CompilerParams(
            dimension_semantics=("parallel",), vmem_limit_bytes=VMEM_LIMIT_BYTES),
        name="wprep",
    )(w_in_t, c, c_ctx[None, :], w_mod, b_mod, a_bs)


def _gate_logs(lr, w2_ref, gb_ref):
    out = []
    for r in range(2):
        logits = _dot(lr[:, r * B_GATE_RANK:(r + 1) * B_GATE_RANK], w2_ref[r]) + gb_ref[r:r + 1, :]
        log_sig = jnp.minimum(logits, 0.0) - jnp.log(1.0 + jnp.exp(-jnp.abs(logits)))
        out.append(log_sig * (LOG2E / B_GATE_TAU))
    return out


def _chunk_tri(reverse):
    i = lax.broadcasted_iota(jnp.int32, (GLA_BLOCK, GLA_BLOCK), 0)
    j = lax.broadcasted_iota(jnp.int32, (GLA_BLOCK, GLA_BLOCK), 1)
    same = (i // B_CHUNK) == (j // B_CHUNK)
    tri = (j >= i) if reverse else (j <= i)
    return (same & tri).astype(BF16)


def _sum_rows(rows):
    acc = rows[0]
    for r in rows[1:]:
        acc = acc + r
    return acc


def _block_cum(a, tri, reverse):
    hi = a.astype(BF16)
    lo = (a - hi.astype(F32)).astype(BF16)
    cum = (jnp.dot(tri, hi, preferred_element_type=F32) + jnp.dot(tri, lo, preferred_element_type=F32))
    last = 0 if reverse else B_CHUNK - 1
    tots = [cum[c * B_CHUNK + last:c * B_CHUNK + last + 1, :] for c in range(GLA_NC)]
    return cum, tots


def _block_keys(k, cum, tots, reverse):
    totb = jnp.concatenate([jnp.broadcast_to(t, (B_CHUNK, B_KEY_WIDTH)) for t in tots], axis=0)
    kdec = k * jnp.exp2(totb - cum)
    later = []
    for c in range(GLA_NC):
        idx = list(range(0, c)) if reverse else list(range(c + 1, GLA_NC))
        if idx:
            later.append(jnp.broadcast_to(jnp.exp2(_sum_rows([tots[m] for m in idx])), (B_CHUNK, B_KEY_WIDTH)))
        else:
            later.append(jnp.ones((B_CHUNK, B_KEY_WIDTH), F32))
    return kdec, kdec * jnp.concatenate(later, axis=0)


def _pre_project(h16, w_ref, want_q):
    lr = jnp.dot(h16, w_ref[:, P_LR:P_LR + LANES], preferred_element_type=F32)[:, 0:2 * B_GATE_RANK]
    k = jnp.dot(h16, w_ref[:, P_K:P_K + B_KEY_WIDTH], preferred_element_type=F32)
    q = jnp.dot(h16, w_ref[:, P_Q:P_Q + B_KEY_WIDTH], preferred_element_type=F32) if want_q else None
    v = jnp.dot(h16, w_ref[:, P_V:P_V + B_VAL_WIDTH], preferred_element_type=F32)
    return lr, k, q, v


def _block_kv_t(v16, kblk):
    k16 = kblk.astype(BF16)
    lane = lax.broadcasted_iota(jnp.int32, (1, B_KEY_WIDTH), 1)
    acc = jnp.zeros((B_DV, B_KEY_WIDTH), F32)
    for h in range(B_HEADS):
        full = _dot_tn(v16[:, h * B_DV:(h + 1) * B_DV], k16)
        acc = acc + jnp.where((lane // B_DK) == h, full, 0.0)
    return acc


def _ctx_kernel(ctx_ref, mod_ref, ng_ref, w_ref, w2_ref, gb_ref, sf_ref, sb_ref, *, ctx_row):
    nb, lc, _ = ctx_ref.shape
    xc = ctx_ref[...].reshape(nb * lc, D_MODEL)
    shift = mod_ref[ctx_row:ctx_row + 1, 0:D_MODEL]
    scale = mod_ref[ctx_row:ctx_row + 1, D_MODEL:2 * D_MODEL]
    hc = _rms_rows(xc) * ng_ref[...] * (1.0 + scale) + shift
    lr, k, _, v = _pre_project(hc.astype(BF16), w_ref, False)
    v16 = v.astype(BF16)
    a_f, a_b = _gate_logs(lr, w2_ref, gb_ref)
    for bi in range(nb):
        rs = slice(bi * lc, (bi + 1) * lc)
        for a, reverse, out_ref in ((a_f, False, sf_ref), (a_b, True, sb_ref)):
            cum, tots = _block_cum(a[rs], _chunk_tri(reverse), reverse)
            _, kblk = _block_keys(k[rs], cum, tots, reverse)
            out_ref[bi] = _block_kv_t(v16[rs], kblk)


def _ctx_call(ctx, mod, ng, w_kvl, w2, gb, ctx_row):
    b, lc, _ = ctx.shape
    assert lc == GLA_BLOCK, "context length must be one GLA block"
    nw = w_kvl.shape[1]
    st = jax.ShapeDtypeStruct((b, B_DV, B_KEY_WIDTH), F32)
    st_spec = pl.BlockSpec((b, B_DV, B_KEY_WIDTH), lambda i: (0, 0, 0))
    return pl.pallas_call(
        functools.partial(_ctx_kernel, ctx_row=ctx_row),
        grid=(1,),
        in_specs=[pl.BlockSpec((b, lc, D_MODEL), lambda i: (0, 0, 0)),
                  pl.BlockSpec((8, 3 * D_MODEL), lambda i: (0, 0)),
                  pl.BlockSpec((1, D_MODEL), lambda i: (0, 0)),
                  pl.BlockSpec((D_MODEL, nw), lambda i: (0, 0)),
                  pl.BlockSpec((2, B_GATE_RANK, B_KEY_WIDTH), lambda i: (0, 0, 0)),
                  pl.BlockSpec((2, B_KEY_WIDTH), lambda i: (0, 0))],
        out_specs=[st_spec, st_spec],
        out_shape=[st, st],
        name="ctx",
    )(ctx, mod, ng, w_kvl, w2, gb)


def _latent_h(x, mod_ref, ng_ref):
    row = pl.ds(pl.program_id(0), 1)
    shift = mod_ref[row, 0:D_MODEL]
    gain = ng_ref[...] * (1.0 + mod_ref[row, D_MODEL:2 * D_MODEL])
    return _rms_rows(x) * gain + shift


def _pre_kernel(x_ref, mod_ref, ng_ref, w_ref, w2_ref, gb_ref, lg_ref, lb_ref, s0b_ref,
                wt_ref, wpa_ref, wpb_ref, wo_ref,
                qdf_ref, kdf_ref, kcf_ref, qdb_ref, kdb_ref, kcb_ref, v_ref,
                totf_ref, totb_ref, kvf_ref, sb_ref, vr_ref, vc_ref,
                wmain_ref, wpa16_ref, wpb16_ref, wo16_ref, sb_scr):
    @pl.when(pl.program_id(1) == 0)
    def _():
        sb_scr[...] = s0b_ref[0]

    h16 = _latent_h(x_ref[0], mod_ref, ng_ref).astype(BF16)
    nblk = x_ref.shape[1] // GLA_BLOCK
    blocks = [slice(blk * GLA_BLOCK, (blk + 1) * GLA_BLOCK) for blk in range(nblk)]
    dirs = ((False, qdf_ref, kdf_ref, kcf_ref, totf_ref), (True, qdb_ref, kdb_ref, kcb_ref, totb_ref))

    lrk = jnp.dot(h16, w_ref[:, P_LR:P_K + B_KEY_WIDTH], preferred_element_type=F32)
    lr = lrk[:, 0:2 * B_GATE_RANK]
    k = lrk[:, P_K:P_K + B_KEY_WIDTH]
    logs = _gate_logs(lr, w2_ref, gb_ref)
    q = jnp.dot(h16, w_ref[:, P_Q:P_Q + B_KEY_WIDTH], preferred_element_type=F32)

    _put_transposed(wmain_ref, 0, wt_ref, ZB0, VA0 - ZB0)
    _put_transposed(wmain_ref, VA0 - ZB0, wt_ref, ZA0, IN_WIDTH - ZA0)
    wpa16_ref[...] = wpa_ref[0].astype(BF16)
    wpb16_ref[...] = wpb_ref[0].astype(BF16)
    wo16_ref[...] = wo_ref[0].astype(BF16)

    v16 = jnp.dot(h16, w_ref[:, P_V:P_V + B_VAL_WIDTH], preferred_element_type=F32).astype(BF16)
    v_ref[0] = v16

    cums = {}
    for d, (reverse, _, _, _, tot_ref) in enumerate(dirs):
        tri = _chunk_tri(reverse)
        for blk, rs in enumerate(blocks):
            cum, tots = _block_cum(logs[d][rs], tri, reverse)
            cums[d, blk] = (cum, tots)
            for c in range(GLA_NC):
                tot_ref[0, blk, c:c + 1, :] = tots[c]

    va = jnp.dot(h16, w_ref[:, P_VA:P_VA + A_WIDTH], preferred_element_type=F32)
    vc = va - jnp.mean(va, axis=-1, keepdims=True)
    vn = vc * lax.rsqrt(jnp.mean(vc * vc, axis=-1, keepdims=True) + EPS) * lg_ref[...] + lb_ref[...]
    nr = A_ROW_GROUPS * A_GROUP_DIM
    vr_ref[0] = vn[:, 0:nr].astype(BF16)
    for g in range(A_GROUPS - A_ROW_GROUPS):
        vg = vn[:, nr + g * A_GROUP_DIM:nr + (g + 1) * A_GROUP_DIM]
        vc_ref[0, g] = jnp.swapaxes(vg.reshape(vg.shape[0] // GRID_W, GRID_W, A_GROUP_DIM), 0, 1).astype(BF16)

    kv_b = {}
    for d, (reverse, qd_ref, kd_ref, kc_ref, _) in enumerate(dirs):
        for blk, rs in enumerate(blocks):
            cum, tots = cums[d, blk]
            kdec, kblk = _block_keys(k[rs], cum, tots, reverse)
            kc_ref[0, rs, :] = kdec.astype(BF16)
            kd_ref[0, rs, :] = (k[rs] * jnp.exp2(-cum)).astype(BF16)
            qd_ref[0, rs, :] = (q[rs] * jnp.exp2(cum)).astype(BF16)
            if reverse:
                kv_b[blk] = _block_kv_t(v16[rs], kblk)
            else:
                kvf_ref[0, blk] = _block_kv_t(v16[rs], kblk)

    s = sb_scr[...]
    for blk in reversed(range(nblk)):
        sb_ref[0, blk] = s.T.astype(BF16)
        s = jnp.exp2(_sum_rows(cums[1, blk][1])) * s + kv_b[blk]
    sb_scr[...] = s


def _pre_call(x, mod, ng, w_pre, w2, gb, lg, lb, s0b, w_in_t, w_proj_a, w_proj_b, w_out, t):
    b, l, _ = x.shape
    nw = w_pre.shape[1]
    ncg = A_GROUPS - A_ROW_GROUPS
    nblk = t // GLA_BLOCK
    nj = l // t
    n_main = (VA0 - ZB0) + (IN_WIDTH - ZA0)
    assert D_MODEL % nj == 0 and (D_MODEL // nj) % LANES == 0, "one weight strip per sequence step"
    strip = lambda i, j: jnp.where(i == 0, j, nj - 1)

    def wrows(a):
        return pl.BlockSpec((1, a.shape[1] // nj, a.shape[2]), lambda i, j: (0, strip(i, j), 0))

    def wout(nrows, ncols):
        return (pl.BlockSpec((nrows // nj, ncols), lambda i, j: (strip(i, j), 0)),
                jax.ShapeDtypeStruct((nrows, ncols), BF16))
    rev3 = lambda i, j: (i, nj - 1 - j, 0)
    rev4 = lambda i, j: (i, nj - 1 - j, 0, 0)

    def tok(width, dtype):
        return (pl.BlockSpec((1, t, width), rev3), jax.ShapeDtypeStruct((b, l, width), dtype))

    tot = (pl.BlockSpec((1, nblk, GLA_NC, B_KEY_WIDTH), rev4),
           jax.ShapeDtypeStruct((b, l // GLA_BLOCK, GLA_NC, B_KEY_WIDTH), F32))
    kv = (pl.BlockSpec((1, nblk, B_DV, B_KEY_WIDTH), rev4),
          jax.ShapeDtypeStruct((b, l // GLA_BLOCK, B_DV, B_KEY_WIDTH), F32))
    st = (pl.BlockSpec((1, nblk, B_KEY_WIDTH, B_DV), rev4),
          jax.ShapeDtypeStruct((b, l // GLA_BLOCK, B_KEY_WIDTH, B_DV), BF16))
    outs = [tok(B_KEY_WIDTH, BF16)] * 6 + [tok(B_VAL_WIDTH, BF16), tot, tot, kv, st,
                                           tok(A_ROW_GROUPS * A_GROUP_DIM, BF16),
                                           (pl.BlockSpec((1, ncg, GRID_W, t // GRID_W, A_GROUP_DIM),
                                                         lambda i, j: (i, 0, 0, nj - 1 - j, 0)),
                                            jax.ShapeDtypeStruct((b, ncg, GRID_W, l // GRID_W, A_GROUP_DIM), BF16)),
                                           wout(D_MODEL, n_main), wout(A_WIDTH, D_MODEL),
                                           wout(B_VAL_WIDTH, D_MODEL), wout(D_MODEL, D_MODEL)]
    const2 = lambda i, j: (0, 0)
    return pl.pallas_call(
        _pre_kernel,
        grid=(b, nj),
        in_specs=[pl.BlockSpec((1, t, D_MODEL), rev3),
                  pl.BlockSpec((8, 3 * D_MODEL), const2),
                  pl.BlockSpec((1, D_MODEL), const2),
                  pl.BlockSpec((D_MODEL, nw), const2),
                  pl.BlockSpec((2, B_GATE_RANK, B_KEY_WIDTH), lambda i, j: (0, 0, 0)),
                  pl.BlockSpec((2, B_KEY_WIDTH), const2),
                  pl.BlockSpec((1, A_WIDTH), const2),
                  pl.BlockSpec((1, A_WIDTH), const2),
                  pl.BlockSpec((1, B_DV, B_KEY_WIDTH), lambda i, j: (i, 0, 0)),
                  pl.BlockSpec((IN_WIDTH, D_MODEL // nj), lambda i, j: (0, strip(i, j))),
                  wrows(w_proj_a), wrows(w_proj_b), wrows(w_out)],
        out_specs=[o[0] for o in outs],
        out_shape=[o[1] for o in outs],
        scratch_shapes=[pltpu.VMEM((B_DV, B_KEY_WIDTH), F32)],
        compiler_params=pltpu.CompilerParams(
            dimension_semantics=("arbitrary", "arbitrary"), vmem_limit_bytes=VMEM_LIMIT_BYTES),
        name="pre",
    )(x, mod, ng, w_pre, w2, gb, lg, lb, s0b, w_in_t, w_proj_a, w_proj_b, w_out)


def _scale_rows(x16, scales):
    parts = []
    for c, s in enumerate(scales):
        xc = x16[c * B_CHUNK:(c + 1) * B_CHUNK, :]
        parts.append(xc if s is None else (xc.astype(F32) * s).astype(BF16))
    return parts[0] if len(parts) == 1 else jnp.concatenate(parts, axis=0)


def _exp_sum(tots, idx):
    return jnp.exp2(_sum_rows([tots[m] for m in idx])) if idx else None


def _gla_block(qdf, kdf, kcf, qdb, kdb, kcb, v16, tf, tb, sf, sb, p_scr):
    nc = GLA_NC
    ch = B_CHUNK
    lane = lax.broadcasted_iota(jnp.int32, (1, LANES), 1)
    hm = [(lane < B_DK).astype(BF16), (lane >= B_DK).astype(BF16)]

    def pair_heads(x):
        return jnp.concatenate([x * hm[0], x * hm[1]], axis=0)

    row = lax.broadcasted_iota(jnp.int32, (ch, 2 * ch), 0)
    col = lax.broadcasted_iota(jnp.int32, (ch, 2 * ch), 1)
    first = col < ch
    zeros = jnp.zeros((ch, LANES), BF16)
    for g in range(B_HEADS // 2):
        ls = slice(g * LANES, (g + 1) * LANES)
        for i in range(nc // 2):
            pr = slice(2 * i * ch, (2 * i + 2) * ch)
            re = slice(2 * i * ch, (2 * i + 1) * ch)
            ro = slice((2 * i + 1) * ch, (2 * i + 2) * ch)
            sc_f = _dot_nt(pair_heads(qdf[pr, ls]), jnp.concatenate([kdf[pr, ls], kcf[re, ls], zeros], axis=0))
            sc_b = _dot_nt(pair_heads(qdb[pr, ls]), jnp.concatenate([kdb[pr, ls], zeros, kcb[ro, ls]], axis=0))
            for hh in range(2):
                he = slice(hh * 2 * ch, hh * 2 * ch + ch)
                ho = slice(hh * 2 * ch + ch, (hh + 1) * 2 * ch)
                even = jnp.where(first,
                                 jnp.where(col <= row, sc_f[he, 0:2 * ch], 0.0)
                                 + jnp.where(col >= row, sc_b[he, 0:2 * ch], 0.0),
                                 sc_b[he, 2 * ch:])
                odd = jnp.where(first, sc_f[ho, 2 * ch:],
                                jnp.where(col - ch <= row, sc_f[ho, 0:2 * ch], 0.0)
                                + jnp.where(col - ch >= row, sc_b[ho, 0:2 * ch], 0.0))
                p_scr[2 * g + hh, re, pr] = even.astype(BF16)
                p_scr[2 * g + hh, ro, pr] = odd.astype(BF16)

    def cross(lo, hi):
        if hi - lo <= 2:
            return
        mid = (lo + hi) // 2
        cross(lo, mid)
        cross(mid, hi)
        rl = slice(lo * ch, mid * ch)
        rh = slice(mid * ch, hi * ch)
        n = (hi - mid) * ch
        qf = _scale_rows(qdf[rh], [_exp_sum(tf, range(mid, c)) for c in range(mid, hi)])
        kf = _scale_rows(kcf[rl], [_exp_sum(tf, range(c + 1, mid)) for c in range(lo, mid)])
        qb = _scale_rows(qdb[rl], [_exp_sum(tb, range(c + 1, mid)) for c in range(lo, mid)])
        kb = _scale_rows(kcb[rh], [_exp_sum(tb, range(mid, c)) for c in range(mid, hi)])
        for g in range(B_HEADS // 2):
            ls = slice(g * LANES, (g + 1) * LANES)
            sc_f = _dot_nt(pair_heads(qf[:, ls]), kf[:, ls]).astype(BF16)
            sc_b = _dot_nt(pair_heads(qb[:, ls]), kb[:, ls]).astype(BF16)
            for hh in range(2):
                p_scr[2 * g + hh, rh, rl] = sc_f[hh * n:(hh + 1) * n]
                p_scr[2 * g + hh, rl, rh] = sc_b[hh * n:(hh + 1) * n]

    cross(0, nc)

    qsf = _scale_rows(qdf, [_exp_sum(tf, range(0, c)) for c in range(nc)])
    qsb = _scale_rows(qdb, [_exp_sum(tb, range(c + 1, nc)) for c in range(nc)])
    head_row = lax.broadcasted_iota(jnp.int32, (LANES, 1), 0) // B_DK
    outs = []
    for h in range(B_HEADS):
        ls = slice((h // 2) * LANES, (h // 2 + 1) * LANES)
        own = (head_row == (h % 2)).astype(BF16)
        lhs = jnp.concatenate([p_scr[h], qsf[:, ls], qsb[:, ls]], axis=1)
        rhs = jnp.concatenate([v16[:, h * B_DV:(h + 1) * B_DV], sf[ls, :] * own, sb[ls, :] * own], axis=0)
        outs.append(jnp.dot(lhs, rhs, preferred_element_type=F32))
    return outs


def _colmix_kernel(ws_ref, bs_ref, vn_ref, o_ref):
    nw = vn_ref.shape[2]
    xs = jnp.concatenate([vn_ref[0, 0, w] for w in range(nw)], axis=1)
    y = _dot(ws_ref[0], xs) + bs_ref[0][:, 0:1]
    for w in range(nw):
        o_ref[0, 0, w] = y[:, w * A_GROUP_DIM:(w + 1) * A_GROUP_DIM]


def _colmix_call(ws, bs, vn_col, nw):
    b, g, width, rows, ch = vn_col.shape
    blk = pl.BlockSpec((1, 1, nw, rows, ch), lambda i, j, m: (i, j, m, 0, 0))
    return pl.pallas_call(
        _colmix_kernel,
        grid=(b, g, width // nw),
        in_specs=[pl.BlockSpec((1, rows, rows), lambda i, j, m: (j + A_ROW_GROUPS, 0, 0)),
                  pl.BlockSpec((1, rows, A_GROUP_DIM), lambda i, j, m: (j + A_ROW_GROUPS, 0, 0)),
                  blk],
        out_specs=blk,
        out_shape=jax.ShapeDtypeStruct(vn_col.shape, F32),
        compiler_params=pltpu.CompilerParams(
            dimension_semantics=("parallel", "parallel", "parallel"), vmem_limit_bytes=VMEM_LIMIT_BYTES),
        name="colmix",
    )(ws, bs, vn_col)


def _main_kernel(x_ref, mod_ref, ng_ref, w_ref, vr_ref, sc_ref,
                 qdf_ref, kdf_ref, kcf_ref, qdb_ref, kdb_ref, kcb_ref, v_ref, totf_ref, totb_ref,
                 kvf_ref, s0f_ref, sb_ref,
                 ws_ref, bs_ref, bg_ref, wpa_ref, wpb_ref, wo_ref, fg_ref, o_ref, acta_scr, on_scr, p_scr, sf_scr):
    t = x_ref.shape[1]
    blocks = [slice(blk * GLA_BLOCK, (blk + 1) * GLA_BLOCK) for blk in range(t // GLA_BLOCK)]

    @pl.when(pl.program_id(1) == 0)
    def _():
        sf_scr[...] = s0f_ref[0]

    s = sf_scr[...]
    sf_in = []
    for blk in range(len(blocks)):
        sf_in.append(s.T.astype(BF16))
        s = jnp.exp2(_sum_rows([totf_ref[0, blk, c:c + 1, :] for c in range(GLA_NC)])) * s + kvf_ref[0, blk]
    sf_scr[...] = s

    def gla(blk):
        rs = blocks[blk]
        tf = [totf_ref[0, blk, c:c + 1, :] for c in range(GLA_NC)]
        tb = [totb_ref[0, blk, c:c + 1, :] for c in range(GLA_NC)]
        o_heads = _gla_block(qdf_ref[0, rs, :], kdf_ref[0, rs, :], kcf_ref[0, rs, :],
                             qdb_ref[0, rs, :], kdb_ref[0, rs, :], kcb_ref[0, rs, :], v_ref[0, rs, :],
                             tf, tb, sf_in[blk], sb_ref[0, blk], p_scr.at[blk])
        for hd in range(B_HEADS):
            on_scr[rs, hd * B_DV:(hd + 1) * B_DV] = _rms_rows(o_heads[hd])

    def proj(c0, n):
        return jnp.dot(h16, w_ref[:, c0:c0 + n], preferred_element_type=F32)

    gla(0)
    h16 = _latent_h(x_ref[0], mod_ref, ng_ref).astype(BF16)
    zb = proj(0, B_VAL_WIDTH)
    u = proj(B_VAL_WIDTH, A_WIDTH)
    za = proj(B_VAL_WIDTH + A_WIDTH, A_WIDTH)
    for blk in range(1, len(blocks)):
        gla(blk)
    gate_a = jax.nn.sigmoid(proj(B_VAL_WIDTH + 2 * A_WIDTH, D_MODEL))

    uz = u * _silu(za)
    for g in range(A_GROUPS):
        cs = slice(g * A_GROUP_DIM, (g + 1) * A_GROUP_DIM)
        if g < A_ROW_GROUPS:
            chunks = [slice(c * A_CHUNK, (c + 1) * A_CHUNK) for c in range(t // A_CHUNK)]
            sv_all = jnp.dot(ws_ref[g].astype(BF16), jnp.concatenate([vr_ref[0, rs, cs] for rs in chunks], axis=1),
                             preferred_element_type=F32)
            for c, rs in enumerate(chunks):
                sv = sv_all[:, c * A_GROUP_DIM:(c + 1) * A_GROUP_DIM] + bs_ref[g][:, 0:1]
                acta_scr[rs, cs] = (uz[rs, cs] * sv).astype(BF16)
        else:
            sv = jnp.swapaxes(sc_ref[0, g - A_ROW_GROUPS], 0, 1).reshape(t, A_GROUP_DIM)
            acta_scr[:, cs] = (uz[:, cs] * sv).astype(BF16)
    ya = jnp.dot(acta_scr[...], wpa_ref[...], preferred_element_type=F32)

    gate_b = jax.nn.sigmoid(proj(B_VAL_WIDTH + 2 * A_WIDTH + D_MODEL, D_MODEL))
    yb = _dot(on_scr[...] * (_silu(zb) * bg_ref[...]), wpb_ref[...])

    m = (gate_a * ya + gate_b * yb).astype(BF16)
    for rs in blocks:
        y = jnp.dot(m[rs], wo_ref[...], preferred_element_type=F32)
        xo = x_ref[0, rs, :] + mod_ref[pl.ds(pl.program_id(0), 1), 2 * D_MODEL:] * y
        o_ref[0, rs, :] = _rms_rows(xo) * fg_ref[...]


def _main_call(x, mod, ng, w_main, vn_row, sv_col, gla_ops, ws, bs, bg, wpa, wpb, wo, fg, t):
    b, l, _ = x.shape
    nw = w_main.shape[1]
    ncg = A_GROUPS - A_ROW_GROUPS
    const2 = lambda i, j: (0, 0)
    const3 = lambda i, j: (0, 0, 0)
    tok = lambda width: pl.BlockSpec((1, t, width), lambda i, j: (i, j, 0))
    nblk = t // GLA_BLOCK
    tot_spec = pl.BlockSpec((1, nblk, GLA_NC, B_KEY_WIDTH), lambda i, j: (i, j, 0, 0))
    st_spec = pl.BlockSpec((1, nblk, B_KEY_WIDTH, B_DV), lambda i, j: (i, j, 0, 0))
    kv_spec = pl.BlockSpec((1, nblk, B_DV, B_KEY_WIDTH), lambda i, j: (i, j, 0, 0))
    s0_spec = pl.BlockSpec((1, B_DV, B_KEY_WIDTH), lambda i, j: (i, 0, 0))
    return pl.pallas_call(
        _main_kernel,
        grid=(b, l // t),
        in_specs=[tok(D_MODEL),
                  pl.BlockSpec((8, 3 * D_MODEL), const2),
                  pl.BlockSpec((1, D_MODEL), const2),
                  pl.BlockSpec((D_MODEL, nw), const2),
                  tok(A_ROW_GROUPS * A_GROUP_DIM),
                  pl.BlockSpec((1, ncg, GRID_W, t // GRID_W, A_GROUP_DIM), lambda i, j: (i, 0, 0, j, 0)),
                  tok(B_KEY_WIDTH), tok(B_KEY_WIDTH), tok(B_KEY_WIDTH),
                  tok(B_KEY_WIDTH), tok(B_KEY_WIDTH), tok(B_KEY_WIDTH), tok(B_VAL_WIDTH),
                  tot_spec, tot_spec, kv_spec, s0_spec, st_spec,
                  pl.BlockSpec((A_GROUPS, A_CHUNK, A_CHUNK), const3),
                  pl.BlockSpec((A_GROUPS, A_CHUNK, A_GROUP_DIM), const3),
                  pl.BlockSpec((1, B_VAL_WIDTH), const2),
                  pl.BlockSpec((A_WIDTH, D_MODEL), const2),
                  pl.BlockSpec((B_VAL_WIDTH, D_MODEL), const2),
                  pl.BlockSpec((D_MODEL, D_MODEL), const2),
                  pl.BlockSpec((1, D_MODEL), const2)],
        out_specs=tok(D_MODEL),
        out_shape=jax.ShapeDtypeStruct((b, l, D_MODEL), F32),
        scratch_shapes=[pltpu.VMEM((t, A_WIDTH), BF16), pltpu.VMEM((t, B_VAL_WIDTH), F32),
                        pltpu.VMEM((nblk, B_HEADS, GLA_BLOCK, GLA_BLOCK), BF16),
                        pltpu.VMEM((B_DV, B_KEY_WIDTH), F32)],
        compiler_params=pltpu.CompilerParams(
            dimension_semantics=("parallel", "arbitrary"), vmem_limit_bytes=VMEM_LIMIT_BYTES),
        name="main",
    )(x, mod, ng, w_main, vn_row, sv_col, *gla_ops, ws, bs, bg, wpa, wpb, wo, fg)


def kernel(x, c, ctx, c_ctx, w_mod, b_mod, norm_g, w_in, a_ln_g, a_ln_b, a_ws, a_bs, b_gate_w2, b_gate_b,
           b_norm_g, w_proj_a, w_proj_b, w_out, final_norm_g):
    assert w_mod.shape[0] == 1, "single-layer block"
    b, l, _ = x.shape
    rows = l // GRID_W
    assert rows == A_CHUNK
    ng = norm_g[0][None, :]

    w_in_t = w_in[0].T
    w_pre, mod, bs_t = _wprep_call(w_in_t, c, c_ctx, w_mod, b_mod, a_bs)
    w2, gb = b_gate_w2[0], b_gate_b[0]
    s0f, s0b = _ctx_call(ctx, mod, ng, w_pre, w2, gb, b)

    (qd_f, kd_f, kc_f, qd_b, kd_b, kc_b, v16, tot_f, tot_b, kv_f, s_b, vn_row, vn_col,
     w_main, wpa16, wpb16, wo16) = _pre_call(
        x, mod, ng, w_pre, w2, gb, a_ln_g[0][None, :], a_ln_b[0][None, :], s0b,
        w_in_t, w_proj_a, w_proj_b, w_out, TOKEN_TILE)
    gla_ops = (qd_f, kd_f, kc_f, qd_b, kd_b, kc_b, v16, tot_f, tot_b, kv_f, s0f, s_b)

    sv_col = _colmix_call(a_ws[0], bs_t, vn_col, COLMIX_COLUMNS)

    return _main_call(x, mod, ng, w_main, vn_row, sv_col, gla_ops, a_ws[0], bs_t, b_norm_g[0][None, :],
                      wpa16, wpb16, wo16, final_norm_g[None, :], TOKEN_TILE)
```

```python
import functools

import jax
import jax.numpy as jnp
from jax import lax
from jax.experimental import pallas as pl
from jax.experimental.pallas import tpu as pltpu

D_MODEL = 1024
GRID_W = 64
EPS = 1e-6

A_WIDTH = 512
A_GROUPS = 4
A_GROUP_DIM = 128
A_CHUNK = 128
A_ROW_GROUPS = 2

B_HEADS = 4
B_DK = 64
B_DV = 128
B_KEY_WIDTH = 256
B_VAL_WIDTH = 512
B_GATE_RANK = 16
B_GATE_TAU = 16.0
LOG2E = 1.4426950408889634
B_CHUNK = 64

Q0 = 0
K0 = Q0 + B_KEY_WIDTH
V0 = K0 + B_KEY_WIDTH
LR0 = V0 + B_VAL_WIDTH
ZB0 = LR0 + 2 * B_GATE_RANK
UA0 = ZB0 + B_VAL_WIDTH
VA0 = UA0 + A_WIDTH
ZA0 = VA0 + A_WIDTH
G0 = ZA0 + A_WIDTH
IN_WIDTH = G0 + 2 * D_MODEL

LANES = 128

P_LR = 0
P_K = P_LR + LANES
P_Q = P_K + B_KEY_WIDTH
P_V = P_Q + B_KEY_WIDTH
P_VA = P_V + B_VAL_WIDTH
PRE_WIDTH = P_VA + A_WIDTH
GLA_BLOCK = 256
TOKEN_TILE = 1024
COLMIX_COLUMNS = 64
GLA_NC = GLA_BLOCK // B_CHUNK

VMEM_LIMIT_BYTES = 56 * 1024 * 1024

BF16 = jnp.bfloat16
F32 = jnp.float32


def _dot(a, b):
    return jnp.dot(a.astype(BF16), b.astype(BF16), preferred_element_type=F32)


def _dot_nt(a, b):
    return lax.dot_general(a.astype(BF16), b.astype(BF16), (((1,), (1,)), ((), ())),
                           preferred_element_type=F32)


def _dot_tn(a, b):
    return lax.dot_general(a.astype(BF16), b.astype(BF16), (((0,), (0,)), ((), ())),
                           preferred_element_type=F32)


def _silu(x):
    return x * jax.nn.sigmoid(x)


def _rms_rows(x):
    return x * lax.rsqrt(jnp.mean(x * x, axis=-1, keepdims=True) + EPS)


def _put_transposed(dst_ref, c0, wt_ref, r0, n, scale=None):
    for s in range(0, n, 2 * LANES):
        m = min(2 * LANES, n - s)
        blk = wt_ref[r0 + s:r0 + s + m, :].T
        dst_ref[:, c0 + s:c0 + s + m] = (blk if scale is None else blk * scale).astype(BF16)


def _wprep_kernel(wt_ref, c_ref, cctx_ref, wm_ref, bm_ref, abs_ref, wpre_ref, mod_ref, bst_ref):
    _put_transposed(wpre_ref, P_LR, wt_ref, LR0, LANES)
    _put_transposed(wpre_ref, P_K, wt_ref, K0, B_KEY_WIDTH)
    _put_transposed(wpre_ref, P_Q, wt_ref, Q0, B_KEY_WIDTH, B_DK ** -0.5)
    _put_transposed(wpre_ref, P_V, wt_ref, V0, B_VAL_WIDTH)
    _put_transposed(wpre_ref, P_VA, wt_ref, VA0, A_WIDTH)
    nb = c_ref.shape[0]
    cc = jnp.concatenate([c_ref[...], cctx_ref[...], jnp.zeros((8 - nb - 1, D_MODEL), F32)], axis=0)
    mod_ref[...] = _dot(_silu(cc), wm_ref[0]) + bm_ref[...]
    g = pl.program_id(0)
    bst_ref[0] = jnp.broadcast_to(abs_ref[0, pl.ds(g, 1), :], (A_GROUP_DIM, A_CHUNK)).T


def _wprep_call(w_in_t, c, c_ctx, w_mod, b_mod, a_bs):
    steps = A_GROUPS
    n_mod = w_mod.shape[2]
    outs = [(pl.BlockSpec((D_MODEL // steps, PRE_WIDTH), lambda i: (i, 0)),
             jax.ShapeDtypeStruct((D_MODEL, PRE_WIDTH), BF16)),
            (pl.BlockSpec((8, n_mod // steps), lambda i: (0, i)), jax.ShapeDtypeStruct((8, n_mod), F32)),
            (pl.BlockSpec((1, A_CHUNK, A_GROUP_DIM), lambda i: (i, 0, 0)),
             jax.ShapeDtypeStruct((A_GROUPS, A_CHUNK, A_GROUP_DIM), F32))]
    return pl.pallas_call(
        _wprep_kernel,
        grid=(steps,),
        in_specs=[pl.BlockSpec((VA0 + A_WIDTH, D_MODEL // steps), lambda i: (0, i)),
                  pl.BlockSpec(c.shape, lambda i: (0, 0)),
                  pl.BlockSpec((1, D_MODEL), lambda i: (0, 0)),
                  pl.BlockSpec((1, D_MODEL, n_mod // steps), lambda i: (0, 0, i)),
                  pl.BlockSpec((1, n_mod // steps), lambda i: (0, i)),
                  pl.BlockSpec(a_bs.shape, lambda i: (0, 0, 0))],
        out_specs=[o[0] for o in outs],
        out_shape=[o[1] for o in outs],
        compiler_params=pltpu.CompilerParams(
            dimension_semantics=("parallel",), vmem_limit_bytes=VMEM_LIMIT_BYTES),
        name="wprep",
    )(w_in_t, c, c_ctx[None, :], w_mod, b_mod, a_bs)


def _gate_logs(lr, w2_ref, gb_ref):
    out = []
    for r in range(2):
        logits = _dot(lr[:, r * B_GATE_RANK:(r + 1) * B_GATE_RANK], w2_ref[r]) + gb_ref[r:r + 1, :]
        log_sig = jnp.minimum(logits, 0.0) - jnp.log(1.0 + jnp.exp(-jnp.abs(logits)))
        out.append(log_sig * (LOG2E / B_GATE_TAU))
    return out


def _chunk_tri(reverse):
    i = lax.broadcasted_iota(jnp.int32, (GLA_BLOCK, GLA_BLOCK), 0)
    j = lax.broadcasted_iota(jnp.int32, (GLA_BLOCK, GLA_BLOCK), 1)
    same = (i // B_CHUNK) == (j // B_CHUNK)
    tri = (j >= i) if reverse else (j <= i)
    return (same & tri).astype(BF16)


def _sum_rows(rows):
    acc = rows[0]
    for r in rows[1:]:
        acc = acc + r
    return acc


def _block_cum(a, tri, reverse):
    hi = a.astype(BF16)
    lo = (a - hi.astype(F32)).astype(BF16)
    cum = (jnp.dot(tri, hi, preferred_element_type=F32) + jnp.dot(tri, lo, preferred_element_type=F32))
    last = 0 if reverse else B_CHUNK - 1
    tots = [cum[c * B_CHUNK + last:c * B_CHUNK + last + 1, :] for c in range(GLA_NC)]
    return cum, tots


def _block_keys(k, cum, tots, reverse):
    totb = jnp.concatenate([jnp.broadcast_to(t, (B_CHUNK, B_KEY_WIDTH)) for t in tots], axis=0)
    kdec = k * jnp.exp2(totb - cum)
    later = []
    for c in range(GLA_NC):
        idx = list(range(0, c)) if reverse else list(range(c + 1, GLA_NC))
        if idx:
            later.append(jnp.broadcast_to(jnp.exp2(_sum_rows([tots[m] for m in idx])), (B_CHUNK, B_KEY_WIDTH)))
        else:
            later.append(jnp.ones((B_CHUNK, B_KEY_WIDTH), F32))
    return kdec, kdec * jnp.concatenate(later, axis=0)


def _pre_project(h16, w_ref, want_q):
    lr = jnp.dot(h16, w_ref[:, P_LR:P_LR + LANES], preferred_element_type=F32)[:, 0:2 * B_GATE_RANK]
    k = jnp.dot(h16, w_ref[:, P_K:P_K + B_KEY_WIDTH], preferred_element_type=F32)
    q = jnp.dot(h16, w_ref[:, P_Q:P_Q + B_KEY_WIDTH], preferred_element_type=F32) if want_q else None
    v = jnp.dot(h16, w_ref[:, P_V:P_V + B_VAL_WIDTH], preferred_element_type=F32)
    return lr, k, q, v


def _block_kv_t(v16, kblk):
    k16 = kblk.astype(BF16)
    lane = lax.broadcasted_iota(jnp.int32, (1, B_KEY_WIDTH), 1)
    acc = jnp.zeros((B_DV, B_KEY_WIDTH), F32)
    for h in range(B_HEADS):
        full = _dot_tn(v16[:, h * B_DV:(h + 1) * B_DV], k16)
        acc = acc + jnp.where((lane // B_DK) == h, full, 0.0)
    return acc


def _ctx_kernel(ctx_ref, mod_ref, ng_ref, w_ref, w2_ref, gb_ref, sf_ref, sb_ref, *, ctx_row):
    nb, lc, _ = ctx_ref.shape
    xc = ctx_ref[...].reshape(nb * lc, D_MODEL)
    shift = mod_ref[ctx_row:ctx_row + 1, 0:D_MODEL]
    scale = mod_ref[ctx_row:ctx_row + 1, D_MODEL:2 * D_MODEL]
    hc = _rms_rows(xc) * ng_ref[...] * (1.0 + scale) + shift
    lr, k, _, v = _pre_project(hc.astype(BF16), w_ref, False)
    v16 = v.astype(BF16)
    a_f, a_b = _gate_logs(lr, w2_ref, gb_ref)
    for bi in range(nb):
        rs = slice(bi * lc, (bi + 1) * lc)
        for a, reverse, out_ref in ((a_f, False, sf_ref), (a_b, True, sb_ref)):
            cum, tots = _block_cum(a[rs], _chunk_tri(reverse), reverse)
            _, kblk = _block_keys(k[rs], cum, tots, reverse)
            out_ref[bi] = _block_kv_t(v16[rs], kblk)


def _ctx_call(ctx, mod, ng, w_kvl, w2, gb, ctx_row):
    b, lc, _ = ctx.shape
    assert lc == GLA_BLOCK, "context length must be one GLA block"
    nw = w_kvl.shape[1]
    st = jax.ShapeDtypeStruct((b, B_DV, B_KEY_WIDTH), F32)
    st_spec = pl.BlockSpec((b, B_DV, B_KEY_WIDTH), lambda i: (0, 0, 0))
    return pl.pallas_call(
        functools.partial(_ctx_kernel, ctx_row=ctx_row),
        grid=(1,),
        in_specs=[pl.BlockSpec((b, lc, D_MODEL), lambda i: (0, 0, 0)),
                  pl.BlockSpec((8, 3 * D_MODEL), lambda i: (0, 0)),
                  pl.BlockSpec((1, D_MODEL), lambda i: (0, 0)),
                  pl.BlockSpec((D_MODEL, nw), lambda i: (0, 0)),
                  pl.BlockSpec((2, B_GATE_RANK, B_KEY_WIDTH), lambda i: (0, 0, 0)),
                  pl.BlockSpec((2, B_KEY_WIDTH), lambda i: (0, 0))],
        out_specs=[st_spec, st_spec],
        out_shape=[st, st],
        name="ctx",
    )(ctx, mod, ng, w_kvl, w2, gb)


def _latent_h(x, mod_ref, ng_ref):
    row = pl.ds(pl.program_id(0), 1)
    shift = mod_ref[row, 0:D_MODEL]
    gain = ng_ref[...] * (1.0 + mod_ref[row, D_MODEL:2 * D_MODEL])
    return _rms_rows(x) * gain + shift


def _pre_kernel(x_ref, mod_ref, ng_ref, w_ref, w2_ref, gb_ref, lg_ref, lb_ref, s0b_ref,
                wt_ref, wpa_ref, wpb_ref, wo_ref,
                qdf_ref, kdf_ref, kcf_ref, qdb_ref, kdb_ref, kcb_ref, v_ref,
                totf_ref, totb_ref, kvf_ref, sb_ref, vr_ref, vc_ref,
                wmain_ref, wpa16_ref, wpb16_ref, wo16_ref, sb_scr):
    @pl.when(pl.program_id(1) == 0)
    def _():
        sb_scr[...] = s0b_ref[0]

    h16 = _latent_h(x_ref[0], mod_ref, ng_ref).astype(BF16)
    nblk = x_ref.shape[1] // GLA_BLOCK
    blocks = [slice(blk * GLA_BLOCK, (blk + 1) * GLA_BLOCK) for blk in range(nblk)]
    dirs = ((False, qdf_ref, kdf_ref, kcf_ref, totf_ref), (True, qdb_ref, kdb_ref, kcb_ref, totb_ref))

    lrk = jnp.dot(h16, w_ref[:, P_LR:P_K + B_KEY_WIDTH], preferred_element_type=F32)
    lr = lrk[:, 0:2 * B_GATE_RANK]
    k = lrk[:, P_K:P_K + B_KEY_WIDTH]
    logs = _gate_logs(lr, w2_ref, gb_ref)
    q = jnp.dot(h16, w_ref[:, P_Q:P_Q + B_KEY_WIDTH], preferred_element_type=F32)

    _put_transposed(wmain_ref, 0, wt_ref, ZB0, VA0 - ZB0)
    _put_transposed(wmain_ref, VA0 - ZB0, wt_ref, ZA0, IN_WIDTH - ZA0)
    wpa16_ref[...] = wpa_ref[0].astype(BF16)
    wpb16_ref[...] = wpb_ref[0].astype(BF16)
    wo16_ref[...] = wo_ref[0].astype(BF16)

    v16 = jnp.dot(h16, w_ref[:, P_V:P_V + B_VAL_WIDTH], preferred_element_type=F32).astype(BF16)
    v_ref[0] = v16

    cums = {}
    for d, (reverse, _, _, _, tot_ref) in enumerate(dirs):
        tri = _chunk_tri(reverse)
        for blk, rs in enumerate(blocks):
            cum, tots = _block_cum(logs[d][rs], tri, reverse)
            cums[d, blk] = (cum, tots)
            for c in range(GLA_NC):
                tot_ref[0, blk, c:c + 1, :] = tots[c]

    va = jnp.dot(h16, w_ref[:, P_VA:P_VA + A_WIDTH], preferred_element_type=F32)
    vc = va - jnp.mean(va, axis=-1, keepdims=True)
    vn = vc * lax.rsqrt(jnp.mean(vc * vc, axis=-1, keepdims=True) + EPS) * lg_ref[...] + lb_ref[...]
    nr = A_ROW_GROUPS * A_GROUP_DIM
    vr_ref[0] = vn[:, 0:nr].astype(BF16)
    for g in range(A_GROUPS - A_ROW_GROUPS):
        vg = vn[:, nr + g * A_GROUP_DIM:nr + (g + 1) * A_GROUP_DIM]
        vc_ref[0, g] = jnp.swapaxes(vg.reshape(vg.shape[0] // GRID_W, GRID_W, A_GROUP_DIM), 0, 1).astype(BF16)

    kv_b = {}
    for d, (reverse, qd_ref, kd_ref, kc_ref, _) in enumerate(dirs):
        for blk, rs in enumerate(blocks):
            cum, tots = cums[d, blk]
            kdec, kblk = _block_keys(k[rs], cum, tots, reverse)
            kc_ref[0, rs, :] = kdec.astype(BF16)
            kd_ref[0, rs, :] = (k[rs] * jnp.exp2(-cum)).astype(BF16)
            qd_ref[0, rs, :] = (q[rs] * jnp.exp2(cum)).astype(BF16)
            if reverse:
                kv_b[blk] = _block_kv_t(v16[rs], kblk)
            else:
                kvf_ref[0, blk] = _block_kv_t(v16[rs], kblk)

    s = sb_scr[...]
    for blk in reversed(range(nblk)):
        sb_ref[0, blk] = s.T.astype(BF16)
        s = jnp.exp2(_sum_rows(cums[1, blk][1])) * s + kv_b[blk]
    sb_scr[...] = s


def _pre_call(x, mod, ng, w_pre, w2, gb, lg, lb, s0b, w_in_t, w_proj_a, w_proj_b, w_out, t):
    b, l, _ = x.shape
    nw = w_pre.shape[1]
    ncg = A_GROUPS - A_ROW_GROUPS
    nblk = t // GLA_BLOCK
    nj = l // t
    n_main = (VA0 - ZB0) + (IN_WIDTH - ZA0)
    assert D_MODEL % nj == 0 and (D_MODEL // nj) % LANES == 0, "one weight strip per sequence step"
    strip = lambda i, j: jnp.where(i == 0, j, nj - 1)

    def wrows(a):
        return pl.BlockSpec((1, a.shape[1] // nj, a.shape[2]), lambda i, j: (0, strip(i, j), 0))

    def wout(nrows, ncols):
        return (pl.BlockSpec((nrows // nj, ncols), lambda i, j: (strip(i, j), 0)),
                jax.ShapeDtypeStruct((nrows, ncols), BF16))
    rev3 = lambda i, j: (i, nj - 1 - j, 0)
    rev4 = lambda i, j: (i, nj - 1 - j, 0, 0)

    def tok(width, dtype):
        return (pl.BlockSpec((1, t, width), rev3), jax.ShapeDtypeStruct((b, l, width), dtype))

    tot = (pl.BlockSpec((1, nblk, GLA_NC, B_KEY_WIDTH), rev4),
           jax.ShapeDtypeStruct((b, l // GLA_BLOCK, GLA_NC, B_KEY_WIDTH), F32))
    kv = (pl.BlockSpec((1, nblk, B_DV, B_KEY_WIDTH), rev4),
          jax.ShapeDtypeStruct((b, l // GLA_BLOCK, B_DV, B_KEY_WIDTH), F32))
    st = (pl.BlockSpec((1, nblk, B_KEY_WIDTH, B_DV), rev4),
          jax.ShapeDtypeStruct((b, l // GLA_BLOCK, B_KEY_WIDTH, B_DV), BF16))
    outs = [tok(B_KEY_WIDTH, BF16)] * 6 + [tok(B_VAL_WIDTH, BF16), tot, tot, kv, st,
                                           tok(A_ROW_GROUPS * A_GROUP_DIM, BF16),
                                           (pl.BlockSpec((1, ncg, GRID_W, t // GRID_W, A_GROUP_DIM),
                                                         lambda i, j: (i, 0, 0, nj - 1 - j, 0)),
                                            jax.ShapeDtypeStruct((b, ncg, GRID_W, l // GRID_W, A_GROUP_DIM), BF16)),
                                           wout(D_MODEL, n_main), wout(A_WIDTH, D_MODEL),
                                           wout(B_VAL_WIDTH, D_MODEL), wout(D_MODEL, D_MODEL)]
    const2 = lambda i, j: (0, 0)
    return pl.pallas_call(
        _pre_kernel,
        grid=(b, nj),
        in_specs=[pl.BlockSpec((1, t, D_MODEL), rev3),
                  pl.BlockSpec((8, 3 * D_MODEL), const2),
                  pl.BlockSpec((1, D_MODEL), const2),
                  pl.BlockSpec((D_MODEL, nw), const2),
                  pl.BlockSpec((2, B_GATE_RANK, B_KEY_WIDTH), lambda i, j: (0, 0, 0)),
                  pl.BlockSpec((2, B_KEY_WIDTH), const2),
                  pl.BlockSpec((1, A_WIDTH), const2),
                  pl.BlockSpec((1, A_WIDTH), const2),
                  pl.BlockSpec((1, B_DV, B_KEY_WIDTH), lambda i, j: (i, 0, 0)),
                  pl.BlockSpec((IN_WIDTH, D_MODEL // nj), lambda i, j: (0, strip(i, j))),
                  wrows(w_proj_a), wrows(w_proj_b), wrows(w_out)],
        out_specs=[o[0] for o in outs],
        out_shape=[o[1] for o in outs],
        scratch_shapes=[pltpu.VMEM((B_DV, B_KEY_WIDTH), F32)],
        compiler_params=pltpu.CompilerParams(
            dimension_semantics=("arbitrary", "arbitrary"), vmem_limit_bytes=VMEM_LIMIT_BYTES),
        name="pre",
    )(x, mod, ng, w_pre, w2, gb, lg, lb, s0b, w_in_t, w_proj_a, w_proj_b, w_out)


def _scale_rows(x16, scales):
    parts = []
    for c, s in enumerate(scales):
        xc = x16[c * B_CHUNK:(c + 1) * B_CHUNK, :]
        parts.append(xc if s is None else (xc.astype(F32) * s).astype(BF16))
    return parts[0] if len(parts) == 1 else jnp.concatenate(parts, axis=0)


def _exp_sum(tots, idx):
    return jnp.exp2(_sum_rows([tots[m] for m in idx])) if idx else None


def _gla_block(qdf, kdf, kcf, qdb, kdb, kcb, v16, tf, tb, sf, sb, p_scr):
    nc = GLA_NC
    ch = B_CHUNK
    lane = lax.broadcasted_iota(jnp.int32, (1, LANES), 1)
    hm = [(lane < B_DK).astype(BF16), (lane >= B_DK).astype(BF16)]

    def pair_heads(x):
        return jnp.concatenate([x * hm[0], x * hm[1]], axis=0)

    row = lax.broadcasted_iota(jnp.int32, (ch, 2 * ch), 0)
    col = lax.broadcasted_iota(jnp.int32, (ch, 2 * ch), 1)
    first = col < ch
    zeros = jnp.zeros((ch, LANES), BF16)
    for g in range(B_HEADS // 2):
        ls = slice(g * LANES, (g + 1) * LANES)
        for i in range(nc // 2):
            pr = slice(2 * i * ch, (2 * i + 2) * ch)
            re = slice(2 * i * ch, (2 * i + 1) * ch)
            ro = slice((2 * i + 1) * ch, (2 * i + 2) * ch)
            sc_f = _dot_nt(pair_heads(qdf[pr, ls]), jnp.concatenate([kdf[pr, ls], kcf[re, ls], zeros], axis=0))
            sc_b = _dot_nt(pair_heads(qdb[pr, ls]), jnp.concatenate([kdb[pr, ls], zeros, kcb[ro, ls]], axis=0))
            for hh in range(2):
                he = slice(hh * 2 * ch, hh * 2 * ch + ch)
                ho = slice(hh * 2 * ch + ch, (hh + 1) * 2 * ch)
                even = jnp.where(first,
                                 jnp.where(col <= row, sc_f[he, 0:2 * ch], 0.0)
                                 + jnp.where(col >= row, sc_b[he, 0:2 * ch], 0.0),
                                 sc_b[he, 2 * ch:])
                odd = jnp.where(first, sc_f[ho, 2 * ch:],
                                jnp.where(col - ch <= row, sc_f[ho, 0:2 * ch], 0.0)
                                + jnp.where(col - ch >= row, sc_b[ho, 0:2 * ch], 0.0))
                p_scr[2 * g + hh, re, pr] = even.astype(BF16)
                p_scr[2 * g + hh, ro, pr] = odd.astype(BF16)

    def cross(lo, hi):
        if hi - lo <= 2:
            return
        mid = (lo + hi) // 2
        cross(lo, mid)
        cross(mid, hi)
        rl = slice(lo * ch, mid * ch)
        rh = slice(mid * ch, hi * ch)
        n = (hi - mid) * ch
        qf = _scale_rows(qdf[rh], [_exp_sum(tf, range(mid, c)) for c in range(mid, hi)])
        kf = _scale_rows(kcf[rl], [_exp_sum(tf, range(c + 1, mid)) for c in range(lo, mid)])
        qb = _scale_rows(qdb[rl], [_exp_sum(tb, range(c + 1, mid)) for c in range(lo, mid)])
        kb = _scale_rows(kcb[rh], [_exp_sum(tb, range(mid, c)) for c in range(mid, hi)])
        for g in range(B_HEADS // 2):
            ls = slice(g * LANES, (g + 1) * LANES)
            sc_f = _dot_nt(pair_heads(qf[:, ls]), kf[:, ls]).astype(BF16)
            sc_b = _dot_nt(pair_heads(qb[:, ls]), kb[:, ls]).astype(BF16)
            for hh in range(2):
                p_scr[2 * g + hh, rh, rl] = sc_f[hh * n:(hh + 1) * n]
                p_scr[2 * g + hh, rl, rh] = sc_b[hh * n:(hh + 1) * n]

    cross(0, nc)

    qsf = _scale_rows(qdf, [_exp_sum(tf, range(0, c)) for c in range(nc)])
    qsb = _scale_rows(qdb, [_exp_sum(tb, range(c + 1, nc)) for c in range(nc)])
    head_row = lax.broadcasted_iota(jnp.int32, (LANES, 1), 0) // B_DK
    outs = []
    for h in range(B_HEADS):
        ls = slice((h // 2) * LANES, (h // 2 + 1) * LANES)
        own = (head_row == (h % 2)).astype(BF16)
        lhs = jnp.concatenate([p_scr[h], qsf[:, ls], qsb[:, ls]], axis=1)
        rhs = jnp.concatenate([v16[:, h * B_DV:(h + 1) * B_DV], sf[ls, :] * own, sb[ls, :] * own], axis=0)
        outs.append(jnp.dot(lhs, rhs, preferred_element_type=F32))
    return outs


def _colmix_kernel(ws_ref, bs_ref, vn_ref, o_ref):
    nw = vn_ref.shape[2]
    xs = jnp.concatenate([vn_ref[0, 0, w] for w in range(nw)], axis=1)
    y = _dot(ws_ref[0], xs) + bs_ref[0][:, 0:1]
    for w in range(nw):
        o_ref[0, 0, w] = y[:, w * A_GROUP_DIM:(w + 1) * A_GROUP_DIM]


def _colmix_call(ws, bs, vn_col, nw):
    b, g, width, rows, ch = vn_col.shape
    blk = pl.BlockSpec((1, 1, nw, rows, ch), lambda i, j, m: (i, j, m, 0, 0))
    return pl.pallas_call(
        _colmix_kernel,
        grid=(b, g, width // nw),
        in_specs=[pl.BlockSpec((1, rows, rows), lambda i, j, m: (j + A_ROW_GROUPS, 0, 0)),
                  pl.BlockSpec((1, rows, A_GROUP_DIM), lambda i, j, m: (j + A_ROW_GROUPS, 0, 0)),
                  blk],
        out_specs=blk,
        out_shape=jax.ShapeDtypeStruct(vn_col.shape, F32),
        compiler_params=pltpu.CompilerParams(
            dimension_semantics=("parallel", "parallel", "parallel"), vmem_limit_bytes=VMEM_LIMIT_BYTES),
        name="colmix",
    )(ws, bs, vn_col)


def _main_kernel(x_ref, mod_ref, ng_ref, w_ref, vr_ref, sc_ref,
                 qdf_ref, kdf_ref, kcf_ref, qdb_ref, kdb_ref, kcb_ref, v_ref, totf_ref, totb_ref,
                 kvf_ref, s0f_ref, sb_ref,
                 ws_ref, bs_ref, bg_ref, wpa_ref, wpb_ref, wo_ref, fg_ref, o_ref, acta_scr, on_scr, p_scr, sf_scr):
    t = x_ref.shape[1]
    blocks = [slice(blk * GLA_BLOCK, (blk + 1) * GLA_BLOCK) for blk in range(t // GLA_BLOCK)]

    @pl.when(pl.program_id(1) == 0)
    def _():
        sf_scr[...] = s0f_ref[0]

    s = sf_scr[...]
    sf_in = []
    for blk in range(len(blocks)):
        sf_in.append(s.T.astype(BF16))
        s = jnp.exp2(_sum_rows([totf_ref[0, blk, c:c + 1, :] for c in range(GLA_NC)])) * s + kvf_ref[0, blk]
    sf_scr[...] = s

    def gla(blk):
        rs = blocks[blk]
        tf = [totf_ref[0, blk, c:c + 1, :] for c in range(GLA_NC)]
        tb = [totb_ref[0, blk, c:c + 1, :] for c in range(GLA_NC)]
        o_heads = _gla_block(qdf_ref[0, rs, :], kdf_ref[0, rs, :], kcf_ref[0, rs, :],
                             qdb_ref[0, rs, :], kdb_ref[0, rs, :], kcb_ref[0, rs, :], v_ref[0, rs, :],
                             tf, tb, sf_in[blk], sb_ref[0, blk], p_scr.at[blk])
        for hd in range(B_HEADS):
            on_scr[rs, hd * B_DV:(hd + 1) * B_DV] = _rms_rows(o_heads[hd])

    def proj(c0, n):
        return jnp.dot(h16, w_ref[:, c0:c0 + n], preferred_element_type=F32)

    gla(0)
    h16 = _latent_h(x_ref[0], mod_ref, ng_ref).astype(BF16)
    zb = proj(0, B_VAL_WIDTH)
    u = proj(B_VAL_WIDTH, A_WIDTH)
    za = proj(B_VAL_WIDTH + A_WIDTH, A_WIDTH)
    for blk in range(1, len(blocks)):
        gla(blk)
    gate_a = jax.nn.sigmoid(proj(B_VAL_WIDTH + 2 * A_WIDTH, D_MODEL))

    uz = u * _silu(za)
    for g in range(A_GROUPS):
        cs = slice(g * A_GROUP_DIM, (g + 1) * A_GROUP_DIM)
        if g < A_ROW_GROUPS:
            chunks = [slice(c * A_CHUNK, (c + 1) * A_CHUNK) for c in range(t // A_CHUNK)]
            sv_all = jnp.dot(ws_ref[g].astype(BF16), jnp.concatenate([vr_ref[0, rs, cs] for rs in chunks], axis=1),
                             preferred_element_type=F32)
            for c, rs in enumerate(chunks):
                sv = sv_all[:, c * A_GROUP_DIM:(c + 1) * A_GROUP_DIM] + bs_ref[g][:, 0:1]
                acta_scr[rs, cs] = (uz[rs, cs] * sv).astype(BF16)
        else:
            sv = jnp.swapaxes(sc_ref[0, g - A_ROW_GROUPS], 0, 1).reshape(t, A_GROUP_DIM)
            acta_scr[:, cs] = (uz[:, cs] * sv).astype(BF16)
    ya = jnp.dot(acta_scr[...], wpa_ref[...], preferred_element_type=F32)

    gate_b = jax.nn.sigmoid(proj(B_VAL_WIDTH + 2 * A_WIDTH + D_MODEL, D_MODEL))
    yb = _dot(on_scr[...] * (_silu(zb) * bg_ref[...]), wpb_ref[...])

    m = (gate_a * ya + gate_b * yb).astype(BF16)
    for rs in blocks:
        y = jnp.dot(m[rs], wo_ref[...], preferred_element_type=F32)
        xo = x_ref[0, rs, :] + mod_ref[pl.ds(pl.program_id(0), 1), 2 * D_MODEL:] * y
        o_ref[0, rs, :] = _rms_rows(xo) * fg_ref[...]


def _main_call(x, mod, ng, w_main, vn_row, sv_col, gla_ops, ws, bs, bg, wpa, wpb, wo, fg, t):
    b, l, _ = x.shape
    nw = w_main.shape[1]
    ncg = A_GROUPS - A_ROW_GROUPS
    const2 = lambda i, j: (0, 0)
    const3 = lambda i, j: (0, 0, 0)
    tok = lambda width: pl.BlockSpec((1, t, width), lambda i, j: (i, j, 0))
    nblk = t // GLA_BLOCK
    tot_spec = pl.BlockSpec((1, nblk, GLA_NC, B_KEY_WIDTH), lambda i, j: (i, j, 0, 0))
    st_spec = pl.BlockSpec((1, nblk, B_KEY_WIDTH, B_DV), lambda i, j: (i, j, 0, 0))
    kv_spec = pl.BlockSpec((1, nblk, B_DV, B_KEY_WIDTH), lambda i, j: (i, j, 0, 0))
    s0_spec = pl.BlockSpec((1, B_DV, B_KEY_WIDTH), lambda i, j: (i, 0, 0))
    return pl.pallas_call(
        _main_kernel,
        grid=(b, l // t),
        in_specs=[tok(D_MODEL),
                  pl.BlockSpec((8, 3 * D_MODEL), const2),
                  pl.BlockSpec((1, D_MODEL), const2),
                  pl.BlockSpec((D_MODEL, nw), const2),
                  tok(A_ROW_GROUPS * A_GROUP_DIM),
                  pl.BlockSpec((1, ncg, GRID_W, t // GRID_W, A_GROUP_DIM), lambda i, j: (i, 0, 0, j, 0)),
                  tok(B_KEY_WIDTH), tok(B_KEY_WIDTH), tok(B_KEY_WIDTH),
                  tok(B_KEY_WIDTH), tok(B_KEY_WIDTH), tok(B_KEY_WIDTH), tok(B_VAL_WIDTH),
                  tot_spec, tot_spec, kv_spec, s0_spec, st_spec,
                  pl.BlockSpec((A_GROUPS, A_CHUNK, A_CHUNK), const3),
                  pl.BlockSpec((A_GROUPS, A_CHUNK, A_GROUP_DIM), const3),
                  pl.BlockSpec((1, B_VAL_WIDTH), const2),
                  pl.BlockSpec((A_WIDTH, D_MODEL), const2),
                  pl.BlockSpec((B_VAL_WIDTH, D_MODEL), const2),
                  pl.BlockSpec((D_MODEL, D_MODEL), const2),
                  pl.BlockSpec((1, D_MODEL), const2)],
        out_specs=tok(D_MODEL),
        out_shape=jax.ShapeDtypeStruct((b, l, D_MODEL), F32),
        scratch_shapes=[pltpu.VMEM((t, A_WIDTH), BF16), pltpu.VMEM((t, B_VAL_WIDTH), F32),
                        pltpu.VMEM((nblk, B_HEADS, GLA_BLOCK, GLA_BLOCK), BF16),
                        pltpu.VMEM((B_DV, B_KEY_WIDTH), F32)],
        compiler_params=pltpu.CompilerParams(
            dimension_semantics=("parallel", "arbitrary"), vmem_limit_bytes=VMEM_LIMIT_BYTES),
        name="main",
    )(x, mod, ng, w_main, vn_row, sv_col, *gla_ops, ws, bs, bg, wpa, wpb, wo, fg)


def kernel(x, c, ctx, c_ctx, w_mod, b_mod, norm_g, w_in, a_ln_g, a_ln_b, a_ws, a_bs, b_gate_w2, b_gate_b,
           b_norm_g, w_proj_a, w_proj_b, w_out, final_norm_g):
    assert w_mod.shape[0] == 1, "single-layer block"
    b, l, _ = x.shape
    rows = l // GRID_W
    assert rows == A_CHUNK
    ng = norm_g[0][None, :]

    w_in_t = w_in[0].T
    w_pre, mod, bs_t = _wprep_call(w_in_t, c, c_ctx, w_mod, b_mod, a_bs)
    w2, gb = b_gate_w2[0], b_gate_b[0]
    s0f, s0b = _ctx_call(ctx, mod, ng, w_pre, w2, gb, b)

    (qd_f, kd_f, kc_f, qd_b, kd_b, kc_b, v16, tot_f, tot_b, kv_f, s_b, vn_row, vn_col,
     w_main, wpa16, wpb16, wo16) = _pre_call(
        x, mod, ng, w_pre, w2, gb, a_ln_g[0][None, :], a_ln_b[0][None, :], s0b,
        w_in_t, w_proj_a, w_proj_b, w_out, TOKEN_TILE)
    gla_ops = (qd_f, kd_f, kc_f, qd_b, kd_b, kc_b, v16, tot_f, tot_b, kv_f, s0f, s_b)

    sv_col = _colmix_call(a_ws[0], bs_t, vn_col, COLMIX_COLUMNS)

    return _main_call(x, mod, ng, w_main, vn_row, sv_col, gla_ops, a_ws[0], bs_t, b_norm_g[0][None, :],
                      wpa16, wpb16, wo16, final_norm_g[None, :], TOKEN_TILE)
```

```python
import functools

import jax
import jax.numpy as jnp
from jax import lax
from jax.experimental import pallas as pl
from jax.experimental.pallas import tpu as pltpu

D_MODEL = 1024
GRID_W = 64
EPS = 1e-6

A_WIDTH = 512
A_GROUPS = 4
A_GROUP_DIM = 128
A_CHUNK = 128
A_ROW_GROUPS = 2

B_HEADS = 4
B_DK = 64
B_DV = 128
B_KEY_WIDTH = 256
B_VAL_WIDTH = 512
B_GATE_RANK = 16
B_GATE_TAU = 16.0
LOG2E = 1.4426950408889634
B_CHUNK = 64

Q0 = 0
K0 = Q0 + B_KEY_WIDTH
V0 = K0 + B_KEY_WIDTH
LR0 = V0 + B_VAL_WIDTH
ZB0 = LR0 + 2 * B_GATE_RANK
UA0 = ZB0 + B_VAL_WIDTH
VA0 = UA0 + A_WIDTH
ZA0 = VA0 + A_WIDTH
G0 = ZA0 + A_WIDTH
IN_WIDTH = G0 + 2 * D_MODEL

LANES = 128

P_LR = 0
P_K = P_LR + LANES
P_Q = P_K + B_KEY_WIDTH
P_V = P_Q + B_KEY_WIDTH
P_VA = P_V + B_VAL_WIDTH
PRE_WIDTH = P_VA + A_WIDTH
GLA_BLOCK = 256
TOKEN_TILE = 1024
COLMIX_COLUMNS = 64
GLA_NC = GLA_BLOCK // B_CHUNK

VMEM_LIMIT_BYTES = 56 * 1024 * 1024

BF16 = jnp.bfloat16
F32 = jnp.float32


def _dot(a, b):
    return jnp.dot(a.astype(BF16), b.astype(BF16), preferred_element_type=F32)


def _dot_nt(a, b):
    return lax.dot_general(a.astype(BF16), b.astype(BF16), (((1,), (1,)), ((), ())),
                           preferred_element_type=F32)


def _dot_tn(a, b):
    return lax.dot_general(a.astype(BF16), b.astype(BF16), (((0,), (0,)), ((), ())),
                           preferred_element_type=F32)


def _silu(x):
    return x * jax.nn.sigmoid(x)


def _rms_rows(x):
    return x * lax.rsqrt(jnp.mean(x * x, axis=-1, keepdims=True) + EPS)


def _put_transposed(dst_ref, c0, wt_ref, r0, n, scale=None):
    for s in range(0, n, 2 * LANES):
        m = min(2 * LANES, n - s)
        blk = wt_ref[r0 + s:r0 + s + m, :].T
        dst_ref[:, c0 + s:c0 + s + m] = (blk if scale is None else blk * scale).astype(BF16)


def _wprep_kernel(wt_ref, c_ref, cctx_ref, wm_ref, bm_ref, abs_ref, wpre_ref, mod_ref, bst_ref):
    _put_transposed(wpre_ref, P_LR, wt_ref, LR0, LANES)
    _put_transposed(wpre_ref, P_K, wt_ref, K0, B_KEY_WIDTH)
    _put_transposed(wpre_ref, P_Q, wt_ref, Q0, B_KEY_WIDTH, B_DK ** -0.5)
    _put_transposed(wpre_ref, P_V, wt_ref, V0, B_VAL_WIDTH)
    _put_transposed(wpre_ref, P_VA, wt_ref, VA0, A_WIDTH)
    nb = c_ref.shape[0]
    cc = jnp.concatenate([c_ref[...], cctx_ref[...], jnp.zeros((8 - nb - 1, D_MODEL), F32)], axis=0)
    mod_ref[...] = _dot(_silu(cc), wm_ref[0]) + bm_ref[...]
    g = pl.program_id(0)
    bst_ref[0] = jnp.broadcast_to(abs_ref[0, pl.ds(g, 1), :], (A_GROUP_DIM, A_CHUNK)).T


def _wprep_call(w_in_t, c, c_ctx, w_mod, b_mod, a_bs):
    steps = A_GROUPS
    n_mod = w_mod.shape[2]
    outs = [(pl.BlockSpec((D_MODEL // steps, PRE_WIDTH), lambda i: (i, 0)),
             jax.ShapeDtypeStruct((D_MODEL, PRE_WIDTH), BF16)),
            (pl.BlockSpec((8, n_mod // steps), lambda i: (0, i)), jax.ShapeDtypeStruct((8, n_mod), F32)),
            (pl.BlockSpec((1, A_CHUNK, A_GROUP_DIM), lambda i: (i, 0, 0)),
             jax.ShapeDtypeStruct((A_GROUPS, A_CHUNK, A_GROUP_DIM), F32))]
    return pl.pallas_call(
        _wprep_kernel,
        grid=(steps,),
        in_specs=[pl.BlockSpec((VA0 + A_WIDTH, D_MODEL // steps), lambda i: (0, i)),
                  pl.BlockSpec(c.shape, lambda i: (0, 0)),
                  pl.BlockSpec((1, D_MODEL), lambda i: (0, 0)),
                  pl.BlockSpec((1, D_MODEL, n_mod // steps), lambda i: (0, 0, i)),
                  pl.BlockSpec((1, n_mod // steps), lambda i: (0, i)),
                  pl.BlockSpec(a_bs.shape, lambda i: (0, 0, 0))],
        out_specs=[o[0] for o in outs],
        out_shape=[o[1] for o in outs],
        compiler_params=pltpu.CompilerParams(
            dimension_semantics=("parallel",), vmem_limit_bytes=VMEM_LIMIT_BYTES),
        name="wprep",
    )(w_in_t, c, c_ctx[None, :], w_mod, b_mod, a_bs)


def _gate_logs(lr, w2_ref, gb_ref):
    out = []
    for r in range(2):
        logits = _dot(lr[:, r * B_GATE_RANK:(r + 1) * B_GATE_RANK], w2_ref[r]) + gb_ref[r:r + 1, :]
        log_sig = jnp.minimum(logits, 0.0) - jnp.log(1.0 + jnp.exp(-jnp.abs(logits)))
        out.append(log_sig * (LOG2E / B_GATE_TAU))
    return out


def _chunk_tri(reverse):
    i = lax.broadcasted_iota(jnp.int32, (GLA_BLOCK, GLA_BLOCK), 0)
    j = lax.broadcasted_iota(jnp.int32, (GLA_BLOCK, GLA_BLOCK), 1)
    same = (i // B_CHUNK) == (j // B_CHUNK)
    tri = (j >= i) if reverse else (j <= i)
    return (same & tri).astype(BF16)


def _sum_rows(rows):
    acc = rows[0]
    for r in rows[1:]:
        acc = acc + r
    return acc


def _block_cum(a, tri, reverse):
    hi = a.astype(BF16)
    lo = (a - hi.astype(F32)).astype(BF16)
    cum = (jnp.dot(tri, hi, preferred_element_type=F32) + jnp.dot(tri, lo, preferred_element_type=F32))
    last = 0 if reverse else B_CHUNK - 1
    tots = [cum[c * B_CHUNK + last:c * B_CHUNK + last + 1, :] for c in range(GLA_NC)]
    return cum, tots


def _block_keys(k, cum, tots, reverse):
    totb = jnp.concatenate([jnp.broadcast_to(t, (B_CHUNK, B_KEY_WIDTH)) for t in tots], axis=0)
    kdec = k * jnp.exp2(totb - cum)
    later = []
    for c in range(GLA_NC):
        idx = list(range(0, c)) if reverse else list(range(c + 1, GLA_NC))
        if idx:
            later.append(jnp.broadcast_to(jnp.exp2(_sum_rows([tots[m] for m in idx])), (B_CHUNK, B_KEY_WIDTH)))
        else:
            later.append(jnp.ones((B_CHUNK, B_KEY_WIDTH), F32))
    return kdec, kdec * jnp.concatenate(later, axis=0)


def _pre_project(h16, w_ref, want_q):
    lr = jnp.dot(h16, w_ref[:, P_LR:P_LR + LANES], preferred_element_type=F32)[:, 0:2 * B_GATE_RANK]
    k = jnp.dot(h16, w_ref[:, P_K:P_K + B_KEY_WIDTH], preferred_element_type=F32)
    q = jnp.dot(h16, w_ref[:, P_Q:P_Q + B_KEY_WIDTH], preferred_element_type=F32) if want_q else None
    v = jnp.dot(h16, w_ref[:, P_V:P_V + B_VAL_WIDTH], preferred_element_type=F32)
    return lr, k, q, v


def _block_kv_t(v16, kblk):
    k16 = kblk.astype(BF16)
    lane = lax.broadcasted_iota(jnp.int32, (1, B_KEY_WIDTH), 1)
    acc = jnp.zeros((B_DV, B_KEY_WIDTH), F32)
    for h in range(B_HEADS):
        full = _dot_tn(v16[:, h * B_DV:(h + 1) * B_DV], k16)
        acc = acc + jnp.where((lane // B_DK) == h, full, 0.0)
    return acc


def _latent_h(x, mod_ref, ng_ref):
    row = pl.ds(pl.program_id(0), 1)
    shift = mod_ref[row, 0:D_MODEL]
    gain = ng_ref[...] * (1.0 + mod_ref[row, D_MODEL:2 * D_MODEL])
    return _rms_rows(x) * gain + shift


def _pre_kernel(x_ref, mod_ref, ng_ref, w_ref, w2_ref, gb_ref, lg_ref, lb_ref, ctx_ref,
                wt_ref, wpa_ref, wpb_ref, wo_ref,
                qdf_ref, kdf_ref, kcf_ref, qdb_ref, kdb_ref, kcb_ref, v_ref,
                totf_ref, totb_ref, kvf_ref, sb_ref, vr_ref, vc_ref,
                wmain_ref, wpa16_ref, wpb16_ref, wo16_ref, s0f_ref, sb_scr, *, ctx_row):
    @pl.when(pl.program_id(1) == 0)
    def _():
        shift = mod_ref[ctx_row:ctx_row + 1, 0:D_MODEL]
        scale = mod_ref[ctx_row:ctx_row + 1, D_MODEL:2 * D_MODEL]
        hc = _rms_rows(ctx_ref[0]) * ng_ref[...] * (1.0 + scale) + shift
        lr_c, k_c, _, v_c = _pre_project(hc.astype(BF16), w_ref, False)
        v16_c = v_c.astype(BF16)
        logs_c = _gate_logs(lr_c, w2_ref, gb_ref)
        for d, reverse in enumerate((False, True)):
            cum_c, tots_c = _block_cum(logs_c[d], _chunk_tri(reverse), reverse)
            _, kblk_c = _block_keys(k_c, cum_c, tots_c, reverse)
            state = _block_kv_t(v16_c, kblk_c)
            if reverse:
                sb_scr[...] = state
            else:
                s0f_ref[0] = state

    h16 = _latent_h(x_ref[0], mod_ref, ng_ref).astype(BF16)
    nblk = x_ref.shape[1] // GLA_BLOCK
    blocks = [slice(blk * GLA_BLOCK, (blk + 1) * GLA_BLOCK) for blk in range(nblk)]
    dirs = ((False, qdf_ref, kdf_ref, kcf_ref, totf_ref), (True, qdb_ref, kdb_ref, kcb_ref, totb_ref))

    lrk = jnp.dot(h16, w_ref[:, P_LR:P_K + B_KEY_WIDTH], preferred_element_type=F32)
    lr = lrk[:, 0:2 * B_GATE_RANK]
    k = lrk[:, P_K:P_K + B_KEY_WIDTH]
    logs = _gate_logs(lr, w2_ref, gb_ref)
    q = jnp.dot(h16, w_ref[:, P_Q:P_Q + B_KEY_WIDTH], preferred_element_type=F32)

    _put_transposed(wmain_ref, 0, wt_ref, ZB0, VA0 - ZB0)
    _put_transposed(wmain_ref, VA0 - ZB0, wt_ref, ZA0, IN_WIDTH - ZA0)
    wpa16_ref[...] = wpa_ref[0].astype(BF16)
    wpb16_ref[...] = wpb_ref[0].astype(BF16)
    wo16_ref[...] = wo_ref[0].astype(BF16)

    v16 = jnp.dot(h16, w_ref[:, P_V:P_V + B_VAL_WIDTH], preferred_element_type=F32).astype(BF16)
    v_ref[0] = v16

    cums = {}
    for d, (reverse, _, _, _, tot_ref) in enumerate(dirs):
        tri = _chunk_tri(reverse)
        for blk, rs in enumerate(blocks):
            cum, tots = _block_cum(logs[d][rs], tri, reverse)
            cums[d, blk] = (cum, tots)
            for c in range(GLA_NC):
                tot_ref[0, blk, c:c + 1, :] = tots[c]

    va = jnp.dot(h16, w_ref[:, P_VA:P_VA + A_WIDTH], preferred_element_type=F32)
    vc = va - jnp.mean(va, axis=-1, keepdims=True)
    vn = vc * lax.rsqrt(jnp.mean(vc * vc, axis=-1, keepdims=True) + EPS) * lg_ref[...] + lb_ref[...]
    nr = A_ROW_GROUPS * A_GROUP_DIM
    vr_ref[0] = vn[:, 0:nr].astype(BF16)
    for g in range(A_GROUPS - A_ROW_GROUPS):
        vg = vn[:, nr + g * A_GROUP_DIM:nr + (g + 1) * A_GROUP_DIM]
        vc_ref[0, g] = jnp.swapaxes(vg.reshape(vg.shape[0] // GRID_W, GRID_W, A_GROUP_DIM), 0, 1).astype(BF16)

    kv_b = {}
    for d, (reverse, qd_ref, kd_ref, kc_ref, _) in enumerate(dirs):
        for blk, rs in enumerate(blocks):
            cum, tots = cums[d, blk]
            kdec, kblk = _block_keys(k[rs], cum, tots, reverse)
            kc_ref[0, rs, :] = kdec.astype(BF16)
            kd_ref[0, rs, :] = (k[rs] * jnp.exp2(-cum)).astype(BF16)
            qd_ref[0, rs, :] = (q[rs] * jnp.exp2(cum)).astype(BF16)
            if reverse:
                kv_b[blk] = _block_kv_t(v16[rs], kblk)
            else:
                kvf_ref[0, blk] = _block_kv_t(v16[rs], kblk)

    s = sb_scr[...]
    for blk in reversed(range(nblk)):
        sb_ref[0, blk] = s.T.astype(BF16)
        s = jnp.exp2(_sum_rows(cums[1, blk][1])) * s + kv_b[blk]
    sb_scr[...] = s


def _pre_call(x, mod, ng, w_pre, w2, gb, lg, lb, ctx, w_in_t, w_proj_a, w_proj_b, w_out, t):
    b, l, _ = x.shape
    assert ctx.shape[1] == GLA_BLOCK, "context length must be one GLA block"
    nw = w_pre.shape[1]
    ncg = A_GROUPS - A_ROW_GROUPS
    nblk = t // GLA_BLOCK
    nj = l // t
    n_main = (VA0 - ZB0) + (IN_WIDTH - ZA0)
    assert D_MODEL % nj == 0 and (D_MODEL // nj) % LANES == 0, "one weight strip per sequence step"
    strip = lambda i, j: jnp.where(i == 0, j, nj - 1)

    def wrows(a):
        return pl.BlockSpec((1, a.shape[1] // nj, a.shape[2]), lambda i, j: (0, strip(i, j), 0))

    def wout(nrows, ncols):
        return (pl.BlockSpec((nrows // nj, ncols), lambda i, j: (strip(i, j), 0)),
                jax.ShapeDtypeStruct((nrows, ncols), BF16))
    rev3 = lambda i, j: (i, nj - 1 - j, 0)
    rev4 = lambda i, j: (i, nj - 1 - j, 0, 0)

    def tok(width, dtype):
        return (pl.BlockSpec((1, t, width), rev3), jax.ShapeDtypeStruct((b, l, width), dtype))

    tot = (pl.BlockSpec((1, nblk, GLA_NC, B_KEY_WIDTH), rev4),
           jax.ShapeDtypeStruct((b, l // GLA_BLOCK, GLA_NC, B_KEY_WIDTH), F32))
    kv = (pl.BlockSpec((1, nblk, B_DV, B_KEY_WIDTH), rev4),
          jax.ShapeDtypeStruct((b, l // GLA_BLOCK, B_DV, B_KEY_WIDTH), F32))
    st = (pl.BlockSpec((1, nblk, B_KEY_WIDTH, B_DV), rev4),
          jax.ShapeDtypeStruct((b, l // GLA_BLOCK, B_KEY_WIDTH, B_DV), BF16))
    outs = [tok(B_KEY_WIDTH, BF16)] * 6 + [tok(B_VAL_WIDTH, BF16), tot, tot, kv, st,
                                           tok(A_ROW_GROUPS * A_GROUP_DIM, BF16),
                                           (pl.BlockSpec((1, ncg, GRID_W, t // GRID_W, A_GROUP_DIM),
                                                         lambda i, j: (i, 0, 0, nj - 1 - j, 0)),
                                            jax.ShapeDtypeStruct((b, ncg, GRID_W, l // GRID_W, A_GROUP_DIM), BF16)),
                                           wout(D_MODEL, n_main), wout(A_WIDTH, D_MODEL),
                                           wout(B_VAL_WIDTH, D_MODEL), wout(D_MODEL, D_MODEL),
                                           (pl.BlockSpec((1, B_DV, B_KEY_WIDTH), lambda i, j: (i, 0, 0)),
                                            jax.ShapeDtypeStruct((b, B_DV, B_KEY_WIDTH), F32))]
    const2 = lambda i, j: (0, 0)
    return pl.pallas_call(
        functools.partial(_pre_kernel, ctx_row=b),
        grid=(b, nj),
        in_specs=[pl.BlockSpec((1, t, D_MODEL), rev3),
                  pl.BlockSpec((8, 3 * D_MODEL), const2),
                  pl.BlockSpec((1, D_MODEL), const2),
                  pl.BlockSpec((D_MODEL, nw), const2),
                  pl.BlockSpec((2, B_GATE_RANK, B_KEY_WIDTH), lambda i, j: (0, 0, 0)),
                  pl.BlockSpec((2, B_KEY_WIDTH), const2),
                  pl.BlockSpec((1, A_WIDTH), const2),
                  pl.BlockSpec((1, A_WIDTH), const2),
                  pl.BlockSpec((1, GLA_BLOCK, D_MODEL), lambda i, j: (i, 0, 0)),
                  pl.BlockSpec((IN_WIDTH, D_MODEL // nj), lambda i, j: (0, strip(i, j))),
                  wrows(w_proj_a), wrows(w_proj_b), wrows(w_out)],
        out_specs=[o[0] for o in outs],
        out_shape=[o[1] for o in outs],
        scratch_shapes=[pltpu.VMEM((B_DV, B_KEY_WIDTH), F32)],
        compiler_params=pltpu.CompilerParams(
            dimension_semantics=("arbitrary", "arbitrary"), vmem_limit_bytes=VMEM_LIMIT_BYTES),
        name="pre",
    )(x, mod, ng, w_pre, w2, gb, lg, lb, ctx, w_in_t, w_proj_a, w_proj_b, w_out)


def _scale_rows(x16, scales):
    parts = []
    for c, s in enumerate(scales):
        xc = x16[c * B_CHUNK:(c + 1) * B_CHUNK, :]
        parts.append(xc if s is None else (xc.astype(F32) * s).astype(BF16))
    return parts[0] if len(parts) == 1 else jnp.concatenate(parts, axis=0)


def _exp_sum(tots, idx):
    return jnp.exp2(_sum_rows([tots[m] for m in idx])) if idx else None


def _gla_block(qdf, kdf, kcf, qdb, kdb, kcb, v16, tf, tb, sf, sb, p_scr):
    nc = GLA_NC
    ch = B_CHUNK
    lane = lax.broadcasted_iota(jnp.int32, (1, LANES), 1)
    hm = [(lane < B_DK).astype(BF16), (lane >= B_DK).astype(BF16)]

    def pair_heads(x):
        return jnp.concatenate([x * hm[0], x * hm[1]], axis=0)

    row = lax.broadcasted_iota(jnp.int32, (ch, 2 * ch), 0)
    col = lax.broadcasted_iota(jnp.int32, (ch, 2 * ch), 1)
    first = col < ch
    zeros = jnp.zeros((ch, LANES), BF16)
    for g in range(B_HEADS // 2):
        ls = slice(g * LANES, (g + 1) * LANES)
        for i in range(nc // 2):
            pr = slice(2 * i * ch, (2 * i + 2) * ch)
            re = slice(2 * i * ch, (2 * i + 1) * ch)
            ro = slice((2 * i + 1) * ch, (2 * i + 2) * ch)
            sc_f = _dot_nt(pair_heads(qdf[pr, ls]), jnp.concatenate([kdf[pr, ls], kcf[re, ls], zeros], axis=0))
            sc_b = _dot_nt(pair_heads(qdb[pr, ls]), jnp.concatenate([kdb[pr, ls], zeros, kcb[ro, ls]], axis=0))
            for hh in range(2):
                he = slice(hh * 2 * ch, hh * 2 * ch + ch)
                ho = slice(hh * 2 * ch + ch, (hh + 1) * 2 * ch)
                even = jnp.where(first,
                                 jnp.where(col <= row, sc_f[he, 0:2 * ch], 0.0)
                                 + jnp.where(col >= row, sc_b[he, 0:2 * ch], 0.0),
                                 sc_b[he, 2 * ch:])
                odd = jnp.where(first, sc_f[ho, 2 * ch:],
                                jnp.where(col - ch <= row, sc_f[ho, 0:2 * ch], 0.0)
                                + jnp.where(col - ch >= row, sc_b[ho, 0:2 * ch], 0.0))
                p_scr[2 * g + hh, re, pr] = even.astype(BF16)
                p_scr[2 * g + hh, ro, pr] = odd.astype(BF16)

    def cross(lo, hi):
        if hi - lo <= 2:
            return
        mid = (lo + hi) // 2
        cross(lo, mid)
        cross(mid, hi)
        rl = slice(lo * ch, mid * ch)
        rh = slice(mid * ch, hi * ch)
        n = (hi - mid) * ch
        qf = _scale_rows(qdf[rh], [_exp_sum(tf, range(mid, c)) for c in range(mid, hi)])
        kf = _scale_rows(kcf[rl], [_exp_sum(tf, range(c + 1, mid)) for c in range(lo, mid)])
        qb = _scale_rows(qdb[rl], [_exp_sum(tb, range(c + 1, mid)) for c in range(lo, mid)])
        kb = _scale_rows(kcb[rh], [_exp_sum(tb, range(mid, c)) for c in range(mid, hi)])
        for g in range(B_HEADS // 2):
            ls = slice(g * LANES, (g + 1) * LANES)
            sc_f = _dot_nt(pair_heads(qf[:, ls]), kf[:, ls]).astype(BF16)
            sc_b = _dot_nt(pair_heads(qb[:, ls]), kb[:, ls]).astype(BF16)
            for hh in range(2):
                p_scr[2 * g + hh, rh, rl] = sc_f[hh * n:(hh + 1) * n]
                p_scr[2 * g + hh, rl, rh] = sc_b[hh * n:(hh + 1) * n]

    cross(0, nc)

    qsf = _scale_rows(qdf, [_exp_sum(tf, range(0, c)) for c in range(nc)])
    qsb = _scale_rows(qdb, [_exp_sum(tb, range(c + 1, nc)) for c in range(nc)])
    head_row = lax.broadcasted_iota(jnp.int32, (LANES, 1), 0) // B_DK
    outs = []
    for h in range(B_HEADS):
        ls = slice((h // 2) * LANES, (h // 2 + 1) * LANES)
        own = (head_row == (h % 2)).astype(BF16)
        lhs = jnp.concatenate([p_scr[h], qsf[:, ls], qsb[:, ls]], axis=1)
        rhs = jnp.concatenate([v16[:, h * B_DV:(h + 1) * B_DV], sf[ls, :] * own, sb[ls, :] * own], axis=0)
        outs.append(jnp.dot(lhs, rhs, preferred_element_type=F32))
    return outs


def _colmix_kernel(ws_ref, bs_ref, vn_ref, o_ref):
    nw = vn_ref.shape[2]
    xs = jnp.concatenate([vn_ref[0, 0, w] for w in range(nw)], axis=1)
    y = _dot(ws_ref[0], xs) + bs_ref[0][:, 0:1]
    for w in range(nw):
        o_ref[0, 0, w] = y[:, w * A_GROUP_DIM:(w + 1) * A_GROUP_DIM]


def _colmix_call(ws, bs, vn_col, nw):
    b, g, width, rows, ch = vn_col.shape
    blk = pl.BlockSpec((1, 1, nw, rows, ch), lambda i, j, m: (i, j, m, 0, 0))
    return pl.pallas_call(
        _colmix_kernel,
        grid=(b, g, width // nw),
        in_specs=[pl.BlockSpec((1, rows, rows), lambda i, j, m: (j + A_ROW_GROUPS, 0, 0)),
                  pl.BlockSpec((1, rows, A_GROUP_DIM), lambda i, j, m: (j + A_ROW_GROUPS, 0, 0)),
                  blk],
        out_specs=blk,
        out_shape=jax.ShapeDtypeStruct(vn_col.shape, F32),
        compiler_params=pltpu.CompilerParams(
            dimension_semantics=("parallel", "parallel", "parallel"), vmem_limit_bytes=VMEM_LIMIT_BYTES),
        name="colmix",
    )(ws, bs, vn_col)


def _main_kernel(x_ref, mod_ref, ng_ref, w_ref, vr_ref, sc_ref,
                 qdf_ref, kdf_ref, kcf_ref, qdb_ref, kdb_ref, kcb_ref, v_ref, totf_ref, totb_ref,
                 kvf_ref, s0f_ref, sb_ref,
                 ws_ref, bs_ref, bg_ref, wpa_ref, wpb_ref, wo_ref, fg_ref, o_ref, acta_scr, on_scr, p_scr, sf_scr):
    t = x_ref.shape[1]
    blocks = [slice(blk * GLA_BLOCK, (blk + 1) * GLA_BLOCK) for blk in range(t // GLA_BLOCK)]

    @pl.when(pl.program_id(1) == 0)
    def _():
        sf_scr[...] = s0f_ref[0]

    s = sf_scr[...]
    sf_in = []
    for blk in range(len(blocks)):
        sf_in.append(s.T.astype(BF16))
        s = jnp.exp2(_sum_rows([totf_ref[0, blk, c:c + 1, :] for c in range(GLA_NC)])) * s + kvf_ref[0, blk]
    sf_scr[...] = s

    def gla(blk):
        rs = blocks[blk]
        tf = [totf_ref[0, blk, c:c + 1, :] for c in range(GLA_NC)]
        tb = [totb_ref[0, blk, c:c + 1, :] for c in range(GLA_NC)]
        o_heads = _gla_block(qdf_ref[0, rs, :], kdf_ref[0, rs, :], kcf_ref[0, rs, :],
                             qdb_ref[0, rs, :], kdb_ref[0, rs, :], kcb_ref[0, rs, :], v_ref[0, rs, :],
                             tf, tb, sf_in[blk], sb_ref[0, blk], p_scr.at[blk])
        for hd in range(B_HEADS):
            on_scr[rs, hd * B_DV:(hd + 1) * B_DV] = _rms_rows(o_heads[hd])

    def proj(c0, n):
        return jnp.dot(h16, w_ref[:, c0:c0 + n], preferred_element_type=F32)

    gla(0)
    h16 = _latent_h(x_ref[0], mod_ref, ng_ref).astype(BF16)
    zb = proj(0, B_VAL_WIDTH)
    u = proj(B_VAL_WIDTH, A_WIDTH)
    za = proj(B_VAL_WIDTH + A_WIDTH, A_WIDTH)
    for blk in range(1, len(blocks)):
        gla(blk)
    gate_a = jax.nn.sigmoid(proj(B_VAL_WIDTH + 2 * A_WIDTH, D_MODEL))

    uz = u * _silu(za)
    for g in range(A_GROUPS):
        cs = slice(g * A_GROUP_DIM, (g + 1) * A_GROUP_DIM)
        if g < A_ROW_GROUPS:
            chunks = [slice(c * A_CHUNK, (c + 1) * A_CHUNK) for c in range(t // A_CHUNK)]
            sv_all = jnp.dot(ws_ref[g].astype(BF16), jnp.concatenate([vr_ref[0, rs, cs] for rs in chunks], axis=1),
                             preferred_element_type=F32)
            for c, rs in enumerate(chunks):
                sv = sv_all[:, c * A_GROUP_DIM:(c + 1) * A_GROUP_DIM] + bs_ref[g][:, 0:1]
                acta_scr[rs, cs] = (uz[rs, cs] * sv).astype(BF16)
        else:
            sv = jnp.swapaxes(sc_ref[0, g - A_ROW_GROUPS], 0, 1).reshape(t, A_GROUP_DIM)
            acta_scr[:, cs] = (uz[:, cs] * sv).astype(BF16)
    ya = jnp.dot(acta_scr[...], wpa_ref[...], preferred_element_type=F32)

    gate_b = jax.nn.sigmoid(proj(B_VAL_WIDTH + 2 * A_WIDTH + D_MODEL, D_MODEL))
    yb = _dot(on_scr[...] * (_silu(zb) * bg_ref[...]), wpb_ref[...])

    m = (gate_a * ya + gate_b * yb).astype(BF16)
    for rs in blocks:
        y = jnp.dot(m[rs], wo_ref[...], preferred_element_type=F32)
        xo = x_ref[0, rs, :] + mod_ref[pl.ds(pl.program_id(0), 1), 2 * D_MODEL:] * y
        o_ref[0, rs, :] = _rms_rows(xo) * fg_ref[...]


def _main_call(x, mod, ng, w_main, vn_row, sv_col, gla_ops, ws, bs, bg, wpa, wpb, wo, fg, t):
    b, l, _ = x.shape
    nw = w_main.shape[1]
    ncg = A_GROUPS - A_ROW_GROUPS
    const2 = lambda i, j: (0, 0)
    const3 = lambda i, j: (0, 0, 0)
    tok = lambda width: pl.BlockSpec((1, t, width), lambda i, j: (i, j, 0))
    nblk = t // GLA_BLOCK
    tot_spec = pl.BlockSpec((1, nblk, GLA_NC, B_KEY_WIDTH), lambda i, j: (i, j, 0, 0))
    st_spec = pl.BlockSpec((1, nblk, B_KEY_WIDTH, B_DV), lambda i, j: (i, j, 0, 0))
    kv_spec = pl.BlockSpec((1, nblk, B_DV, B_KEY_WIDTH), lambda i, j: (i, j, 0, 0))
    s0_spec = pl.BlockSpec((1, B_DV, B_KEY_WIDTH), lambda i, j: (i, 0, 0))
    return pl.pallas_call(
        _main_kernel,
        grid=(b, l // t),
        in_specs=[tok(D_MODEL),
                  pl.BlockSpec((8, 3 * D_MODEL), const2),
                  pl.BlockSpec((1, D_MODEL), const2),
                  pl.BlockSpec((D_MODEL, nw), const2),
                  tok(A_ROW_GROUPS * A_GROUP_DIM),
                  pl.BlockSpec((1, ncg, GRID_W, t // GRID_W, A_GROUP_DIM), lambda i, j: (i, 0, 0, j, 0)),
                  tok(B_KEY_WIDTH), tok(B_KEY_WIDTH), tok(B_KEY_WIDTH),
                  tok(B_KEY_WIDTH), tok(B_KEY_WIDTH), tok(B_KEY_WIDTH), tok(B_VAL_WIDTH),
                  tot_spec, tot_spec, kv_spec, s0_spec, st_spec,
                  pl.BlockSpec((A_GROUPS, A_CHUNK, A_CHUNK), const3),
                  pl.BlockSpec((A_GROUPS, A_CHUNK, A_GROUP_DIM), const3),
                  pl.BlockSpec((1, B_VAL_WIDTH), const2),
                  pl.BlockSpec((A_WIDTH, D_MODEL), const2),
                  pl.BlockSpec((B_VAL_WIDTH, D_MODEL), const2),
                  pl.BlockSpec((D_MODEL, D_MODEL), const2),
                  pl.BlockSpec((1, D_MODEL), const2)],
        out_specs=tok(D_MODEL),
        out_shape=jax.ShapeDtypeStruct((b, l, D_MODEL), F32),
        scratch_shapes=[pltpu.VMEM((t, A_WIDTH), BF16), pltpu.VMEM((t, B_VAL_WIDTH), F32),
                        pltpu.VMEM((nblk, B_HEADS, GLA_BLOCK, GLA_BLOCK), BF16),
                        pltpu.VMEM((B_DV, B_KEY_WIDTH), F32)],
        compiler_params=pltpu.CompilerParams(
            dimension_semantics=("parallel", "arbitrary"), vmem_limit_bytes=VMEM_LIMIT_BYTES),
        name="main",
    )(x, mod, ng, w_main, vn_row, sv_col, *gla_ops, ws, bs, bg, wpa, wpb, wo, fg)


def kernel(x, c, ctx, c_ctx, w_mod, b_mod, norm_g, w_in, a_ln_g, a_ln_b, a_ws, a_bs, b_gate_w2, b_gate_b,
           b_norm_g, w_proj_a, w_proj_b, w_out, final_norm_g):
    assert w_mod.shape[0] == 1, "single-layer block"
    b, l, _ = x.shape
    rows = l // GRID_W
    assert rows == A_CHUNK
    ng = norm_g[0][None, :]

    w_in_t = w_in[0].T
    w_pre, mod, bs_t = _wprep_call(w_in_t, c, c_ctx, w_mod, b_mod, a_bs)
    w2, gb = b_gate_w2[0], b_gate_b[0]

    (qd_f, kd_f, kc_f, qd_b, kd_b, kc_b, v16, tot_f, tot_b, kv_f, s_b, vn_row, vn_col,
     w_main, wpa16, wpb16, wo16, s0f) = _pre_call(
        x, mod, ng, w_pre, w2, gb, a_ln_g[0][None, :], a_ln_b[0][None, :], ctx,
        w_in_t, w_proj_a, w_proj_b, w_out, TOKEN_TILE)
    gla_ops = (qd_f, kd_f, kc_f, qd_b, kd_b, kc_b, v16, tot_f, tot_b, kv_f, s0f, s_b)

    sv_col = _colmix_call(a_ws[0], bs_t, vn_col, COLMIX_COLUMNS)

    return _main_call(x, mod, ng, w_main, vn_row, sv_col, gla_ops, a_ws[0], bs_t, b_norm_g[0][None, :],
                      wpa16, wpb16, wo16, final_norm_g[None, :], TOKEN_TILE)
```

```python
import functools

import jax
import jax.numpy as jnp
from jax import lax
from jax.experimental import pallas as pl
from jax.experimental.pallas import tpu as pltpu

D_MODEL = 1024
GRID_W = 64
EPS = 1e-6

A_WIDTH = 512
A_GROUPS = 4
A_GROUP_DIM = 128
A_CHUNK = 128
A_ROW_GROUPS = 2

B_HEADS = 4
B_DK = 64
B_DV = 128
B_KEY_WIDTH = 256
B_VAL_WIDTH = 512
B_GATE_RANK = 16
B_GATE_TAU = 16.0
LOG2E = 1.4426950408889634
B_CHUNK = 64

Q0 = 0
K0 = Q0 + B_KEY_WIDTH
V0 = K0 + B_KEY_WIDTH
LR0 = V0 + B_VAL_WIDTH
ZB0 = LR0 + 2 * B_GATE_RANK
UA0 = ZB0 + B_VAL_WIDTH
VA0 = UA0 + A_WIDTH
ZA0 = VA0 + A_WIDTH
G0 = ZA0 + A_WIDTH
IN_WIDTH = G0 + 2 * D_MODEL

LANES = 128

P_LR = 0
P_K = P_LR + LANES
P_Q = P_K + B_KEY_WIDTH
P_V = P_Q + B_KEY_WIDTH
P_VA = P_V + B_VAL_WIDTH
PRE_WIDTH = P_VA + A_WIDTH
GLA_BLOCK = 256
TOKEN_TILE = 1024
COLMIX_COLUMNS = 16
GLA_NC = GLA_BLOCK // B_CHUNK

VMEM_LIMIT_BYTES = 56 * 1024 * 1024

BF16 = jnp.bfloat16
F32 = jnp.float32


def _dot(a, b):
    return jnp.dot(a.astype(BF16), b.astype(BF16), preferred_element_type=F32)


def _dot_nt(a, b):
    return lax.dot_general(a.astype(BF16), b.astype(BF16), (((1,), (1,)), ((), ())),
                           preferred_element_type=F32)


def _dot_tn(a, b):
    return lax.dot_general(a.astype(BF16), b.astype(BF16), (((0,), (0,)), ((), ())),
                           preferred_element_type=F32)


def _silu(x):
    return x * jax.nn.sigmoid(x)


def _rms_rows(x):
    return x * lax.rsqrt(jnp.mean(x * x, axis=-1, keepdims=True) + EPS)


def _put_transposed(dst_ref, c0, wt_ref, r0, n, scale=None):
    for s in range(0, n, 2 * LANES):
        m = min(2 * LANES, n - s)
        blk = wt_ref[r0 + s:r0 + s + m, :].T
        dst_ref[:, c0 + s:c0 + s + m] = (blk if scale is None else blk * scale).astype(BF16)


def _wprep_kernel(wt_ref, c_ref, cctx_ref, wm_ref, bm_ref, abs_ref, wpre_ref, mod_ref, bst_ref):
    _put_transposed(wpre_ref, P_LR, wt_ref, LR0, LANES)
    _put_transposed(wpre_ref, P_K, wt_ref, K0, B_KEY_WIDTH)
    _put_transposed(wpre_ref, P_Q, wt_ref, Q0, B_KEY_WIDTH, B_DK ** -0.5)
    _put_transposed(wpre_ref, P_V, wt_ref, V0, B_VAL_WIDTH)
    _put_transposed(wpre_ref, P_VA, wt_ref, VA0, A_WIDTH)
    nb = c_ref.shape[0]
    cc = jnp.concatenate([c_ref[...], cctx_ref[...], jnp.zeros((8 - nb - 1, D_MODEL), F32)], axis=0)
    mod_ref[...] = _dot(_silu(cc), wm_ref[0]) + bm_ref[...]
    g = pl.program_id(0)
    bst_ref[0] = jnp.broadcast_to(abs_ref[0, pl.ds(g, 1), :], (A_GROUP_DIM, A_CHUNK)).T


def _wprep_call(w_in_t, c, c_ctx, w_mod, b_mod, a_bs):
    steps = A_GROUPS
    n_mod = w_mod.shape[2]
    outs = [(pl.BlockSpec((D_MODEL // steps, PRE_WIDTH), lambda i: (i, 0)),
             jax.ShapeDtypeStruct((D_MODEL, PRE_WIDTH), BF16)),
            (pl.BlockSpec((8, n_mod // steps), lambda i: (0, i)), jax.ShapeDtypeStruct((8, n_mod), F32)),
            (pl.BlockSpec((1, A_CHUNK, A_GROUP_DIM), lambda i: (i, 0, 0)),
             jax.ShapeDtypeStruct((A_GROUPS, A_CHUNK, A_GROUP_DIM), F32))]
    return pl.pallas_call(
        _wprep_kernel,
        grid=(steps,),
        in_specs=[pl.BlockSpec((VA0 + A_WIDTH, D_MODEL // steps), lambda i: (0, i)),
                  pl.BlockSpec(c.shape, lambda i: (0, 0)),
                  pl.BlockSpec((1, D_MODEL), lambda i: (0, 0)),
                  pl.BlockSpec((1, D_MODEL, n_mod // steps), lambda i: (0, 0, i)),
                  pl.BlockSpec((1, n_mod // steps), lambda i: (0, i)),
                  pl.BlockSpec(a_bs.shape, lambda i: (0, 0, 0))],
        out_specs=[o[0] for o in outs],
        out_shape=[o[1] for o in outs],
        compiler_params=pltpu.CompilerParams(
            dimension_semantics=("parallel",), vmem_limit_bytes=VMEM_LIMIT_BYTES),
        name="wprep",
    )(w_in_t, c, c_ctx[None, :], w_mod, b_mod, a_bs)


def _gate_logs(lr, w2_ref, gb_ref):
    out = []
    for r in range(2):
        logits = _dot(lr[:, r * B_GATE_RANK:(r + 1) * B_GATE_RANK], w2_ref[r]) + gb_ref[r:r + 1, :]
        log_sig = jnp.minimum(logits, 0.0) - jnp.log(1.0 + jnp.exp(-jnp.abs(logits)))
        out.append(log_sig * (LOG2E / B_GATE_TAU))
    return out


def _chunk_tri(reverse):
    i = lax.broadcasted_iota(jnp.int32, (GLA_BLOCK, GLA_BLOCK), 0)
    j = lax.broadcasted_iota(jnp.int32, (GLA_BLOCK, GLA_BLOCK), 1)
    same = (i // B_CHUNK) == (j // B_CHUNK)
    tri = (j >= i) if reverse else (j <= i)
    return (same & tri).astype(BF16)


def _sum_rows(rows):
    acc = rows[0]
    for r in rows[1:]:
        acc = acc + r
    return acc


def _block_cum(a, tri, reverse):
    hi = a.astype(BF16)
    lo = (a - hi.astype(F32)).astype(BF16)
    cum = (jnp.dot(tri, hi, preferred_element_type=F32) + jnp.dot(tri, lo, preferred_element_type=F32))
    last = 0 if reverse else B_CHUNK - 1
    tots = [cum[c * B_CHUNK + last:c * B_CHUNK + last + 1, :] for c in range(GLA_NC)]
    return cum, tots


def _block_keys(k, cum, tots, reverse):
    totb = jnp.concatenate([jnp.broadcast_to(t, (B_CHUNK, B_KEY_WIDTH)) for t in tots], axis=0)
    kdec = k * jnp.exp2(totb - cum)
    later = []
    for c in range(GLA_NC):
        idx = list(range(0, c)) if reverse else list(range(c + 1, GLA_NC))
        if idx:
            later.append(jnp.broadcast_to(jnp.exp2(_sum_rows([tots[m] for m in idx])), (B_CHUNK, B_KEY_WIDTH)))
        else:
            later.append(jnp.ones((B_CHUNK, B_KEY_WIDTH), F32))
    return kdec, kdec * jnp.concatenate(later, axis=0)


def _pre_project(h16, w_ref, want_q):
    lr = jnp.dot(h16, w_ref[:, P_LR:P_LR + LANES], preferred_element_type=F32)[:, 0:2 * B_GATE_RANK]
    k = jnp.dot(h16, w_ref[:, P_K:P_K + B_KEY_WIDTH], preferred_element_type=F32)
    q = jnp.dot(h16, w_ref[:, P_Q:P_Q + B_KEY_WIDTH], preferred_element_type=F32) if want_q else None
    v = jnp.dot(h16, w_ref[:, P_V:P_V + B_VAL_WIDTH], preferred_element_type=F32)
    return lr, k, q, v


def _block_kv_t(v16, kblk):
    k16 = kblk.astype(BF16)
    lane = lax.broadcasted_iota(jnp.int32, (1, B_KEY_WIDTH), 1)
    acc = jnp.zeros((B_DV, B_KEY_WIDTH), F32)
    for h in range(B_HEADS):
        full = _dot_tn(v16[:, h * B_DV:(h + 1) * B_DV], k16)
        acc = acc + jnp.where((lane // B_DK) == h, full, 0.0)
    return acc


def _latent_h(x, mod_ref, ng_ref):
    row = pl.ds(pl.program_id(0), 1)
    shift = mod_ref[row, 0:D_MODEL]
    gain = ng_ref[...] * (1.0 + mod_ref[row, D_MODEL:2 * D_MODEL])
    return _rms_rows(x) * gain + shift


def _pre_kernel(x_ref, mod_ref, ng_ref, w_ref, w2_ref, gb_ref, lg_ref, lb_ref, ctx_ref,
                wt_ref, wpa_ref, wpb_ref, wo_ref, ws_ref, bst_ref,
                qdf_ref, kdf_ref, kcf_ref, qdb_ref, kdb_ref, kcb_ref, v_ref,
                totf_ref, totb_ref, kvf_ref, sb_ref, vr_ref, sv_ref,
                wmain_ref, wpa16_ref, wpb16_ref, wo16_ref, s0f_ref, sb_scr, vn_scr, *, ctx_row):
    @pl.when(pl.program_id(1) == 0)
    def _():
        shift = mod_ref[ctx_row:ctx_row + 1, 0:D_MODEL]
        scale = mod_ref[ctx_row:ctx_row + 1, D_MODEL:2 * D_MODEL]
        hc = _rms_rows(ctx_ref[0]) * ng_ref[...] * (1.0 + scale) + shift
        lr_c, k_c, _, v_c = _pre_project(hc.astype(BF16), w_ref, False)
        v16_c = v_c.astype(BF16)
        logs_c = _gate_logs(lr_c, w2_ref, gb_ref)
        for d, reverse in enumerate((False, True)):
            cum_c, tots_c = _block_cum(logs_c[d], _chunk_tri(reverse), reverse)
            _, kblk_c = _block_keys(k_c, cum_c, tots_c, reverse)
            state = _block_kv_t(v16_c, kblk_c)
            if reverse:
                sb_scr[...] = state
            else:
                s0f_ref[0] = state

    h16 = _latent_h(x_ref[0], mod_ref, ng_ref).astype(BF16)
    nblk = x_ref.shape[1] // GLA_BLOCK
    blocks = [slice(blk * GLA_BLOCK, (blk + 1) * GLA_BLOCK) for blk in range(nblk)]
    dirs = ((False, qdf_ref, kdf_ref, kcf_ref, totf_ref), (True, qdb_ref, kdb_ref, kcb_ref, totb_ref))

    lrk = jnp.dot(h16, w_ref[:, P_LR:P_K + B_KEY_WIDTH], preferred_element_type=F32)
    lr = lrk[:, 0:2 * B_GATE_RANK]
    k = lrk[:, P_K:P_K + B_KEY_WIDTH]
    logs = _gate_logs(lr, w2_ref, gb_ref)
    q = jnp.dot(h16, w_ref[:, P_Q:P_Q + B_KEY_WIDTH], preferred_element_type=F32)

    _put_transposed(wmain_ref, 0, wt_ref, ZB0, VA0 - ZB0)
    _put_transposed(wmain_ref, VA0 - ZB0, wt_ref, ZA0, IN_WIDTH - ZA0)
    wpa16_ref[...] = wpa_ref[0].astype(BF16)
    wpb16_ref[...] = wpb_ref[0].astype(BF16)
    wo16_ref[...] = wo_ref[0].astype(BF16)

    v16 = jnp.dot(h16, w_ref[:, P_V:P_V + B_VAL_WIDTH], preferred_element_type=F32).astype(BF16)
    v_ref[0] = v16

    cums = {}
    for d, (reverse, _, _, _, tot_ref) in enumerate(dirs):
        tri = _chunk_tri(reverse)
        for blk, rs in enumerate(blocks):
            cum, tots = _block_cum(logs[d][rs], tri, reverse)
            cums[d, blk] = (cum, tots)
            for c in range(GLA_NC):
                tot_ref[0, blk, c:c + 1, :] = tots[c]

    va = jnp.dot(h16, w_ref[:, P_VA:P_VA + A_WIDTH], preferred_element_type=F32)
    vc = va - jnp.mean(va, axis=-1, keepdims=True)
    vn = vc * lax.rsqrt(jnp.mean(vc * vc, axis=-1, keepdims=True) + EPS) * lg_ref[...] + lb_ref[...]
    nr = A_ROW_GROUPS * A_GROUP_DIM
    vr_ref[0] = vn[:, 0:nr].astype(BF16)
    n_rows = x_ref.shape[1] // GRID_W
    row0 = pl.multiple_of((pl.num_programs(1) - 1 - pl.program_id(1)) * n_rows, n_rows)
    for g in range(A_GROUPS - A_ROW_GROUPS):
        vg = vn[:, nr + g * A_GROUP_DIM:nr + (g + 1) * A_GROUP_DIM]
        vn_scr[g, :, pl.ds(row0, n_rows), :] = jnp.swapaxes(
            vg.reshape(n_rows, GRID_W, A_GROUP_DIM), 0, 1).astype(BF16)

    kv_b = {}
    for d, (reverse, qd_ref, kd_ref, kc_ref, _) in enumerate(dirs):
        for blk, rs in enumerate(blocks):
            cum, tots = cums[d, blk]
            kdec, kblk = _block_keys(k[rs], cum, tots, reverse)
            kc_ref[0, rs, :] = kdec.astype(BF16)
            kd_ref[0, rs, :] = (k[rs] * jnp.exp2(-cum)).astype(BF16)
            qd_ref[0, rs, :] = (q[rs] * jnp.exp2(cum)).astype(BF16)
            if reverse:
                kv_b[blk] = _block_kv_t(v16[rs], kblk)
            else:
                kvf_ref[0, blk] = _block_kv_t(v16[rs], kblk)

    s = sb_scr[...]
    for blk in reversed(range(nblk)):
        sb_ref[0, blk] = s.T.astype(BF16)
        s = jnp.exp2(_sum_rows(cums[1, blk][1])) * s + kv_b[blk]
    sb_scr[...] = s

    @pl.when(pl.program_id(1) == pl.num_programs(1) - 1)
    def _():
        for g in range(A_GROUPS - A_ROW_GROUPS):
            w_mix = ws_ref[0, A_ROW_GROUPS + g].astype(BF16)
            bias = bst_ref[A_ROW_GROUPS + g][:, 0:1]
            for w0 in range(0, GRID_W, COLMIX_COLUMNS):
                xs = jnp.concatenate([vn_scr[g, w] for w in range(w0, w0 + COLMIX_COLUMNS)], axis=1)
                y = jnp.dot(w_mix, xs, preferred_element_type=F32) + bias
                for w in range(COLMIX_COLUMNS):
                    sv_ref[0, g, w0 + w] = y[:, w * A_GROUP_DIM:(w + 1) * A_GROUP_DIM].astype(BF16)


def _pre_call(x, mod, ng, w_pre, w2, gb, lg, lb, ctx, w_in_t, w_proj_a, w_proj_b, w_out, ws, bs_t, t):
    b, l, _ = x.shape
    assert ctx.shape[1] == GLA_BLOCK, "context length must be one GLA block"
    nw = w_pre.shape[1]
    ncg = A_GROUPS - A_ROW_GROUPS
    nblk = t // GLA_BLOCK
    nj = l // t
    n_main = (VA0 - ZB0) + (IN_WIDTH - ZA0)
    assert D_MODEL % nj == 0 and (D_MODEL // nj) % LANES == 0, "one weight strip per sequence step"
    strip = lambda i, j: jnp.where(i == 0, j, nj - 1)

    def wrows(a):
        return pl.BlockSpec((1, a.shape[1] // nj, a.shape[2]), lambda i, j: (0, strip(i, j), 0))

    def wout(nrows, ncols):
        return (pl.BlockSpec((nrows // nj, ncols), lambda i, j: (strip(i, j), 0)),
                jax.ShapeDtypeStruct((nrows, ncols), BF16))
    rev3 = lambda i, j: (i, nj - 1 - j, 0)
    rev4 = lambda i, j: (i, nj - 1 - j, 0, 0)

    def tok(width, dtype):
        return (pl.BlockSpec((1, t, width), rev3), jax.ShapeDtypeStruct((b, l, width), dtype))

    tot = (pl.BlockSpec((1, nblk, GLA_NC, B_KEY_WIDTH), rev4),
           jax.ShapeDtypeStruct((b, l // GLA_BLOCK, GLA_NC, B_KEY_WIDTH), F32))
    kv = (pl.BlockSpec((1, nblk, B_DV, B_KEY_WIDTH), rev4),
          jax.ShapeDtypeStruct((b, l // GLA_BLOCK, B_DV, B_KEY_WIDTH), F32))
    st = (pl.BlockSpec((1, nblk, B_KEY_WIDTH, B_DV), rev4),
          jax.ShapeDtypeStruct((b, l // GLA_BLOCK, B_KEY_WIDTH, B_DV), BF16))
    outs = [tok(B_KEY_WIDTH, BF16)] * 6 + [tok(B_VAL_WIDTH, BF16), tot, tot, kv, st,
                                           tok(A_ROW_GROUPS * A_GROUP_DIM, BF16),
                                           (pl.BlockSpec((1, ncg, GRID_W, l // GRID_W, A_GROUP_DIM),
                                                         lambda i, j: (i, 0, 0, 0, 0)),
                                            jax.ShapeDtypeStruct((b, ncg, GRID_W, l // GRID_W, A_GROUP_DIM), BF16)),
                                           wout(D_MODEL, n_main), wout(A_WIDTH, D_MODEL),
                                           wout(B_VAL_WIDTH, D_MODEL), wout(D_MODEL, D_MODEL),
                                           (pl.BlockSpec((1, B_DV, B_KEY_WIDTH), lambda i, j: (i, 0, 0)),
                                            jax.ShapeDtypeStruct((b, B_DV, B_KEY_WIDTH), F32))]
    const2 = lambda i, j: (0, 0)
    return pl.pallas_call(
        functools.partial(_pre_kernel, ctx_row=b),
        grid=(b, nj),
        in_specs=[pl.BlockSpec((1, t, D_MODEL), rev3),
                  pl.BlockSpec((8, 3 * D_MODEL), const2),
                  pl.BlockSpec((1, D_MODEL), const2),
                  pl.BlockSpec((D_MODEL, nw), const2),
                  pl.BlockSpec((2, B_GATE_RANK, B_KEY_WIDTH), lambda i, j: (0, 0, 0)),
                  pl.BlockSpec((2, B_KEY_WIDTH), const2),
                  pl.BlockSpec((1, A_WIDTH), const2),
                  pl.BlockSpec((1, A_WIDTH), const2),
                  pl.BlockSpec((1, GLA_BLOCK, D_MODEL), lambda i, j: (i, 0, 0)),
                  pl.BlockSpec((IN_WIDTH, D_MODEL // nj), lambda i, j: (0, strip(i, j))),
                  wrows(w_proj_a), wrows(w_proj_b), wrows(w_out),
                  pl.BlockSpec(ws.shape, lambda i, j: (0, 0, 0, 0)),
                  pl.BlockSpec(bs_t.shape, lambda i, j: (0, 0, 0))],
        out_specs=[o[0] for o in outs],
        out_shape=[o[1] for o in outs],
        scratch_shapes=[pltpu.VMEM((B_DV, B_KEY_WIDTH), F32),
                        pltpu.VMEM((ncg, GRID_W, l // GRID_W, A_GROUP_DIM), BF16)],
        compiler_params=pltpu.CompilerParams(
            dimension_semantics=("arbitrary", "arbitrary"), vmem_limit_bytes=VMEM_LIMIT_BYTES),
        name="pre",
    )(x, mod, ng, w_pre, w2, gb, lg, lb, ctx, w_in_t, w_proj_a, w_proj_b, w_out, ws, bs_t)


def _scale_rows(x16, scales):
    parts = []
    for c, s in enumerate(scales):
        xc = x16[c * B_CHUNK:(c + 1) * B_CHUNK, :]
        parts.append(xc if s is None else (xc.astype(F32) * s).astype(BF16))
    return parts[0] if len(parts) == 1 else jnp.concatenate(parts, axis=0)


def _exp_sum(tots, idx):
    return jnp.exp2(_sum_rows([tots[m] for m in idx])) if idx else None


def _gla_block(qdf, kdf, kcf, qdb, kdb, kcb, v16, tf, tb, sf, sb, p_scr):
    nc = GLA_NC
    ch = B_CHUNK
    lane = lax.broadcasted_iota(jnp.int32, (1, LANES), 1)
    hm = [(lane < B_DK).astype(BF16), (lane >= B_DK).astype(BF16)]

    def pair_heads(x):
        return jnp.concatenate([x * hm[0], x * hm[1]], axis=0)

    row = lax.broadcasted_iota(jnp.int32, (ch, 2 * ch), 0)
    col = lax.broadcasted_iota(jnp.int32, (ch, 2 * ch), 1)
    first = col < ch
    zeros = jnp.zeros((ch, LANES), BF16)
    for g in range(B_HEADS // 2):
        ls = slice(g * LANES, (g + 1) * LANES)
        for i in range(nc // 2):
            pr = slice(2 * i * ch, (2 * i + 2) * ch)
            re = slice(2 * i * ch, (2 * i + 1) * ch)
            ro = slice((2 * i + 1) * ch, (2 * i + 2) * ch)
            sc_f = _dot_nt(pair_heads(qdf[pr, ls]), jnp.concatenate([kdf[pr, ls], kcf[re, ls], zeros], axis=0))
            sc_b = _dot_nt(pair_heads(qdb[pr, ls]), jnp.concatenate([kdb[pr, ls], zeros, kcb[ro, ls]], axis=0))
            for hh in range(2):
                he = slice(hh * 2 * ch, hh * 2 * ch + ch)
                ho = slice(hh * 2 * ch + ch, (hh + 1) * 2 * ch)
                even = jnp.where(first,
                                 jnp.where(col <= row, sc_f[he, 0:2 * ch], 0.0)
                                 + jnp.where(col >= row, sc_b[he, 0:2 * ch], 0.0),
                                 sc_b[he, 2 * ch:])
                odd = jnp.where(first, sc_f[ho, 2 * ch:],
                                jnp.where(col - ch <= row, sc_f[ho, 0:2 * ch], 0.0)
                                + jnp.where(col - ch >= row, sc_b[ho, 0:2 * ch], 0.0))
                p_scr[2 * g + hh, re, pr] = even.astype(BF16)
                p_scr[2 * g + hh, ro, pr] = odd.astype(BF16)

    def cross(lo, hi):
        if hi - lo <= 2:
            return
        mid = (lo + hi) // 2
        cross(lo, mid)
        cross(mid, hi)
        rl = slice(lo * ch, mid * ch)
        rh = slice(mid * ch, hi * ch)
        n = (hi - mid) * ch
        qf = _scale_rows(qdf[rh], [_exp_sum(tf, range(mid, c)) for c in range(mid, hi)])
        kf = _scale_rows(kcf[rl], [_exp_sum(tf, range(c + 1, mid)) for c in range(lo, mid)])
        qb = _scale_rows(qdb[rl], [_exp_sum(tb, range(c + 1, mid)) for c in range(lo, mid)])
        kb = _scale_rows(kcb[rh], [_exp_sum(tb, range(mid, c)) for c in range(mid, hi)])
        for g in range(B_HEADS // 2):
            ls = slice(g * LANES, (g + 1) * LANES)
            sc_f = _dot_nt(pair_heads(qf[:, ls]), kf[:, ls]).astype(BF16)
            sc_b = _dot_nt(pair_heads(qb[:, ls]), kb[:, ls]).astype(BF16)
            for hh in range(2):
                p_scr[2 * g + hh, rh, rl] = sc_f[hh * n:(hh + 1) * n]
                p_scr[2 * g + hh, rl, rh] = sc_b[hh * n:(hh + 1) * n]

    cross(0, nc)

    qsf = _scale_rows(qdf, [_exp_sum(tf, range(0, c)) for c in range(nc)])
    qsb = _scale_rows(qdb, [_exp_sum(tb, range(c + 1, nc)) for c in range(nc)])
    head_row = lax.broadcasted_iota(jnp.int32, (LANES, 1), 0) // B_DK
    outs = []
    for h in range(B_HEADS):
        ls = slice((h // 2) * LANES, (h // 2 + 1) * LANES)
        own = (head_row == (h % 2)).astype(BF16)
        lhs = jnp.concatenate([p_scr[h], qsf[:, ls], qsb[:, ls]], axis=1)
        rhs = jnp.concatenate([v16[:, h * B_DV:(h + 1) * B_DV], sf[ls, :] * own, sb[ls, :] * own], axis=0)
        outs.append(jnp.dot(lhs, rhs, preferred_element_type=F32))
    return outs


def _main_kernel(x_ref, mod_ref, ng_ref, w_ref, vr_ref, sc_ref,
                 qdf_ref, kdf_ref, kcf_ref, qdb_ref, kdb_ref, kcb_ref, v_ref, totf_ref, totb_ref,
                 kvf_ref, s0f_ref, sb_ref,
                 ws_ref, bs_ref, bg_ref, wpa_ref, wpb_ref, wo_ref, fg_ref, o_ref, acta_scr, on_scr, p_scr, sf_scr):
    t = x_ref.shape[1]
    blocks = [slice(blk * GLA_BLOCK, (blk + 1) * GLA_BLOCK) for blk in range(t // GLA_BLOCK)]

    @pl.when(pl.program_id(1) == 0)
    def _():
        sf_scr[...] = s0f_ref[0]

    s = sf_scr[...]
    sf_in = []
    for blk in range(len(blocks)):
        sf_in.append(s.T.astype(BF16))
        s = jnp.exp2(_sum_rows([totf_ref[0, blk, c:c + 1, :] for c in range(GLA_NC)])) * s + kvf_ref[0, blk]
    sf_scr[...] = s

    def gla(blk):
        rs = blocks[blk]
        tf = [totf_ref[0, blk, c:c + 1, :] for c in range(GLA_NC)]
        tb = [totb_ref[0, blk, c:c + 1, :] for c in range(GLA_NC)]
        o_heads = _gla_block(qdf_ref[0, rs, :], kdf_ref[0, rs, :], kcf_ref[0, rs, :],
                             qdb_ref[0, rs, :], kdb_ref[0, rs, :], kcb_ref[0, rs, :], v_ref[0, rs, :],
                             tf, tb, sf_in[blk], sb_ref[0, blk], p_scr.at[blk])
        for hd in range(B_HEADS):
            on_scr[rs, hd * B_DV:(hd + 1) * B_DV] = _rms_rows(o_heads[hd])

    def proj(c0, n):
        return jnp.dot(h16, w_ref[:, c0:c0 + n], preferred_element_type=F32)

    gla(0)
    h16 = _latent_h(x_ref[0], mod_ref, ng_ref).astype(BF16)
    zb = proj(0, B_VAL_WIDTH)
    u = proj(B_VAL_WIDTH, A_WIDTH)
    za = proj(B_VAL_WIDTH + A_WIDTH, A_WIDTH)
    for blk in range(1, len(blocks)):
        gla(blk)
    gate_a = jax.nn.sigmoid(proj(B_VAL_WIDTH + 2 * A_WIDTH, D_MODEL))

    uz = u * _silu(za)
    for g in range(A_GROUPS):
        cs = slice(g * A_GROUP_DIM, (g + 1) * A_GROUP_DIM)
        if g < A_ROW_GROUPS:
            chunks = [slice(c * A_CHUNK, (c + 1) * A_CHUNK) for c in range(t // A_CHUNK)]
            sv_all = jnp.dot(ws_ref[g].astype(BF16), jnp.concatenate([vr_ref[0, rs, cs] for rs in chunks], axis=1),
                             preferred_element_type=F32)
            for c, rs in enumerate(chunks):
                sv = sv_all[:, c * A_GROUP_DIM:(c + 1) * A_GROUP_DIM] + bs_ref[g][:, 0:1]
                acta_scr[rs, cs] = (uz[rs, cs] * sv).astype(BF16)
        else:
            sv = jnp.swapaxes(sc_ref[0, g - A_ROW_GROUPS].astype(F32), 0, 1).reshape(t, A_GROUP_DIM)
            acta_scr[:, cs] = (uz[:, cs] * sv).astype(BF16)
    ya = jnp.dot(acta_scr[...], wpa_ref[...], preferred_element_type=F32)

    gate_b = jax.nn.sigmoid(proj(B_VAL_WIDTH + 2 * A_WIDTH + D_MODEL, D_MODEL))
    yb = _dot(on_scr[...] * (_silu(zb) * bg_ref[...]), wpb_ref[...])

    m = (gate_a * ya + gate_b * yb).astype(BF16)
    for rs in blocks:
        y = jnp.dot(m[rs], wo_ref[...], preferred_element_type=F32)
        xo = x_ref[0, rs, :] + mod_ref[pl.ds(pl.program_id(0), 1), 2 * D_MODEL:] * y
        o_ref[0, rs, :] = _rms_rows(xo) * fg_ref[...]


def _main_call(x, mod, ng, w_main, vn_row, sv_col, gla_ops, ws, bs, bg, wpa, wpb, wo, fg, t):
    b, l, _ = x.shape
    nw = w_main.shape[1]
    ncg = A_GROUPS - A_ROW_GROUPS
    const2 = lambda i, j: (0, 0)
    const3 = lambda i, j: (0, 0, 0)
    tok = lambda width: pl.BlockSpec((1, t, width), lambda i, j: (i, j, 0))
    nblk = t // GLA_BLOCK
    tot_spec = pl.BlockSpec((1, nblk, GLA_NC, B_KEY_WIDTH), lambda i, j: (i, j, 0, 0))
    st_spec = pl.BlockSpec((1, nblk, B_KEY_WIDTH, B_DV), lambda i, j: (i, j, 0, 0))
    kv_spec = pl.BlockSpec((1, nblk, B_DV, B_KEY_WIDTH), lambda i, j: (i, j, 0, 0))
    s0_spec = pl.BlockSpec((1, B_DV, B_KEY_WIDTH), lambda i, j: (i, 0, 0))
    return pl.pallas_call(
        _main_kernel,
        grid=(b, l // t),
        in_specs=[tok(D_MODEL),
                  pl.BlockSpec((8, 3 * D_MODEL), const2),
                  pl.BlockSpec((1, D_MODEL), const2),
                  pl.BlockSpec((D_MODEL, nw), const2),
                  tok(A_ROW_GROUPS * A_GROUP_DIM),
                  pl.BlockSpec((1, ncg, GRID_W, t // GRID_W, A_GROUP_DIM), lambda i, j: (i, 0, 0, j, 0)),
                  tok(B_KEY_WIDTH), tok(B_KEY_WIDTH), tok(B_KEY_WIDTH),
                  tok(B_KEY_WIDTH), tok(B_KEY_WIDTH), tok(B_KEY_WIDTH), tok(B_VAL_WIDTH),
                  tot_spec, tot_spec, kv_spec, s0_spec, st_spec,
                  pl.BlockSpec((A_GROUPS, A_CHUNK, A_CHUNK), const3),
                  pl.BlockSpec((A_GROUPS, A_CHUNK, A_GROUP_DIM), const3),
                  pl.BlockSpec((1, B_VAL_WIDTH), const2),
                  pl.BlockSpec((A_WIDTH, D_MODEL), const2),
                  pl.BlockSpec((B_VAL_WIDTH, D_MODEL), const2),
                  pl.BlockSpec((D_MODEL, D_MODEL), const2),
                  pl.BlockSpec((1, D_MODEL), const2)],
        out_specs=tok(D_MODEL),
        out_shape=jax.ShapeDtypeStruct((b, l, D_MODEL), F32),
        scratch_shapes=[pltpu.VMEM((t, A_WIDTH), BF16), pltpu.VMEM((t, B_VAL_WIDTH), F32),
                        pltpu.VMEM((nblk, B_HEADS, GLA_BLOCK, GLA_BLOCK), BF16),
                        pltpu.VMEM((B_DV, B_KEY_WIDTH), F32)],
        compiler_params=pltpu.CompilerParams(
            dimension_semantics=("parallel", "arbitrary"), vmem_limit_bytes=VMEM_LIMIT_BYTES),
        name="main",
    )(x, mod, ng, w_main, vn_row, sv_col, *gla_ops, ws, bs, bg, wpa, wpb, wo, fg)


def kernel(x, c, ctx, c_ctx, w_mod, b_mod, norm_g, w_in, a_ln_g, a_ln_b, a_ws, a_bs, b_gate_w2, b_gate_b,
           b_norm_g, w_proj_a, w_proj_b, w_out, final_norm_g):
    assert w_mod.shape[0] == 1, "single-layer block"
    b, l, _ = x.shape
    rows = l // GRID_W
    assert rows == A_CHUNK
    ng = norm_g[0][None, :]

    w_in_t = w_in[0].T
    w_pre, mod, bs_t = _wprep_call(w_in_t, c, c_ctx, w_mod, b_mod, a_bs)
    w2, gb = b_gate_w2[0], b_gate_b[0]

    (qd_f, kd_f, kc_f, qd_b, kd_b, kc_b, v16, tot_f, tot_b, kv_f, s_b, vn_row, sv_col,
     w_main, wpa16, wpb16, wo16, s0f) = _pre_call(
        x, mod, ng, w_pre, w2, gb, a_ln_g[0][None, :], a_ln_b[0][None, :], ctx,
        w_in_t, w_proj_a, w_proj_b, w_out, a_ws, bs_t, TOKEN_TILE)
    gla_ops = (qd_f, kd_f, kc_f, qd_b, kd_b, kc_b, v16, tot_f, tot_b, kv_f, s0f, s_b)


    return _main_call(x, mod, ng, w_main, vn_row, sv_col, gla_ops, a_ws[0], bs_t, b_norm_g[0][None, :],
                      wpa16, wpb16, wo16, final_norm_g[None, :], TOKEN_TILE)
```

```python
import functools

import jax
import jax.numpy as jnp
from jax import lax
from jax.experimental import pallas as pl
from jax.experimental.pallas import tpu as pltpu

D_MODEL = 1024
GRID_W = 64
EPS = 1e-6

A_WIDTH = 512
A_GROUPS = 4
A_GROUP_DIM = 128
A_CHUNK = 128
A_ROW_GROUPS = 2

B_HEADS = 4
B_DK = 64
B_DV = 128
B_KEY_WIDTH = 256
B_VAL_WIDTH = 512
B_GATE_RANK = 16
B_GATE_TAU = 16.0
LOG2E = 1.4426950408889634
B_CHUNK = 64

Q0 = 0
K0 = Q0 + B_KEY_WIDTH
V0 = K0 + B_KEY_WIDTH
LR0 = V0 + B_VAL_WIDTH
ZB0 = LR0 + 2 * B_GATE_RANK
UA0 = ZB0 + B_VAL_WIDTH
VA0 = UA0 + A_WIDTH
ZA0 = VA0 + A_WIDTH
G0 = ZA0 + A_WIDTH
IN_WIDTH = G0 + 2 * D_MODEL

LANES = 128

P_LR = 0
P_K = P_LR + LANES
P_Q = P_K + B_KEY_WIDTH
P_V = P_Q + B_KEY_WIDTH
P_VA = P_V + B_VAL_WIDTH
PRE_WIDTH = P_VA + A_WIDTH
GLA_BLOCK = 256
TOKEN_TILE = 1024
COLMIX_COLUMNS = 16
GLA_NC = GLA_BLOCK // B_CHUNK

VMEM_LIMIT_BYTES = 56 * 1024 * 1024

BF16 = jnp.bfloat16
F32 = jnp.float32


def _dot(a, b):
    return jnp.dot(a.astype(BF16), b.astype(BF16), preferred_element_type=F32)


def _dot_nt(a, b):
    return lax.dot_general(a.astype(BF16), b.astype(BF16), (((1,), (1,)), ((), ())),
                           preferred_element_type=F32)


def _dot_tn(a, b):
    return lax.dot_general(a.astype(BF16), b.astype(BF16), (((0,), (0,)), ((), ())),
                           preferred_element_type=F32)


def _silu(x):
    return x * jax.nn.sigmoid(x)


def _rms_rows(x):
    return x * lax.rsqrt(jnp.mean(x * x, axis=-1, keepdims=True) + EPS)


def _put_transposed(dst_ref, c0, wt_ref, r0, n, scale=None):
    for s in range(0, n, 2 * LANES):
        m = min(2 * LANES, n - s)
        blk = wt_ref[r0 + s:r0 + s + m, :].T
        dst_ref[:, c0 + s:c0 + s + m] = (blk if scale is None else blk * scale).astype(BF16)


def _wprep_kernel(wt_ref, c_ref, cctx_ref, wm_ref, bm_ref, abs_ref, wpre_ref, mod_ref, bst_ref):
    _put_transposed(wpre_ref, P_LR, wt_ref, LR0, LANES)
    _put_transposed(wpre_ref, P_K, wt_ref, K0, B_KEY_WIDTH)
    _put_transposed(wpre_ref, P_Q, wt_ref, Q0, B_KEY_WIDTH, B_DK ** -0.5)
    _put_transposed(wpre_ref, P_V, wt_ref, V0, B_VAL_WIDTH)
    _put_transposed(wpre_ref, P_VA, wt_ref, VA0, A_WIDTH)
    nb = c_ref.shape[0]
    cc = jnp.concatenate([c_ref[...], cctx_ref[...], jnp.zeros((8 - nb - 1, D_MODEL), F32)], axis=0)
    mod_ref[...] = _dot(_silu(cc), wm_ref[0]) + bm_ref[...]
    g = pl.program_id(0)
    bst_ref[0] = jnp.broadcast_to(abs_ref[0, pl.ds(g, 1), :], (A_GROUP_DIM, A_CHUNK)).T


def _wprep_call(w_in_t, c, c_ctx, w_mod, b_mod, a_bs):
    steps = A_GROUPS
    n_mod = w_mod.shape[2]
    outs = [(pl.BlockSpec((D_MODEL // steps, PRE_WIDTH), lambda i: (i, 0)),
             jax.ShapeDtypeStruct((D_MODEL, PRE_WIDTH), BF16)),
            (pl.BlockSpec((8, n_mod // steps), lambda i: (0, i)), jax.ShapeDtypeStruct((8, n_mod), F32)),
            (pl.BlockSpec((1, A_CHUNK, A_GROUP_DIM), lambda i: (i, 0, 0)),
             jax.ShapeDtypeStruct((A_GROUPS, A_CHUNK, A_GROUP_DIM), F32))]
    return pl.pallas_call(
        _wprep_kernel,
        grid=(steps,),
        in_specs=[pl.BlockSpec((VA0 + A_WIDTH, D_MODEL // steps), lambda i: (0, i)),
                  pl.BlockSpec(c.shape, lambda i: (0, 0)),
                  pl.BlockSpec((1, D_MODEL), lambda i: (0, 0)),
                  pl.BlockSpec((1, D_MODEL, n_mod // steps), lambda i: (0, 0, i)),
                  pl.BlockSpec((1, n_mod // steps), lambda i: (0, i)),
                  pl.BlockSpec(a_bs.shape, lambda i: (0, 0, 0))],
        out_specs=[o[0] for o in outs],
        out_shape=[o[1] for o in outs],
        compiler_params=pltpu.CompilerParams(
            dimension_semantics=("parallel",), vmem_limit_bytes=VMEM_LIMIT_BYTES),
        name="wprep",
    )(w_in_t, c, c_ctx[None, :], w_mod, b_mod, a_bs)


def _gate_logs(lr, w2_ref, gb_ref):
    out = []
    for r in range(2):
        logits = _dot(lr[:, r * B_GATE_RANK:(r + 1) * B_GATE_RANK], w2_ref[r]) + gb_ref[r:r + 1, :]
        log_sig = jnp.minimum(logits, 0.0) - jnp.log(1.0 + jnp.exp(-jnp.abs(logits)))
        out.append(log_sig * (LOG2E / B_GATE_TAU))
    return out


def _chunk_tri(reverse):
    i = lax.broadcasted_iota(jnp.int32, (GLA_BLOCK, GLA_BLOCK), 0)
    j = lax.broadcasted_iota(jnp.int32, (GLA_BLOCK, GLA_BLOCK), 1)
    same = (i // B_CHUNK) == (j // B_CHUNK)
    tri = (j >= i) if reverse else (j <= i)
    return (same & tri).astype(BF16)


def _sum_rows(rows):
    acc = rows[0]
    for r in rows[1:]:
        acc = acc + r
    return acc


def _block_cum(a, tri, reverse):
    hi = a.astype(BF16)
    lo = (a - hi.astype(F32)).astype(BF16)
    cum = (jnp.dot(tri, hi, preferred_element_type=F32) + jnp.dot(tri, lo, preferred_element_type=F32))
    last = 0 if reverse else B_CHUNK - 1
    tots = [cum[c * B_CHUNK + last:c * B_CHUNK + last + 1, :] for c in range(GLA_NC)]
    return cum, tots


def _block_keys(k, cum, tots, reverse):
    totb = jnp.concatenate([jnp.broadcast_to(t, (B_CHUNK, B_KEY_WIDTH)) for t in tots], axis=0)
    kdec = k * jnp.exp2(totb - cum)
    later = []
    for c in range(GLA_NC):
        idx = list(range(0, c)) if reverse else list(range(c + 1, GLA_NC))
        if idx:
            later.append(jnp.broadcast_to(jnp.exp2(_sum_rows([tots[m] for m in idx])), (B_CHUNK, B_KEY_WIDTH)))
        else:
            later.append(jnp.ones((B_CHUNK, B_KEY_WIDTH), F32))
    return kdec, kdec * jnp.concatenate(later, axis=0)


def _pre_project(h16, w_ref, want_q):
    lr = jnp.dot(h16, w_ref[:, P_LR:P_LR + LANES], preferred_element_type=F32)[:, 0:2 * B_GATE_RANK]
    k = jnp.dot(h16, w_ref[:, P_K:P_K + B_KEY_WIDTH], preferred_element_type=F32)
    q = jnp.dot(h16, w_ref[:, P_Q:P_Q + B_KEY_WIDTH], preferred_element_type=F32) if want_q else None
    v = jnp.dot(h16, w_ref[:, P_V:P_V + B_VAL_WIDTH], preferred_element_type=F32)
    return lr, k, q, v


def _block_kv_t(v16, kblk):
    k16 = kblk.astype(BF16)
    lane = lax.broadcasted_iota(jnp.int32, (1, B_KEY_WIDTH), 1)
    acc = jnp.zeros((B_DV, B_KEY_WIDTH), F32)
    for h in range(B_HEADS):
        full = _dot_tn(v16[:, h * B_DV:(h + 1) * B_DV], k16)
        acc = acc + jnp.where((lane // B_DK) == h, full, 0.0)
    return acc


def _latent_h(x, mod_ref, ng_ref):
    row = pl.ds(pl.program_id(0), 1)
    shift = mod_ref[row, 0:D_MODEL]
    gain = ng_ref[...] * (1.0 + mod_ref[row, D_MODEL:2 * D_MODEL])
    return _rms_rows(x) * gain + shift


def _pre_kernel(x_ref, mod_ref, ng_ref, w_ref, w2_ref, gb_ref, lg_ref, lb_ref, ctx_ref,
                wt_ref, wpa_ref, wpb_ref, wo_ref, ws_ref, bst_ref,
                qdf_ref, kdf_ref, kcf_ref, qdb_ref, kdb_ref, kcb_ref, v_ref,
                totf_ref, totb_ref, kvf_ref, sb_ref, vr_ref, sv_ref,
                wmain_ref, wpa16_ref, wpb16_ref, wo16_ref, s0f_ref, sb_scr, vn_scr, *, ctx_row):
    @pl.when(pl.program_id(1) == 0)
    def _():
        shift = mod_ref[ctx_row:ctx_row + 1, 0:D_MODEL]
        scale = mod_ref[ctx_row:ctx_row + 1, D_MODEL:2 * D_MODEL]
        hc = _rms_rows(ctx_ref[0]) * ng_ref[...] * (1.0 + scale) + shift
        lr_c, k_c, _, v_c = _pre_project(hc.astype(BF16), w_ref, False)
        v16_c = v_c.astype(BF16)
        logs_c = _gate_logs(lr_c, w2_ref, gb_ref)
        for d, reverse in enumerate((False, True)):
            cum_c, tots_c = _block_cum(logs_c[d], _chunk_tri(reverse), reverse)
            _, kblk_c = _block_keys(k_c, cum_c, tots_c, reverse)
            state = _block_kv_t(v16_c, kblk_c)
            if reverse:
                sb_scr[...] = state
            else:
                s0f_ref[0] = state

    h16 = _latent_h(x_ref[0], mod_ref, ng_ref).astype(BF16)
    nblk = x_ref.shape[1] // GLA_BLOCK
    blocks = [slice(blk * GLA_BLOCK, (blk + 1) * GLA_BLOCK) for blk in range(nblk)]
    dirs = ((False, qdf_ref, kdf_ref, kcf_ref, totf_ref), (True, qdb_ref, kdb_ref, kcb_ref, totb_ref))

    lrk = jnp.dot(h16, w_ref[:, P_LR:P_K + B_KEY_WIDTH], preferred_element_type=F32)
    lr = lrk[:, 0:2 * B_GATE_RANK]
    k = lrk[:, P_K:P_K + B_KEY_WIDTH]
    logs = _gate_logs(lr, w2_ref, gb_ref)
    q = jnp.dot(h16, w_ref[:, P_Q:P_Q + B_KEY_WIDTH], preferred_element_type=F32)

    _put_transposed(wmain_ref, 0, wt_ref, ZB0, VA0 - ZB0)
    _put_transposed(wmain_ref, VA0 - ZB0, wt_ref, ZA0, IN_WIDTH - ZA0)
    wpa16_ref[...] = wpa_ref[0].astype(BF16)
    wpb16_ref[...] = wpb_ref[0].astype(BF16)
    wo16_ref[...] = wo_ref[0].astype(BF16)

    v16 = jnp.dot(h16, w_ref[:, P_V:P_V + B_VAL_WIDTH], preferred_element_type=F32).astype(BF16)
    v_ref[0] = v16

    cums = {}
    for d, (reverse, _, _, _, tot_ref) in enumerate(dirs):
        tri = _chunk_tri(reverse)
        for blk, rs in enumerate(blocks):
            cum, tots = _block_cum(logs[d][rs], tri, reverse)
            cums[d, blk] = (cum, tots)
            for c in range(GLA_NC):
                tot_ref[0, blk, c:c + 1, :] = tots[c]

    va = jnp.dot(h16, w_ref[:, P_VA:P_VA + A_WIDTH], preferred_element_type=F32)
    vc = va - jnp.mean(va, axis=-1, keepdims=True)
    vn = vc * lax.rsqrt(jnp.mean(vc * vc, axis=-1, keepdims=True) + EPS) * lg_ref[...] + lb_ref[...]
    nr = A_ROW_GROUPS * A_GROUP_DIM
    vr_ref[0] = vn[:, 0:nr].astype(BF16)
    n_rows = x_ref.shape[1] // GRID_W
    row0 = pl.multiple_of((pl.num_programs(1) - 1 - pl.program_id(1)) * n_rows, n_rows)
    for g in range(A_GROUPS - A_ROW_GROUPS):
        vg = vn[:, nr + g * A_GROUP_DIM:nr + (g + 1) * A_GROUP_DIM]
        vn_scr[g, :, pl.ds(row0, n_rows), :] = jnp.swapaxes(
            vg.reshape(n_rows, GRID_W, A_GROUP_DIM), 0, 1).astype(BF16)

    kv_b = {}
    for d, (reverse, qd_ref, kd_ref, kc_ref, _) in enumerate(dirs):
        for blk, rs in enumerate(blocks):
            cum, tots = cums[d, blk]
            kdec, kblk = _block_keys(k[rs], cum, tots, reverse)
            kc_ref[0, rs, :] = kdec.astype(BF16)
            kd_ref[0, rs, :] = (k[rs] * jnp.exp2(-cum)).astype(BF16)
            qd_ref[0, rs, :] = (q[rs] * jnp.exp2(cum)).astype(BF16)
            if reverse:
                kv_b[blk] = _block_kv_t(v16[rs], kblk)
            else:
                kvf_ref[0, blk] = _block_kv_t(v16[rs], kblk)

    s = sb_scr[...]
    for blk in reversed(range(nblk)):
        sb_ref[0, blk] = s.T.astype(BF16)
        s = jnp.exp2(_sum_rows(cums[1, blk][1])) * s + kv_b[blk]
    sb_scr[...] = s

    @pl.when(pl.program_id(1) == pl.num_programs(1) - 1)
    def _():
        for g in range(A_GROUPS - A_ROW_GROUPS):
            w_mix = ws_ref[0, A_ROW_GROUPS + g].astype(BF16)
            bias = bst_ref[A_ROW_GROUPS + g][:, 0:1]
            for w0 in range(0, GRID_W, COLMIX_COLUMNS):
                xs = jnp.concatenate([vn_scr[g, w] for w in range(w0, w0 + COLMIX_COLUMNS)], axis=1)
                y = jnp.dot(w_mix, xs, preferred_element_type=F32) + bias
                for w in range(COLMIX_COLUMNS):
                    sv_ref[0, g, w0 + w] = y[:, w * A_GROUP_DIM:(w + 1) * A_GROUP_DIM].astype(BF16)


def _pre_call(x, mod, ng, w_pre, w2, gb, lg, lb, ctx, w_in_t, w_proj_a, w_proj_b, w_out, ws, bs_t, t):
    b, l, _ = x.shape
    assert ctx.shape[1] == GLA_BLOCK, "context length must be one GLA block"
    nw = w_pre.shape[1]
    ncg = A_GROUPS - A_ROW_GROUPS
    nblk = t // GLA_BLOCK
    nj = l // t
    n_main = (VA0 - ZB0) + (IN_WIDTH - ZA0)
    assert D_MODEL % nj == 0 and (D_MODEL // nj) % LANES == 0, "one weight strip per sequence step"
    strip = lambda i, j: jnp.where(i == 0, j, nj - 1)

    def wrows(a):
        return pl.BlockSpec((1, a.shape[1] // nj, a.shape[2]), lambda i, j: (0, strip(i, j), 0))

    def wout(nrows, ncols):
        return (pl.BlockSpec((nrows // nj, ncols), lambda i, j: (strip(i, j), 0)),
                jax.ShapeDtypeStruct((nrows, ncols), BF16))
    rev3 = lambda i, j: (i, nj - 1 - j, 0)
    rev4 = lambda i, j: (i, nj - 1 - j, 0, 0)

    def tok(width, dtype):
        return (pl.BlockSpec((1, t, width), rev3), jax.ShapeDtypeStruct((b, l, width), dtype))

    tot = (pl.BlockSpec((1, nblk, GLA_NC, B_KEY_WIDTH), rev4),
           jax.ShapeDtypeStruct((b, l // GLA_BLOCK, GLA_NC, B_KEY_WIDTH), F32))
    kv = (pl.BlockSpec((1, nblk, B_DV, B_KEY_WIDTH), rev4),
          jax.ShapeDtypeStruct((b, l // GLA_BLOCK, B_DV, B_KEY_WIDTH), F32))
    st = (pl.BlockSpec((1, nblk, B_KEY_WIDTH, B_DV), rev4),
          jax.ShapeDtypeStruct((b, l // GLA_BLOCK, B_KEY_WIDTH, B_DV), BF16))
    outs = [tok(B_KEY_WIDTH, BF16)] * 6 + [tok(B_VAL_WIDTH, BF16), tot, tot, kv, st,
                                           tok(A_ROW_GROUPS * A_GROUP_DIM, BF16),
                                           (pl.BlockSpec((1, ncg, GRID_W, l // GRID_W, A_GROUP_DIM),
                                                         lambda i, j: (i, 0, 0, 0, 0)),
                                            jax.ShapeDtypeStruct((b, ncg, GRID_W, l // GRID_W, A_GROUP_DIM), BF16)),
                                           wout(D_MODEL, n_main), wout(A_WIDTH, D_MODEL),
                                           wout(B_VAL_WIDTH, D_MODEL), wout(D_MODEL, D_MODEL),
                                           (pl.BlockSpec((1, B_DV, B_KEY_WIDTH), lambda i, j: (i, 0, 0)),
                                            jax.ShapeDtypeStruct((b, B_DV, B_KEY_WIDTH), F32))]
    const2 = lambda i, j: (0, 0)
    return pl.pallas_call(
        functools.partial(_pre_kernel, ctx_row=b),
        grid=(b, nj),
        in_specs=[pl.BlockSpec((1, t, D_MODEL), rev3),
                  pl.BlockSpec((8, 3 * D_MODEL), const2),
                  pl.BlockSpec((1, D_MODEL), const2),
                  pl.BlockSpec((D_MODEL, nw), const2),
                  pl.BlockSpec((2, B_GATE_RANK, B_KEY_WIDTH), lambda i, j: (0, 0, 0)),
                  pl.BlockSpec((2, B_KEY_WIDTH), const2),
                  pl.BlockSpec((1, A_WIDTH), const2),
                  pl.BlockSpec((1, A_WIDTH), const2),
                  pl.BlockSpec((1, GLA_BLOCK, D_MODEL), lambda i, j: (i, 0, 0)),
                  pl.BlockSpec((IN_WIDTH, D_MODEL // nj), lambda i, j: (0, strip(i, j))),
                  wrows(w_proj_a), wrows(w_proj_b), wrows(w_out),
                  pl.BlockSpec(ws.shape, lambda i, j: (0, 0, 0, 0)),
                  pl.BlockSpec(bs_t.shape, lambda i, j: (0, 0, 0))],
        out_specs=[o[0] for o in outs],
        out_shape=[o[1] for o in outs],
        scratch_shapes=[pltpu.VMEM((B_DV, B_KEY_WIDTH), F32),
                        pltpu.VMEM((ncg, GRID_W, l // GRID_W, A_GROUP_DIM), BF16)],
        compiler_params=pltpu.CompilerParams(
            dimension_semantics=("arbitrary", "arbitrary"), vmem_limit_bytes=VMEM_LIMIT_BYTES),
        name="pre",
    )(x, mod, ng, w_pre, w2, gb, lg, lb, ctx, w_in_t, w_proj_a, w_proj_b, w_out, ws, bs_t)


def _scale_rows(x16, scales):
    parts = []
    for c, s in enumerate(scales):
        xc = x16[c * B_CHUNK:(c + 1) * B_CHUNK, :]
        parts.append(xc if s is None else (xc.astype(F32) * s).astype(BF16))
    return parts[0] if len(parts) == 1 else jnp.concatenate(parts, axis=0)


def _exp_sum(tots, idx):
    return jnp.exp2(_sum_rows([tots[m] for m in idx])) if idx else None


def _gla_block(qdf, kdf, kcf, qdb, kdb, kcb, v16, tf, tb, sf, sb, p_scr):
    nc = GLA_NC
    ch = B_CHUNK
    lane = lax.broadcasted_iota(jnp.int32, (1, LANES), 1)
    hm = [(lane < B_DK).astype(BF16), (lane >= B_DK).astype(BF16)]

    def pair_heads(x):
        return jnp.concatenate([x * hm[0], x * hm[1]], axis=0)

    row = lax.broadcasted_iota(jnp.int32, (ch, 2 * ch), 0)
    col = lax.broadcasted_iota(jnp.int32, (ch, 2 * ch), 1)
    first = col < ch
    zeros = jnp.zeros((ch, LANES), BF16)
    for g in range(B_HEADS // 2):
        ls = slice(g * LANES, (g + 1) * LANES)
        for i in range(nc // 2):
            pr = slice(2 * i * ch, (2 * i + 2) * ch)
            re = slice(2 * i * ch, (2 * i + 1) * ch)
            ro = slice((2 * i + 1) * ch, (2 * i + 2) * ch)
            sc_f = _dot_nt(pair_heads(qdf[pr, ls]), jnp.concatenate([kdf[pr, ls], kcf[re, ls], zeros], axis=0))
            sc_b = _dot_nt(pair_heads(qdb[pr, ls]), jnp.concatenate([kdb[pr, ls], zeros, kcb[ro, ls]], axis=0))
            for hh in range(2):
                he = slice(hh * 2 * ch, hh * 2 * ch + ch)
                ho = slice(hh * 2 * ch + ch, (hh + 1) * 2 * ch)
                even = jnp.where(first,
                                 jnp.where(col <= row, sc_f[he, 0:2 * ch], 0.0)
                                 + jnp.where(col >= row, sc_b[he, 0:2 * ch], 0.0),
                                 sc_b[he, 2 * ch:])
                odd = jnp.where(first, sc_f[ho, 2 * ch:],
                                jnp.where(col - ch <= row, sc_f[ho, 0:2 * ch], 0.0)
                                + jnp.where(col - ch >= row, sc_b[ho, 0:2 * ch], 0.0))
                p_scr[2 * g + hh, re, pr] = even.astype(BF16)
                p_scr[2 * g + hh, ro, pr] = odd.astype(BF16)

    def cross(lo, hi):
        if hi - lo <= 2:
            return
        mid = (lo + hi) // 2
        cross(lo, mid)
        cross(mid, hi)
        rl = slice(lo * ch, mid * ch)
        rh = slice(mid * ch, hi * ch)
        n = (hi - mid) * ch
        qf = _scale_rows(qdf[rh], [_exp_sum(tf, range(mid, c)) for c in range(mid, hi)])
        kf = _scale_rows(kcf[rl], [_exp_sum(tf, range(c + 1, mid)) for c in range(lo, mid)])
        qb = _scale_rows(qdb[rl], [_exp_sum(tb, range(c + 1, mid)) for c in range(lo, mid)])
        kb = _scale_rows(kcb[rh], [_exp_sum(tb, range(mid, c)) for c in range(mid, hi)])
        for g in range(B_HEADS // 2):
            ls = slice(g * LANES, (g + 1) * LANES)
            sc_f = _dot_nt(pair_heads(qf[:, ls]), kf[:, ls]).astype(BF16)
            sc_b = _dot_nt(pair_heads(qb[:, ls]), kb[:, ls]).astype(BF16)
            for hh in range(2):
                p_scr[2 * g + hh, rh, rl] = sc_f[hh * n:(hh + 1) * n]
                p_scr[2 * g + hh, rl, rh] = sc_b[hh * n:(hh + 1) * n]

    cross(0, nc)

    qsf = _scale_rows(qdf, [_exp_sum(tf, range(0, c)) for c in range(nc)])
    qsb = _scale_rows(qdb, [_exp_sum(tb, range(c + 1, nc)) for c in range(nc)])
    head_row = lax.broadcasted_iota(jnp.int32, (LANES, 1), 0) // B_DK
    own = [(head_row == hh).astype(BF16) for hh in range(2)]
    zv = jnp.zeros((GLA_BLOCK, B_DV), BF16)
    outs = []
    for g in range(B_HEADS // 2):
        ls = slice(g * LANES, (g + 1) * LANES)
        v0 = v16[:, 2 * g * B_DV:(2 * g + 1) * B_DV]
        v1 = v16[:, (2 * g + 1) * B_DV:(2 * g + 2) * B_DV]
        lhs = jnp.concatenate([p_scr[2 * g], p_scr[2 * g + 1], qsf[:, ls], qsb[:, ls]], axis=1)
        rhs = jnp.concatenate([
            jnp.concatenate([v0, zv], axis=1),
            jnp.concatenate([zv, v1], axis=1),
            jnp.concatenate([sf[ls, :] * own[0], sf[ls, :] * own[1]], axis=1),
            jnp.concatenate([sb[ls, :] * own[0], sb[ls, :] * own[1]], axis=1)], axis=0)
        o2 = jnp.dot(lhs, rhs, preferred_element_type=F32)
        outs.append(o2[:, 0:B_DV])
        outs.append(o2[:, B_DV:2 * B_DV])
    return outs


def _main_kernel(x_ref, mod_ref, ng_ref, w_ref, vr_ref, sc_ref,
                 qdf_ref, kdf_ref, kcf_ref, qdb_ref, kdb_ref, kcb_ref, v_ref, totf_ref, totb_ref,
                 kvf_ref, s0f_ref, sb_ref,
                 ws_ref, bs_ref, bg_ref, wpa_ref, wpb_ref, wo_ref, fg_ref, o_ref, acta_scr, on_scr, p_scr, sf_scr):
    t = x_ref.shape[1]
    blocks = [slice(blk * GLA_BLOCK, (blk + 1) * GLA_BLOCK) for blk in range(t // GLA_BLOCK)]

    @pl.when(pl.program_id(1) == 0)
    def _():
        sf_scr[...] = s0f_ref[0]

    s = sf_scr[...]
    sf_in = []
    for blk in range(len(blocks)):
        sf_in.append(s.T.astype(BF16))
        s = jnp.exp2(_sum_rows([totf_ref[0, blk, c:c + 1, :] for c in range(GLA_NC)])) * s + kvf_ref[0, blk]
    sf_scr[...] = s

    def gla(blk):
        rs = blocks[blk]
        tf = [totf_ref[0, blk, c:c + 1, :] for c in range(GLA_NC)]
        tb = [totb_ref[0, blk, c:c + 1, :] for c in range(GLA_NC)]
        o_heads = _gla_block(qdf_ref[0, rs, :], kdf_ref[0, rs, :], kcf_ref[0, rs, :],
                             qdb_ref[0, rs, :], kdb_ref[0, rs, :], kcb_ref[0, rs, :], v_ref[0, rs, :],
                             tf, tb, sf_in[blk], sb_ref[0, blk], p_scr.at[blk])
        for hd in range(B_HEADS):
            on_scr[rs, hd * B_DV:(hd + 1) * B_DV] = _rms_rows(o_heads[hd])

    def proj(c0, n):
        return jnp.dot(h16, w_ref[:, c0:c0 + n], preferred_element_type=F32)

    gla(0)
    h16 = _latent_h(x_ref[0], mod_ref, ng_ref).astype(BF16)
    zb = proj(0, B_VAL_WIDTH)
    u = proj(B_VAL_WIDTH, A_WIDTH)
    za = proj(B_VAL_WIDTH + A_WIDTH, A_WIDTH)
    for blk in range(1, len(blocks)):
        gla(blk)
    gate_a = jax.nn.sigmoid(proj(B_VAL_WIDTH + 2 * A_WIDTH, D_MODEL))

    uz = u * _silu(za)
    for g in range(A_GROUPS):
        cs = slice(g * A_GROUP_DIM, (g + 1) * A_GROUP_DIM)
        if g < A_ROW_GROUPS:
            chunks = [slice(c * A_CHUNK, (c + 1) * A_CHUNK) for c in range(t // A_CHUNK)]
            sv_all = jnp.dot(ws_ref[g].astype(BF16), jnp.concatenate([vr_ref[0, rs, cs] for rs in chunks], axis=1),
                             preferred_element_type=F32)
            for c, rs in enumerate(chunks):
                sv = sv_all[:, c * A_GROUP_DIM:(c + 1) * A_GROUP_DIM] + bs_ref[g][:, 0:1]
                acta_scr[rs, cs] = (uz[rs, cs] * sv).astype(BF16)
        else:
            sv = jnp.swapaxes(sc_ref[0, g - A_ROW_GROUPS].astype(F32), 0, 1).reshape(t, A_GROUP_DIM)
            acta_scr[:, cs] = (uz[:, cs] * sv).astype(BF16)
    ya = jnp.dot(acta_scr[...], wpa_ref[...], preferred_element_type=F32)

    gate_b = jax.nn.sigmoid(proj(B_VAL_WIDTH + 2 * A_WIDTH + D_MODEL, D_MODEL))
    yb = _dot(on_scr[...] * (_silu(zb) * bg_ref[...]), wpb_ref[...])

    m = (gate_a * ya + gate_b * yb).astype(BF16)
    for rs in blocks:
        y = jnp.dot(m[rs], wo_ref[...], preferred_element_type=F32)
        xo = x_ref[0, rs, :] + mod_ref[pl.ds(pl.program_id(0), 1), 2 * D_MODEL:] * y
        o_ref[0, rs, :] = _rms_rows(xo) * fg_ref[...]


def _main_call(x, mod, ng, w_main, vn_row, sv_col, gla_ops, ws, bs, bg, wpa, wpb, wo, fg, t):
    b, l, _ = x.shape
    nw = w_main.shape[1]
    ncg = A_GROUPS - A_ROW_GROUPS
    const2 = lambda i, j: (0, 0)
    const3 = lambda i, j: (0, 0, 0)
    tok = lambda width: pl.BlockSpec((1, t, width), lambda i, j: (i, j, 0))
    nblk = t // GLA_BLOCK
    tot_spec = pl.BlockSpec((1, nblk, GLA_NC, B_KEY_WIDTH), lambda i, j: (i, j, 0, 0))
    st_spec = pl.BlockSpec((1, nblk, B_KEY_WIDTH, B_DV), lambda i, j: (i, j, 0, 0))
    kv_spec = pl.BlockSpec((1, nblk, B_DV, B_KEY_WIDTH), lambda i, j: (i, j, 0, 0))
    s0_spec = pl.BlockSpec((1, B_DV, B_KEY_WIDTH), lambda i, j: (i, 0, 0))
    return pl.pallas_call(
        _main_kernel,
        grid=(b, l // t),
        in_specs=[tok(D_MODEL),
                  pl.BlockSpec((8, 3 * D_MODEL), const2),
                  pl.BlockSpec((1, D_MODEL), const2),
                  pl.BlockSpec((D_MODEL, nw), const2),
                  tok(A_ROW_GROUPS * A_GROUP_DIM),
                  pl.BlockSpec((1, ncg, GRID_W, t // GRID_W, A_GROUP_DIM), lambda i, j: (i, 0, 0, j, 0)),
                  tok(B_KEY_WIDTH), tok(B_KEY_WIDTH), tok(B_KEY_WIDTH),
                  tok(B_KEY_WIDTH), tok(B_KEY_WIDTH), tok(B_KEY_WIDTH), tok(B_VAL_WIDTH),
                  tot_spec, tot_spec, kv_spec, s0_spec, st_spec,
                  pl.BlockSpec((A_GROUPS, A_CHUNK, A_CHUNK), const3),
                  pl.BlockSpec((A_GROUPS, A_CHUNK, A_GROUP_DIM), const3),
                  pl.BlockSpec((1, B_VAL_WIDTH), const2),
                  pl.BlockSpec((A_WIDTH, D_MODEL), const2),
                  pl.BlockSpec((B_VAL_WIDTH, D_MODEL), const2),
                  pl.BlockSpec((D_MODEL, D_MODEL), const2),
                  pl.BlockSpec((1, D_MODEL), const2)],
        out_specs=tok(D_MODEL),
        out_shape=jax.ShapeDtypeStruct((b, l, D_MODEL), F32),
        scratch_shapes=[pltpu.VMEM((t, A_WIDTH), BF16), pltpu.VMEM((t, B_VAL_WIDTH), F32),
                        pltpu.VMEM((nblk, B_HEADS, GLA_BLOCK, GLA_BLOCK), BF16),
                        pltpu.VMEM((B_DV, B_KEY_WIDTH), F32)],
        compiler_params=pltpu.CompilerParams(
            dimension_semantics=("parallel", "arbitrary"), vmem_limit_bytes=VMEM_LIMIT_BYTES),
        name="main",
    )(x, mod, ng, w_main, vn_row, sv_col, *gla_ops, ws, bs, bg, wpa, wpb, wo, fg)


def kernel(x, c, ctx, c_ctx, w_mod, b_mod, norm_g, w_in, a_ln_g, a_ln_b, a_ws, a_bs, b_gate_w2, b_gate_b,
           b_norm_g, w_proj_a, w_proj_b, w_out, final_norm_g):
    assert w_mod.shape[0] == 1, "single-layer block"
    b, l, _ = x.shape
    rows = l // GRID_W
    assert rows == A_CHUNK
    ng = norm_g[0][None, :]

    w_in_t = w_in[0].T
    w_pre, mod, bs_t = _wprep_call(w_in_t, c, c_ctx, w_mod, b_mod, a_bs)
    w2, gb = b_gate_w2[0], b_gate_b[0]

    (qd_f, kd_f, kc_f, qd_b, kd_b, kc_b, v16, tot_f, tot_b, kv_f, s_b, vn_row, sv_col,
     w_main, wpa16, wpb16, wo16, s0f) = _pre_call(
        x, mod, ng, w_pre, w2, gb, a_ln_g[0][None, :], a_ln_b[0][None, :], ctx,
        w_in_t, w_proj_a, w_proj_b, w_out, a_ws, bs_t, TOKEN_TILE)
    gla_ops = (qd_f, kd_f, kc_f, qd_b, kd_b, kc_b, v16, tot_f, tot_b, kv_f, s0f, s_b)


    return _main_call(x, mod, ng, w_main, vn_row, sv_col, gla_ops, a_ws[0], bs_t, b_norm_g[0][None, :],
                      wpa16, wpb16, wo16, final_norm_g[None, :], TOKEN_TILE)
```

```python
import functools

import jax
import jax.numpy as jnp
from jax import lax
from jax.experimental import pallas as pl
from jax.experimental.pallas import tpu as pltpu

D_MODEL = 1024
GRID_W = 64
EPS = 1e-6

A_WIDTH = 512
A_GROUPS = 4
A_GROUP_DIM = 128
A_CHUNK = 128
A_ROW_GROUPS = 2

B_HEADS = 4
B_DK = 64
B_DV = 128
B_KEY_WIDTH = 256
B_VAL_WIDTH = 512
B_GATE_RANK = 16
B_GATE_TAU = 16.0
LOG2E = 1.4426950408889634
B_CHUNK = 64

Q0 = 0
K0 = Q0 + B_KEY_WIDTH
V0 = K0 + B_KEY_WIDTH
LR0 = V0 + B_VAL_WIDTH
ZB0 = LR0 + 2 * B_GATE_RANK
UA0 = ZB0 + B_VAL_WIDTH
VA0 = UA0 + A_WIDTH
ZA0 = VA0 + A_WIDTH
G0 = ZA0 + A_WIDTH
IN_WIDTH = G0 + 2 * D_MODEL

LANES = 128

P_LR = 0
P_K = P_LR + LANES
P_Q = P_K + B_KEY_WIDTH
P_V = P_Q + B_KEY_WIDTH
P_VA = P_V + B_VAL_WIDTH
PRE_WIDTH = P_VA + A_WIDTH
GLA_BLOCK = 256
TOKEN_TILE = 1024
COLMIX_COLUMNS = 16
GLA_NC = GLA_BLOCK // B_CHUNK

VMEM_LIMIT_BYTES = 56 * 1024 * 1024

BF16 = jnp.bfloat16
F32 = jnp.float32


def _dot(a, b):
    return jnp.dot(a.astype(BF16), b.astype(BF16), preferred_element_type=F32)


def _dot_nt(a, b):
    return lax.dot_general(a.astype(BF16), b.astype(BF16), (((1,), (1,)), ((), ())),
                           preferred_element_type=F32)


def _dot_tn(a, b):
    return lax.dot_general(a.astype(BF16), b.astype(BF16), (((0,), (0,)), ((), ())),
                           preferred_element_type=F32)


def _silu(x):
    return x * jax.nn.sigmoid(x)


def _rms_rows(x):
    return x * lax.rsqrt(jnp.mean(x * x, axis=-1, keepdims=True) + EPS)


def _put_transposed(dst_ref, c0, wt_ref, r0, n, scale=None):
    for s in range(0, n, 2 * LANES):
        m = min(2 * LANES, n - s)
        blk = wt_ref[r0 + s:r0 + s + m, :].T
        dst_ref[:, c0 + s:c0 + s + m] = (blk if scale is None else blk * scale).astype(BF16)


def _wprep_kernel(wt_ref, c_ref, cctx_ref, wm_ref, bm_ref, abs_ref, wpre_ref, mod_ref, bst_ref):
    _put_transposed(wpre_ref, P_LR, wt_ref, LR0, LANES)
    _put_transposed(wpre_ref, P_K, wt_ref, K0, B_KEY_WIDTH)
    _put_transposed(wpre_ref, P_Q, wt_ref, Q0, B_KEY_WIDTH, B_DK ** -0.5)
    _put_transposed(wpre_ref, P_V, wt_ref, V0, B_VAL_WIDTH)
    _put_transposed(wpre_ref, P_VA, wt_ref, VA0, A_WIDTH)
    nb = c_ref.shape[0]
    cc = jnp.concatenate([c_ref[...], cctx_ref[...], jnp.zeros((8 - nb - 1, D_MODEL), F32)], axis=0)
    mod_ref[...] = _dot(_silu(cc), wm_ref[0]) + bm_ref[...]
    g = pl.program_id(0)
    bst_ref[0] = jnp.broadcast_to(abs_ref[0, pl.ds(g, 1), :], (A_GROUP_DIM, A_CHUNK)).T


def _wprep_call(w_in_t, c, c_ctx, w_mod, b_mod, a_bs):
    steps = A_GROUPS
    n_mod = w_mod.shape[2]
    outs = [(pl.BlockSpec((D_MODEL // steps, PRE_WIDTH), lambda i: (i, 0)),
             jax.ShapeDtypeStruct((D_MODEL, PRE_WIDTH), BF16)),
            (pl.BlockSpec((8, n_mod // steps), lambda i: (0, i)), jax.ShapeDtypeStruct((8, n_mod), F32)),
            (pl.BlockSpec((1, A_CHUNK, A_GROUP_DIM), lambda i: (i, 0, 0)),
             jax.ShapeDtypeStruct((A_GROUPS, A_CHUNK, A_GROUP_DIM), F32))]
    return pl.pallas_call(
        _wprep_kernel,
        grid=(steps,),
        in_specs=[pl.BlockSpec((VA0 + A_WIDTH, D_MODEL // steps), lambda i: (0, i)),
                  pl.BlockSpec(c.shape, lambda i: (0, 0)),
                  pl.BlockSpec((1, D_MODEL), lambda i: (0, 0)),
                  pl.BlockSpec((1, D_MODEL, n_mod // steps), lambda i: (0, 0, i)),
                  pl.BlockSpec((1, n_mod // steps), lambda i: (0, i)),
                  pl.BlockSpec(a_bs.shape, lambda i: (0, 0, 0))],
        out_specs=[o[0] for o in outs],
        out_shape=[o[1] for o in outs],
        compiler_params=pltpu.CompilerParams(
            dimension_semantics=("parallel",), vmem_limit_bytes=VMEM_LIMIT_BYTES),
        name="wprep",
    )(w_in_t, c, c_ctx[None, :], w_mod, b_mod, a_bs)


def _gate_logs(lr, w2_ref, gb_ref):
    out = []
    for r in range(2):
        logits = _dot(lr[:, r * B_GATE_RANK:(r + 1) * B_GATE_RANK], w2_ref[r]) + gb_ref[r:r + 1, :]
        log_sig = jnp.minimum(logits, 0.0) - jnp.log(1.0 + jnp.exp(-jnp.abs(logits)))
        out.append(log_sig * (LOG2E / B_GATE_TAU))
    return out


def _chunk_tri(reverse):
    i = lax.broadcasted_iota(jnp.int32, (GLA_BLOCK, GLA_BLOCK), 0)
    j = lax.broadcasted_iota(jnp.int32, (GLA_BLOCK, GLA_BLOCK), 1)
    same = (i // B_CHUNK) == (j // B_CHUNK)
    tri = (j >= i) if reverse else (j <= i)
    return (same & tri).astype(BF16)


def _sum_rows(rows):
    acc = rows[0]
    for r in rows[1:]:
        acc = acc + r
    return acc


def _block_cum(a, tri, reverse):
    hi = a.astype(BF16)
    lo = (a - hi.astype(F32)).astype(BF16)
    cum = (jnp.dot(tri, hi, preferred_element_type=F32) + jnp.dot(tri, lo, preferred_element_type=F32))
    last = 0 if reverse else B_CHUNK - 1
    tots = [cum[c * B_CHUNK + last:c * B_CHUNK + last + 1, :] for c in range(GLA_NC)]
    return cum, tots


def _block_keys(k, cum, tots, reverse):
    totb = jnp.concatenate([jnp.broadcast_to(t, (B_CHUNK, B_KEY_WIDTH)) for t in tots], axis=0)
    kdec = k * jnp.exp2(totb - cum)
    later = []
    for c in range(GLA_NC):
        idx = list(range(0, c)) if reverse else list(range(c + 1, GLA_NC))
        if idx:
            later.append(jnp.broadcast_to(jnp.exp2(_sum_rows([tots[m] for m in idx])), (B_CHUNK, B_KEY_WIDTH)))
        else:
            later.append(jnp.ones((B_CHUNK, B_KEY_WIDTH), F32))
    return kdec, kdec * jnp.concatenate(later, axis=0)


def _pre_project(h16, w_ref, want_q):
    lr = jnp.dot(h16, w_ref[:, P_LR:P_LR + LANES], preferred_element_type=F32)[:, 0:2 * B_GATE_RANK]
    k = jnp.dot(h16, w_ref[:, P_K:P_K + B_KEY_WIDTH], preferred_element_type=F32)
    q = jnp.dot(h16, w_ref[:, P_Q:P_Q + B_KEY_WIDTH], preferred_element_type=F32) if want_q else None
    v = jnp.dot(h16, w_ref[:, P_V:P_V + B_VAL_WIDTH], preferred_element_type=F32)
    return lr, k, q, v


def _block_kv_t(v16, kblk):
    k16 = kblk.astype(BF16)
    lane = lax.broadcasted_iota(jnp.int32, (1, B_KEY_WIDTH), 1)
    lhs = jnp.concatenate([v16[:, h * B_DV:(h + 1) * B_DV] for h in range(B_HEADS)], axis=0)
    rhs = jnp.concatenate([k16 * ((lane // B_DK) == h).astype(BF16) for h in range(B_HEADS)], axis=0)
    return _dot_tn(lhs, rhs)


def _latent_h(x, mod_ref, ng_ref):
    row = pl.ds(pl.program_id(0), 1)
    shift = mod_ref[row, 0:D_MODEL]
    gain = ng_ref[...] * (1.0 + mod_ref[row, D_MODEL:2 * D_MODEL])
    return _rms_rows(x) * gain + shift


def _pre_kernel(x_ref, mod_ref, ng_ref, w_ref, w2_ref, gb_ref, lg_ref, lb_ref, ctx_ref,
                wt_ref, wpa_ref, wpb_ref, wo_ref, ws_ref, bst_ref,
                qdf_ref, kdf_ref, kcf_ref, qdb_ref, kdb_ref, kcb_ref, v_ref,
                totf_ref, totb_ref, kvf_ref, sb_ref, vr_ref, sv_ref,
                wmain_ref, wpa16_ref, wpb16_ref, wo16_ref, s0f_ref, sb_scr, vn_scr, *, ctx_row):
    @pl.when(pl.program_id(1) == 0)
    def _():
        shift = mod_ref[ctx_row:ctx_row + 1, 0:D_MODEL]
        scale = mod_ref[ctx_row:ctx_row + 1, D_MODEL:2 * D_MODEL]
        hc = _rms_rows(ctx_ref[0]) * ng_ref[...] * (1.0 + scale) + shift
        lr_c, k_c, _, v_c = _pre_project(hc.astype(BF16), w_ref, False)
        v16_c = v_c.astype(BF16)
        logs_c = _gate_logs(lr_c, w2_ref, gb_ref)
        for d, reverse in enumerate((False, True)):
            cum_c, tots_c = _block_cum(logs_c[d], _chunk_tri(reverse), reverse)
            _, kblk_c = _block_keys(k_c, cum_c, tots_c, reverse)
            state = _block_kv_t(v16_c, kblk_c)
            if reverse:
                sb_scr[...] = state
            else:
                s0f_ref[0] = state

    h16 = _latent_h(x_ref[0], mod_ref, ng_ref).astype(BF16)
    nblk = x_ref.shape[1] // GLA_BLOCK
    blocks = [slice(blk * GLA_BLOCK, (blk + 1) * GLA_BLOCK) for blk in range(nblk)]
    dirs = ((False, qdf_ref, kdf_ref, kcf_ref, totf_ref), (True, qdb_ref, kdb_ref, kcb_ref, totb_ref))

    lrk = jnp.dot(h16, w_ref[:, P_LR:P_K + B_KEY_WIDTH], preferred_element_type=F32)
    lr = lrk[:, 0:2 * B_GATE_RANK]
    k = lrk[:, P_K:P_K + B_KEY_WIDTH]
    logs = _gate_logs(lr, w2_ref, gb_ref)
    q = jnp.dot(h16, w_ref[:, P_Q:P_Q + B_KEY_WIDTH], preferred_element_type=F32)

    _put_transposed(wmain_ref, 0, wt_ref, ZB0, VA0 - ZB0)
    _put_transposed(wmain_ref, VA0 - ZB0, wt_ref, ZA0, IN_WIDTH - ZA0)
    wpa16_ref[...] = wpa_ref[0].astype(BF16)
    wpb16_ref[...] = wpb_ref[0].astype(BF16)
    wo16_ref[...] = wo_ref[0].astype(BF16)

    v16 = jnp.dot(h16, w_ref[:, P_V:P_V + B_VAL_WIDTH], preferred_element_type=F32).astype(BF16)
    v_ref[0] = v16

    cums = {}
    for d, (reverse, _, _, _, tot_ref) in enumerate(dirs):
        tri = _chunk_tri(reverse)
        for blk, rs in enumerate(blocks):
            cum, tots = _block_cum(logs[d][rs], tri, reverse)
            cums[d, blk] = (cum, tots)
            for c in range(GLA_NC):
                tot_ref[0, blk, c:c + 1, :] = tots[c]

    va = jnp.dot(h16, w_ref[:, P_VA:P_VA + A_WIDTH], preferred_element_type=F32)
    vc = va - jnp.mean(va, axis=-1, keepdims=True)
    vn = vc * lax.rsqrt(jnp.mean(vc * vc, axis=-1, keepdims=True) + EPS) * lg_ref[...] + lb_ref[...]
    nr = A_ROW_GROUPS * A_GROUP_DIM
    vr_ref[0] = vn[:, 0:nr].astype(BF16)
    n_rows = x_ref.shape[1] // GRID_W
    row0 = pl.multiple_of((pl.num_programs(1) - 1 - pl.program_id(1)) * n_rows, n_rows)
    for g in range(A_GROUPS - A_ROW_GROUPS):
        vg = vn[:, nr + g * A_GROUP_DIM:nr + (g + 1) * A_GROUP_DIM]
        vn_scr[g, :, pl.ds(row0, n_rows), :] = jnp.swapaxes(
            vg.reshape(n_rows, GRID_W, A_GROUP_DIM), 0, 1).astype(BF16)

    kv_b = {}
    for d, (reverse, qd_ref, kd_ref, kc_ref, _) in enumerate(dirs):
        for blk, rs in enumerate(blocks):
            cum, tots = cums[d, blk]
            kdec, kblk = _block_keys(k[rs], cum, tots, reverse)
            kc_ref[0, rs, :] = kdec.astype(BF16)
            kd_ref[0, rs, :] = (k[rs] * jnp.exp2(-cum)).astype(BF16)
            qd_ref[0, rs, :] = (q[rs] * jnp.exp2(cum)).astype(BF16)
            if reverse:
                kv_b[blk] = _block_kv_t(v16[rs], kblk)
            else:
                kvf_ref[0, blk] = _block_kv_t(v16[rs], kblk)

    s = sb_scr[...]
    for blk in reversed(range(nblk)):
        sb_ref[0, blk] = s.T.astype(BF16)
        s = jnp.exp2(_sum_rows(cums[1, blk][1])) * s + kv_b[blk]
    sb_scr[...] = s

    @pl.when(pl.program_id(1) == pl.num_programs(1) - 1)
    def _():
        for g in range(A_GROUPS - A_ROW_GROUPS):
            w_mix = ws_ref[0, A_ROW_GROUPS + g].astype(BF16)
            bias = bst_ref[A_ROW_GROUPS + g][:, 0:1]
            for w0 in range(0, GRID_W, COLMIX_COLUMNS):
                xs = jnp.concatenate([vn_scr[g, w] for w in range(w0, w0 + COLMIX_COLUMNS)], axis=1)
                y = jnp.dot(w_mix, xs, preferred_element_type=F32) + bias
                for w in range(COLMIX_COLUMNS):
                    sv_ref[0, g, w0 + w] = y[:, w * A_GROUP_DIM:(w + 1) * A_GROUP_DIM].astype(BF16)


def _pre_call(x, mod, ng, w_pre, w2, gb, lg, lb, ctx, w_in_t, w_proj_a, w_proj_b, w_out, ws, bs_t, t):
    b, l, _ = x.shape
    assert ctx.shape[1] == GLA_BLOCK, "context length must be one GLA block"
    nw = w_pre.shape[1]
    ncg = A_GROUPS - A_ROW_GROUPS
    nblk = t // GLA_BLOCK
    nj = l // t
    n_main = (VA0 - ZB0) + (IN_WIDTH - ZA0)
    assert D_MODEL % nj == 0 and (D_MODEL // nj) % LANES == 0, "one weight strip per sequence step"
    strip = lambda i, j: jnp.where(i == 0, j, nj - 1)

    def wrows(a):
        return pl.BlockSpec((1, a.shape[1] // nj, a.shape[2]), lambda i, j: (0, strip(i, j), 0))

    def wout(nrows, ncols):
        return (pl.BlockSpec((nrows // nj, ncols), lambda i, j: (strip(i, j), 0)),
                jax.ShapeDtypeStruct((nrows, ncols), BF16))
    rev3 = lambda i, j: (i, nj - 1 - j, 0)
    rev4 = lambda i, j: (i, nj - 1 - j, 0, 0)

    def tok(width, dtype):
        return (pl.BlockSpec((1, t, width), rev3), jax.ShapeDtypeStruct((b, l, width), dtype))

    tot = (pl.BlockSpec((1, nblk, GLA_NC, B_KEY_WIDTH), rev4),
           jax.ShapeDtypeStruct((b, l // GLA_BLOCK, GLA_NC, B_KEY_WIDTH), F32))
    kv = (pl.BlockSpec((1, nblk, B_DV, B_KEY_WIDTH), rev4),
          jax.ShapeDtypeStruct((b, l // GLA_BLOCK, B_DV, B_KEY_WIDTH), F32))
    st = (pl.BlockSpec((1, nblk, B_KEY_WIDTH, B_DV), rev4),
          jax.ShapeDtypeStruct((b, l // GLA_BLOCK, B_KEY_WIDTH, B_DV), BF16))
    outs = [tok(B_KEY_WIDTH, BF16)] * 6 + [tok(B_VAL_WIDTH, BF16), tot, tot, kv, st,
                                           tok(A_ROW_GROUPS * A_GROUP_DIM, BF16),
                                           (pl.BlockSpec((1, ncg, GRID_W, l // GRID_W, A_GROUP_DIM),
                                                         lambda i, j: (i, 0, 0, 0, 0)),
                                            jax.ShapeDtypeStruct((b, ncg, GRID_W, l // GRID_W, A_GROUP_DIM), BF16)),
                                           wout(D_MODEL, n_main), wout(A_WIDTH, D_MODEL),
                                           wout(B_VAL_WIDTH, D_MODEL), wout(D_MODEL, D_MODEL),
                                           (pl.BlockSpec((1, B_DV, B_KEY_WIDTH), lambda i, j: (i, 0, 0)),
                                            jax.ShapeDtypeStruct((b, B_DV, B_KEY_WIDTH), F32))]
    const2 = lambda i, j: (0, 0)
    return pl.pallas_call(
        functools.partial(_pre_kernel, ctx_row=b),
        grid=(b, nj),
        in_specs=[pl.BlockSpec((1, t, D_MODEL), rev3),
                  pl.BlockSpec((8, 3 * D_MODEL), const2),
                  pl.BlockSpec((1, D_MODEL), const2),
                  pl.BlockSpec((D_MODEL, nw), const2),
                  pl.BlockSpec((2, B_GATE_RANK, B_KEY_WIDTH), lambda i, j: (0, 0, 0)),
                  pl.BlockSpec((2, B_KEY_WIDTH), const2),
                  pl.BlockSpec((1, A_WIDTH), const2),
                  pl.BlockSpec((1, A_WIDTH), const2),
                  pl.BlockSpec((1, GLA_BLOCK, D_MODEL), lambda i, j: (i, 0, 0)),
                  pl.BlockSpec((IN_WIDTH, D_MODEL // nj), lambda i, j: (0, strip(i, j))),
                  wrows(w_proj_a), wrows(w_proj_b), wrows(w_out),
                  pl.BlockSpec(ws.shape, lambda i, j: (0, 0, 0, 0)),
                  pl.BlockSpec(bs_t.shape, lambda i, j: (0, 0, 0))],
        out_specs=[o[0] for o in outs],
        out_shape=[o[1] for o in outs],
        scratch_shapes=[pltpu.VMEM((B_DV, B_KEY_WIDTH), F32),
                        pltpu.VMEM((ncg, GRID_W, l // GRID_W, A_GROUP_DIM), BF16)],
        compiler_params=pltpu.CompilerParams(
            dimension_semantics=("arbitrary", "arbitrary"), vmem_limit_bytes=VMEM_LIMIT_BYTES),
        name="pre",
    )(x, mod, ng, w_pre, w2, gb, lg, lb, ctx, w_in_t, w_proj_a, w_proj_b, w_out, ws, bs_t)


def _scale_rows(x16, scales):
    parts = []
    for c, s in enumerate(scales):
        xc = x16[c * B_CHUNK:(c + 1) * B_CHUNK, :]
        parts.append(xc if s is None else (xc.astype(F32) * s).astype(BF16))
    return parts[0] if len(parts) == 1 else jnp.concatenate(parts, axis=0)


def _exp_sum(tots, idx):
    return jnp.exp2(_sum_rows([tots[m] for m in idx])) if idx else None


def _gla_block(qdf, kdf, kcf, qdb, kdb, kcb, v16, tf, tb, sf, sb, p_scr):
    nc = GLA_NC
    ch = B_CHUNK
    lane = lax.broadcasted_iota(jnp.int32, (1, LANES), 1)
    hm = [(lane < B_DK).astype(BF16), (lane >= B_DK).astype(BF16)]

    def pair_heads(x):
        return jnp.concatenate([x * hm[0], x * hm[1]], axis=0)

    row = lax.broadcasted_iota(jnp.int32, (ch, 2 * ch), 0)
    col = lax.broadcasted_iota(jnp.int32, (ch, 2 * ch), 1)
    first = col < ch
    zeros = jnp.zeros((ch, LANES), BF16)
    for g in range(B_HEADS // 2):
        ls = slice(g * LANES, (g + 1) * LANES)
        for i in range(nc // 2):
            pr = slice(2 * i * ch, (2 * i + 2) * ch)
            re = slice(2 * i * ch, (2 * i + 1) * ch)
            ro = slice((2 * i + 1) * ch, (2 * i + 2) * ch)
            sc_f = _dot_nt(pair_heads(qdf[pr, ls]), jnp.concatenate([kdf[pr, ls], kcf[re, ls], zeros], axis=0))
            sc_b = _dot_nt(pair_heads(qdb[pr, ls]), jnp.concatenate([kdb[pr, ls], zeros, kcb[ro, ls]], axis=0))
            for hh in range(2):
                he = slice(hh * 2 * ch, hh * 2 * ch + ch)
                ho = slice(hh * 2 * ch + ch, (hh + 1) * 2 * ch)
                even = jnp.where(first,
                                 jnp.where(col <= row, sc_f[he, 0:2 * ch], 0.0)
                                 + jnp.where(col >= row, sc_b[he, 0:2 * ch], 0.0),
                                 sc_b[he, 2 * ch:])
                odd = jnp.where(first, sc_f[ho, 2 * ch:],
                                jnp.where(col - ch <= row, sc_f[ho, 0:2 * ch], 0.0)
                                + jnp.where(col - ch >= row, sc_b[ho, 0:2 * ch], 0.0))
                p_scr[2 * g + hh, re, pr] = even.astype(BF16)
                p_scr[2 * g + hh, ro, pr] = odd.astype(BF16)

    def cross(lo, hi):
        if hi - lo <= 2:
            return
        mid = (lo + hi) // 2
        cross(lo, mid)
        cross(mid, hi)
        rl = slice(lo * ch, mid * ch)
        rh = slice(mid * ch, hi * ch)
        n = (hi - mid) * ch
        qf = _scale_rows(qdf[rh], [_exp_sum(tf, range(mid, c)) for c in range(mid, hi)])
        kf = _scale_rows(kcf[rl], [_exp_sum(tf, range(c + 1, mid)) for c in range(lo, mid)])
        qb = _scale_rows(qdb[rl], [_exp_sum(tb, range(c + 1, mid)) for c in range(lo, mid)])
        kb = _scale_rows(kcb[rh], [_exp_sum(tb, range(mid, c)) for c in range(mid, hi)])
        for g in range(B_HEADS // 2):
            ls = slice(g * LANES, (g + 1) * LANES)
            sc_f = _dot_nt(pair_heads(qf[:, ls]), kf[:, ls]).astype(BF16)
            sc_b = _dot_nt(pair_heads(qb[:, ls]), kb[:, ls]).astype(BF16)
            for hh in range(2):
                p_scr[2 * g + hh, rh, rl] = sc_f[hh * n:(hh + 1) * n]
                p_scr[2 * g + hh, rl, rh] = sc_b[hh * n:(hh + 1) * n]

    cross(0, nc)

    qsf = _scale_rows(qdf, [_exp_sum(tf, range(0, c)) for c in range(nc)])
    qsb = _scale_rows(qdb, [_exp_sum(tb, range(c + 1, nc)) for c in range(nc)])
    head_row = lax.broadcasted_iota(jnp.int32, (LANES, 1), 0) // B_DK
    outs = []
    for h in range(B_HEADS):
        ls = slice((h // 2) * LANES, (h // 2 + 1) * LANES)
        own = (head_row == (h % 2)).astype(BF16)
        lhs = jnp.concatenate([p_scr[h], qsf[:, ls], qsb[:, ls]], axis=1)
        rhs = jnp.concatenate([v16[:, h * B_DV:(h + 1) * B_DV], sf[ls, :] * own, sb[ls, :] * own], axis=0)
        outs.append(jnp.dot(lhs, rhs, preferred_element_type=F32))
    return outs


def _main_kernel(x_ref, mod_ref, ng_ref, w_ref, vr_ref, sc_ref,
                 qdf_ref, kdf_ref, kcf_ref, qdb_ref, kdb_ref, kcb_ref, v_ref, totf_ref, totb_ref,
                 kvf_ref, s0f_ref, sb_ref,
                 ws_ref, bs_ref, bg_ref, wpa_ref, wpb_ref, wo_ref, fg_ref, o_ref, acta_scr, on_scr, p_scr, sf_scr):
    t = x_ref.shape[1]
    blocks = [slice(blk * GLA_BLOCK, (blk + 1) * GLA_BLOCK) for blk in range(t // GLA_BLOCK)]

    @pl.when(pl.program_id(1) == 0)
    def _():
        sf_scr[...] = s0f_ref[0]

    s = sf_scr[...]
    sf_in = []
    for blk in range(len(blocks)):
        sf_in.append(s.T.astype(BF16))
        s = jnp.exp2(_sum_rows([totf_ref[0, blk, c:c + 1, :] for c in range(GLA_NC)])) * s + kvf_ref[0, blk]
    sf_scr[...] = s

    def gla(blk):
        rs = blocks[blk]
        tf = [totf_ref[0, blk, c:c + 1, :] for c in range(GLA_NC)]
        tb = [totb_ref[0, blk, c:c + 1, :] for c in range(GLA_NC)]
        o_heads = _gla_block(qdf_ref[0, rs, :], kdf_ref[0, rs, :], kcf_ref[0, rs, :],
                             qdb_ref[0, rs, :], kdb_ref[0, rs, :], kcb_ref[0, rs, :], v_ref[0, rs, :],
                             tf, tb, sf_in[blk], sb_ref[0, blk], p_scr.at[blk])
        for hd in range(B_HEADS):
            on_scr[rs, hd * B_DV:(hd + 1) * B_DV] = _rms_rows(o_heads[hd])

    def proj(c0, n):
        return jnp.dot(h16, w_ref[:, c0:c0 + n], preferred_element_type=F32)

    gla(0)
    h16 = _latent_h(x_ref[0], mod_ref, ng_ref).astype(BF16)
    zb = proj(0, B_VAL_WIDTH)
    u = proj(B_VAL_WIDTH, A_WIDTH)
    za = proj(B_VAL_WIDTH + A_WIDTH, A_WIDTH)
    for blk in range(1, len(blocks)):
        gla(blk)
    gate_a = jax.nn.sigmoid(proj(B_VAL_WIDTH + 2 * A_WIDTH, D_MODEL))

    uz = u * _silu(za)
    for g in range(A_GROUPS):
        cs = slice(g * A_GROUP_DIM, (g + 1) * A_GROUP_DIM)
        if g < A_ROW_GROUPS:
            chunks = [slice(c * A_CHUNK, (c + 1) * A_CHUNK) for c in range(t // A_CHUNK)]
            sv_all = jnp.dot(ws_ref[g].astype(BF16), jnp.concatenate([vr_ref[0, rs, cs] for rs in chunks], axis=1),
                             preferred_element_type=F32)
            for c, rs in enumerate(chunks):
                sv = sv_all[:, c * A_GROUP_DIM:(c + 1) * A_GROUP_DIM] + bs_ref[g][:, 0:1]
                acta_scr[rs, cs] = (uz[rs, cs] * sv).astype(BF16)
        else:
            sv = jnp.swapaxes(sc_ref[0, g - A_ROW_GROUPS].astype(F32), 0, 1).reshape(t, A_GROUP_DIM)
            acta_scr[:, cs] = (uz[:, cs] * sv).astype(BF16)
    ya = jnp.dot(acta_scr[...], wpa_ref[...], preferred_element_type=F32)

    gate_b = jax.nn.sigmoid(proj(B_VAL_WIDTH + 2 * A_WIDTH + D_MODEL, D_MODEL))
    yb = _dot(on_scr[...] * (_silu(zb) * bg_ref[...]), wpb_ref[...])

    m = (gate_a * ya + gate_b * yb).astype(BF16)
    for rs in blocks:
        y = jnp.dot(m[rs], wo_ref[...], preferred_element_type=F32)
        xo = x_ref[0, rs, :] + mod_ref[pl.ds(pl.program_id(0), 1), 2 * D_MODEL:] * y
        o_ref[0, rs, :] = _rms_rows(xo) * fg_ref[...]


def _main_call(x, mod, ng, w_main, vn_row, sv_col, gla_ops, ws, bs, bg, wpa, wpb, wo, fg, t):
    b, l, _ = x.shape
    nw = w_main.shape[1]
    ncg = A_GROUPS - A_ROW_GROUPS
    const2 = lambda i, j: (0, 0)
    const3 = lambda i, j: (0, 0, 0)
    tok = lambda width: pl.BlockSpec((1, t, width), lambda i, j: (i, j, 0))
    nblk = t // GLA_BLOCK
    tot_spec = pl.BlockSpec((1, nblk, GLA_NC, B_KEY_WIDTH), lambda i, j: (i, j, 0, 0))
    st_spec = pl.BlockSpec((1, nblk, B_KEY_WIDTH, B_DV), lambda i, j: (i, j, 0, 0))
    kv_spec = pl.BlockSpec((1, nblk, B_DV, B_KEY_WIDTH), lambda i, j: (i, j, 0, 0))
    s0_spec = pl.BlockSpec((1, B_DV, B_KEY_WIDTH), lambda i, j: (i, 0, 0))
    return pl.pallas_call(
        _main_kernel,
        grid=(b, l // t),
        in_specs=[tok(D_MODEL),
                  pl.BlockSpec((8, 3 * D_MODEL), const2),
                  pl.BlockSpec((1, D_MODEL), const2),
                  pl.BlockSpec((D_MODEL, nw), const2),
                  tok(A_ROW_GROUPS * A_GROUP_DIM),
                  pl.BlockSpec((1, ncg, GRID_W, t // GRID_W, A_GROUP_DIM), lambda i, j: (i, 0, 0, j, 0)),
                  tok(B_KEY_WIDTH), tok(B_KEY_WIDTH), tok(B_KEY_WIDTH),
                  tok(B_KEY_WIDTH), tok(B_KEY_WIDTH), tok(B_KEY_WIDTH), tok(B_VAL_WIDTH),
                  tot_spec, tot_spec, kv_spec, s0_spec, st_spec,
                  pl.BlockSpec((A_GROUPS, A_CHUNK, A_CHUNK), const3),
                  pl.BlockSpec((A_GROUPS, A_CHUNK, A_GROUP_DIM), const3),
                  pl.BlockSpec((1, B_VAL_WIDTH), const2),
                  pl.BlockSpec((A_WIDTH, D_MODEL), const2),
                  pl.BlockSpec((B_VAL_WIDTH, D_MODEL), const2),
                  pl.BlockSpec((D_MODEL, D_MODEL), const2),
                  pl.BlockSpec((1, D_MODEL), const2)],
        out_specs=tok(D_MODEL),
        out_shape=jax.ShapeDtypeStruct((b, l, D_MODEL), F32),
        scratch_shapes=[pltpu.VMEM((t, A_WIDTH), BF16), pltpu.VMEM((t, B_VAL_WIDTH), F32),
                        pltpu.VMEM((nblk, B_HEADS, GLA_BLOCK, GLA_BLOCK), BF16),
                        pltpu.VMEM((B_DV, B_KEY_WIDTH), F32)],
        compiler_params=pltpu.CompilerParams(
            dimension_semantics=("parallel", "arbitrary"), vmem_limit_bytes=VMEM_LIMIT_BYTES),
        name="main",
    )(x, mod, ng, w_main, vn_row, sv_col, *gla_ops, ws, bs, bg, wpa, wpb, wo, fg)


def kernel(x, c, ctx, c_ctx, w_mod, b_mod, norm_g, w_in, a_ln_g, a_ln_b, a_ws, a_bs, b_gate_w2, b_gate_b,
           b_norm_g, w_proj_a, w_proj_b, w_out, final_norm_g):
    assert w_mod.shape[0] == 1, "single-layer block"
    b, l, _ = x.shape
    rows = l // GRID_W
    assert rows == A_CHUNK
    ng = norm_g[0][None, :]

    w_in_t = w_in[0].T
    w_pre, mod, bs_t = _wprep_call(w_in_t, c, c_ctx, w_mod, b_mod, a_bs)
    w2, gb = b_gate_w2[0], b_gate_b[0]

    (qd_f, kd_f, kc_f, qd_b, kd_b, kc_b, v16, tot_f, tot_b, kv_f, s_b, vn_row, sv_col,
     w_main, wpa16, wpb16, wo16, s0f) = _pre_call(
        x, mod, ng, w_pre, w2, gb, a_ln_g[0][None, :], a_ln_b[0][None, :], ctx,
        w_in_t, w_proj_a, w_proj_b, w_out, a_ws, bs_t, TOKEN_TILE)
    gla_ops = (qd_f, kd_f, kc_f, qd_b, kd_b, kc_b, v16, tot_f, tot_b, kv_f, s0f, s_b)


    return _main_call(x, mod, ng, w_main, vn_row, sv_col, gla_ops, a_ws[0], bs_t, b_norm_g[0][None, :],
                      wpa16, wpb16, wo16, final_norm_g[None, :], TOKEN_TILE)
```

```python
import functools

import jax
import jax.numpy as jnp
from jax import lax
from jax.experimental import pallas as pl
from jax.experimental.pallas import tpu as pltpu

D_MODEL = 1024
GRID_W = 64
EPS = 1e-6

A_WIDTH = 512
A_GROUPS = 4
A_GROUP_DIM = 128
A_CHUNK = 128
A_ROW_GROUPS = 2

B_HEADS = 4
B_DK = 64
B_DV = 128
B_KEY_WIDTH = 256
B_VAL_WIDTH = 512
B_GATE_RANK = 16
B_GATE_TAU = 16.0
LOG2E = 1.4426950408889634
B_CHUNK = 64

Q0 = 0
K0 = Q0 + B_KEY_WIDTH
V0 = K0 + B_KEY_WIDTH
LR0 = V0 + B_VAL_WIDTH
ZB0 = LR0 + 2 * B_GATE_RANK
UA0 = ZB0 + B_VAL_WIDTH
VA0 = UA0 + A_WIDTH
ZA0 = VA0 + A_WIDTH
G0 = ZA0 + A_WIDTH
IN_WIDTH = G0 + 2 * D_MODEL

LANES = 128

P_LR = 0
P_K = P_LR + LANES
P_Q = P_K + B_KEY_WIDTH
P_V = P_Q + B_KEY_WIDTH
P_VA = P_V + B_VAL_WIDTH
PRE_WIDTH = P_VA + A_WIDTH
GLA_BLOCK = 256
TOKEN_TILE = 1024
COLMIX_COLUMNS = 16
GLA_NC = GLA_BLOCK // B_CHUNK

VMEM_LIMIT_BYTES = 56 * 1024 * 1024

BF16 = jnp.bfloat16
F32 = jnp.float32


def _dot(a, b):
    return jnp.dot(a.astype(BF16), b.astype(BF16), preferred_element_type=F32)


def _dot_nt(a, b):
    return lax.dot_general(a.astype(BF16), b.astype(BF16), (((1,), (1,)), ((), ())),
                           preferred_element_type=F32)


def _dot_tn(a, b):
    return lax.dot_general(a.astype(BF16), b.astype(BF16), (((0,), (0,)), ((), ())),
                           preferred_element_type=F32)


def _silu(x):
    return x * jax.nn.sigmoid(x)


def _rms_rows(x):
    return x * lax.rsqrt(jnp.mean(x * x, axis=-1, keepdims=True) + EPS)


def _put_transposed(dst_ref, c0, wt_ref, r0, n, scale=None):
    for s in range(0, n, 2 * LANES):
        m = min(2 * LANES, n - s)
        blk = wt_ref[r0 + s:r0 + s + m, :].T
        dst_ref[:, c0 + s:c0 + s + m] = (blk if scale is None else blk * scale).astype(BF16)


def _wprep_kernel(wt_ref, c_ref, cctx_ref, wm_ref, bm_ref, abs_ref, wpre_ref, mod_ref, bst_ref):
    _put_transposed(wpre_ref, P_LR, wt_ref, LR0, LANES)
    _put_transposed(wpre_ref, P_K, wt_ref, K0, B_KEY_WIDTH)
    _put_transposed(wpre_ref, P_Q, wt_ref, Q0, B_KEY_WIDTH, B_DK ** -0.5)
    _put_transposed(wpre_ref, P_V, wt_ref, V0, B_VAL_WIDTH)
    _put_transposed(wpre_ref, P_VA, wt_ref, VA0, A_WIDTH)
    nb = c_ref.shape[0]
    cc = jnp.concatenate([c_ref[...], cctx_ref[...], jnp.zeros((8 - nb - 1, D_MODEL), F32)], axis=0)
    mod_ref[...] = _dot(_silu(cc), wm_ref[0]) + bm_ref[...]
    g = pl.program_id(0)
    bst_ref[0] = jnp.broadcast_to(abs_ref[0, pl.ds(g, 1), :], (A_GROUP_DIM, A_CHUNK)).T


def _wprep_call(w_in_t, c, c_ctx, w_mod, b_mod, a_bs):
    steps = A_GROUPS
    n_mod = w_mod.shape[2]
    outs = [(pl.BlockSpec((D_MODEL // steps, PRE_WIDTH), lambda i: (i, 0)),
             jax.ShapeDtypeStruct((D_MODEL, PRE_WIDTH), BF16)),
            (pl.BlockSpec((8, n_mod // steps), lambda i: (0, i)), jax.ShapeDtypeStruct((8, n_mod), F32)),
            (pl.BlockSpec((1, A_CHUNK, A_GROUP_DIM), lambda i: (i, 0, 0)),
             jax.ShapeDtypeStruct((A_GROUPS, A_CHUNK, A_GROUP_DIM), F32))]
    return pl.pallas_call(
        _wprep_kernel,
        grid=(steps,),
        in_specs=[pl.BlockSpec((VA0 + A_WIDTH, D_MODEL // steps), lambda i: (0, i)),
                  pl.BlockSpec(c.shape, lambda i: (0, 0)),
                  pl.BlockSpec((1, D_MODEL), lambda i: (0, 0)),
                  pl.BlockSpec((1, D_MODEL, n_mod // steps), lambda i: (0, 0, i)),
                  pl.BlockSpec((1, n_mod // steps), lambda i: (0, i)),
                  pl.BlockSpec(a_bs.shape, lambda i: (0, 0, 0))],
        out_specs=[o[0] for o in outs],
        out_shape=[o[1] for o in outs],
        compiler_params=pltpu.CompilerParams(
            dimension_semantics=("parallel",), vmem_limit_bytes=VMEM_LIMIT_BYTES),
        name="wprep",
    )(w_in_t, c, c_ctx[None, :], w_mod, b_mod, a_bs)


def _gate_logs(lr, w2_ref, gb_ref):
    out = []
    for r in range(2):
        logits = _dot(lr[:, r * B_GATE_RANK:(r + 1) * B_GATE_RANK], w2_ref[r]) + gb_ref[r:r + 1, :]
        log_sig = jnp.minimum(logits, 0.0) - jnp.log(1.0 + jnp.exp(-jnp.abs(logits)))
        out.append(log_sig * (LOG2E / B_GATE_TAU))
    return out


def _chunk_tri(reverse):
    i = lax.broadcasted_iota(jnp.int32, (GLA_BLOCK, GLA_BLOCK), 0)
    j = lax.broadcasted_iota(jnp.int32, (GLA_BLOCK, GLA_BLOCK), 1)
    same = (i // B_CHUNK) == (j // B_CHUNK)
    tri = (j >= i) if reverse else (j <= i)
    return (same & tri).astype(BF16)


def _sum_rows(rows):
    acc = rows[0]
    for r in rows[1:]:
        acc = acc + r
    return acc


def _block_cum(a, tri, reverse):
    hi = a.astype(BF16)
    lo = (a - hi.astype(F32)).astype(BF16)
    cum = jnp.dot(jnp.concatenate([tri, tri], axis=1), jnp.concatenate([hi, lo], axis=0),
                  preferred_element_type=F32)
    last = 0 if reverse else B_CHUNK - 1
    tots = [cum[c * B_CHUNK + last:c * B_CHUNK + last + 1, :] for c in range(GLA_NC)]
    return cum, tots


def _block_keys(k, cum, tots, reverse):
    totb = jnp.concatenate([jnp.broadcast_to(t, (B_CHUNK, B_KEY_WIDTH)) for t in tots], axis=0)
    kdec = k * jnp.exp2(totb - cum)
    later = []
    for c in range(GLA_NC):
        idx = list(range(0, c)) if reverse else list(range(c + 1, GLA_NC))
        if idx:
            later.append(jnp.broadcast_to(jnp.exp2(_sum_rows([tots[m] for m in idx])), (B_CHUNK, B_KEY_WIDTH)))
        else:
            later.append(jnp.ones((B_CHUNK, B_KEY_WIDTH), F32))
    return kdec, kdec * jnp.concatenate(later, axis=0)


def _pre_project(h16, w_ref, want_q):
    lr = jnp.dot(h16, w_ref[:, P_LR:P_LR + LANES], preferred_element_type=F32)[:, 0:2 * B_GATE_RANK]
    k = jnp.dot(h16, w_ref[:, P_K:P_K + B_KEY_WIDTH], preferred_element_type=F32)
    q = jnp.dot(h16, w_ref[:, P_Q:P_Q + B_KEY_WIDTH], preferred_element_type=F32) if want_q else None
    v = jnp.dot(h16, w_ref[:, P_V:P_V + B_VAL_WIDTH], preferred_element_type=F32)
    return lr, k, q, v


def _block_kv_t(v16, kblk):
    k16 = kblk.astype(BF16)
    lane = lax.broadcasted_iota(jnp.int32, (1, B_KEY_WIDTH), 1)
    lhs = jnp.concatenate([v16[:, h * B_DV:(h + 1) * B_DV] for h in range(B_HEADS)], axis=0)
    rhs = jnp.concatenate([k16 * ((lane // B_DK) == h).astype(BF16) for h in range(B_HEADS)], axis=0)
    return _dot_tn(lhs, rhs)


def _latent_h(x, mod_ref, ng_ref):
    row = pl.ds(pl.program_id(0), 1)
    shift = mod_ref[row, 0:D_MODEL]
    gain = ng_ref[...] * (1.0 + mod_ref[row, D_MODEL:2 * D_MODEL])
    return _rms_rows(x) * gain + shift


def _pre_kernel(x_ref, mod_ref, ng_ref, w_ref, w2_ref, gb_ref, lg_ref, lb_ref, ctx_ref,
                wt_ref, wpa_ref, wpb_ref, wo_ref, ws_ref, bst_ref,
                qdf_ref, kdf_ref, kcf_ref, qdb_ref, kdb_ref, kcb_ref, v_ref,
                totf_ref, totb_ref, kvf_ref, sb_ref, vr_ref, sv_ref,
                wmain_ref, wpa16_ref, wpb16_ref, wo16_ref, s0f_ref, sb_scr, vn_scr, *, ctx_row):
    @pl.when(pl.program_id(1) == 0)
    def _():
        shift = mod_ref[ctx_row:ctx_row + 1, 0:D_MODEL]
        scale = mod_ref[ctx_row:ctx_row + 1, D_MODEL:2 * D_MODEL]
        hc = _rms_rows(ctx_ref[0]) * ng_ref[...] * (1.0 + scale) + shift
        lr_c, k_c, _, v_c = _pre_project(hc.astype(BF16), w_ref, False)
        v16_c = v_c.astype(BF16)
        logs_c = _gate_logs(lr_c, w2_ref, gb_ref)
        for d, reverse in enumerate((False, True)):
            cum_c, tots_c = _block_cum(logs_c[d], _chunk_tri(reverse), reverse)
            _, kblk_c = _block_keys(k_c, cum_c, tots_c, reverse)
            state = _block_kv_t(v16_c, kblk_c)
            if reverse:
                sb_scr[...] = state
            else:
                s0f_ref[0] = state

    h16 = _latent_h(x_ref[0], mod_ref, ng_ref).astype(BF16)
    nblk = x_ref.shape[1] // GLA_BLOCK
    blocks = [slice(blk * GLA_BLOCK, (blk + 1) * GLA_BLOCK) for blk in range(nblk)]
    dirs = ((False, qdf_ref, kdf_ref, kcf_ref, totf_ref), (True, qdb_ref, kdb_ref, kcb_ref, totb_ref))

    lrk = jnp.dot(h16, w_ref[:, P_LR:P_K + B_KEY_WIDTH], preferred_element_type=F32)
    lr = lrk[:, 0:2 * B_GATE_RANK]
    k = lrk[:, P_K:P_K + B_KEY_WIDTH]
    logs = _gate_logs(lr, w2_ref, gb_ref)
    q = jnp.dot(h16, w_ref[:, P_Q:P_Q + B_KEY_WIDTH], preferred_element_type=F32)

    _put_transposed(wmain_ref, 0, wt_ref, ZB0, VA0 - ZB0)
    _put_transposed(wmain_ref, VA0 - ZB0, wt_ref, ZA0, IN_WIDTH - ZA0)
    wpa16_ref[...] = wpa_ref[0].astype(BF16)
    wpb16_ref[...] = wpb_ref[0].astype(BF16)
    wo16_ref[...] = wo_ref[0].astype(BF16)

    v16 = jnp.dot(h16, w_ref[:, P_V:P_V + B_VAL_WIDTH], preferred_element_type=F32).astype(BF16)
    v_ref[0] = v16

    cums = {}
    for d, (reverse, _, _, _, tot_ref) in enumerate(dirs):
        tri = _chunk_tri(reverse)
        for blk, rs in enumerate(blocks):
            cum, tots = _block_cum(logs[d][rs], tri, reverse)
            cums[d, blk] = (cum, tots)
            for c in range(GLA_NC):
                tot_ref[0, blk, c:c + 1, :] = tots[c]

    va = jnp.dot(h16, w_ref[:, P_VA:P_VA + A_WIDTH], preferred_element_type=F32)
    vc = va - jnp.mean(va, axis=-1, keepdims=True)
    vn = vc * lax.rsqrt(jnp.mean(vc * vc, axis=-1, keepdims=True) + EPS) * lg_ref[...] + lb_ref[...]
    nr = A_ROW_GROUPS * A_GROUP_DIM
    vr_ref[0] = vn[:, 0:nr].astype(BF16)
    n_rows = x_ref.shape[1] // GRID_W
    row0 = pl.multiple_of((pl.num_programs(1) - 1 - pl.program_id(1)) * n_rows, n_rows)
    for g in range(A_GROUPS - A_ROW_GROUPS):
        vg = vn[:, nr + g * A_GROUP_DIM:nr + (g + 1) * A_GROUP_DIM]
        vn_scr[g, :, pl.ds(row0, n_rows), :] = jnp.swapaxes(
            vg.reshape(n_rows, GRID_W, A_GROUP_DIM), 0, 1).astype(BF16)

    kv_b = {}
    for d, (reverse, qd_ref, kd_ref, kc_ref, _) in enumerate(dirs):
        for blk, rs in enumerate(blocks):
            cum, tots = cums[d, blk]
            kdec, kblk = _block_keys(k[rs], cum, tots, reverse)
            kc_ref[0, rs, :] = kdec.astype(BF16)
            kd_ref[0, rs, :] = (k[rs] * jnp.exp2(-cum)).astype(BF16)
            qd_ref[0, rs, :] = (q[rs] * jnp.exp2(cum)).astype(BF16)
            if reverse:
                kv_b[blk] = _block_kv_t(v16[rs], kblk)
            else:
                kvf_ref[0, blk] = _block_kv_t(v16[rs], kblk)

    s = sb_scr[...]
    for blk in reversed(range(nblk)):
        sb_ref[0, blk] = s.T.astype(BF16)
        s = jnp.exp2(_sum_rows(cums[1, blk][1])) * s + kv_b[blk]
    sb_scr[...] = s

    @pl.when(pl.program_id(1) == pl.num_programs(1) - 1)
    def _():
        for g in range(A_GROUPS - A_ROW_GROUPS):
            w_mix = ws_ref[0, A_ROW_GROUPS + g].astype(BF16)
            bias = bst_ref[A_ROW_GROUPS + g][:, 0:1]
            for w0 in range(0, GRID_W, COLMIX_COLUMNS):
                xs = jnp.concatenate([vn_scr[g, w] for w in range(w0, w0 + COLMIX_COLUMNS)], axis=1)
                y = jnp.dot(w_mix, xs, preferred_element_type=F32) + bias
                for w in range(COLMIX_COLUMNS):
                    sv_ref[0, g, w0 + w] = y[:, w * A_GROUP_DIM:(w + 1) * A_GROUP_DIM].astype(BF16)


def _pre_call(x, mod, ng, w_pre, w2, gb, lg, lb, ctx, w_in_t, w_proj_a, w_proj_b, w_out, ws, bs_t, t):
    b, l, _ = x.shape
    assert ctx.shape[1] == GLA_BLOCK, "context length must be one GLA block"
    nw = w_pre.shape[1]
    ncg = A_GROUPS - A_ROW_GROUPS
    nblk = t // GLA_BLOCK
    nj = l // t
    n_main = (VA0 - ZB0) + (IN_WIDTH - ZA0)
    assert D_MODEL % nj == 0 and (D_MODEL // nj) % LANES == 0, "one weight strip per sequence step"
    strip = lambda i, j: jnp.where(i == 0, j, nj - 1)

    def wrows(a):
        return pl.BlockSpec((1, a.shape[1] // nj, a.shape[2]), lambda i, j: (0, strip(i, j), 0))

    def wout(nrows, ncols):
        return (pl.BlockSpec((nrows // nj, ncols), lambda i, j: (strip(i, j), 0)),
                jax.ShapeDtypeStruct((nrows, ncols), BF16))
    rev3 = lambda i, j: (i, nj - 1 - j, 0)
    rev4 = lambda i, j: (i, nj - 1 - j, 0, 0)

    def tok(width, dtype):
        return (pl.BlockSpec((1, t, width), rev3), jax.ShapeDtypeStruct((b, l, width), dtype))

    tot = (pl.BlockSpec((1, nblk, GLA_NC, B_KEY_WIDTH), rev4),
           jax.ShapeDtypeStruct((b, l // GLA_BLOCK, GLA_NC, B_KEY_WIDTH), F32))
    kv = (pl.BlockSpec((1, nblk, B_DV, B_KEY_WIDTH), rev4),
          jax.ShapeDtypeStruct((b, l // GLA_BLOCK, B_DV, B_KEY_WIDTH), F32))
    st = (pl.BlockSpec((1, nblk, B_KEY_WIDTH, B_DV), rev4),
          jax.ShapeDtypeStruct((b, l // GLA_BLOCK, B_KEY_WIDTH, B_DV), BF16))
    outs = [tok(B_KEY_WIDTH, BF16)] * 6 + [tok(B_VAL_WIDTH, BF16), tot, tot, kv, st,
                                           tok(A_ROW_GROUPS * A_GROUP_DIM, BF16),
                                           (pl.BlockSpec((1, ncg, GRID_W, l // GRID_W, A_GROUP_DIM),
                                                         lambda i, j: (i, 0, 0, 0, 0)),
                                            jax.ShapeDtypeStruct((b, ncg, GRID_W, l // GRID_W, A_GROUP_DIM), BF16)),
                                           wout(D_MODEL, n_main), wout(A_WIDTH, D_MODEL),
                                           wout(B_VAL_WIDTH, D_MODEL), wout(D_MODEL, D_MODEL),
                                           (pl.BlockSpec((1, B_DV, B_KEY_WIDTH), lambda i, j: (i, 0, 0)),
                                            jax.ShapeDtypeStruct((b, B_DV, B_KEY_WIDTH), F32))]
    const2 = lambda i, j: (0, 0)
    return pl.pallas_call(
        functools.partial(_pre_kernel, ctx_row=b),
        grid=(b, nj),
        in_specs=[pl.BlockSpec((1, t, D_MODEL), rev3),
                  pl.BlockSpec((8, 3 * D_MODEL), const2),
                  pl.BlockSpec((1, D_MODEL), const2),
                  pl.BlockSpec((D_MODEL, nw), const2),
                  pl.BlockSpec((2, B_GATE_RANK, B_KEY_WIDTH), lambda i, j: (0, 0, 0)),
                  pl.BlockSpec((2, B_KEY_WIDTH), const2),
                  pl.BlockSpec((1, A_WIDTH), const2),
                  pl.BlockSpec((1, A_WIDTH), const2),
                  pl.BlockSpec((1, GLA_BLOCK, D_MODEL), lambda i, j: (i, 0, 0)),
                  pl.BlockSpec((IN_WIDTH, D_MODEL // nj), lambda i, j: (0, strip(i, j))),
                  wrows(w_proj_a), wrows(w_proj_b), wrows(w_out),
                  pl.BlockSpec(ws.shape, lambda i, j: (0, 0, 0, 0)),
                  pl.BlockSpec(bs_t.shape, lambda i, j: (0, 0, 0))],
        out_specs=[o[0] for o in outs],
        out_shape=[o[1] for o in outs],
        scratch_shapes=[pltpu.VMEM((B_DV, B_KEY_WIDTH), F32),
                        pltpu.VMEM((ncg, GRID_W, l // GRID_W, A_GROUP_DIM), BF16)],
        compiler_params=pltpu.CompilerParams(
            dimension_semantics=("arbitrary", "arbitrary"), vmem_limit_bytes=VMEM_LIMIT_BYTES),
        name="pre",
    )(x, mod, ng, w_pre, w2, gb, lg, lb, ctx, w_in_t, w_proj_a, w_proj_b, w_out, ws, bs_t)


def _scale_rows(x16, scales):
    parts = []
    for c, s in enumerate(scales):
        xc = x16[c * B_CHUNK:(c + 1) * B_CHUNK, :]
        parts.append(xc if s is None else (xc.astype(F32) * s).astype(BF16))
    return parts[0] if len(parts) == 1 else jnp.concatenate(parts, axis=0)


def _exp_sum(tots, idx):
    return jnp.exp2(_sum_rows([tots[m] for m in idx])) if idx else None


def _gla_block(qdf, kdf, kcf, qdb, kdb, kcb, v16, tf, tb, sf, sb, p_scr):
    nc = GLA_NC
    ch = B_CHUNK
    lane = lax.broadcasted_iota(jnp.int32, (1, LANES), 1)
    hm = [(lane < B_DK).astype(BF16), (lane >= B_DK).astype(BF16)]

    def pair_heads(x):
        return jnp.concatenate([x * hm[0], x * hm[1]], axis=0)

    row = lax.broadcasted_iota(jnp.int32, (ch, 2 * ch), 0)
    col = lax.broadcasted_iota(jnp.int32, (ch, 2 * ch), 1)
    first = col < ch
    zeros = jnp.zeros((ch, LANES), BF16)
    for g in range(B_HEADS // 2):
        ls = slice(g * LANES, (g + 1) * LANES)
        for i in range(nc // 2):
            pr = slice(2 * i * ch, (2 * i + 2) * ch)
            re = slice(2 * i * ch, (2 * i + 1) * ch)
            ro = slice((2 * i + 1) * ch, (2 * i + 2) * ch)
            sc_f = _dot_nt(pair_heads(qdf[pr, ls]), jnp.concatenate([kdf[pr, ls], kcf[re, ls], zeros], axis=0))
            sc_b = _dot_nt(pair_heads(qdb[pr, ls]), jnp.concatenate([kdb[pr, ls], zeros, kcb[ro, ls]], axis=0))
            for hh in range(2):
                he = slice(hh * 2 * ch, hh * 2 * ch + ch)
                ho = slice(hh * 2 * ch + ch, (hh + 1) * 2 * ch)
                even = jnp.where(first,
                                 jnp.where(col <= row, sc_f[he, 0:2 * ch], 0.0)
                                 + jnp.where(col >= row, sc_b[he, 0:2 * ch], 0.0),
                                 sc_b[he, 2 * ch:])
                odd = jnp.where(first, sc_f[ho, 2 * ch:],
                                jnp.where(col - ch <= row, sc_f[ho, 0:2 * ch], 0.0)
                                + jnp.where(col - ch >= row, sc_b[ho, 0:2 * ch], 0.0))
                p_scr[2 * g + hh, re, pr] = even.astype(BF16)
                p_scr[2 * g + hh, ro, pr] = odd.astype(BF16)

    def cross(lo, hi):
        if hi - lo <= 2:
            return
        mid = (lo + hi) // 2
        cross(lo, mid)
        cross(mid, hi)
        rl = slice(lo * ch, mid * ch)
        rh = slice(mid * ch, hi * ch)
        n = (hi - mid) * ch
        qf = _scale_rows(qdf[rh], [_exp_sum(tf, range(mid, c)) for c in range(mid, hi)])
        kf = _scale_rows(kcf[rl], [_exp_sum(tf, range(c + 1, mid)) for c in range(lo, mid)])
        qb = _scale_rows(qdb[rl], [_exp_sum(tb, range(c + 1, mid)) for c in range(lo, mid)])
        kb = _scale_rows(kcb[rh], [_exp_sum(tb, range(mid, c)) for c in range(mid, hi)])
        for g in range(B_HEADS // 2):
            ls = slice(g * LANES, (g + 1) * LANES)
            sc_f = _dot_nt(pair_heads(qf[:, ls]), kf[:, ls]).astype(BF16)
            sc_b = _dot_nt(pair_heads(qb[:, ls]), kb[:, ls]).astype(BF16)
            for hh in range(2):
                p_scr[2 * g + hh, rh, rl] = sc_f[hh * n:(hh + 1) * n]
                p_scr[2 * g + hh, rl, rh] = sc_b[hh * n:(hh + 1) * n]

    cross(0, nc)

    qsf = _scale_rows(qdf, [_exp_sum(tf, range(0, c)) for c in range(nc)])
    qsb = _scale_rows(qdb, [_exp_sum(tb, range(c + 1, nc)) for c in range(nc)])
    head_row = lax.broadcasted_iota(jnp.int32, (LANES, 1), 0) // B_DK
    outs = []
    for h in range(B_HEADS):
        ls = slice((h // 2) * LANES, (h // 2 + 1) * LANES)
        own = (head_row == (h % 2)).astype(BF16)
        lhs = jnp.concatenate([p_scr[h], qsf[:, ls], qsb[:, ls]], axis=1)
        rhs = jnp.concatenate([v16[:, h * B_DV:(h + 1) * B_DV], sf[ls, :] * own, sb[ls, :] * own], axis=0)
        outs.append(jnp.dot(lhs, rhs, preferred_element_type=F32))
    return outs


def _main_kernel(x_ref, mod_ref, ng_ref, w_ref, vr_ref, sc_ref,
                 qdf_ref, kdf_ref, kcf_ref, qdb_ref, kdb_ref, kcb_ref, v_ref, totf_ref, totb_ref,
                 kvf_ref, s0f_ref, sb_ref,
                 ws_ref, bs_ref, bg_ref, wpa_ref, wpb_ref, wo_ref, fg_ref, o_ref, acta_scr, on_scr, p_scr, sf_scr):
    t = x_ref.shape[1]
    blocks = [slice(blk * GLA_BLOCK, (blk + 1) * GLA_BLOCK) for blk in range(t // GLA_BLOCK)]

    @pl.when(pl.program_id(1) == 0)
    def _():
        sf_scr[...] = s0f_ref[0]

    s = sf_scr[...]
    sf_in = []
    for blk in range(len(blocks)):
        sf_in.append(s.T.astype(BF16))
        s = jnp.exp2(_sum_rows([totf_ref[0, blk, c:c + 1, :] for c in range(GLA_NC)])) * s + kvf_ref[0, blk]
    sf_scr[...] = s

    def gla(blk):
        rs = blocks[blk]
        tf = [totf_ref[0, blk, c:c + 1, :] for c in range(GLA_NC)]
        tb = [totb_ref[0, blk, c:c + 1, :] for c in range(GLA_NC)]
        o_heads = _gla_block(qdf_ref[0, rs, :], kdf_ref[0, rs, :], kcf_ref[0, rs, :],
                             qdb_ref[0, rs, :], kdb_ref[0, rs, :], kcb_ref[0, rs, :], v_ref[0, rs, :],
                             tf, tb, sf_in[blk], sb_ref[0, blk], p_scr.at[blk])
        for hd in range(B_HEADS):
            on_scr[rs, hd * B_DV:(hd + 1) * B_DV] = _rms_rows(o_heads[hd])

    def proj(c0, n):
        return jnp.dot(h16, w_ref[:, c0:c0 + n], preferred_element_type=F32)

    gla(0)
    h16 = _latent_h(x_ref[0], mod_ref, ng_ref).astype(BF16)
    zb = proj(0, B_VAL_WIDTH)
    u = proj(B_VAL_WIDTH, A_WIDTH)
    za = proj(B_VAL_WIDTH + A_WIDTH, A_WIDTH)
    for blk in range(1, len(blocks)):
        gla(blk)
    gate_a = jax.nn.sigmoid(proj(B_VAL_WIDTH + 2 * A_WIDTH, D_MODEL))

    uz = u * _silu(za)
    for g in range(A_GROUPS):
        cs = slice(g * A_GROUP_DIM, (g + 1) * A_GROUP_DIM)
        if g < A_ROW_GROUPS:
            chunks = [slice(c * A_CHUNK, (c + 1) * A_CHUNK) for c in range(t // A_CHUNK)]
            sv_all = jnp.dot(ws_ref[g].astype(BF16), jnp.concatenate([vr_ref[0, rs, cs] for rs in chunks], axis=1),
                             preferred_element_type=F32)
            for c, rs in enumerate(chunks):
                sv = sv_all[:, c * A_GROUP_DIM:(c + 1) * A_GROUP_DIM] + bs_ref[g][:, 0:1]
                acta_scr[rs, cs] = (uz[rs, cs] * sv).astype(BF16)
        else:
            sv = jnp.swapaxes(sc_ref[0, g - A_ROW_GROUPS].astype(F32), 0, 1).reshape(t, A_GROUP_DIM)
            acta_scr[:, cs] = (uz[:, cs] * sv).astype(BF16)
    ya = jnp.dot(acta_scr[...], wpa_ref[...], preferred_element_type=F32)

    gate_b = jax.nn.sigmoid(proj(B_VAL_WIDTH + 2 * A_WIDTH + D_MODEL, D_MODEL))
    yb = _dot(on_scr[...] * (_silu(zb) * bg_ref[...]), wpb_ref[...])

    m = (gate_a * ya + gate_b * yb).astype(BF16)
    for rs in blocks:
        y = jnp.dot(m[rs], wo_ref[...], preferred_element_type=F32)
        xo = x_ref[0, rs, :] + mod_ref[pl.ds(pl.program_id(0), 1), 2 * D_MODEL:] * y
        o_ref[0, rs, :] = _rms_rows(xo) * fg_ref[...]


def _main_call(x, mod, ng, w_main, vn_row, sv_col, gla_ops, ws, bs, bg, wpa, wpb, wo, fg, t):
    b, l, _ = x.shape
    nw = w_main.shape[1]
    ncg = A_GROUPS - A_ROW_GROUPS
    const2 = lambda i, j: (0, 0)
    const3 = lambda i, j: (0, 0, 0)
    tok = lambda width: pl.BlockSpec((1, t, width), lambda i, j: (i, j, 0))
    nblk = t // GLA_BLOCK
    tot_spec = pl.BlockSpec((1, nblk, GLA_NC, B_KEY_WIDTH), lambda i, j: (i, j, 0, 0))
    st_spec = pl.BlockSpec((1, nblk, B_KEY_WIDTH, B_DV), lambda i, j: (i, j, 0, 0))
    kv_spec = pl.BlockSpec((1, nblk, B_DV, B_KEY_WIDTH), lambda i, j: (i, j, 0, 0))
    s0_spec = pl.BlockSpec((1, B_DV, B_KEY_WIDTH), lambda i, j: (i, 0, 0))
    return pl.pallas_call(
        _main_kernel,
        grid=(b, l // t),
        in_specs=[tok(D_MODEL),
                  pl.BlockSpec((8, 3 * D_MODEL), const2),
                  pl.BlockSpec((1, D_MODEL), const2),
                  pl.BlockSpec((D_MODEL, nw), const2),
                  tok(A_ROW_GROUPS * A_GROUP_DIM),
                  pl.BlockSpec((1, ncg, GRID_W, t // GRID_W, A_GROUP_DIM), lambda i, j: (i, 0, 0, j, 0)),
                  tok(B_KEY_WIDTH), tok(B_KEY_WIDTH), tok(B_KEY_WIDTH),
                  tok(B_KEY_WIDTH), tok(B_KEY_WIDTH), tok(B_KEY_WIDTH), tok(B_VAL_WIDTH),
                  tot_spec, tot_spec, kv_spec, s0_spec, st_spec,
                  pl.BlockSpec((A_GROUPS, A_CHUNK, A_CHUNK), const3),
                  pl.BlockSpec((A_GROUPS, A_CHUNK, A_GROUP_DIM), const3),
                  pl.BlockSpec((1, B_VAL_WIDTH), const2),
                  pl.BlockSpec((A_WIDTH, D_MODEL), const2),
                  pl.BlockSpec((B_VAL_WIDTH, D_MODEL), const2),
                  pl.BlockSpec((D_MODEL, D_MODEL), const2),
                  pl.BlockSpec((1, D_MODEL), const2)],
        out_specs=tok(D_MODEL),
        out_shape=jax.ShapeDtypeStruct((b, l, D_MODEL), F32),
        scratch_shapes=[pltpu.VMEM((t, A_WIDTH), BF16), pltpu.VMEM((t, B_VAL_WIDTH), F32),
                        pltpu.VMEM((nblk, B_HEADS, GLA_BLOCK, GLA_BLOCK), BF16),
                        pltpu.VMEM((B_DV, B_KEY_WIDTH), F32)],
        compiler_params=pltpu.CompilerParams(
            dimension_semantics=("parallel", "arbitrary"), vmem_limit_bytes=VMEM_LIMIT_BYTES),
        name="main",
    )(x, mod, ng, w_main, vn_row, sv_col, *gla_ops, ws, bs, bg, wpa, wpb, wo, fg)


def kernel(x, c, ctx, c_ctx, w_mod, b_mod, norm_g, w_in, a_ln_g, a_ln_b, a_ws, a_bs, b_gate_w2, b_gate_b,
           b_norm_g, w_proj_a, w_proj_b, w_out, final_norm_g):
    assert w_mod.shape[0] == 1, "single-layer block"
    b, l, _ = x.shape
    rows = l // GRID_W
    assert rows == A_CHUNK
    ng = norm_g[0][None, :]

    w_in_t = w_in[0].T
    w_pre, mod, bs_t = _wprep_call(w_in_t, c, c_ctx, w_mod, b_mod, a_bs)
    w2, gb = b_gate_w2[0], b_gate_b[0]

    (qd_f, kd_f, kc_f, qd_b, kd_b, kc_b, v16, tot_f, tot_b, kv_f, s_b, vn_row, sv_col,
     w_main, wpa16, wpb16, wo16, s0f) = _pre_call(
        x, mod, ng, w_pre, w2, gb, a_ln_g[0][None, :], a_ln_b[0][None, :], ctx,
        w_in_t, w_proj_a, w_proj_b, w_out, a_ws, bs_t, TOKEN_TILE)
    gla_ops = (qd_f, kd_f, kc_f, qd_b, kd_b, kc_b, v16, tot_f, tot_b, kv_f, s0f, s_b)


    return _main_call(x, mod, ng, w_main, vn_row, sv_col, gla_ops, a_ws[0], bs_t, b_norm_g[0][None, :],
                      wpa16, wpb16, wo16, final_norm_g[None, :], TOKEN_TILE)
```

```python
import functools

import jax
import jax.numpy as jnp
from jax import lax
from jax.experimental import pallas as pl
from jax.experimental.pallas import tpu as pltpu

D_MODEL = 1024
GRID_W = 64
EPS = 1e-6

A_WIDTH = 512
A_GROUPS = 4
A_GROUP_DIM = 128
A_CHUNK = 128
A_ROW_GROUPS = 2

B_HEADS = 4
B_DK = 64
B_DV = 128
B_KEY_WIDTH = 256
B_VAL_WIDTH = 512
B_GATE_RANK = 16
B_GATE_TAU = 16.0
LOG2E = 1.4426950408889634
B_CHUNK = 64

Q0 = 0
K0 = Q0 + B_KEY_WIDTH
V0 = K0 + B_KEY_WIDTH
LR0 = V0 + B_VAL_WIDTH
ZB0 = LR0 + 2 * B_GATE_RANK
UA0 = ZB0 + B_VAL_WIDTH
VA0 = UA0 + A_WIDTH
ZA0 = VA0 + A_WIDTH
G0 = ZA0 + A_WIDTH
IN_WIDTH = G0 + 2 * D_MODEL

LANES = 128

P_LR = 0
P_K = P_LR + LANES
P_Q = P_K + B_KEY_WIDTH
P_V = P_Q + B_KEY_WIDTH
P_VA = P_V + B_VAL_WIDTH
PRE_WIDTH = P_VA + A_WIDTH
GLA_BLOCK = 256
TOKEN_TILE = 1024
COLMIX_COLUMNS = 16
GLA_NC = GLA_BLOCK // B_CHUNK

VMEM_LIMIT_BYTES = 56 * 1024 * 1024

BF16 = jnp.bfloat16
F32 = jnp.float32


def _dot(a, b):
    return jnp.dot(a.astype(BF16), b.astype(BF16), preferred_element_type=F32)


def _dot_nt(a, b):
    return lax.dot_general(a.astype(BF16), b.astype(BF16), (((1,), (1,)), ((), ())),
                           preferred_element_type=F32)


def _dot_tn(a, b):
    return lax.dot_general(a.astype(BF16), b.astype(BF16), (((0,), (0,)), ((), ())),
                           preferred_element_type=F32)


def _silu(x):
    return x * jax.nn.sigmoid(x)


def _rms_rows(x):
    return x * lax.rsqrt(jnp.mean(x * x, axis=-1, keepdims=True) + EPS)


def _put_transposed(dst_ref, c0, wt_ref, r0, n, scale=None):
    for s in range(0, n, 2 * LANES):
        m = min(2 * LANES, n - s)
        blk = wt_ref[r0 + s:r0 + s + m, :].T
        dst_ref[:, c0 + s:c0 + s + m] = (blk if scale is None else blk * scale).astype(BF16)


def _wprep_kernel(wt_ref, c_ref, cctx_ref, wm_ref, bm_ref, abs_ref, wpre_ref, mod_ref, bst_ref):
    _put_transposed(wpre_ref, P_LR, wt_ref, LR0, LANES)
    _put_transposed(wpre_ref, P_K, wt_ref, K0, B_KEY_WIDTH)
    _put_transposed(wpre_ref, P_Q, wt_ref, Q0, B_KEY_WIDTH, B_DK ** -0.5)
    _put_transposed(wpre_ref, P_V, wt_ref, V0, B_VAL_WIDTH)
    _put_transposed(wpre_ref, P_VA, wt_ref, VA0, A_WIDTH)
    nb = c_ref.shape[0]
    cc = jnp.concatenate([c_ref[...], cctx_ref[...], jnp.zeros((8 - nb - 1, D_MODEL), F32)], axis=0)
    mod_ref[...] = _dot(_silu(cc), wm_ref[0]) + bm_ref[...]
    g = pl.program_id(0)
    bst_ref[0] = jnp.broadcast_to(abs_ref[0, pl.ds(g, 1), :], (A_GROUP_DIM, A_CHUNK)).T


def _wprep_call(w_in_t, c, c_ctx, w_mod, b_mod, a_bs):
    steps = A_GROUPS
    n_mod = w_mod.shape[2]
    outs = [(pl.BlockSpec((D_MODEL // steps, PRE_WIDTH), lambda i: (i, 0)),
             jax.ShapeDtypeStruct((D_MODEL, PRE_WIDTH), BF16)),
            (pl.BlockSpec((8, n_mod // steps), lambda i: (0, i)), jax.ShapeDtypeStruct((8, n_mod), F32)),
            (pl.BlockSpec((1, A_CHUNK, A_GROUP_DIM), lambda i: (i, 0, 0)),
             jax.ShapeDtypeStruct((A_GROUPS, A_CHUNK, A_GROUP_DIM), F32))]
    return pl.pallas_call(
        _wprep_kernel,
        grid=(steps,),
        in_specs=[pl.BlockSpec((VA0 + A_WIDTH, D_MODEL // steps), lambda i: (0, i)),
                  pl.BlockSpec(c.shape, lambda i: (0, 0)),
                  pl.BlockSpec((1, D_MODEL), lambda i: (0, 0)),
                  pl.BlockSpec((1, D_MODEL, n_mod // steps), lambda i: (0, 0, i)),
                  pl.BlockSpec((1, n_mod // steps), lambda i: (0, i)),
                  pl.BlockSpec(a_bs.shape, lambda i: (0, 0, 0))],
        out_specs=[o[0] for o in outs],
        out_shape=[o[1] for o in outs],
        compiler_params=pltpu.CompilerParams(
            dimension_semantics=("parallel",), vmem_limit_bytes=VMEM_LIMIT_BYTES),
        name="wprep",
    )(w_in_t, c, c_ctx[None, :], w_mod, b_mod, a_bs)


def _gate_logs(lr, w2_ref, gb_ref):
    out = []
    for r in range(2):
        logits = _dot(lr[:, r * B_GATE_RANK:(r + 1) * B_GATE_RANK], w2_ref[r]) + gb_ref[r:r + 1, :]
        log_sig = jnp.minimum(logits, 0.0) - jnp.log(1.0 + jnp.exp(-jnp.abs(logits)))
        out.append(log_sig * (LOG2E / B_GATE_TAU))
    return out


def _chunk_tri(reverse):
    i = lax.broadcasted_iota(jnp.int32, (GLA_BLOCK, GLA_BLOCK), 0)
    j = lax.broadcasted_iota(jnp.int32, (GLA_BLOCK, GLA_BLOCK), 1)
    same = (i // B_CHUNK) == (j // B_CHUNK)
    tri = (j >= i) if reverse else (j <= i)
    return (same & tri).astype(BF16)


def _sum_rows(rows):
    acc = rows[0]
    for r in rows[1:]:
        acc = acc + r
    return acc


def _block_cum(a, tri, reverse):
    hi = a.astype(BF16)
    lo = (a - hi.astype(F32)).astype(BF16)
    cum = (jnp.dot(tri, hi, preferred_element_type=F32) + jnp.dot(tri, lo, preferred_element_type=F32))
    last = 0 if reverse else B_CHUNK - 1
    tots = [cum[c * B_CHUNK + last:c * B_CHUNK + last + 1, :] for c in range(GLA_NC)]
    return cum, tots


def _block_keys(k, cum, tots, reverse):
    totb = jnp.concatenate([jnp.broadcast_to(t, (B_CHUNK, B_KEY_WIDTH)) for t in tots], axis=0)
    kdec = k * jnp.exp2(totb - cum)
    later = []
    for c in range(GLA_NC):
        idx = list(range(0, c)) if reverse else list(range(c + 1, GLA_NC))
        if idx:
            later.append(jnp.broadcast_to(jnp.exp2(_sum_rows([tots[m] for m in idx])), (B_CHUNK, B_KEY_WIDTH)))
        else:
            later.append(jnp.ones((B_CHUNK, B_KEY_WIDTH), F32))
    return kdec, kdec * jnp.concatenate(later, axis=0)


def _pre_project(h16, w_ref, want_q):
    lr = jnp.dot(h16, w_ref[:, P_LR:P_LR + LANES], preferred_element_type=F32)[:, 0:2 * B_GATE_RANK]
    k = jnp.dot(h16, w_ref[:, P_K:P_K + B_KEY_WIDTH], preferred_element_type=F32)
    q = jnp.dot(h16, w_ref[:, P_Q:P_Q + B_KEY_WIDTH], preferred_element_type=F32) if want_q else None
    v = jnp.dot(h16, w_ref[:, P_V:P_V + B_VAL_WIDTH], preferred_element_type=F32)
    return lr, k, q, v


def _block_kv_t(v16, kblk):
    k16 = kblk.astype(BF16)
    lane = lax.broadcasted_iota(jnp.int32, (1, B_KEY_WIDTH), 1)
    lhs = jnp.concatenate([v16[:, h * B_DV:(h + 1) * B_DV] for h in range(B_HEADS)], axis=0)
    rhs = jnp.concatenate([jnp.where((lane // B_DK) == h, k16, jnp.zeros_like(k16)) for h in range(B_HEADS)], axis=0)
    return _dot_tn(lhs, rhs)


def _latent_h(x, mod_ref, ng_ref):
    row = pl.ds(pl.program_id(0), 1)
    shift = mod_ref[row, 0:D_MODEL]
    gain = ng_ref[...] * (1.0 + mod_ref[row, D_MODEL:2 * D_MODEL])
    return _rms_rows(x) * gain + shift


def _pre_kernel(x_ref, mod_ref, ng_ref, w_ref, w2_ref, gb_ref, lg_ref, lb_ref, ctx_ref,
                wt_ref, wpa_ref, wpb_ref, wo_ref, ws_ref, bst_ref,
                qdf_ref, kdf_ref, kcf_ref, qdb_ref, kdb_ref, kcb_ref, v_ref,
                totf_ref, totb_ref, kvf_ref, sb_ref, vr_ref, sv_ref,
                wmain_ref, wpa16_ref, wpb16_ref, wo16_ref, s0f_ref, sb_scr, vn_scr, *, ctx_row):
    @pl.when(pl.program_id(1) == 0)
    def _():
        shift = mod_ref[ctx_row:ctx_row + 1, 0:D_MODEL]
        scale = mod_ref[ctx_row:ctx_row + 1, D_MODEL:2 * D_MODEL]
        hc = _rms_rows(ctx_ref[0]) * ng_ref[...] * (1.0 + scale) + shift
        lr_c, k_c, _, v_c = _pre_project(hc.astype(BF16), w_ref, False)
        v16_c = v_c.astype(BF16)
        logs_c = _gate_logs(lr_c, w2_ref, gb_ref)
        for d, reverse in enumerate((False, True)):
            cum_c, tots_c = _block_cum(logs_c[d], _chunk_tri(reverse), reverse)
            _, kblk_c = _block_keys(k_c, cum_c, tots_c, reverse)
            state = _block_kv_t(v16_c, kblk_c)
            if reverse:
                sb_scr[...] = state
            else:
                s0f_ref[0] = state

    h16 = _latent_h(x_ref[0], mod_ref, ng_ref).astype(BF16)
    nblk = x_ref.shape[1] // GLA_BLOCK
    blocks = [slice(blk * GLA_BLOCK, (blk + 1) * GLA_BLOCK) for blk in range(nblk)]
    dirs = ((False, qdf_ref, kdf_ref, kcf_ref, totf_ref), (True, qdb_ref, kdb_ref, kcb_ref, totb_ref))

    lrk = jnp.dot(h16, w_ref[:, P_LR:P_K + B_KEY_WIDTH], preferred_element_type=F32)
    lr = lrk[:, 0:2 * B_GATE_RANK]
    k = lrk[:, P_K:P_K + B_KEY_WIDTH]
    logs = _gate_logs(lr, w2_ref, gb_ref)
    q = jnp.dot(h16, w_ref[:, P_Q:P_Q + B_KEY_WIDTH], preferred_element_type=F32)

    _put_transposed(wmain_ref, 0, wt_ref, ZB0, VA0 - ZB0)
    _put_transposed(wmain_ref, VA0 - ZB0, wt_ref, ZA0, IN_WIDTH - ZA0)
    wpa16_ref[...] = wpa_ref[0].astype(BF16)
    wpb16_ref[...] = wpb_ref[0].astype(BF16)
    wo16_ref[...] = wo_ref[0].astype(BF16)

    v16 = jnp.dot(h16, w_ref[:, P_V:P_V + B_VAL_WIDTH], preferred_element_type=F32).astype(BF16)
    v_ref[0] = v16

    cums = {}
    for d, (reverse, _, _, _, tot_ref) in enumerate(dirs):
        tri = _chunk_tri(reverse)
        for blk, rs in enumerate(blocks):
            cum, tots = _block_cum(logs[d][rs], tri, reverse)
            cums[d, blk] = (cum, tots)
            for c in range(GLA_NC):
                tot_ref[0, blk, c:c + 1, :] = tots[c]

    va = jnp.dot(h16, w_ref[:, P_VA:P_VA + A_WIDTH], preferred_element_type=F32)
    vc = va - jnp.mean(va, axis=-1, keepdims=True)
    vn = vc * lax.rsqrt(jnp.mean(vc * vc, axis=-1, keepdims=True) + EPS) * lg_ref[...] + lb_ref[...]
    nr = A_ROW_GROUPS * A_GROUP_DIM
    vr_ref[0] = vn[:, 0:nr].astype(BF16)
    n_rows = x_ref.shape[1] // GRID_W
    row0 = pl.multiple_of((pl.num_programs(1) - 1 - pl.program_id(1)) * n_rows, n_rows)
    for g in range(A_GROUPS - A_ROW_GROUPS):
        vg = vn[:, nr + g * A_GROUP_DIM:nr + (g + 1) * A_GROUP_DIM]
        vn_scr[g, :, pl.ds(row0, n_rows), :] = jnp.swapaxes(
            vg.reshape(n_rows, GRID_W, A_GROUP_DIM), 0, 1).astype(BF16)

    kv_b = {}
    for d, (reverse, qd_ref, kd_ref, kc_ref, _) in enumerate(dirs):
        for blk, rs in enumerate(blocks):
            cum, tots = cums[d, blk]
            kdec, kblk = _block_keys(k[rs], cum, tots, reverse)
            kc_ref[0, rs, :] = kdec.astype(BF16)
            kd_ref[0, rs, :] = (k[rs] * jnp.exp2(-cum)).astype(BF16)
            qd_ref[0, rs, :] = (q[rs] * jnp.exp2(cum)).astype(BF16)
            if reverse:
                kv_b[blk] = _block_kv_t(v16[rs], kblk)
            else:
                kvf_ref[0, blk] = _block_kv_t(v16[rs], kblk)

    s = sb_scr[...]
    for blk in reversed(range(nblk)):
        sb_ref[0, blk] = s.T.astype(BF16)
        s = jnp.exp2(_sum_rows(cums[1, blk][1])) * s + kv_b[blk]
    sb_scr[...] = s

    @pl.when(pl.program_id(1) == pl.num_programs(1) - 1)
    def _():
        for g in range(A_GROUPS - A_ROW_GROUPS):
            w_mix = ws_ref[0, A_ROW_GROUPS + g].astype(BF16)
            bias = bst_ref[A_ROW_GROUPS + g][:, 0:1]
            for w0 in range(0, GRID_W, COLMIX_COLUMNS):
                xs = jnp.concatenate([vn_scr[g, w] for w in range(w0, w0 + COLMIX_COLUMNS)], axis=1)
                y = jnp.dot(w_mix, xs, preferred_element_type=F32) + bias
                for w in range(COLMIX_COLUMNS):
                    sv_ref[0, g, w0 + w] = y[:, w * A_GROUP_DIM:(w + 1) * A_GROUP_DIM].astype(BF16)


def _pre_call(x, mod, ng, w_pre, w2, gb, lg, lb, ctx, w_in_t, w_proj_a, w_proj_b, w_out, ws, bs_t, t):
    b, l, _ = x.shape
    assert ctx.shape[1] == GLA_BLOCK, "context length must be one GLA block"
    nw = w_pre.shape[1]
    ncg = A_GROUPS - A_ROW_GROUPS
    nblk = t // GLA_BLOCK
    nj = l // t
    n_main = (VA0 - ZB0) + (IN_WIDTH - ZA0)
    assert D_MODEL % nj == 0 and (D_MODEL // nj) % LANES == 0, "one weight strip per sequence step"
    strip = lambda i, j: jnp.where(i == 0, j, nj - 1)

    def wrows(a):
        return pl.BlockSpec((1, a.shape[1] // nj, a.shape[2]), lambda i, j: (0, strip(i, j), 0))

    def wout(nrows, ncols):
        return (pl.BlockSpec((nrows // nj, ncols), lambda i, j: (strip(i, j), 0)),
                jax.ShapeDtypeStruct((nrows, ncols), BF16))
    rev3 = lambda i, j: (i, nj - 1 - j, 0)
    rev4 = lambda i, j: (i, nj - 1 - j, 0, 0)

    def tok(width, dtype):
        return (pl.BlockSpec((1, t, width), rev3), jax.ShapeDtypeStruct((b, l, width), dtype))

    tot = (pl.BlockSpec((1, nblk, GLA_NC, B_KEY_WIDTH), rev4),
           jax.ShapeDtypeStruct((b, l // GLA_BLOCK, GLA_NC, B_KEY_WIDTH), F32))
    kv = (pl.BlockSpec((1, nblk, B_DV, B_KEY_WIDTH), rev4),
          jax.ShapeDtypeStruct((b, l // GLA_BLOCK, B_DV, B_KEY_WIDTH), F32))
    st = (pl.BlockSpec((1, nblk, B_KEY_WIDTH, B_DV), rev4),
          jax.ShapeDtypeStruct((b, l // GLA_BLOCK, B_KEY_WIDTH, B_DV), BF16))
    outs = [tok(B_KEY_WIDTH, BF16)] * 6 + [tok(B_VAL_WIDTH, BF16), tot, tot, kv, st,
                                           tok(A_ROW_GROUPS * A_GROUP_DIM, BF16),
                                           (pl.BlockSpec((1, ncg, GRID_W, l // GRID_W, A_GROUP_DIM),
                                                         lambda i, j: (i, 0, 0, 0, 0)),
                                            jax.ShapeDtypeStruct((b, ncg, GRID_W, l // GRID_W, A_GROUP_DIM), BF16)),
                                           wout(D_MODEL, n_main), wout(A_WIDTH, D_MODEL),
                                           wout(B_VAL_WIDTH, D_MODEL), wout(D_MODEL, D_MODEL),
                                           (pl.BlockSpec((1, B_DV, B_KEY_WIDTH), lambda i, j: (i, 0, 0)),
                                            jax.ShapeDtypeStruct((b, B_DV, B_KEY_WIDTH), F32))]
    const2 = lambda i, j: (0, 0)
    return pl.pallas_call(
        functools.partial(_pre_kernel, ctx_row=b),
        grid=(b, nj),
        in_specs=[pl.BlockSpec((1, t, D_MODEL), rev3),
                  pl.BlockSpec((8, 3 * D_MODEL), const2),
                  pl.BlockSpec((1, D_MODEL), const2),
                  pl.BlockSpec((D_MODEL, nw), const2),
                  pl.BlockSpec((2, B_GATE_RANK, B_KEY_WIDTH), lambda i, j: (0, 0, 0)),
                  pl.BlockSpec((2, B_KEY_WIDTH), const2),
                  pl.BlockSpec((1, A_WIDTH), const2),
                  pl.BlockSpec((1, A_WIDTH), const2),
                  pl.BlockSpec((1, GLA_BLOCK, D_MODEL), lambda i, j: (i, 0, 0)),
                  pl.BlockSpec((IN_WIDTH, D_MODEL // nj), lambda i, j: (0, strip(i, j))),
                  wrows(w_proj_a), wrows(w_proj_b), wrows(w_out),
                  pl.BlockSpec(ws.shape, lambda i, j: (0, 0, 0, 0)),
                  pl.BlockSpec(bs_t.shape, lambda i, j: (0, 0, 0))],
        out_specs=[o[0] for o in outs],
        out_shape=[o[1] for o in outs],
        scratch_shapes=[pltpu.VMEM((B_DV, B_KEY_WIDTH), F32),
                        pltpu.VMEM((ncg, GRID_W, l // GRID_W, A_GROUP_DIM), BF16)],
        compiler_params=pltpu.CompilerParams(
            dimension_semantics=("arbitrary", "arbitrary"), vmem_limit_bytes=VMEM_LIMIT_BYTES),
        name="pre",
    )(x, mod, ng, w_pre, w2, gb, lg, lb, ctx, w_in_t, w_proj_a, w_proj_b, w_out, ws, bs_t)


def _scale_rows(x16, scales):
    parts = []
    for c, s in enumerate(scales):
        xc = x16[c * B_CHUNK:(c + 1) * B_CHUNK, :]
        parts.append(xc if s is None else (xc.astype(F32) * s).astype(BF16))
    return parts[0] if len(parts) == 1 else jnp.concatenate(parts, axis=0)


def _exp_sum(tots, idx):
    return jnp.exp2(_sum_rows([tots[m] for m in idx])) if idx else None


def _gla_block(qdf, kdf, kcf, qdb, kdb, kcb, v16, tf, tb, sf, sb, p_scr):
    nc = GLA_NC
    ch = B_CHUNK
    lane = lax.broadcasted_iota(jnp.int32, (1, LANES), 1)
    hm = [(lane < B_DK).astype(BF16), (lane >= B_DK).astype(BF16)]

    def pair_heads(x):
        return jnp.concatenate([x * hm[0], x * hm[1]], axis=0)

    row = lax.broadcasted_iota(jnp.int32, (ch, 2 * ch), 0)
    col = lax.broadcasted_iota(jnp.int32, (ch, 2 * ch), 1)
    first = col < ch
    zeros = jnp.zeros((ch, LANES), BF16)
    for g in range(B_HEADS // 2):
        ls = slice(g * LANES, (g + 1) * LANES)
        for i in range(nc // 2):
            pr = slice(2 * i * ch, (2 * i + 2) * ch)
            re = slice(2 * i * ch, (2 * i + 1) * ch)
            ro = slice((2 * i + 1) * ch, (2 * i + 2) * ch)
            sc_f = _dot_nt(pair_heads(qdf[pr, ls]), jnp.concatenate([kdf[pr, ls], kcf[re, ls], zeros], axis=0))
            sc_b = _dot_nt(pair_heads(qdb[pr, ls]), jnp.concatenate([kdb[pr, ls], zeros, kcb[ro, ls]], axis=0))
            for hh in range(2):
                he = slice(hh * 2 * ch, hh * 2 * ch + ch)
                ho = slice(hh * 2 * ch + ch, (hh + 1) * 2 * ch)
                even = jnp.where(first,
                                 jnp.where(col <= row, sc_f[he, 0:2 * ch], 0.0)
                                 + jnp.where(col >= row, sc_b[he, 0:2 * ch], 0.0),
                                 sc_b[he, 2 * ch:])
                odd = jnp.where(first, sc_f[ho, 2 * ch:],
                                jnp.where(col - ch <= row, sc_f[ho, 0:2 * ch], 0.0)
                                + jnp.where(col - ch >= row, sc_b[ho, 0:2 * ch], 0.0))
                p_scr[2 * g + hh, re, pr] = even.astype(BF16)
                p_scr[2 * g + hh, ro, pr] = odd.astype(BF16)

    def cross(lo, hi):
        if hi - lo <= 2:
            return
        mid = (lo + hi) // 2
        cross(lo, mid)
        cross(mid, hi)
        rl = slice(lo * ch, mid * ch)
        rh = slice(mid * ch, hi * ch)
        n = (hi - mid) * ch
        qf = _scale_rows(qdf[rh], [_exp_sum(tf, range(mid, c)) for c in range(mid, hi)])
        kf = _scale_rows(kcf[rl], [_exp_sum(tf, range(c + 1, mid)) for c in range(lo, mid)])
        qb = _scale_rows(qdb[rl], [_exp_sum(tb, range(c + 1, mid)) for c in range(lo, mid)])
        kb = _scale_rows(kcb[rh], [_exp_sum(tb, range(mid, c)) for c in range(mid, hi)])
        for g in range(B_HEADS // 2):
            ls = slice(g * LANES, (g + 1) * LANES)
            sc_f = _dot_nt(pair_heads(qf[:, ls]), kf[:, ls]).astype(BF16)
            sc_b = _dot_nt(pair_heads(qb[:, ls]), kb[:, ls]).astype(BF16)
            for hh in range(2):
                p_scr[2 * g + hh, rh, rl] = sc_f[hh * n:(hh + 1) * n]
                p_scr[2 * g + hh, rl, rh] = sc_b[hh * n:(hh + 1) * n]

    cross(0, nc)

    qsf = _scale_rows(qdf, [_exp_sum(tf, range(0, c)) for c in range(nc)])
    qsb = _scale_rows(qdb, [_exp_sum(tb, range(c + 1, nc)) for c in range(nc)])
    head_row = lax.broadcasted_iota(jnp.int32, (LANES, 1), 0) // B_DK
    outs = []
    for h in range(B_HEADS):
        ls = slice((h // 2) * LANES, (h // 2 + 1) * LANES)
        own = (head_row == (h % 2)).astype(BF16)
        lhs = jnp.concatenate([p_scr[h], qsf[:, ls], qsb[:, ls]], axis=1)
        rhs = jnp.concatenate([v16[:, h * B_DV:(h + 1) * B_DV], sf[ls, :] * own, sb[ls, :] * own], axis=0)
        outs.append(jnp.dot(lhs, rhs, preferred_element_type=F32))
    return outs


def _main_kernel(x_ref, mod_ref, ng_ref, w_ref, vr_ref, sc_ref,
                 qdf_ref, kdf_ref, kcf_ref, qdb_ref, kdb_ref, kcb_ref, v_ref, totf_ref, totb_ref,
                 kvf_ref, s0f_ref, sb_ref,
                 ws_ref, bs_ref, bg_ref, wpa_ref, wpb_ref, wo_ref, fg_ref, o_ref, acta_scr, on_scr, p_scr, sf_scr):
    t = x_ref.shape[1]
    blocks = [slice(blk * GLA_BLOCK, (blk + 1) * GLA_BLOCK) for blk in range(t // GLA_BLOCK)]

    @pl.when(pl.program_id(1) == 0)
    def _():
        sf_scr[...] = s0f_ref[0]

    s = sf_scr[...]
    sf_in = []
    for blk in range(len(blocks)):
        sf_in.append(s.T.astype(BF16))
        s = jnp.exp2(_sum_rows([totf_ref[0, blk, c:c + 1, :] for c in range(GLA_NC)])) * s + kvf_ref[0, blk]
    sf_scr[...] = s

    def gla(blk):
        rs = blocks[blk]
        tf = [totf_ref[0, blk, c:c + 1, :] for c in range(GLA_NC)]
        tb = [totb_ref[0, blk, c:c + 1, :] for c in range(GLA_NC)]
        o_heads = _gla_block(qdf_ref[0, rs, :], kdf_ref[0, rs, :], kcf_ref[0, rs, :],
                             qdb_ref[0, rs, :], kdb_ref[0, rs, :], kcb_ref[0, rs, :], v_ref[0, rs, :],
                             tf, tb, sf_in[blk], sb_ref[0, blk], p_scr.at[blk])
        for hd in range(B_HEADS):
            on_scr[rs, hd * B_DV:(hd + 1) * B_DV] = _rms_rows(o_heads[hd])

    def proj(c0, n):
        return jnp.dot(h16, w_ref[:, c0:c0 + n], preferred_element_type=F32)

    gla(0)
    h16 = _latent_h(x_ref[0], mod_ref, ng_ref).astype(BF16)
    zb = proj(0, B_VAL_WIDTH)
    u = proj(B_VAL_WIDTH, A_WIDTH)
    za = proj(B_VAL_WIDTH + A_WIDTH, A_WIDTH)
    for blk in range(1, len(blocks)):
        gla(blk)
    gate_a = jax.nn.sigmoid(proj(B_VAL_WIDTH + 2 * A_WIDTH, D_MODEL))

    uz = u * _silu(za)
    for g in range(A_GROUPS):
        cs = slice(g * A_GROUP_DIM, (g + 1) * A_GROUP_DIM)
        if g < A_ROW_GROUPS:
            chunks = [slice(c * A_CHUNK, (c + 1) * A_CHUNK) for c in range(t // A_CHUNK)]
            sv_all = jnp.dot(ws_ref[g].astype(BF16), jnp.concatenate([vr_ref[0, rs, cs] for rs in chunks], axis=1),
                             preferred_element_type=F32)
            for c, rs in enumerate(chunks):
                sv = sv_all[:, c * A_GROUP_DIM:(c + 1) * A_GROUP_DIM] + bs_ref[g][:, 0:1]
                acta_scr[rs, cs] = (uz[rs, cs] * sv).astype(BF16)
        else:
            sv = jnp.swapaxes(sc_ref[0, g - A_ROW_GROUPS].astype(F32), 0, 1).reshape(t, A_GROUP_DIM)
            acta_scr[:, cs] = (uz[:, cs] * sv).astype(BF16)
    ya = jnp.dot(acta_scr[...], wpa_ref[...], preferred_element_type=F32)

    gate_b = jax.nn.sigmoid(proj(B_VAL_WIDTH + 2 * A_WIDTH + D_MODEL, D_MODEL))
    yb = _dot(on_scr[...] * (_silu(zb) * bg_ref[...]), wpb_ref[...])

    m = (gate_a * ya + gate_b * yb).astype(BF16)
    for rs in blocks:
        y = jnp.dot(m[rs], wo_ref[...], preferred_element_type=F32)
        xo = x_ref[0, rs, :] + mod_ref[pl.ds(pl.program_id(0), 1), 2 * D_MODEL:] * y
        o_ref[0, rs, :] = _rms_rows(xo) * fg_ref[...]


def _main_call(x, mod, ng, w_main, vn_row, sv_col, gla_ops, ws, bs, bg, wpa, wpb, wo, fg, t):
    b, l, _ = x.shape
    nw = w_main.shape[1]
    ncg = A_GROUPS - A_ROW_GROUPS
    const2 = lambda i, j: (0, 0)
    const3 = lambda i, j: (0, 0, 0)
    tok = lambda width: pl.BlockSpec((1, t, width), lambda i, j: (i, j, 0))
    nblk = t // GLA_BLOCK
    tot_spec = pl.BlockSpec((1, nblk, GLA_NC, B_KEY_WIDTH), lambda i, j: (i, j, 0, 0))
    st_spec = pl.BlockSpec((1, nblk, B_KEY_WIDTH, B_DV), lambda i, j: (i, j, 0, 0))
    kv_spec = pl.BlockSpec((1, nblk, B_DV, B_KEY_WIDTH), lambda i, j: (i, j, 0, 0))
    s0_spec = pl.BlockSpec((1, B_DV, B_KEY_WIDTH), lambda i, j: (i, 0, 0))
    return pl.pallas_call(
        _main_kernel,
        grid=(b, l // t),
        in_specs=[tok(D_MODEL),
                  pl.BlockSpec((8, 3 * D_MODEL), const2),
                  pl.BlockSpec((1, D_MODEL), const2),
                  pl.BlockSpec((D_MODEL, nw), const2),
                  tok(A_ROW_GROUPS * A_GROUP_DIM),
                  pl.BlockSpec((1, ncg, GRID_W, t // GRID_W, A_GROUP_DIM), lambda i, j: (i, 0, 0, j, 0)),
                  tok(B_KEY_WIDTH), tok(B_KEY_WIDTH), tok(B_KEY_WIDTH),
                  tok(B_KEY_WIDTH), tok(B_KEY_WIDTH), tok(B_KEY_WIDTH), tok(B_VAL_WIDTH),
                  tot_spec, tot_spec, kv_spec, s0_spec, st_spec,
                  pl.BlockSpec((A_GROUPS, A_CHUNK, A_CHUNK), const3),
                  pl.BlockSpec((A_GROUPS, A_CHUNK, A_GROUP_DIM), const3),
                  pl.BlockSpec((1, B_VAL_WIDTH), const2),
                  pl.BlockSpec((A_WIDTH, D_MODEL), const2),
                  pl.BlockSpec((B_VAL_WIDTH, D_MODEL), const2),
                  pl.BlockSpec((D_MODEL, D_MODEL), const2),
                  pl.BlockSpec((1, D_MODEL), const2)],
        out_specs=tok(D_MODEL),
        out_shape=jax.ShapeDtypeStruct((b, l, D_MODEL), F32),
        scratch_shapes=[pltpu.VMEM((t, A_WIDTH), BF16), pltpu.VMEM((t, B_VAL_WIDTH), F32),
                        pltpu.VMEM((nblk, B_HEADS, GLA_BLOCK, GLA_BLOCK), BF16),
                        pltpu.VMEM((B_DV, B_KEY_WIDTH), F32)],
        compiler_params=pltpu.CompilerParams(
            dimension_semantics=("parallel", "arbitrary"), vmem_limit_bytes=VMEM_LIMIT_BYTES),
        name="main",
    )(x, mod, ng, w_main, vn_row, sv_col, *gla_ops, ws, bs, bg, wpa, wpb, wo, fg)


def kernel(x, c, ctx, c_ctx, w_mod, b_mod, norm_g, w_in, a_ln_g, a_ln_b, a_ws, a_bs, b_gate_w2, b_gate_b,
           b_norm_g, w_proj_a, w_proj_b, w_out, final_norm_g):
    assert w_mod.shape[0] == 1, "single-layer block"
    b, l, _ = x.shape
    rows = l // GRID_W
    assert rows == A_CHUNK
    ng = norm_g[0][None, :]

    w_in_t = w_in[0].T
    w_pre, mod, bs_t = _wprep_call(w_in_t, c, c_ctx, w_mod, b_mod, a_bs)
    w2, gb = b_gate_w2[0], b_gate_b[0]

    (qd_f, kd_f, kc_f, qd_b, kd_b, kc_b, v16, tot_f, tot_b, kv_f, s_b, vn_row, sv_col,
     w_main, wpa16, wpb16, wo16, s0f) = _pre_call(
        x, mod, ng, w_pre, w2, gb, a_ln_g[0][None, :], a_ln_b[0][None, :], ctx,
        w_in_t, w_proj_a, w_proj_b, w_out, a_ws, bs_t, TOKEN_TILE)
    gla_ops = (qd_f, kd_f, kc_f, qd_b, kd_b, kc_b, v16, tot_f, tot_b, kv_f, s0f, s_b)


    return _main_call(x, mod, ng, w_main, vn_row, sv_col, gla_ops, a_ws[0], bs_t, b_norm_g[0][None, :],
                      wpa16, wpb16, wo16, final_norm_g[None, :], TOKEN_TILE)
```
